```python
import math
import jax
import jax.numpy as jnp
from jax import lax
import numpy as np

D_MODEL = 1024
BATCH = 8
SEQ = 8192
DEPTH = 1
DEC_BATCH = 1
DEC_SEQ = 16384
PAST_LEN = 128

GRID_W = 64
N_HEADS = 8
N_KV_HEADS = 2
HEAD_DIM = 64
GQA_GROUP = N_HEADS // N_KV_HEADS
Q_W = N_HEADS * HEAD_DIM
KV_W = N_KV_HEADS * HEAD_DIM
ROPE_AXIS_DIM = HEAD_DIM // 2
ROPE_THETA = 10000.0
Q_BLOCK = 128
SGU_GROUPS = 8
SGU_W = D_MODEL // 2
SGU_GROUP_DIM = SGU_W // SGU_GROUPS
CHUNK = 128
IN_W = Q_W + 2 * KV_W + 2 * SGU_W + 2 * D_MODEL
SPLITS = (Q_W, Q_W + KV_W, Q_W + 2 * KV_W, Q_W + 2 * KV_W + SGU_W,
          Q_W + 2 * KV_W + 2 * SGU_W, Q_W + 2 * KV_W + 2 * SGU_W + D_MODEL)
N_EXPERTS = 32
TOP_K = 4
D_FF = D_MODEL
SWIGLU_LIMIT = 7.0
SWIGLU_ALPHA = 1.702
EPS = 1e-6

kernel_name = "hybrid_gqa_sgu_moe_encoder"


def _rms_norm(x, g):
    xf = x.astype(jnp.float32)
    y = xf * lax.rsqrt(jnp.mean(xf * xf, axis=-1, keepdims=True) + EPS)
    return (y * g.astype(jnp.float32)).astype(x.dtype)


def _axial_rope_tables(S):
    rows = S // GRID_W
    r = jnp.repeat(jnp.arange(rows, dtype=jnp.float32), GRID_W)
    c = jnp.tile(jnp.arange(GRID_W, dtype=jnp.float32), rows)
    inv = jnp.float32(ROPE_THETA) ** (-jnp.arange(0, ROPE_AXIS_DIM, 2, dtype=jnp.float32) / ROPE_AXIS_DIM)
    ang = jnp.concatenate([r[:, None] * inv[None, :], c[:, None] * inv[None, :]], axis=-1)
    return jnp.cos(ang), jnp.sin(ang)


def _apply_rope(x, cos, sin):
    xf = x.astype(jnp.float32).reshape(*x.shape[:-1], HEAD_DIM // 2, 2)
    x0, x1 = xf[..., 0], xf[..., 1]
    c = cos[None, :, None, :]
    s = sin[None, :, None, :]
    out = jnp.stack([x0 * c - x1 * s, x0 * s + x1 * c], axis=-1)
    return out.reshape(x.shape).astype(x.dtype)


def _block_attention(q, k, v):
    B, S, _, _ = q.shape
    nb = S // Q_BLOCK
    scale = 1.0 / math.sqrt(HEAD_DIM)
    qb = q.reshape(B, nb, Q_BLOCK, N_KV_HEADS, GQA_GROUP, HEAD_DIM).transpose(1, 0, 2, 3, 4, 5)

    def one_block(q_blk):
        s = jnp.einsum('bqkgd,bskd->bkgqs', q_blk, k).astype(jnp.float32) * scale
        p = jax.nn.softmax(s, axis=-1)
        return jnp.einsum('bkgqs,bskd->bqkgd', p.astype(v.dtype), v)

    o = lax.map(one_block, qb)
    return o.transpose(1, 0, 2, 3, 4, 5).reshape(B, S, Q_W)


def _spatial_gating(u, vs, g_norm, w_s, b_s):
    B, S, _ = u.shape
    nc = S // CHUNK
    u = jax.nn.gelu(u, approximate=False)
    vs = _rms_norm(jax.nn.gelu(vs, approximate=False), g_norm)
    vc = vs.reshape(B, nc, CHUNK, SGU_GROUPS, SGU_GROUP_DIM)
    mixed = jnp.einsum('gij,bcjgd->bcigd', w_s, vc) + b_s.T[None, None, :, :, None]
    return u * mixed.reshape(B, S, SGU_W)


def _moe(h, w_router, b_router, w_gate_up, b_gate_up, w_down, b_down):
    B, S, D = h.shape
    t = h.reshape(B * S, D)
    logits = (t @ w_router + b_router).astype(jnp.float32)
    top_v, top_i = lax.top_k(logits, TOP_K)
    top_w = jax.nn.softmax(top_v, axis=-1)
    combine = jnp.sum(jax.nn.one_hot(top_i, N_EXPERTS, dtype=jnp.float32) * top_w[..., None], axis=1)
    out = jnp.zeros((B * S, D), jnp.float32)
    for e in range(N_EXPERTS):
        gu = t @ w_gate_up[e] + b_gate_up[e]
        glu = jnp.minimum(gu[:, ::2], SWIGLU_LIMIT)
        lin = jnp.clip(gu[:, 1::2], -SWIGLU_LIMIT, SWIGLU_LIMIT)
        a = glu * jax.nn.sigmoid(SWIGLU_ALPHA * glu) * (lin + 1.0)
        y = a @ w_down[e] + b_down[e]
        out = out + combine[:, e:e + 1] * y.astype(jnp.float32)
    return out.astype(h.dtype).reshape(B, S, D)


def _trunk(x, norm_mix_g, w_in, q_norm_g, k_norm_g, sgu_norm_g, w_spatial, b_spatial,
           w_proj_attn, w_proj_sgu, w_out, norm_ffn_g, w_router, b_router,
           w_gate_up, b_gate_up, w_down, b_down, norm_final_g):
    B, S, _ = x.shape
    cos, sin = _axial_rope_tables(S)
    for l in range(DEPTH):
        h = _rms_norm(x, norm_mix_g[l])
        z = jnp.einsum('bsd,de->bse', h, w_in[l])
        q, k, v, u, vs, ga, gb = jnp.split(z, SPLITS, axis=-1)
        q = _apply_rope(_rms_norm(q.reshape(B, S, N_HEADS, HEAD_DIM), q_norm_g[l]), cos, sin)
        k = _apply_rope(_rms_norm(k.reshape(B, S, N_KV_HEADS, HEAD_DIM), k_norm_g[l]), cos, sin)
        v = v.reshape(B, S, N_KV_HEADS, HEAD_DIM)
        pa = _block_attention(q, k, v) @ w_proj_attn[l]
        pb = _spatial_gating(u, vs, sgu_norm_g[l], w_spatial[l], b_spatial[l]) @ w_proj_sgu[l]
        m = jax.nn.sigmoid(ga) * pa + jax.nn.sigmoid(gb) * pb
        x = x + m @ w_out[l]
        x = x + _moe(_rms_norm(x, norm_ffn_g[l]), w_router[l], b_router[l],
                     w_gate_up[l], b_gate_up[l], w_down[l], b_down[l])
    return _rms_norm(x, norm_final_g)


def setup_inputs(seed: int = 0) -> dict:
    key = jax.random.key(seed)
    ks = jax.random.split(key, 20)
    f32 = jnp.float32
    nrm = lambda k, shape, s: jax.random.normal(k, shape, f32) * s
    return {
        "x_prompt": nrm(ks[0], (BATCH, SEQ, D_MODEL), 1.0),
        "x_sample": nrm(ks[1], (DEC_BATCH, DEC_SEQ, D_MODEL), 1.0),
        "norm_mix_g": 1.0 + nrm(ks[2], (DEPTH, D_MODEL), 0.02),
        "w_in": nrm(ks[3], (DEPTH, D_MODEL, IN_W), D_MODEL ** -0.5),
        "q_norm_g": 1.0 + nrm(ks[4], (DEPTH, HEAD_DIM), 0.02),
        "k_norm_g": 1.0 + nrm(ks[5], (DEPTH, HEAD_DIM), 0.02),
        "sgu_norm_g": 1.0 + nrm(ks[6], (DEPTH, SGU_W), 0.02),
        "w_spatial": nrm(ks[7], (DEPTH, SGU_GROUPS, CHUNK, CHUNK), CHUNK ** -0.5),
        "b_spatial": nrm(ks[8], (DEPTH, SGU_GROUPS, CHUNK), 0.02),
        "w_proj_attn": nrm(ks[9], (DEPTH, Q_W, D_MODEL), Q_W ** -0.5),
        "w_proj_sgu": nrm(ks[10], (DEPTH, SGU_W, D_MODEL), SGU_W ** -0.5),
        "w_out": nrm(ks[11], (DEPTH, D_MODEL, D_MODEL), D_MODEL ** -0.5),
        "norm_ffn_g": 1.0 + nrm(ks[12], (DEPTH, D_MODEL), 0.02),
        "w_router": nrm(ks[13], (DEPTH, D_MODEL, N_EXPERTS), D_MODEL ** -0.5),
        "b_router": nrm(ks[14], (DEPTH, N_EXPERTS), 0.01),
        "w_gate_up": nrm(ks[15], (DEPTH, N_EXPERTS, D_MODEL, 2 * D_FF), D_MODEL ** -0.5),
        "b_gate_up": nrm(ks[16], (DEPTH, N_EXPERTS, 2 * D_FF), 0.02),
        "w_down": nrm(ks[17], (DEPTH, N_EXPERTS, D_FF, D_MODEL), D_FF ** -0.5),
        "b_down": nrm(ks[18], (DEPTH, N_EXPERTS, D_MODEL), 0.02),
        "norm_final_g": 1.0 + nrm(ks[19], (D_MODEL,), 0.02),
    }


def reference(x_prompt, x_sample, norm_mix_g, w_in, q_norm_g, k_norm_g, sgu_norm_g,
              w_spatial, b_spatial, w_proj_attn, w_proj_sgu, w_out, norm_ffn_g,
              w_router, b_router, w_gate_up, b_gate_up, w_down, b_down, norm_final_g):
    y_prompt = _trunk(x_prompt, norm_mix_g, w_in, q_norm_g, k_norm_g, sgu_norm_g, w_spatial,
                      b_spatial, w_proj_attn, w_proj_sgu, w_out, norm_ffn_g, w_router,
                      b_router, w_gate_up, b_gate_up, w_down, b_down, norm_final_g)
    y_sample = _trunk(x_sample, norm_mix_g, w_in, q_norm_g, k_norm_g, sgu_norm_g, w_spatial,
                      b_spatial, w_proj_attn, w_proj_sgu, w_out, norm_ffn_g, w_router,
                      b_router, w_gate_up, b_gate_up, w_down, b_down, norm_final_g)
    return (y_prompt, y_sample)
```

```python
import functools
import math

import jax
import jax.numpy as jnp
from jax import lax
from jax.experimental import pallas as pl
from jax.experimental.pallas import tpu as pltpu

D_MODEL = 1024
GRID_W = 64
N_HEADS = 8
N_KV_HEADS = 2
HEAD_DIM = 64
GQA_GROUP = N_HEADS // N_KV_HEADS
Q_W = N_HEADS * HEAD_DIM
KV_W = N_KV_HEADS * HEAD_DIM
ROPE_AXIS_DIM = HEAD_DIM // 2
ROPE_THETA = 10000.0
SGU_GROUPS = 8
SGU_W = D_MODEL // 2
SGU_GROUP_DIM = SGU_W // SGU_GROUPS
CHUNK = 128
IN_W = Q_W + 2 * KV_W + 2 * SGU_W + 2 * D_MODEL
N_EXPERTS = 32
TOP_K = 4
D_FF = D_MODEL
SWIGLU_LIMIT = 7.0
SWIGLU_ALPHA = 1.702
EPS = 1e-6

_Q0, _K0, _V0 = 0, Q_W, Q_W + KV_W
_U0 = Q_W + 2 * KV_W
_VS0 = _U0 + SGU_W
_GA0 = _VS0 + SGU_W
_GB0 = _GA0 + D_MODEL

TOKEN_TILE = 512
Q_TILE = 256
KEY_TILE = TOKEN_TILE
EXPERT_TILE = 512
DISPATCH_TILE = 256
COMBINE_TILE = 256

_LANES = 128
_NEG_BIG = -1e30
_MIB = 1024 * 1024

_NT_DIMS = (((1,), (1,)), ((), ()))
_TN_DIMS = (((0,), (0,)), ((), ()))


def _compiler_params(semantics, vmem_mib):
    return pltpu.CompilerParams(
        dimension_semantics=semantics, vmem_limit_bytes=vmem_mib * _MIB)


def _swap_adjacent_rows(x):
    n = x.shape[0]
    row = lax.broadcasted_iota(jnp.int32, x.shape, 0)
    nxt = pltpu.roll(x, n - 1, 0)
    prv = pltpu.roll(x, 1, 0)
    return jnp.where((row & 1) == 0, nxt, prv)


def _head_norm_rope(z, tab_a, tab_b, n_heads):
    tm = z.shape[1]
    z3 = z.reshape(n_heads, HEAD_DIM, tm)
    ms = jnp.mean(z3 * z3, axis=1, keepdims=True)
    r = lax.rsqrt(ms + EPS)
    zs = _swap_adjacent_rows(z).reshape(n_heads, HEAD_DIM, tm)
    out = (z3 * tab_a[None] + zs * tab_b[None]) * r
    return out.reshape(n_heads * HEAD_DIM, tm)


def _gelu(x):
    return 0.5 * x * (1.0 + lax.erf(x * (1.0 / math.sqrt(2.0))))


def _tile_lanes(x, reps):
    return jnp.concatenate([x] * reps, axis=1) if reps > 1 else x


def _in_proj_kernel(x_ref, gmix_ref, w_ref, qa_ref, qb_ref, ka_ref, kb_ref, gs_ref,
                    qT_ref, k_ref, vT_ref, uT_ref, vsT_ref, gaT_ref, gbT_ref):
    tm = x_ref.shape[1]
    x = x_ref[0]
    ms = jnp.mean(x * x, axis=-1, keepdims=True)
    h = (x * lax.rsqrt(ms + EPS) * gmix_ref[...]).astype(jnp.bfloat16)

    def proj(r0, rows):
        return lax.dot_general(w_ref[r0:r0 + rows, :], h, _NT_DIMS,
                               preferred_element_type=jnp.float32)

    zq = proj(_Q0, Q_W)
    qT_ref[0] = _head_norm_rope(zq, qa_ref[...], qb_ref[...], N_HEADS).astype(jnp.bfloat16)

    zkv = proj(_K0, 2 * KV_W)
    kT = _head_norm_rope(zkv[:KV_W], ka_ref[...], kb_ref[...], N_KV_HEADS)
    k_ref[0] = kT.T.astype(jnp.bfloat16)
    vT_ref[0, 0] = zkv[KV_W:].astype(jnp.bfloat16)

    uT_ref[0] = _gelu(proj(_U0, SGU_W)).astype(jnp.bfloat16)

    vs = _gelu(proj(_VS0, SGU_W))
    vms = jnp.mean(vs * vs, axis=0, keepdims=True)
    gs = _tile_lanes(gs_ref[...], tm // _LANES)
    vsT_ref[0] = (vs * lax.rsqrt(vms + EPS) * gs).astype(jnp.bfloat16)

    gaT_ref[0] = jax.nn.sigmoid(proj(_GA0, D_MODEL)).astype(jnp.bfloat16)
    gbT_ref[0] = jax.nn.sigmoid(proj(_GB0, D_MODEL)).astype(jnp.bfloat16)


def _in_proj(x, gmix, w_inT, qa, qb, ka, kb, gs):
    B, S, _ = x.shape
    tm = TOKEN_TILE
    nt = S // tm
    bf = jnp.bfloat16
    const2 = lambda b, i: (0, 0)
    tab = pl.BlockSpec((HEAD_DIM, tm), lambda b, i: (0, i))
    fm = lambda rows: pl.BlockSpec((1, rows, tm), lambda b, i: (b, 0, i))
    return pl.pallas_call(
        _in_proj_kernel,
        grid=(B, nt),
        in_specs=[
            pl.BlockSpec((1, tm, D_MODEL), lambda b, i: (b, i, 0)),
            pl.BlockSpec((1, D_MODEL), const2),
            pl.BlockSpec((IN_W, D_MODEL), const2),
            tab, tab, tab, tab,
            pl.BlockSpec((SGU_W, _LANES), const2),
        ],
        out_specs=[
            fm(Q_W),
            pl.BlockSpec((1, tm, KV_W), lambda b, i: (b, i, 0)),
            pl.BlockSpec((1, 1, KV_W, tm), lambda b, i: (b, i, 0, 0)),
            fm(SGU_W), fm(SGU_W), fm(D_MODEL), fm(D_MODEL),
        ],
        out_shape=[
            jax.ShapeDtypeStruct((B, Q_W, S), bf),
            jax.ShapeDtypeStruct((B, S, KV_W), bf),
            jax.ShapeDtypeStruct((B, nt, KV_W, tm), bf),
            jax.ShapeDtypeStruct((B, SGU_W, S), bf),
            jax.ShapeDtypeStruct((B, SGU_W, S), bf),
            jax.ShapeDtypeStruct((B, D_MODEL, S), bf),
            jax.ShapeDtypeStruct((B, D_MODEL, S), bf),
        ],
        compiler_params=_compiler_params(("arbitrary", "arbitrary"), 56),
        name="in_proj",
    )(x, gmix, w_inT, qa, qb, ka, kb, gs)


def _attention_kernel(qT_ref, k_ref, vT_ref, o_ref, m_ref, l_ref, acc_ref):
    tq = qT_ref.shape[2]
    n_kt = vT_ref.shape[1]
    tk = vT_ref.shape[3]
    nq = GQA_GROUP * tq

    zeros = jnp.zeros((HEAD_DIM, nq), jnp.bfloat16)
    qp = []
    for kv in range(N_KV_HEADS):
        heads = [qT_ref[0, (kv * GQA_GROUP + g) * HEAD_DIM:(kv * GQA_GROUP + g + 1) * HEAD_DIM, :]
                 for g in range(GQA_GROUP)]
        qk = jnp.concatenate(heads, axis=1)
        qp.append(jnp.concatenate([qk, zeros] if kv == 0 else [zeros, qk], axis=0))

    m_ref[...] = jnp.full(m_ref.shape, _NEG_BIG, jnp.float32)
    l_ref[...] = jnp.zeros(l_ref.shape, jnp.float32)
    acc_ref[...] = jnp.zeros(acc_ref.shape, jnp.float32)

    def step(ki, carry):
        kt = k_ref[0, pl.ds(pl.multiple_of(ki * tk, tk), tk), :]
        vt = vT_ref[0, ki]
        for kv in range(N_KV_HEADS):
            s = jnp.dot(kt, qp[kv], preferred_element_type=jnp.float32)
            m_old = m_ref[kv]
            m_new = jnp.maximum(m_old, jnp.max(s, axis=0, keepdims=True))
            alpha = jnp.exp(m_old - m_new)
            p = jnp.exp(s - m_new)
            l_ref[kv] = alpha * l_ref[kv] + jnp.sum(p, axis=0, keepdims=True)
            pv = jnp.dot(vt[kv * HEAD_DIM:(kv + 1) * HEAD_DIM, :], p.astype(jnp.bfloat16),
                         preferred_element_type=jnp.float32)
            acc_ref[kv] = alpha * acc_ref[kv] + pv
            m_ref[kv] = m_new
        return carry

    lax.fori_loop(0, n_kt, step, 0)

    for kv in range(N_KV_HEADS):
        o = acc_ref[kv] / l_ref[kv]
        for g in range(GQA_GROUP):
            r0 = (kv * GQA_GROUP + g) * HEAD_DIM
            o_ref[0, r0:r0 + HEAD_DIM, :] = o[:, g * tq:(g + 1) * tq].astype(jnp.bfloat16)


def _attention(qT, k, vT):
    B, _, S = qT.shape
    tq = Q_TILE
    nq = GQA_GROUP * tq
    n_kt, tk = vT.shape[1], vT.shape[3]
    return pl.pallas_call(
        _attention_kernel,
        grid=(B, S // tq),
        in_specs=[
            pl.BlockSpec((1, Q_W, tq), lambda b, i: (b, 0, i)),
            pl.BlockSpec((1, S, KV_W), lambda b, i: (b, 0, 0)),
            pl.BlockSpec((1, n_kt, KV_W, tk), lambda b, i: (b, 0, 0, 0)),
        ],
        out_specs=pl.BlockSpec((1, Q_W, tq), lambda b, i: (b, 0, i)),
        out_shape=jax.ShapeDtypeStruct((B, Q_W, S), jnp.bfloat16),
        scratch_shapes=[
            pltpu.VMEM((N_KV_HEADS, 1, nq), jnp.float32),
            pltpu.VMEM((N_KV_HEADS, 1, nq), jnp.float32),
            pltpu.VMEM((N_KV_HEADS, HEAD_DIM, nq), jnp.float32),
        ],
        compiler_params=_compiler_params(("arbitrary", "arbitrary"), 56),
        name="attention",
    )(qT, k, vT)


def _post_kernel(attnT_ref, uT_ref, vsT_ref, gaT_ref, gbT_ref, x_ref,
                 wsT_ref, bs_ref, wpaT_ref, wpbT_ref, wout_ref, gffn_ref, wrT_ref, br_ref,
                 x1_ref, h2_ref, ids_ref, wts_ref, rank_ref, cnt_ref, carry_ref):
    tm = x_ref.shape[1]
    n_chunks = tm // CHUNK
    first = jnp.logical_and(pl.program_id(0) == 0, pl.program_id(1) == 0)

    @pl.when(first)
    def _():
        carry_ref[...] = jnp.zeros(carry_ref.shape, jnp.float32)

    gate_rows = []
    for g in range(SGU_GROUPS):
        r0 = g * SGU_GROUP_DIM
        vs_g = vsT_ref[0, r0:r0 + SGU_GROUP_DIM, :]
        lhs = jnp.concatenate(
            [vs_g[:, c * CHUNK:(c + 1) * CHUNK] for c in range(n_chunks)], axis=0)
        mixed = jnp.dot(lhs, wsT_ref[g], preferred_element_type=jnp.float32)
        mixed = mixed + bs_ref[g]
        mixedT = jnp.concatenate(
            [mixed[c * SGU_GROUP_DIM:(c + 1) * SGU_GROUP_DIM] for c in range(n_chunks)], axis=1)
        u_g = uT_ref[0, r0:r0 + SGU_GROUP_DIM, :].astype(jnp.float32)
        gate_rows.append((u_g * mixedT).astype(jnp.bfloat16))
    gateT = jnp.concatenate(gate_rows, axis=0)

    paT = jnp.dot(wpaT_ref[...], attnT_ref[0], preferred_element_type=jnp.float32)
    pbT = jnp.dot(wpbT_ref[...], gateT, preferred_element_type=jnp.float32)
    mT = (gaT_ref[0].astype(jnp.float32) * paT
          + gbT_ref[0].astype(jnp.float32) * pbT).astype(jnp.bfloat16)
    y = lax.dot_general(mT, wout_ref[...], _TN_DIMS, preferred_element_type=jnp.float32)
    x1 = x_ref[0] + y
    x1_ref[0] = x1

    ms = jnp.mean(x1 * x1, axis=-1, keepdims=True)
    h2 = x1 * lax.rsqrt(ms + EPS) * gffn_ref[...]
    h2_ref[0] = h2

    logits = lax.dot_general(wrT_ref[...], h2, _NT_DIMS, precision=lax.Precision.HIGHEST,
                             preferred_element_type=jnp.float32) + br_ref[...]
    eidx = lax.broadcasted_iota(jnp.int32, logits.shape, 0).astype(jnp.float32)
    work = logits
    vals, ids, sels = [], [], []
    for _ in range(TOP_K):
        mx = jnp.max(work, axis=0, keepdims=True)
        idx = jnp.min(jnp.where(work == mx, eidx, float(N_EXPERTS)), axis=0, keepdims=True)
        sel = eidx == idx
        vals.append(mx)
        ids.append(idx)
        sels.append(sel)
        work = jnp.where(sel, -jnp.inf, work)
    exps = [jnp.exp(v - vals[0]) for v in vals]
    denom = exps[0] + exps[1] + exps[2] + exps[3]
    ids_ref[...] = jnp.concatenate(ids, axis=0).astype(jnp.int32)
    wts_ref[...] = jnp.concatenate([e / denom for e in exps], axis=0)

    onehot = [s.astype(jnp.float32) for s in sels]
    hits = onehot[0] + onehot[1] + onehot[2] + onehot[3]
    ti = lax.broadcasted_iota(jnp.int32, (tm, tm), 0)
    tj = lax.broadcasted_iota(jnp.int32, (tm, tm), 1)
    upper = (ti < tj).astype(jnp.bfloat16)
    prefix = jnp.dot(hits.astype(jnp.bfloat16), upper, preferred_element_type=jnp.float32)
    base = prefix + carry_ref[...]
    ranks = [jnp.sum(oh * base, axis=0, keepdims=True) for oh in onehot]
    rank_ref[...] = jnp.concatenate(ranks, axis=0).astype(jnp.int32)
    carry_ref[...] = carry_ref[...] + jnp.sum(hits, axis=1, keepdims=True)
    cnt_ref[...] = carry_ref[...].astype(jnp.int32)


def _post(attnT, uT, vsT, gaT, gbT, x, wsT, bs, wpaT, wpbT, wout, gffn, wrT, br):
    B, S, _ = x.shape
    tm = TOKEN_TILE
    nt = S // tm
    T = B * S
    fm = lambda rows: pl.BlockSpec((1, rows, tm), lambda b, i: (b, 0, i))
    rowm = pl.BlockSpec((1, tm, D_MODEL), lambda b, i: (b, i, 0))
    c2 = lambda b, i: (0, 0)
    c3 = lambda b, i: (0, 0, 0)
    tokT = pl.BlockSpec((TOP_K, tm), lambda b, i: (0, b * nt + i))
    return pl.pallas_call(
        _post_kernel,
        grid=(B, nt),
        in_specs=[
            fm(Q_W), fm(SGU_W), fm(SGU_W), fm(D_MODEL), fm(D_MODEL), rowm,
            pl.BlockSpec((SGU_GROUPS, CHUNK, CHUNK), c3),
            pl.BlockSpec((SGU_GROUPS, 1, CHUNK), c3),
            pl.BlockSpec((D_MODEL, Q_W), c2),
            pl.BlockSpec((D_MODEL, SGU_W), c2),
            pl.BlockSpec((D_MODEL, D_MODEL), c2),
            pl.BlockSpec((1, D_MODEL), c2),
            pl.BlockSpec((N_EXPERTS, D_MODEL), c2),
            pl.BlockSpec((N_EXPERTS, 1), c2),
        ],
        out_specs=[rowm, rowm, tokT, tokT, tokT, pl.BlockSpec((N_EXPERTS, 1), c2)],
        out_shape=[
            jax.ShapeDtypeStruct((B, S, D_MODEL), jnp.float32),
            jax.ShapeDtypeStruct((B, S, D_MODEL), jnp.float32),
            jax.ShapeDtypeStruct((TOP_K, T), jnp.int32),
            jax.ShapeDtypeStruct((TOP_K, T), jnp.float32),
            jax.ShapeDtypeStruct((TOP_K, T), jnp.int32),
            jax.ShapeDtypeStruct((N_EXPERTS, 1), jnp.int32),
        ],
        scratch_shapes=[pltpu.VMEM((N_EXPERTS, 1), jnp.float32)],
        compiler_params=_compiler_params(("arbitrary", "arbitrary"), 56),
        name="post",
    )(attnT, uT, vsT, gaT, gbT, x, wsT, bs, wpaT, wpbT, wout, gffn, wrT, br)


def _row_copy(src, src_row, dst, dst_row, sem):
    return pltpu.make_async_copy(src.at[pl.ds(src_row, 1)], dst.at[pl.ds(dst_row, 1)], sem)


def _dispatch_kernel(dest_ref, h_ref, out_ref, sem):
    tt = h_ref.shape[0]

    def issue(r, carry):
        for k in range(TOP_K):
            _row_copy(h_ref, r, out_ref, dest_ref[0, 0, k * tt + r], sem).start()
        return carry

    lax.fori_loop(0, tt, issue, 0)
    for _ in range(TOP_K):
        pltpu.make_async_copy(h_ref, out_ref.at[pl.ds(0, tt)], sem).wait()


def _dispatch(h2, dest_tiles, n_rows):
    T = h2.shape[0]
    tt = DISPATCH_TILE
    return pl.pallas_call(
        _dispatch_kernel,
        grid=(T // tt,),
        in_specs=[
            pl.BlockSpec((1, 1, TOP_K * tt), lambda i: (i, 0, 0), memory_space=pltpu.SMEM),
            pl.BlockSpec((tt, D_MODEL), lambda i: (i, 0)),
        ],
        out_specs=pl.BlockSpec(memory_space=pl.ANY),
        out_shape=jax.ShapeDtypeStruct((n_rows, D_MODEL), jnp.float32),
        scratch_shapes=[pltpu.SemaphoreType.DMA],
        compiler_params=_compiler_params(("arbitrary",), 32),
        name="dispatch",
    )(dest_tiles, h2)


def _combine_kernel(dest_ref, w_ref, x1_ref, gfin_ref, y_ref, o_ref, ybuf, sem):
    tt = x1_ref.shape[0]

    def issue(r, carry):
        for k in range(TOP_K):
            _row_copy(y_ref, dest_ref[0, 0, k * tt + r], ybuf.at[k], r, sem).start()
        return carry

    lax.fori_loop(0, tt, issue, 0)
    for k in range(TOP_K):
        pltpu.make_async_copy(y_ref.at[pl.ds(0, tt)], ybuf.at[k], sem).wait()

    w = w_ref[...]
    moe = (w[:, 0:1] * ybuf[0] + w[:, 1:2] * ybuf[1]
           + w[:, 2:3] * ybuf[2] + w[:, 3:4] * ybuf[3])
    x2 = x1_ref[...] + moe
    ms = jnp.mean(x2 * x2, axis=-1, keepdims=True)
    o_ref[...] = x2 * lax.rsqrt(ms + EPS) * gfin_ref[...]


def _combine(dest_tiles, wts, x1, gfin, y_sorted):
    T = x1.shape[0]
    tt = COMBINE_TILE
    return pl.pallas_call(
        _combine_kernel,
        grid=(T // tt,),
        in_specs=[
            pl.BlockSpec((1, 1, TOP_K * tt), lambda i: (i, 0, 0), memory_space=pltpu.SMEM),
            pl.BlockSpec((tt, TOP_K), lambda i: (i, 0)),
            pl.BlockSpec((tt, D_MODEL), lambda i: (i, 0)),
            pl.BlockSpec((1, D_MODEL), lambda i: (0, 0)),
            pl.BlockSpec(memory_space=pl.ANY),
        ],
        out_specs=pl.BlockSpec((tt, D_MODEL), lambda i: (i, 0)),
        out_shape=jax.ShapeDtypeStruct((T, D_MODEL), jnp.float32),
        scratch_shapes=[pltpu.VMEM((TOP_K, tt, D_MODEL), jnp.float32),
                        pltpu.SemaphoreType.DMA],
        compiler_params=_compiler_params(("arbitrary",), 32),
        name="combine",
    )(dest_tiles, wts, x1, gfin, y_sorted)


def _experts_kernel(te_ref, nv_ref, x_ref, wgu_ref, bgu_ref, wd_ref, bd_ref, y_ref):
    @pl.when(pl.program_id(0) < nv_ref[0])
    def _():
        x = x_ref[...].astype(jnp.bfloat16)
        gu = jnp.dot(x, wgu_ref[0], preferred_element_type=jnp.float32) + bgu_ref[0]
        glu = jnp.minimum(gu[:, :D_FF], SWIGLU_LIMIT)
        lin = jnp.clip(gu[:, D_FF:], -SWIGLU_LIMIT, SWIGLU_LIMIT)
        a = glu * jax.nn.sigmoid(SWIGLU_ALPHA * glu) * (lin + 1.0)
        y_ref[...] = jnp.dot(a.astype(jnp.bfloat16), wd_ref[0],
                             preferred_element_type=jnp.float32) + bd_ref[0]


def _experts(tile_expert, n_valid, x_sorted, wgu, bgu, wd, bd):
    n_rows = x_sorted.shape[0]
    tr = EXPERT_TILE
    n_tiles = n_rows // tr

    def row_map(j, te, nv):
        return (jnp.minimum(j, nv[0] - 1), 0)

    def exp_map(j, te, nv):
        return (te[j], 0, 0)

    grid_spec = pltpu.PrefetchScalarGridSpec(
        num_scalar_prefetch=2,
        grid=(n_tiles,),
        in_specs=[
            pl.BlockSpec((tr, D_MODEL), row_map),
            pl.BlockSpec((1, D_MODEL, 2 * D_FF), exp_map),
            pl.BlockSpec((1, 1, 2 * D_FF), exp_map),
            pl.BlockSpec((1, D_FF, D_MODEL), exp_map),
            pl.BlockSpec((1, 1, D_MODEL), exp_map),
        ],
        out_specs=pl.BlockSpec((tr, D_MODEL), row_map),
    )
    return pl.pallas_call(
        _experts_kernel,
        grid_spec=grid_spec,
        out_shape=jax.ShapeDtypeStruct((n_rows, D_MODEL), jnp.float32),
        compiler_params=_compiler_params(("arbitrary",), 56),
        name="experts",
    )(tile_expert, n_valid, x_sorted, wgu, bgu, wd, bd)


def _rope_tables(S, gain, scale):
    t = jnp.arange(S, dtype=jnp.int32)
    r = (t // GRID_W).astype(jnp.float32)
    c = (t % GRID_W).astype(jnp.float32)
    inv = jnp.float32(ROPE_THETA) ** (
        -jnp.arange(0, ROPE_AXIS_DIM, 2, dtype=jnp.float32) / ROPE_AXIS_DIM)
    ang = jnp.concatenate([r[None, :] * inv[:, None], c[None, :] * inv[:, None]], axis=0)
    cos = jnp.repeat(jnp.cos(ang), 2, axis=0)
    sin = jnp.repeat(jnp.sin(ang), 2, axis=0)
    sign = jnp.where(jnp.arange(HEAD_DIM) % 2 == 0, -1.0, 1.0).astype(jnp.float32)
    g = gain.astype(jnp.float32) * scale
    g_swapped = g.reshape(HEAD_DIM // 2, 2)[:, ::-1].reshape(HEAD_DIM)
    return g[:, None] * cos, (g_swapped * sign)[:, None] * sin


def _token_tiles(a, tile):
    T = a.shape[1]
    return a.reshape(TOP_K, T // tile, tile).transpose(1, 0, 2).reshape(T // tile, 1, TOP_K * tile)


def _prepare_weights(norm_mix_g, w_in, q_norm_g, k_norm_g, sgu_norm_g, w_spatial, b_spatial,
                     w_proj_attn, w_proj_sgu, w_out, norm_ffn_g, w_router, b_router,
                     w_gate_up, b_gate_up, w_down, b_down, norm_final_g):
    bf = jnp.bfloat16
    l = 0
    return dict(
        gmix=norm_mix_g[l][None, :],
        w_inT=w_in[l].T.astype(bf),
        q_gain=q_norm_g[l], k_gain=k_norm_g[l],
        gs=jnp.broadcast_to(sgu_norm_g[l][:, None], (SGU_W, _LANES)),
        wsT=jnp.swapaxes(w_spatial[l], 1, 2).astype(bf),
        bs=b_spatial[l][:, None, :],
        wpaT=w_proj_attn[l].T.astype(bf),
        wpbT=w_proj_sgu[l].T.astype(bf),
        wout=w_out[l].astype(bf),
        gffn=norm_ffn_g[l][None, :],
        wrT=w_router[l].T,
        br=b_router[l][:, None],
        wgu=jnp.concatenate([w_gate_up[l][:, :, 0::2], w_gate_up[l][:, :, 1::2]], axis=-1).astype(bf),
        bgu=jnp.concatenate([b_gate_up[l][:, 0::2], b_gate_up[l][:, 1::2]], axis=-1)[:, None, :],
        wd=w_down[l].astype(bf),
        bd=b_down[l][:, None, :],
        gfin=norm_final_g[None, :],
    )


def _trunk(x, w):
    B, S, _ = x.shape
    T = B * S
    qa, qb = _rope_tables(S, w["q_gain"], 1.0 / math.sqrt(HEAD_DIM))
    ka, kb = _rope_tables(S, w["k_gain"], 1.0)

    qT, k, vT, uT, vsT, gaT, gbT = _in_proj(x, w["gmix"], w["w_inT"], qa, qb, ka, kb, w["gs"])
    attnT = _attention(qT, k, vT)
    x1, h2, ids, wts, rank, counts = _post(
        attnT, uT, vsT, gaT, gbT, x, w["wsT"], w["bs"], w["wpaT"], w["wpbT"], w["wout"],
        w["gffn"], w["wrT"], w["br"])

    tr = EXPERT_TILE
    counts = counts[:, 0]
    padded = (counts + tr - 1) // tr * tr
    ends = jnp.cumsum(padded)
    starts = ends - padded
    n_tiles = (TOP_K * T) // tr + N_EXPERTS
    n_rows = n_tiles * tr
    dest = starts[ids] + rank
    tile_start = jnp.arange(n_tiles, dtype=jnp.int32) * tr
    tile_expert = jnp.minimum(
        jnp.searchsorted(ends, tile_start, side="right"), N_EXPERTS - 1).astype(jnp.int32)
    n_valid = (ends[-1] // tr).astype(jnp.int32)[None]

    x_sorted = _dispatch(h2.reshape(T, D_MODEL), _token_tiles(dest, DISPATCH_TILE), n_rows)
    y_sorted = _experts(tile_expert, n_valid, x_sorted, w["wgu"], w["bgu"], w["wd"], w["bd"])
    out = _combine(_token_tiles(dest, COMBINE_TILE), wts.T, x1.reshape(T, D_MODEL),
                   w["gfin"], y_sorted)
    return out.reshape(B, S, D_MODEL)


def kernel(x_prompt, x_sample, norm_mix_g, w_in, q_norm_g, k_norm_g, sgu_norm_g, w_spatial,
           b_spatial, w_proj_attn, w_proj_sgu, w_out, norm_ffn_g, w_router, b_router,
           w_gate_up, b_gate_up, w_down, b_down, norm_final_g):
    w = _prepare_weights(norm_mix_g, w_in, q_norm_g, k_norm_g, sgu_norm_g, w_spatial,
                         b_spatial, w_proj_attn, w_proj_sgu, w_out, norm_ffn_g, w_router,
                         b_router, w_gate_up, b_gate_up, w_down, b_down, norm_final_g)
    return (_trunk(x_prompt, w), _trunk(x_sample, w))
```

```python
import functools
import math

import jax
import jax.numpy as jnp
from jax import lax
from jax.experimental import pallas as pl
from jax.experimental.pallas import tpu as pltpu

D_MODEL = 1024
GRID_W = 64
N_HEADS = 8
N_KV_HEADS = 2
HEAD_DIM = 64
GQA_GROUP = N_HEADS // N_KV_HEADS
Q_W = N_HEADS * HEAD_DIM
KV_W = N_KV_HEADS * HEAD_DIM
ROPE_AXIS_DIM = HEAD_DIM // 2
ROPE_THETA = 10000.0
SGU_GROUPS = 8
SGU_W = D_MODEL // 2
SGU_GROUP_DIM = SGU_W // SGU_GROUPS
CHUNK = 128
IN_W = Q_W + 2 * KV_W + 2 * SGU_W + 2 * D_MODEL
N_EXPERTS = 32
TOP_K = 4
D_FF = D_MODEL
SWIGLU_LIMIT = 7.0
SWIGLU_ALPHA = 1.702
EPS = 1e-6

_Q0, _K0, _V0 = 0, Q_W, Q_W + KV_W
_U0 = Q_W + 2 * KV_W
_VS0 = _U0 + SGU_W
_GA0 = _VS0 + SGU_W
_GB0 = _GA0 + D_MODEL

TOKEN_TILE = 512
Q_TILE = 256
KEY_TILE = TOKEN_TILE
EXPERT_TILE = 512
DISPATCH_TILE = 256
COMBINE_TILE = 256

_LANES = 128
_BF16_SUBLANES = 16
_SUM_ROWS = _BF16_SUBLANES
_GU_BLOCK = 256
_NEG_BIG = -1e30
_MIB = 1024 * 1024

_NT_DIMS = (((1,), (1,)), ((), ()))
_TN_DIMS = (((0,), (0,)), ((), ()))


def _compiler_params(semantics, vmem_mib):
    return pltpu.CompilerParams(
        dimension_semantics=semantics, vmem_limit_bytes=vmem_mib * _MIB)


def _swap_adjacent_rows(x):
    n = x.shape[0]
    row = lax.broadcasted_iota(jnp.int32, x.shape, 0)
    nxt = pltpu.roll(x, n - 1, 0)
    prv = pltpu.roll(x, 1, 0)
    return jnp.where((row & 1) == 0, nxt, prv)


def _head_norm_rope(z, tab_a, tab_b, n_heads):
    tm = z.shape[1]
    z3 = z.reshape(n_heads, HEAD_DIM, tm)
    ms = jnp.mean(z3 * z3, axis=1, keepdims=True)
    r = lax.rsqrt(ms + EPS)
    zs = _swap_adjacent_rows(z).reshape(n_heads, HEAD_DIM, tm)
    out = (z3 * tab_a[None] + zs * tab_b[None]) * r
    return out.reshape(n_heads * HEAD_DIM, tm)


def _gelu(x):
    return 0.5 * x * (1.0 + lax.erf(x * (1.0 / math.sqrt(2.0))))


def _tile_lanes(x, reps):
    return jnp.concatenate([x] * reps, axis=1) if reps > 1 else x


def _in_proj_kernel(x_ref, gmix_ref, w_ref, qa_ref, qb_ref, ka_ref, kb_ref, gs_ref,
                    qT_ref, k_ref, vT_ref, uT_ref, vsT_ref, gaT_ref, gbT_ref):
    tm = x_ref.shape[1]
    x = x_ref[0]
    ms = jnp.mean(x * x, axis=-1, keepdims=True)
    h = (x * lax.rsqrt(ms + EPS) * gmix_ref[...]).astype(jnp.bfloat16)

    def proj(r0, rows):
        return lax.dot_general(w_ref[r0:r0 + rows, :], h, _NT_DIMS,
                               preferred_element_type=jnp.float32)

    zq = proj(_Q0, Q_W)
    qT_ref[0] = _head_norm_rope(zq, qa_ref[...], qb_ref[...], N_HEADS).astype(jnp.bfloat16)

    zkv = proj(_K0, 2 * KV_W)
    kT = _head_norm_rope(zkv[:KV_W], ka_ref[...], kb_ref[...], N_KV_HEADS)
    k_ref[0] = kT.T.astype(jnp.bfloat16)
    vT_ref[0, 0] = zkv[KV_W:].astype(jnp.bfloat16)

    uT_ref[0] = _gelu(proj(_U0, SGU_W)).astype(jnp.bfloat16)

    vs = _gelu(proj(_VS0, SGU_W))
    vms = jnp.mean(vs * vs, axis=0, keepdims=True)
    gs = _tile_lanes(gs_ref[...], tm // _LANES)
    vsT_ref[0] = (vs * lax.rsqrt(vms + EPS) * gs).astype(jnp.bfloat16)

    gaT_ref[0] = jax.nn.sigmoid(proj(_GA0, D_MODEL)).astype(jnp.bfloat16)
    gbT_ref[0] = jax.nn.sigmoid(proj(_GB0, D_MODEL)).astype(jnp.bfloat16)


def _in_proj(x, gmix, w_inT, qa, qb, ka, kb, gs):
    B, S, _ = x.shape
    tm = TOKEN_TILE
    nt = S // tm
    bf = jnp.bfloat16
    const2 = lambda b, i: (0, 0)
    tab = pl.BlockSpec((HEAD_DIM, tm), lambda b, i: (0, i))
    fm = lambda rows: pl.BlockSpec((1, rows, tm), lambda b, i: (b, 0, i))
    return pl.pallas_call(
        _in_proj_kernel,
        grid=(B, nt),
        in_specs=[
            pl.BlockSpec((1, tm, D_MODEL), lambda b, i: (b, i, 0)),
            pl.BlockSpec((1, D_MODEL), const2),
            pl.BlockSpec((IN_W, D_MODEL), const2),
            tab, tab, tab, tab,
            pl.BlockSpec((SGU_W, _LANES), const2),
        ],
        out_specs=[
            fm(Q_W),
            pl.BlockSpec((1, tm, KV_W), lambda b, i: (b, i, 0)),
            pl.BlockSpec((1, 1, KV_W, tm), lambda b, i: (b, i, 0, 0)),
            fm(SGU_W), fm(SGU_W), fm(D_MODEL), fm(D_MODEL),
        ],
        out_shape=[
            jax.ShapeDtypeStruct((B, Q_W, S), bf),
            jax.ShapeDtypeStruct((B, S, KV_W), bf),
            jax.ShapeDtypeStruct((B, nt, KV_W, tm), bf),
            jax.ShapeDtypeStruct((B, SGU_W, S), bf),
            jax.ShapeDtypeStruct((B, SGU_W, S), bf),
            jax.ShapeDtypeStruct((B, D_MODEL, S), bf),
            jax.ShapeDtypeStruct((B, D_MODEL, S), bf),
        ],
        compiler_params=_compiler_params(("arbitrary", "arbitrary"), 56),
        name="in_proj",
    )(x, gmix, w_inT, qa, qb, ka, kb, gs)


def _attention_kernel(qT_ref, k_ref, vT_ref, o_ref, m_ref, acc_ref, s_ref):
    tq = qT_ref.shape[2]
    n_kt = vT_ref.shape[1]
    tk = vT_ref.shape[3]
    nq = GQA_GROUP * tq
    ones_rows = (lax.broadcasted_iota(jnp.int32, (_SUM_ROWS, tk), 0) == 0).astype(jnp.bfloat16)

    zeros = jnp.zeros((HEAD_DIM, nq), jnp.bfloat16)
    qp = []
    for kv in range(N_KV_HEADS):
        heads = [qT_ref[0, (kv * GQA_GROUP + g) * HEAD_DIM:(kv * GQA_GROUP + g + 1) * HEAD_DIM, :]
                 for g in range(GQA_GROUP)]
        qk = jnp.concatenate(heads, axis=1)
        qp.append(jnp.concatenate([qk, zeros] if kv == 0 else [zeros, qk], axis=0))

    m_ref[...] = jnp.full(m_ref.shape, _NEG_BIG, jnp.float32)
    acc_ref[...] = jnp.zeros(acc_ref.shape, jnp.float32)

    def scores(ki, slot):
        kt = k_ref[0, pl.ds(pl.multiple_of(ki * tk, tk), tk), :]
        for kv in range(N_KV_HEADS):
            s_ref[slot, kv] = jnp.dot(kt, qp[kv], preferred_element_type=jnp.float32)

    def accumulate(ki, slot):
        vt = vT_ref[0, ki]
        for kv in range(N_KV_HEADS):
            s = s_ref[slot, kv]
            m_old = m_ref[kv]
            m_new = jnp.maximum(m_old, jnp.max(s, axis=0, keepdims=True))
            alpha = jnp.exp2(m_old - m_new)
            p = jnp.exp2(s - m_new).astype(jnp.bfloat16)
            v_aug = jnp.concatenate([vt[kv * HEAD_DIM:(kv + 1) * HEAD_DIM, :], ones_rows], axis=0)
            pv = jnp.dot(v_aug, p, preferred_element_type=jnp.float32)
            acc_ref[kv] = alpha * acc_ref[kv] + pv
            m_ref[kv] = m_new

    scores(0, 0)

    def pair(j, carry):
        t0 = 2 * j
        scores(t0 + 1, 1)
        accumulate(t0, 0)
        scores(jnp.minimum(t0 + 2, n_kt - 1), 0)
        accumulate(t0 + 1, 1)
        return carry

    lax.fori_loop(0, n_kt // 2, pair, 0)

    for kv in range(N_KV_HEADS):
        acc = acc_ref[kv]
        o = acc[:HEAD_DIM] / acc[HEAD_DIM:HEAD_DIM + 1]
        for g in range(GQA_GROUP):
            r0 = (kv * GQA_GROUP + g) * HEAD_DIM
            o_ref[0, r0:r0 + HEAD_DIM, :] = o[:, g * tq:(g + 1) * tq].astype(jnp.bfloat16)


def _attention(qT, k, vT):
    B, _, S = qT.shape
    tq = Q_TILE
    nq = GQA_GROUP * tq
    n_kt, tk = vT.shape[1], vT.shape[3]
    return pl.pallas_call(
        _attention_kernel,
        grid=(B, S // tq),
        in_specs=[
            pl.BlockSpec((1, Q_W, tq), lambda b, i: (b, 0, i)),
            pl.BlockSpec((1, S, KV_W), lambda b, i: (b, 0, 0)),
            pl.BlockSpec((1, n_kt, KV_W, tk), lambda b, i: (b, 0, 0, 0)),
        ],
        out_specs=pl.BlockSpec((1, Q_W, tq), lambda b, i: (b, 0, i)),
        out_shape=jax.ShapeDtypeStruct((B, Q_W, S), jnp.bfloat16),
        scratch_shapes=[
            pltpu.VMEM((N_KV_HEADS, 1, nq), jnp.float32),
            pltpu.VMEM((N_KV_HEADS, HEAD_DIM + _SUM_ROWS, nq), jnp.float32),
            pltpu.VMEM((2, N_KV_HEADS, tk, nq), jnp.float32),
        ],
        compiler_params=_compiler_params(("arbitrary", "arbitrary"), 56),
        name="attention",
    )(qT, k, vT)


def _post_kernel(attnT_ref, uT_ref, vsT_ref, gaT_ref, gbT_ref, x_ref,
                 wsT_ref, bs_ref, wpaT_ref, wpbT_ref, wout_ref, gffn_ref, wrT_ref, br_ref,
                 x1_ref, h2_ref, ids_ref, wts_ref, rank_ref, cnt_ref, carry_ref):
    tm = x_ref.shape[1]
    n_chunks = tm // CHUNK
    first = jnp.logical_and(pl.program_id(0) == 0, pl.program_id(1) == 0)

    @pl.when(first)
    def _():
        carry_ref[...] = jnp.zeros(carry_ref.shape, jnp.float32)

    gate_rows = []
    for g in range(SGU_GROUPS):
        r0 = g * SGU_GROUP_DIM
        vs_g = vsT_ref[0, r0:r0 + SGU_GROUP_DIM, :]
        lhs = jnp.concatenate(
            [vs_g[:, c * CHUNK:(c + 1) * CHUNK] for c in range(n_chunks)], axis=0)
        mixed = jnp.dot(lhs, wsT_ref[g], preferred_element_type=jnp.float32)
        mixed = mixed + bs_ref[g]
        mixedT = jnp.concatenate(
            [mixed[c * SGU_GROUP_DIM:(c + 1) * SGU_GROUP_DIM] for c in range(n_chunks)], axis=1)
        u_g = uT_ref[0, r0:r0 + SGU_GROUP_DIM, :].astype(jnp.float32)
        gate_rows.append((u_g * mixedT).astype(jnp.bfloat16))
    gateT = jnp.concatenate(gate_rows, axis=0)

    paT = jnp.dot(wpaT_ref[...], attnT_ref[0], preferred_element_type=jnp.float32)
    pbT = jnp.dot(wpbT_ref[...], gateT, preferred_element_type=jnp.float32)
    mT = (gaT_ref[0].astype(jnp.float32) * paT
          + gbT_ref[0].astype(jnp.float32) * pbT).astype(jnp.bfloat16)
    y = lax.dot_general(mT, wout_ref[...], _TN_DIMS, preferred_element_type=jnp.float32)
    x1 = x_ref[0] + y
    x1_ref[0] = x1

    ms = jnp.mean(x1 * x1, axis=-1, keepdims=True)
    h2 = x1 * lax.rsqrt(ms + EPS) * gffn_ref[...]
    h2_ref[0] = h2

    logits = lax.dot_general(wrT_ref[...], h2, _NT_DIMS, precision=lax.Precision.HIGHEST,
                             preferred_element_type=jnp.float32) + br_ref[...]
    eidx = lax.broadcasted_iota(jnp.int32, logits.shape, 0).astype(jnp.float32)
    work = logits
    vals, ids, sels = [], [], []
    for _ in range(TOP_K):
        mx = jnp.max(work, axis=0, keepdims=True)
        idx = jnp.min(jnp.where(work == mx, eidx, float(N_EXPERTS)), axis=0, keepdims=True)
        sel = eidx == idx
        vals.append(mx)
        ids.append(idx)
        sels.append(sel)
        work = jnp.where(sel, -jnp.inf, work)
    exps = [jnp.exp(v - vals[0]) for v in vals]
    denom = exps[0] + exps[1] + exps[2] + exps[3]
    ids_ref[...] = jnp.concatenate(ids, axis=0).astype(jnp.int32)
    wts_ref[...] = jnp.concatenate([e / denom for e in exps], axis=0)

    onehot = [s.astype(jnp.float32) for s in sels]
    hits = onehot[0] + onehot[1] + onehot[2] + onehot[3]
    ti = lax.broadcasted_iota(jnp.int32, (tm, tm), 0)
    tj = lax.broadcasted_iota(jnp.int32, (tm, tm), 1)
    upper = (ti < tj).astype(jnp.bfloat16)
    prefix = jnp.dot(hits.astype(jnp.bfloat16), upper, preferred_element_type=jnp.float32)
    base = prefix + carry_ref[...]
    ranks = [jnp.sum(oh * base, axis=0, keepdims=True) for oh in onehot]
    rank_ref[...] = jnp.concatenate(ranks, axis=0).astype(jnp.int32)
    carry_ref[...] = carry_ref[...] + jnp.sum(hits, axis=1, keepdims=True)
    cnt_ref[...] = carry_ref[...].astype(jnp.int32)


def _post(attnT, uT, vsT, gaT, gbT, x, wsT, bs, wpaT, wpbT, wout, gffn, wrT, br):
    B, S, _ = x.shape
    tm = TOKEN_TILE
    nt = S // tm
    T = B * S
    fm = lambda rows: pl.BlockSpec((1, rows, tm), lambda b, i: (b, 0, i))
    rowm = pl.BlockSpec((1, tm, D_MODEL), lambda b, i: (b, i, 0))
    c2 = lambda b, i: (0, 0)
    c3 = lambda b, i: (0, 0, 0)
    tokT = pl.BlockSpec((TOP_K, tm), lambda b, i: (0, b * nt + i))
    return pl.pallas_call(
        _post_kernel,
        grid=(B, nt),
        in_specs=[
            fm(Q_W), fm(SGU_W), fm(SGU_W), fm(D_MODEL), fm(D_MODEL), rowm,
            pl.BlockSpec((SGU_GROUPS, CHUNK, CHUNK), c3),
            pl.BlockSpec((SGU_GROUPS, 1, CHUNK), c3),
            pl.BlockSpec((D_MODEL, Q_W), c2),
            pl.BlockSpec((D_MODEL, SGU_W), c2),
            pl.BlockSpec((D_MODEL, D_MODEL), c2),
            pl.BlockSpec((1, D_MODEL), c2),
            pl.BlockSpec((N_EXPERTS, D_MODEL), c2),
            pl.BlockSpec((N_EXPERTS, 1), c2),
        ],
        out_specs=[rowm, rowm, tokT, tokT, tokT, pl.BlockSpec((N_EXPERTS, 1), c2)],
        out_shape=[
            jax.ShapeDtypeStruct((B, S, D_MODEL), jnp.float32),
            jax.ShapeDtypeStruct((B, S, D_MODEL), jnp.float32),
            jax.ShapeDtypeStruct((TOP_K, T), jnp.int32),
            jax.ShapeDtypeStruct((TOP_K, T), jnp.float32),
            jax.ShapeDtypeStruct((TOP_K, T), jnp.int32),
            jax.ShapeDtypeStruct((N_EXPERTS, 1), jnp.int32),
        ],
        scratch_shapes=[pltpu.VMEM((N_EXPERTS, 1), jnp.float32)],
        compiler_params=_compiler_params(("arbitrary", "arbitrary"), 56),
        name="post",
    )(attnT, uT, vsT, gaT, gbT, x, wsT, bs, wpaT, wpbT, wout, gffn, wrT, br)


def _row_copy(src, src_row, dst, dst_row, sem):
    return pltpu.make_async_copy(src.at[pl.ds(src_row, 1)], dst.at[pl.ds(dst_row, 1)], sem)


def _dispatch_kernel(dest_ref, h_ref, out_ref, sem):
    tt = h_ref.shape[0]

    def issue(r, carry):
        for k in range(TOP_K):
            _row_copy(h_ref, r, out_ref, dest_ref[0, 0, k * tt + r], sem).start(priority=k % 2)
        return carry

    lax.fori_loop(0, tt, issue, 0)
    for _ in range(TOP_K):
        pltpu.make_async_copy(h_ref, out_ref.at[pl.ds(0, tt)], sem).wait()


def _dispatch(h2, dest_tiles, n_rows):
    T = h2.shape[0]
    tt = DISPATCH_TILE
    return pl.pallas_call(
        _dispatch_kernel,
        grid=(T // tt,),
        in_specs=[
            pl.BlockSpec((1, 1, TOP_K * tt), lambda i: (i, 0, 0), memory_space=pltpu.SMEM),
            pl.BlockSpec((tt, D_MODEL), lambda i: (i, 0)),
        ],
        out_specs=pl.BlockSpec(memory_space=pl.ANY),
        out_shape=jax.ShapeDtypeStruct((n_rows, D_MODEL), jnp.float32),
        scratch_shapes=[pltpu.SemaphoreType.DMA],
        compiler_params=_compiler_params(("arbitrary",), 32),
        name="dispatch",
    )(dest_tiles, h2)


def _combine_kernel(dest_ref, w_ref, x1_ref, gfin_ref, y_ref, o_ref, ybuf, sem):
    tt = x1_ref.shape[0]

    def issue(r, carry):
        for k in range(TOP_K):
            _row_copy(y_ref, dest_ref[0, 0, k * tt + r], ybuf.at[k], r, sem).start(priority=k % 2)
        return carry

    lax.fori_loop(0, tt, issue, 0)
    for k in range(TOP_K):
        pltpu.make_async_copy(y_ref.at[pl.ds(0, tt)], ybuf.at[k], sem).wait()

    w = w_ref[...]
    moe = (w[:, 0:1] * ybuf[0] + w[:, 1:2] * ybuf[1]
           + w[:, 2:3] * ybuf[2] + w[:, 3:4] * ybuf[3])
    x2 = x1_ref[...] + moe
    ms = jnp.mean(x2 * x2, axis=-1, keepdims=True)
    o_ref[...] = x2 * lax.rsqrt(ms + EPS) * gfin_ref[...]


def _combine(dest_tiles, wts, x1, gfin, y_sorted):
    T = x1.shape[0]
    tt = COMBINE_TILE
    return pl.pallas_call(
        _combine_kernel,
        grid=(T // tt,),
        in_specs=[
            pl.BlockSpec((1, 1, TOP_K * tt), lambda i: (i, 0, 0), memory_space=pltpu.SMEM),
            pl.BlockSpec((tt, TOP_K), lambda i: (i, 0)),
            pl.BlockSpec((tt, D_MODEL), lambda i: (i, 0)),
            pl.BlockSpec((1, D_MODEL), lambda i: (0, 0)),
            pl.BlockSpec(memory_space=pl.ANY),
        ],
        out_specs=pl.BlockSpec((tt, D_MODEL), lambda i: (i, 0)),
        out_shape=jax.ShapeDtypeStruct((T, D_MODEL), jnp.float32),
        scratch_shapes=[pltpu.VMEM((TOP_K, tt, D_MODEL), jnp.float32),
                        pltpu.SemaphoreType.DMA],
        compiler_params=_compiler_params(("arbitrary",), 32),
        name="combine",
    )(dest_tiles, wts, x1, gfin, y_sorted)


def _gate_up_prep_kernel(w_ref, o_ref):
    half = _GU_BLOCK // 2
    i = lax.broadcasted_iota(jnp.int32, (_GU_BLOCK, _GU_BLOCK), 0)
    j = lax.broadcasted_iota(jnp.int32, (_GU_BLOCK, _GU_BLOCK), 1)
    perm = (i == jnp.where(j < half, 2 * j, 2 * (j - half) + 1)).astype(jnp.bfloat16)
    for c in range(2 * D_FF // _GU_BLOCK):
        cols = slice(c * _GU_BLOCK, (c + 1) * _GU_BLOCK)
        blk = w_ref[0, :, cols].astype(jnp.bfloat16)
        o_ref[0, :, cols] = jnp.dot(blk, perm, preferred_element_type=jnp.float32).astype(jnp.bfloat16)


def _gate_up_prep(w_gate_up):
    spec = pl.BlockSpec((1, D_MODEL, 2 * D_FF), lambda e: (e, 0, 0))
    return pl.pallas_call(
        _gate_up_prep_kernel,
        grid=(N_EXPERTS,),
        in_specs=[spec],
        out_specs=spec,
        out_shape=jax.ShapeDtypeStruct(w_gate_up.shape, jnp.bfloat16),
        compiler_params=_compiler_params(("arbitrary",), 48),
        name="gate_up_prep",
    )(w_gate_up)


def _split_gate_up(gu):
    half = _GU_BLOCK // 2
    n = gu.shape[1] // _GU_BLOCK
    glu = jnp.concatenate([gu[:, c * _GU_BLOCK:c * _GU_BLOCK + half] for c in range(n)], axis=1)
    lin = jnp.concatenate([gu[:, c * _GU_BLOCK + half:(c + 1) * _GU_BLOCK] for c in range(n)], axis=1)
    return glu, lin


def _experts_kernel(te_ref, nv_ref, x_ref, wgu_ref, bgu_ref, wd_ref, bd_ref, y_ref):
    @pl.when(pl.program_id(0) < nv_ref[0])
    def _():
        x = x_ref[...].astype(jnp.bfloat16)
        gu = jnp.dot(x, wgu_ref[0], preferred_element_type=jnp.float32) + bgu_ref[0]
        glu, lin = _split_gate_up(gu)
        glu = jnp.minimum(glu, SWIGLU_LIMIT)
        lin = jnp.clip(lin, -SWIGLU_LIMIT, SWIGLU_LIMIT)
        a = glu * jax.nn.sigmoid(SWIGLU_ALPHA * glu) * (lin + 1.0)
        y_ref[...] = jnp.dot(a.astype(jnp.bfloat16), wd_ref[0],
                             preferred_element_type=jnp.float32) + bd_ref[0]


def _experts(tile_expert, n_valid, x_sorted, wgu, bgu, wd, bd):
    n_rows = x_sorted.shape[0]
    tr = EXPERT_TILE
    n_tiles = n_rows // tr

    def row_map(j, te, nv):
        return (jnp.minimum(j, nv[0] - 1), 0)

    def exp_map(j, te, nv):
        return (te[j], 0, 0)

    grid_spec = pltpu.PrefetchScalarGridSpec(
        num_scalar_prefetch=2,
        grid=(n_tiles,),
        in_specs=[
            pl.BlockSpec((tr, D_MODEL), row_map),
            pl.BlockSpec((1, D_MODEL, 2 * D_FF), exp_map),
            pl.BlockSpec((1, 1, 2 * D_FF), exp_map),
            pl.BlockSpec((1, D_FF, D_MODEL), exp_map),
            pl.BlockSpec((1, 1, D_MODEL), exp_map),
        ],
        out_specs=pl.BlockSpec((tr, D_MODEL), row_map),
    )
    return pl.pallas_call(
        _experts_kernel,
        grid_spec=grid_spec,
        out_shape=jax.ShapeDtypeStruct((n_rows, D_MODEL), jnp.float32),
        compiler_params=_compiler_params(("arbitrary",), 56),
        name="experts",
    )(tile_expert, n_valid, x_sorted, wgu, bgu, wd, bd)


def _rope_tables(S, gain, scale):
    t = jnp.arange(S, dtype=jnp.int32)
    r = (t // GRID_W).astype(jnp.float32)
    c = (t % GRID_W).astype(jnp.float32)
    inv = jnp.float32(ROPE_THETA) ** (
        -jnp.arange(0, ROPE_AXIS_DIM, 2, dtype=jnp.float32) / ROPE_AXIS_DIM)
    ang = jnp.concatenate([r[None, :] * inv[:, None], c[None, :] * inv[:, None]], axis=0)
    cos = jnp.repeat(jnp.cos(ang), 2, axis=0)
    sin = jnp.repeat(jnp.sin(ang), 2, axis=0)
    sign = jnp.where(jnp.arange(HEAD_DIM) % 2 == 0, -1.0, 1.0).astype(jnp.float32)
    g = gain.astype(jnp.float32) * scale
    g_swapped = g.reshape(HEAD_DIM // 2, 2)[:, ::-1].reshape(HEAD_DIM)
    return g[:, None] * cos, (g_swapped * sign)[:, None] * sin


def _token_tiles(a, tile):
    T = a.shape[1]
    return a.reshape(TOP_K, T // tile, tile).transpose(1, 0, 2).reshape(T // tile, 1, TOP_K * tile)


def _prepare_weights(norm_mix_g, w_in, q_norm_g, k_norm_g, sgu_norm_g, w_spatial, b_spatial,
                     w_proj_attn, w_proj_sgu, w_out, norm_ffn_g, w_router, b_router,
                     w_gate_up, b_gate_up, w_down, b_down, norm_final_g):
    bf = jnp.bfloat16
    l = 0
    return dict(
        gmix=norm_mix_g[l][None, :],
        w_inT=w_in[l].T.astype(bf),
        q_gain=q_norm_g[l], k_gain=k_norm_g[l],
        gs=jnp.broadcast_to(sgu_norm_g[l][:, None], (SGU_W, _LANES)),
        wsT=jnp.swapaxes(w_spatial[l], 1, 2).astype(bf),
        bs=b_spatial[l][:, None, :],
        wpaT=w_proj_attn[l].T.astype(bf),
        wpbT=w_proj_sgu[l].T.astype(bf),
        wout=w_out[l].astype(bf),
        gffn=norm_ffn_g[l][None, :],
        wrT=w_router[l].T,
        br=b_router[l][:, None],
        wgu=_gate_up_prep(w_gate_up[l]),
        bgu=b_gate_up[l].reshape(N_EXPERTS, -1, _GU_BLOCK // 2, 2).transpose(0, 1, 3, 2)
        .reshape(N_EXPERTS, 1, 2 * D_FF),
        wd=w_down[l].astype(bf),
        bd=b_down[l][:, None, :],
        gfin=norm_final_g[None, :],
    )


def _trunk(x, w):
    B, S, _ = x.shape
    T = B * S
    qa, qb = _rope_tables(S, w["q_gain"], math.log2(math.e) / math.sqrt(HEAD_DIM))
    ka, kb = _rope_tables(S, w["k_gain"], 1.0)

    qT, k, vT, uT, vsT, gaT, gbT = _in_proj(x, w["gmix"], w["w_inT"], qa, qb, ka, kb, w["gs"])
    attnT = _attention(qT, k, vT)
    x1, h2, ids, wts, rank, counts = _post(
        attnT, uT, vsT, gaT, gbT, x, w["wsT"], w["bs"], w["wpaT"], w["wpbT"], w["wout"],
        w["gffn"], w["wrT"], w["br"])

    tr = EXPERT_TILE
    counts = counts[:, 0]
    padded = (counts + tr - 1) // tr * tr
    ends = jnp.cumsum(padded)
    starts = ends - padded
    n_tiles = (TOP_K * T) // tr + N_EXPERTS
    n_rows = n_tiles * tr
    dest = rank
    for e in range(N_EXPERTS):
        dest = dest + jnp.where(ids == e, starts[e], 0)
    tile_start = jnp.arange(n_tiles, dtype=jnp.int32) * tr
    tile_expert = jnp.minimum(
        jnp.sum((tile_start[:, None] >= ends[None, :]).astype(jnp.int32), axis=1), N_EXPERTS - 1)
    n_valid = (ends[-1] // tr).astype(jnp.int32)[None]

    x_sorted = _dispatch(h2.reshape(T, D_MODEL), _token_tiles(dest, DISPATCH_TILE), n_rows)
    y_sorted = _experts(tile_expert, n_valid, x_sorted, w["wgu"], w["bgu"], w["wd"], w["bd"])
    out = _combine(_token_tiles(dest, COMBINE_TILE), wts.T, x1.reshape(T, D_MODEL),
                   w["gfin"], y_sorted)
    return out.reshape(B, S, D_MODEL)


def kernel(x_prompt, x_sample, norm_mix_g, w_in, q_norm_g, k_norm_g, sgu_norm_g, w_spatial,
           b_spatial, w_proj_attn, w_proj_sgu, w_out, norm_ffn_g, w_router, b_router,
           w_gate_up, b_gate_up, w_down, b_down, norm_final_g):
    w = _prepare_weights(norm_mix_g, w_in, q_norm_g, k_norm_g, sgu_norm_g, w_spatial,
                         b_spatial, w_proj_attn, w_proj_sgu, w_out, norm_ffn_g, w_router,
                         b_router, w_gate_up, b_gate_up, w_down, b_down, norm_final_g)
    return (_trunk(x_prompt, w), _trunk(x_sample, w))
```

```python
import functools
import math

import jax
import jax.numpy as jnp
from jax import lax
from jax.experimental import pallas as pl
from jax.experimental.pallas import tpu as pltpu

D_MODEL = 1024
GRID_W = 64
N_HEADS = 8
N_KV_HEADS = 2
HEAD_DIM = 64
GQA_GROUP = N_HEADS // N_KV_HEADS
Q_W = N_HEADS * HEAD_DIM
KV_W = N_KV_HEADS * HEAD_DIM
ROPE_AXIS_DIM = HEAD_DIM // 2
ROPE_THETA = 10000.0
SGU_GROUPS = 8
SGU_W = D_MODEL // 2
SGU_GROUP_DIM = SGU_W // SGU_GROUPS
CHUNK = 128
IN_W = Q_W + 2 * KV_W + 2 * SGU_W + 2 * D_MODEL
N_EXPERTS = 32
TOP_K = 4
D_FF = D_MODEL
SWIGLU_LIMIT = 7.0
SWIGLU_ALPHA = 1.702
EPS = 1e-6

_Q0, _K0, _V0 = 0, Q_W, Q_W + KV_W
_U0 = Q_W + 2 * KV_W
_VS0 = _U0 + SGU_W
_GA0 = _VS0 + SGU_W
_GB0 = _GA0 + D_MODEL

TOKEN_TILE = 512
Q_TILE = 256
KEY_TILE = TOKEN_TILE
EXPERT_TILE = 512
DISPATCH_TILE = 256
COMBINE_TILE = 256

_LANES = 128
_BF16_SUBLANES = 16
_SUM_ROWS = _BF16_SUBLANES
_GU_BLOCK = 256
_NEG_BIG = -1e30
_MAX_LAGGED_EXPONENT = 80.0
_KEY_TILES_PER_TRIP = 4
_MIB = 1024 * 1024

_NT_DIMS = (((1,), (1,)), ((), ()))
_TN_DIMS = (((0,), (0,)), ((), ()))


def _compiler_params(semantics, vmem_mib):
    return pltpu.CompilerParams(
        dimension_semantics=semantics, vmem_limit_bytes=vmem_mib * _MIB)


def _swap_adjacent_rows(x):
    n = x.shape[0]
    row = lax.broadcasted_iota(jnp.int32, x.shape, 0)
    nxt = pltpu.roll(x, n - 1, 0)
    prv = pltpu.roll(x, 1, 0)
    return jnp.where((row & 1) == 0, nxt, prv)


def _head_norm_rope(z, tab_a, tab_b, n_heads):
    tm = z.shape[1]
    z3 = z.reshape(n_heads, HEAD_DIM, tm)
    ms = jnp.mean(z3 * z3, axis=1, keepdims=True)
    r = lax.rsqrt(ms + EPS)
    zs = _swap_adjacent_rows(z).reshape(n_heads, HEAD_DIM, tm)
    out = (z3 * tab_a[None] + zs * tab_b[None]) * r
    return out.reshape(n_heads * HEAD_DIM, tm)


def _gelu(x):
    return 0.5 * x * (1.0 + lax.erf(x * (1.0 / math.sqrt(2.0))))


def _tile_lanes(x, reps):
    return jnp.concatenate([x] * reps, axis=1) if reps > 1 else x


def _in_proj_kernel(x_ref, gmix_ref, w_ref, qa_ref, qb_ref, ka_ref, kb_ref, gs_ref,
                    qT_ref, k_ref, vT_ref, uT_ref, vsT_ref, gaT_ref, gbT_ref):
    tm = x_ref.shape[1]
    x = x_ref[0]
    ms = jnp.mean(x * x, axis=-1, keepdims=True)
    h = (x * lax.rsqrt(ms + EPS) * gmix_ref[...]).astype(jnp.bfloat16)

    def proj(r0, rows):
        return lax.dot_general(w_ref[r0:r0 + rows, :], h, _NT_DIMS,
                               preferred_element_type=jnp.float32)

    zq = proj(_Q0, Q_W)
    qT_ref[0] = _head_norm_rope(zq, qa_ref[...], qb_ref[...], N_HEADS).astype(jnp.bfloat16)

    zkv = proj(_K0, 2 * KV_W)
    kT = _head_norm_rope(zkv[:KV_W], ka_ref[...], kb_ref[...], N_KV_HEADS)
    k_ref[0] = kT.T.astype(jnp.bfloat16)
    vT_ref[0, 0] = zkv[KV_W:].astype(jnp.bfloat16)

    uT_ref[0] = _gelu(proj(_U0, SGU_W)).astype(jnp.bfloat16)

    vs = _gelu(proj(_VS0, SGU_W))
    vms = jnp.mean(vs * vs, axis=0, keepdims=True)
    gs = _tile_lanes(gs_ref[...], tm // _LANES)
    vsT_ref[0] = (vs * lax.rsqrt(vms + EPS) * gs).astype(jnp.bfloat16)

    gaT_ref[0] = jax.nn.sigmoid(proj(_GA0, D_MODEL)).astype(jnp.bfloat16)
    gbT_ref[0] = jax.nn.sigmoid(proj(_GB0, D_MODEL)).astype(jnp.bfloat16)


def _in_proj(x, gmix, w_inT, qa, qb, ka, kb, gs):
    B, S, _ = x.shape
    tm = TOKEN_TILE
    nt = S // tm
    bf = jnp.bfloat16
    const2 = lambda b, i: (0, 0)
    tab = pl.BlockSpec((HEAD_DIM, tm), lambda b, i: (0, i))
    fm = lambda rows: pl.BlockSpec((1, rows, tm), lambda b, i: (b, 0, i))
    return pl.pallas_call(
        _in_proj_kernel,
        grid=(B, nt),
        in_specs=[
            pl.BlockSpec((1, tm, D_MODEL), lambda b, i: (b, i, 0)),
            pl.BlockSpec((1, D_MODEL), const2),
            pl.BlockSpec((IN_W, D_MODEL), const2),
            tab, tab, tab, tab,
            pl.BlockSpec((SGU_W, _LANES), const2),
        ],
        out_specs=[
            fm(Q_W),
            pl.BlockSpec((1, tm, KV_W), lambda b, i: (b, i, 0)),
            pl.BlockSpec((1, 1, KV_W, tm), lambda b, i: (b, i, 0, 0)),
            fm(SGU_W), fm(SGU_W), fm(D_MODEL), fm(D_MODEL),
        ],
        out_shape=[
            jax.ShapeDtypeStruct((B, Q_W, S), bf),
            jax.ShapeDtypeStruct((B, S, KV_W), bf),
            jax.ShapeDtypeStruct((B, nt, KV_W, tm), bf),
            jax.ShapeDtypeStruct((B, SGU_W, S), bf),
            jax.ShapeDtypeStruct((B, SGU_W, S), bf),
            jax.ShapeDtypeStruct((B, D_MODEL, S), bf),
            jax.ShapeDtypeStruct((B, D_MODEL, S), bf),
        ],
        compiler_params=_compiler_params(("arbitrary", "arbitrary"), 56),
        name="in_proj",
    )(x, gmix, w_inT, qa, qb, ka, kb, gs)


def _attention_kernel(qT_ref, k_ref, vT_ref, o_ref, m_ref, gap_ref, acc_ref):
    tq = qT_ref.shape[2]
    n_kt = vT_ref.shape[1]
    tk = vT_ref.shape[3]
    nq = GQA_GROUP * tq
    ones_rows = (lax.broadcasted_iota(jnp.int32, (_SUM_ROWS, tk), 0) == 0).astype(jnp.bfloat16)

    zeros = jnp.zeros((HEAD_DIM, nq), jnp.bfloat16)
    qp = []
    for kv in range(N_KV_HEADS):
        heads = [qT_ref[0, (kv * GQA_GROUP + g) * HEAD_DIM:(kv * GQA_GROUP + g + 1) * HEAD_DIM, :]
                 for g in range(GQA_GROUP)]
        qk = jnp.concatenate(heads, axis=1)
        qp.append(jnp.concatenate([qk, zeros] if kv == 0 else [zeros, qk], axis=0))

    def key_tile(ki):
        return k_ref[0, pl.ds(pl.multiple_of(ki * tk, tk), tk), :]

    def value_rows(ki, kv):
        vt = vT_ref[0, ki]
        return jnp.concatenate([vt[kv * HEAD_DIM:(kv + 1) * HEAD_DIM, :], ones_rows], axis=0)

    acc_ref[...] = jnp.zeros(acc_ref.shape, jnp.float32)
    gap_ref[...] = jnp.zeros(gap_ref.shape, jnp.float32)
    first_keys = k_ref[0, 0:_BF16_SUBLANES, :]
    for kv in range(N_KV_HEADS):
        s0 = jnp.dot(first_keys, qp[kv], preferred_element_type=jnp.float32)
        m_ref[kv] = jnp.max(s0, axis=0, keepdims=True)

    def fast_tile(ki):
        kt = key_tile(ki)
        for kv in range(N_KV_HEADS):
            s = jnp.dot(kt, qp[kv], preferred_element_type=jnp.float32)
            m_old = m_ref[kv]
            p = jnp.exp2(s - m_old).astype(jnp.bfloat16)
            mt = jnp.max(s, axis=0, keepdims=True)
            pv = jnp.dot(value_rows(ki, kv), p, preferred_element_type=jnp.float32)
            m_new = jnp.maximum(m_old, mt)
            acc_ref[kv] = (acc_ref[kv] + pv) * jnp.exp2(m_old - m_new)
            m_ref[kv] = m_new
            gap_ref[kv] = jnp.maximum(gap_ref[kv], mt - m_old)

    per_trip = math.gcd(n_kt, _KEY_TILES_PER_TRIP)

    def fast_group(j, carry):
        for t in range(per_trip):
            fast_tile(per_trip * j + t)
        return carry

    lax.fori_loop(0, n_kt // per_trip, fast_group, 0)

    @pl.when(jnp.max(gap_ref[...]) > _MAX_LAGGED_EXPONENT)
    def _():
        m_ref[...] = jnp.full(m_ref.shape, _NEG_BIG, jnp.float32)
        acc_ref[...] = jnp.zeros(acc_ref.shape, jnp.float32)

        def robust_tile(ki, carry):
            kt = key_tile(ki)
            for kv in range(N_KV_HEADS):
                s = jnp.dot(kt, qp[kv], preferred_element_type=jnp.float32)
                m_old = m_ref[kv]
                m_new = jnp.maximum(m_old, jnp.max(s, axis=0, keepdims=True))
                p = jnp.exp2(s - m_new).astype(jnp.bfloat16)
                pv = jnp.dot(value_rows(ki, kv), p, preferred_element_type=jnp.float32)
                acc_ref[kv] = jnp.exp2(m_old - m_new) * acc_ref[kv] + pv
                m_ref[kv] = m_new
            return carry

        lax.fori_loop(0, n_kt, robust_tile, 0)

    for kv in range(N_KV_HEADS):
        acc = acc_ref[kv]
        o = acc[:HEAD_DIM] / acc[HEAD_DIM:HEAD_DIM + 1]
        for g in range(GQA_GROUP):
            r0 = (kv * GQA_GROUP + g) * HEAD_DIM
            o_ref[0, r0:r0 + HEAD_DIM, :] = o[:, g * tq:(g + 1) * tq].astype(jnp.bfloat16)


def _attention(qT, k, vT):
    B, _, S = qT.shape
    tq = Q_TILE
    nq = GQA_GROUP * tq
    n_kt, tk = vT.shape[1], vT.shape[3]
    return pl.pallas_call(
        _attention_kernel,
        grid=(B, S // tq),
        in_specs=[
            pl.BlockSpec((1, Q_W, tq), lambda b, i: (b, 0, i)),
            pl.BlockSpec((1, S, KV_W), lambda b, i: (b, 0, 0)),
            pl.BlockSpec((1, n_kt, KV_W, tk), lambda b, i: (b, 0, 0, 0)),
        ],
        out_specs=pl.BlockSpec((1, Q_W, tq), lambda b, i: (b, 0, i)),
        out_shape=jax.ShapeDtypeStruct((B, Q_W, S), jnp.bfloat16),
        scratch_shapes=[
            pltpu.VMEM((N_KV_HEADS, 1, nq), jnp.float32),
            pltpu.VMEM((N_KV_HEADS, 1, nq), jnp.float32),
            pltpu.VMEM((N_KV_HEADS, HEAD_DIM + _SUM_ROWS, nq), jnp.float32),
        ],
        compiler_params=_compiler_params(("arbitrary", "arbitrary"), 56),
        name="attention",
    )(qT, k, vT)


def _post_kernel(attnT_ref, uT_ref, vsT_ref, gaT_ref, gbT_ref, x_ref,
                 wsT_ref, bs_ref, wpaT_ref, wpbT_ref, wout_ref, gffn_ref, wrT_ref, br_ref,
                 x1_ref, h2_ref, ids_ref, wts_ref, rank_ref, cnt_ref, carry_ref):
    tm = x_ref.shape[1]
    n_chunks = tm // CHUNK
    first = jnp.logical_and(pl.program_id(0) == 0, pl.program_id(1) == 0)

    @pl.when(first)
    def _():
        carry_ref[...] = jnp.zeros(carry_ref.shape, jnp.float32)

    gate_rows = []
    for g in range(SGU_GROUPS):
        r0 = g * SGU_GROUP_DIM
        vs_g = vsT_ref[0, r0:r0 + SGU_GROUP_DIM, :]
        lhs = jnp.concatenate(
            [vs_g[:, c * CHUNK:(c + 1) * CHUNK] for c in range(n_chunks)], axis=0)
        mixed = jnp.dot(lhs, wsT_ref[g], preferred_element_type=jnp.float32)
        mixed = mixed + bs_ref[g]
        mixedT = jnp.concatenate(
            [mixed[c * SGU_GROUP_DIM:(c + 1) * SGU_GROUP_DIM] for c in range(n_chunks)], axis=1)
        u_g = uT_ref[0, r0:r0 + SGU_GROUP_DIM, :].astype(jnp.float32)
        gate_rows.append((u_g * mixedT).astype(jnp.bfloat16))
    gateT = jnp.concatenate(gate_rows, axis=0)

    paT = jnp.dot(wpaT_ref[...], attnT_ref[0], preferred_element_type=jnp.float32)
    pbT = jnp.dot(wpbT_ref[...], gateT, preferred_element_type=jnp.float32)
    mT = (gaT_ref[0].astype(jnp.float32) * paT
          + gbT_ref[0].astype(jnp.float32) * pbT).astype(jnp.bfloat16)
    y = lax.dot_general(mT, wout_ref[...], _TN_DIMS, preferred_element_type=jnp.float32)
    x1 = x_ref[0] + y
    x1_ref[0] = x1

    ms = jnp.mean(x1 * x1, axis=-1, keepdims=True)
    h2 = x1 * lax.rsqrt(ms + EPS) * gffn_ref[...]
    h2_ref[0] = h2

    logits = lax.dot_general(wrT_ref[...], h2, _NT_DIMS, precision=lax.Precision.HIGHEST,
                             preferred_element_type=jnp.float32) + br_ref[...]
    eidx = lax.broadcasted_iota(jnp.int32, logits.shape, 0).astype(jnp.float32)
    work = logits
    vals, ids, sels = [], [], []
    for _ in range(TOP_K):
        mx = jnp.max(work, axis=0, keepdims=True)
        idx = jnp.min(jnp.where(work == mx, eidx, float(N_EXPERTS)), axis=0, keepdims=True)
        sel = eidx == idx
        vals.append(mx)
        ids.append(idx)
        sels.append(sel)
        work = jnp.where(sel, -jnp.inf, work)
    exps = [jnp.exp(v - vals[0]) for v in vals]
    denom = exps[0] + exps[1] + exps[2] + exps[3]
    ids_ref[...] = jnp.concatenate(ids, axis=0).astype(jnp.int32)
    wts_ref[...] = jnp.concatenate([e / denom for e in exps], axis=0)

    onehot = [s.astype(jnp.float32) for s in sels]
    hits = onehot[0] + onehot[1] + onehot[2] + onehot[3]
    ti = lax.broadcasted_iota(jnp.int32, (tm, tm), 0)
    tj = lax.broadcasted_iota(jnp.int32, (tm, tm), 1)
    upper = (ti < tj).astype(jnp.bfloat16)
    prefix = jnp.dot(hits.astype(jnp.bfloat16), upper, preferred_element_type=jnp.float32)
    base = prefix + carry_ref[...]
    ranks = [jnp.sum(oh * base, axis=0, keepdims=True) for oh in onehot]
    rank_ref[...] = jnp.concatenate(ranks, axis=0).astype(jnp.int32)
    carry_ref[...] = carry_ref[...] + jnp.sum(hits, axis=1, keepdims=True)
    cnt_ref[...] = carry_ref[...].astype(jnp.int32)


def _post(attnT, uT, vsT, gaT, gbT, x, wsT, bs, wpaT, wpbT, wout, gffn, wrT, br):
    B, S, _ = x.shape
    tm = TOKEN_TILE
    nt = S // tm
    T = B * S
    fm = lambda rows: pl.BlockSpec((1, rows, tm), lambda b, i: (b, 0, i))
    rowm = pl.BlockSpec((1, tm, D_MODEL), lambda b, i: (b, i, 0))
    c2 = lambda b, i: (0, 0)
    c3 = lambda b, i: (0, 0, 0)
    tokT = pl.BlockSpec((TOP_K, tm), lambda b, i: (0, b * nt + i))
    return pl.pallas_call(
        _post_kernel,
        grid=(B, nt),
        in_specs=[
            fm(Q_W), fm(SGU_W), fm(SGU_W), fm(D_MODEL), fm(D_MODEL), rowm,
            pl.BlockSpec((SGU_GROUPS, CHUNK, CHUNK), c3),
            pl.BlockSpec((SGU_GROUPS, 1, CHUNK), c3),
            pl.BlockSpec((D_MODEL, Q_W), c2),
            pl.BlockSpec((D_MODEL, SGU_W), c2),
            pl.BlockSpec((D_MODEL, D_MODEL), c2),
            pl.BlockSpec((1, D_MODEL), c2),
            pl.BlockSpec((N_EXPERTS, D_MODEL), c2),
            pl.BlockSpec((N_EXPERTS, 1), c2),
        ],
        out_specs=[rowm, rowm, tokT, tokT, tokT, pl.BlockSpec((N_EXPERTS, 1), c2)],
        out_shape=[
            jax.ShapeDtypeStruct((B, S, D_MODEL), jnp.float32),
            jax.ShapeDtypeStruct((B, S, D_MODEL), jnp.float32),
            jax.ShapeDtypeStruct((TOP_K, T), jnp.int32),
            jax.ShapeDtypeStruct((TOP_K, T), jnp.float32),
            jax.ShapeDtypeStruct((TOP_K, T), jnp.int32),
            jax.ShapeDtypeStruct((N_EXPERTS, 1), jnp.int32),
        ],
        scratch_shapes=[pltpu.VMEM((N_EXPERTS, 1), jnp.float32)],
        compiler_params=_compiler_params(("arbitrary", "arbitrary"), 56),
        name="post",
    )(attnT, uT, vsT, gaT, gbT, x, wsT, bs, wpaT, wpbT, wout, gffn, wrT, br)


def _row_copy(src, src_row, dst, dst_row, sem):
    return pltpu.make_async_copy(src.at[pl.ds(src_row, 1)], dst.at[pl.ds(dst_row, 1)], sem)


def _dispatch_kernel(dest_ref, h_ref, out_ref, sem):
    tt = h_ref.shape[0]

    def issue(r, carry):
        for k in range(TOP_K):
            _row_copy(h_ref, r, out_ref, dest_ref[0, 0, k * tt + r], sem).start(priority=k % 2)
        return carry

    lax.fori_loop(0, tt, issue, 0)
    for _ in range(TOP_K):
        pltpu.make_async_copy(h_ref, out_ref.at[pl.ds(0, tt)], sem).wait()


def _dispatch(h2, dest_tiles, n_rows):
    T = h2.shape[0]
    tt = DISPATCH_TILE
    return pl.pallas_call(
        _dispatch_kernel,
        grid=(T // tt,),
        in_specs=[
            pl.BlockSpec((1, 1, TOP_K * tt), lambda i: (i, 0, 0), memory_space=pltpu.SMEM),
            pl.BlockSpec((tt, D_MODEL), lambda i: (i, 0)),
        ],
        out_specs=pl.BlockSpec(memory_space=pl.ANY),
        out_shape=jax.ShapeDtypeStruct((n_rows, D_MODEL), jnp.float32),
        scratch_shapes=[pltpu.SemaphoreType.DMA],
        compiler_params=_compiler_params(("arbitrary",), 32),
        name="dispatch",
    )(dest_tiles, h2)


def _combine_kernel(dest_ref, w_ref, x1_ref, gfin_ref, y_ref, o_ref, ybuf, sem):
    tt = x1_ref.shape[0]

    def issue(r, carry):
        for k in range(TOP_K):
            _row_copy(y_ref, dest_ref[0, 0, k * tt + r], ybuf.at[k], r, sem).start(priority=k % 2)
        return carry

    lax.fori_loop(0, tt, issue, 0)
    for k in range(TOP_K):
        pltpu.make_async_copy(y_ref.at[pl.ds(0, tt)], ybuf.at[k], sem).wait()

    w = w_ref[...]
    moe = (w[:, 0:1] * ybuf[0] + w[:, 1:2] * ybuf[1]
           + w[:, 2:3] * ybuf[2] + w[:, 3:4] * ybuf[3])
    x2 = x1_ref[...] + moe
    ms = jnp.mean(x2 * x2, axis=-1, keepdims=True)
    o_ref[...] = x2 * lax.rsqrt(ms + EPS) * gfin_ref[...]


def _combine(dest_tiles, wts, x1, gfin, y_sorted):
    T = x1.shape[0]
    tt = COMBINE_TILE
    return pl.pallas_call(
        _combine_kernel,
        grid=(T // tt,),
        in_specs=[
            pl.BlockSpec((1, 1, TOP_K * tt), lambda i: (i, 0, 0), memory_space=pltpu.SMEM),
            pl.BlockSpec((tt, TOP_K), lambda i: (i, 0)),
            pl.BlockSpec((tt, D_MODEL), lambda i: (i, 0)),
            pl.BlockSpec((1, D_MODEL), lambda i: (0, 0)),
            pl.BlockSpec(memory_space=pl.ANY),
        ],
        out_specs=pl.BlockSpec((tt, D_MODEL), lambda i: (i, 0)),
        out_shape=jax.ShapeDtypeStruct((T, D_MODEL), jnp.float32),
        scratch_shapes=[pltpu.VMEM((TOP_K, tt, D_MODEL), jnp.float32),
                        pltpu.SemaphoreType.DMA],
        compiler_params=_compiler_params(("arbitrary",), 32),
        name="combine",
    )(dest_tiles, wts, x1, gfin, y_sorted)


def _gate_up_prep_kernel(w_ref, o_ref):
    half = _GU_BLOCK // 2
    i = lax.broadcasted_iota(jnp.int32, (_GU_BLOCK, _GU_BLOCK), 0)
    j = lax.broadcasted_iota(jnp.int32, (_GU_BLOCK, _GU_BLOCK), 1)
    perm = (i == jnp.where(j < half, 2 * j, 2 * (j - half) + 1)).astype(jnp.bfloat16)
    for c in range(2 * D_FF // _GU_BLOCK):
        cols = slice(c * _GU_BLOCK, (c + 1) * _GU_BLOCK)
        blk = w_ref[0, :, cols].astype(jnp.bfloat16)
        o_ref[0, :, cols] = jnp.dot(blk, perm, preferred_element_type=jnp.float32).astype(jnp.bfloat16)


def _gate_up_prep(w_gate_up):
    spec = pl.BlockSpec((1, D_MODEL, 2 * D_FF), lambda e: (e, 0, 0))
    return pl.pallas_call(
        _gate_up_prep_kernel,
        grid=(N_EXPERTS,),
        in_specs=[spec],
        out_specs=spec,
        out_shape=jax.ShapeDtypeStruct(w_gate_up.shape, jnp.bfloat16),
        compiler_params=_compiler_params(("arbitrary",), 48),
        name="gate_up_prep",
    )(w_gate_up)


def _split_gate_up(gu):
    half = _GU_BLOCK // 2
    n = gu.shape[1] // _GU_BLOCK
    glu = jnp.concatenate([gu[:, c * _GU_BLOCK:c * _GU_BLOCK + half] for c in range(n)], axis=1)
    lin = jnp.concatenate([gu[:, c * _GU_BLOCK + half:(c + 1) * _GU_BLOCK] for c in range(n)], axis=1)
    return glu, lin


def _experts_kernel(te_ref, nv_ref, x_ref, wgu_ref, bgu_ref, wd_ref, bd_ref, y_ref):
    @pl.when(pl.program_id(0) < nv_ref[0])
    def _():
        x = x_ref[...].astype(jnp.bfloat16)
        gu = jnp.dot(x, wgu_ref[0], preferred_element_type=jnp.float32) + bgu_ref[0]
        glu, lin = _split_gate_up(gu)
        glu = jnp.minimum(glu, SWIGLU_LIMIT)
        lin = jnp.clip(lin, -SWIGLU_LIMIT, SWIGLU_LIMIT)
        a = glu * jax.nn.sigmoid(SWIGLU_ALPHA * glu) * (lin + 1.0)
        y_ref[...] = jnp.dot(a.astype(jnp.bfloat16), wd_ref[0],
                             preferred_element_type=jnp.float32) + bd_ref[0]


def _experts(tile_expert, n_valid, x_sorted, wgu, bgu, wd, bd):
    n_rows = x_sorted.shape[0]
    tr = EXPERT_TILE
    n_tiles = n_rows // tr

    def row_map(j, te, nv):
        return (jnp.minimum(j, nv[0] - 1), 0)

    def exp_map(j, te, nv):
        return (te[j], 0, 0)

    grid_spec = pltpu.PrefetchScalarGridSpec(
        num_scalar_prefetch=2,
        grid=(n_tiles,),
        in_specs=[
            pl.BlockSpec((tr, D_MODEL), row_map),
            pl.BlockSpec((1, D_MODEL, 2 * D_FF), exp_map),
            pl.BlockSpec((1, 1, 2 * D_FF), exp_map),
            pl.BlockSpec((1, D_FF, D_MODEL), exp_map),
            pl.BlockSpec((1, 1, D_MODEL), exp_map),
        ],
        out_specs=pl.BlockSpec((tr, D_MODEL), row_map),
    )
    return pl.pallas_call(
        _experts_kernel,
        grid_spec=grid_spec,
        out_shape=jax.ShapeDtypeStruct((n_rows, D_MODEL), jnp.float32),
        compiler_params=_compiler_params(("arbitrary",), 56),
        name="experts",
    )(tile_expert, n_valid, x_sorted, wgu, bgu, wd, bd)


def _rope_tables(S, gain, scale):
    t = jnp.arange(S, dtype=jnp.int32)
    r = (t // GRID_W).astype(jnp.float32)
    c = (t % GRID_W).astype(jnp.float32)
    inv = jnp.float32(ROPE_THETA) ** (
        -jnp.arange(0, ROPE_AXIS_DIM, 2, dtype=jnp.float32) / ROPE_AXIS_DIM)
    ang = jnp.concatenate([r[None, :] * inv[:, None], c[None, :] * inv[:, None]], axis=0)
    cos = jnp.repeat(jnp.cos(ang), 2, axis=0)
    sin = jnp.repeat(jnp.sin(ang), 2, axis=0)
    sign = jnp.where(jnp.arange(HEAD_DIM) % 2 == 0, -1.0, 1.0).astype(jnp.float32)
    g = gain.astype(jnp.float32) * scale
    g_swapped = g.reshape(HEAD_DIM // 2, 2)[:, ::-1].reshape(HEAD_DIM)
    return g[:, None] * cos, (g_swapped * sign)[:, None] * sin


def _token_tiles(a, tile):
    T = a.shape[1]
    return a.reshape(TOP_K, T // tile, tile).transpose(1, 0, 2).reshape(T // tile, 1, TOP_K * tile)


def _prepare_weights(norm_mix_g, w_in, q_norm_g, k_norm_g, sgu_norm_g, w_spatial, b_spatial,
                     w_proj_attn, w_proj_sgu, w_out, norm_ffn_g, w_router, b_router,
                     w_gate_up, b_gate_up, w_down, b_down, norm_final_g):
    bf = jnp.bfloat16
    l = 0
    return dict(
        gmix=norm_mix_g[l][None, :],
        w_inT=w_in[l].T.astype(bf),
        q_gain=q_norm_g[l], k_gain=k_norm_g[l],
        gs=jnp.broadcast_to(sgu_norm_g[l][:, None], (SGU_W, _LANES)),
        wsT=jnp.swapaxes(w_spatial[l], 1, 2).astype(bf),
        bs=b_spatial[l][:, None, :],
        wpaT=w_proj_attn[l].T.astype(bf),
        wpbT=w_proj_sgu[l].T.astype(bf),
        wout=w_out[l].astype(bf),
        gffn=norm_ffn_g[l][None, :],
        wrT=w_router[l].T,
        br=b_router[l][:, None],
        wgu=_gate_up_prep(w_gate_up[l]),
        bgu=b_gate_up[l].reshape(N_EXPERTS, -1, _GU_BLOCK // 2, 2).transpose(0, 1, 3, 2)
        .reshape(N_EXPERTS, 1, 2 * D_FF),
        wd=w_down[l].astype(bf),
        bd=b_down[l][:, None, :],
        gfin=norm_final_g[None, :],
    )


def _trunk(x, w):
    B, S, _ = x.shape
    T = B * S
    qa, qb = _rope_tables(S, w["q_gain"], math.log2(math.e) / math.sqrt(HEAD_DIM))
    ka, kb = _rope_tables(S, w["k_gain"], 1.0)

    qT, k, vT, uT, vsT, gaT, gbT = _in_proj(x, w["gmix"], w["w_inT"], qa, qb, ka, kb, w["gs"])
    attnT = _attention(qT, k, vT)
    x1, h2, ids, wts, rank, counts = _post(
        attnT, uT, vsT, gaT, gbT, x, w["wsT"], w["bs"], w["wpaT"], w["wpbT"], w["wout"],
        w["gffn"], w["wrT"], w["br"])

    tr = EXPERT_TILE
    counts = counts[:, 0]
    padded = (counts + tr - 1) // tr * tr
    ends = jnp.cumsum(padded)
    starts = ends - padded
    n_tiles = (TOP_K * T) // tr + N_EXPERTS
    n_rows = n_tiles * tr
    dest = rank
    for e in range(N_EXPERTS):
        dest = dest + jnp.where(ids == e, starts[e], 0)
    tile_start = jnp.arange(n_tiles, dtype=jnp.int32) * tr
    tile_expert = jnp.minimum(
        jnp.sum((tile_start[:, None] >= ends[None, :]).astype(jnp.int32), axis=1), N_EXPERTS - 1)
    n_valid = (ends[-1] // tr).astype(jnp.int32)[None]

    x_sorted = _dispatch(h2.reshape(T, D_MODEL), _token_tiles(dest, DISPATCH_TILE), n_rows)
    y_sorted = _experts(tile_expert, n_valid, x_sorted, w["wgu"], w["bgu"], w["wd"], w["bd"])
    out = _combine(_token_tiles(dest, COMBINE_TILE), wts.T, x1.reshape(T, D_MODEL),
                   w["gfin"], y_sorted)
    return out.reshape(B, S, D_MODEL)


def kernel(x_prompt, x_sample, norm_mix_g, w_in, q_norm_g, k_norm_g, sgu_norm_g, w_spatial,
           b_spatial, w_proj_attn, w_proj_sgu, w_out, norm_ffn_g, w_router, b_router,
           w_gate_up, b_gate_up, w_down, b_down, norm_final_g):
    w = _prepare_weights(norm_mix_g, w_in, q_norm_g, k_norm_g, sgu_norm_g, w_spatial,
                         b_spatial, w_proj_attn, w_proj_sgu, w_out, norm_ffn_g, w_router,
                         b_router, w_gate_up, b_gate_up, w_down, b_down, norm_final_g)
    return (_trunk(x_prompt, w), _trunk(x_sample, w))
```

```python
import functools
import math

import jax
import jax.numpy as jnp
from jax import lax
from jax.experimental import pallas as pl
from jax.experimental.pallas import tpu as pltpu
from jax.experimental.pallas import tpu_sc as plsc

D_MODEL = 1024
GRID_W = 64
N_HEADS = 8
N_KV_HEADS = 2
HEAD_DIM = 64
GQA_GROUP = N_HEADS // N_KV_HEADS
Q_W = N_HEADS * HEAD_DIM
KV_W = N_KV_HEADS * HEAD_DIM
ROPE_AXIS_DIM = HEAD_DIM // 2
ROPE_THETA = 10000.0
SGU_GROUPS = 8
SGU_W = D_MODEL // 2
SGU_GROUP_DIM = SGU_W // SGU_GROUPS
CHUNK = 128
IN_W = Q_W + 2 * KV_W + 2 * SGU_W + 2 * D_MODEL
N_EXPERTS = 32
TOP_K = 4
D_FF = D_MODEL
SWIGLU_LIMIT = 7.0
SWIGLU_ALPHA = 1.702
EPS = 1e-6

_Q0, _K0, _V0 = 0, Q_W, Q_W + KV_W
_U0 = Q_W + 2 * KV_W
_VS0 = _U0 + SGU_W
_GA0 = _VS0 + SGU_W
_GB0 = _GA0 + D_MODEL

TOKEN_TILE = 512
Q_TILE = 256
KEY_TILE = TOKEN_TILE
EXPERT_TILE = 512
DISPATCH_TILE = 256
COMBINE_TILE = 256
SC_GATHER_WINDOW = 32

_LANES = 128
_BF16_SUBLANES = 16
_SUM_ROWS = _BF16_SUBLANES
_GU_BLOCK = 256
_NEG_BIG = -1e30
_MAX_LAGGED_EXPONENT = 80.0
_KEY_TILES_PER_TRIP = 4
_MIB = 1024 * 1024

_NT_DIMS = (((1,), (1,)), ((), ()))
_TN_DIMS = (((0,), (0,)), ((), ()))


def _compiler_params(semantics, vmem_mib):
    return pltpu.CompilerParams(
        dimension_semantics=semantics, vmem_limit_bytes=vmem_mib * _MIB)


def _swap_adjacent_rows(x):
    n = x.shape[0]
    row = lax.broadcasted_iota(jnp.int32, x.shape, 0)
    nxt = pltpu.roll(x, n - 1, 0)
    prv = pltpu.roll(x, 1, 0)
    return jnp.where((row & 1) == 0, nxt, prv)


def _head_norm_rope(z, tab_a, tab_b, n_heads):
    tm = z.shape[1]
    z3 = z.reshape(n_heads, HEAD_DIM, tm)
    ms = jnp.mean(z3 * z3, axis=1, keepdims=True)
    r = lax.rsqrt(ms + EPS)
    zs = _swap_adjacent_rows(z).reshape(n_heads, HEAD_DIM, tm)
    out = (z3 * tab_a[None] + zs * tab_b[None]) * r
    return out.reshape(n_heads * HEAD_DIM, tm)


def _gelu(x):
    return 0.5 * x * (1.0 + lax.erf(x * (1.0 / math.sqrt(2.0))))


def _tile_lanes(x, reps):
    return jnp.concatenate([x] * reps, axis=1) if reps > 1 else x


def _in_proj_kernel(x_ref, gmix_ref, w_ref, qa_ref, qb_ref, ka_ref, kb_ref, gs_ref,
                    qT_ref, k_ref, vT_ref, uT_ref, vsT_ref, gaT_ref, gbT_ref):
    tm = x_ref.shape[1]
    x = x_ref[0]
    ms = jnp.mean(x * x, axis=-1, keepdims=True)
    h = (x * lax.rsqrt(ms + EPS) * gmix_ref[...]).astype(jnp.bfloat16)

    def proj(r0, rows):
        return lax.dot_general(w_ref[r0:r0 + rows, :], h, _NT_DIMS,
                               preferred_element_type=jnp.float32)

    zq = proj(_Q0, Q_W)
    qT_ref[0] = _head_norm_rope(zq, qa_ref[...], qb_ref[...], N_HEADS).astype(jnp.bfloat16)

    zkv = proj(_K0, 2 * KV_W)
    kT = _head_norm_rope(zkv[:KV_W], ka_ref[...], kb_ref[...], N_KV_HEADS)
    k_ref[0] = kT.T.astype(jnp.bfloat16)
    vT_ref[0, 0] = zkv[KV_W:].astype(jnp.bfloat16)

    uT_ref[0] = _gelu(proj(_U0, SGU_W)).astype(jnp.bfloat16)

    vs = _gelu(proj(_VS0, SGU_W))
    vms = jnp.mean(vs * vs, axis=0, keepdims=True)
    gs = _tile_lanes(gs_ref[...], tm // _LANES)
    vsT_ref[0] = (vs * lax.rsqrt(vms + EPS) * gs).astype(jnp.bfloat16)

    gaT_ref[0] = jax.nn.sigmoid(proj(_GA0, D_MODEL)).astype(jnp.bfloat16)
    gbT_ref[0] = jax.nn.sigmoid(proj(_GB0, D_MODEL)).astype(jnp.bfloat16)


def _in_proj(x, gmix, w_inT, qa, qb, ka, kb, gs):
    B, S, _ = x.shape
    tm = TOKEN_TILE
    nt = S // tm
    bf = jnp.bfloat16
    const2 = lambda b, i: (0, 0)
    tab = pl.BlockSpec((HEAD_DIM, tm), lambda b, i: (0, i))
    fm = lambda rows: pl.BlockSpec((1, rows, tm), lambda b, i: (b, 0, i))
    return pl.pallas_call(
        _in_proj_kernel,
        grid=(B, nt),
        in_specs=[
            pl.BlockSpec((1, tm, D_MODEL), lambda b, i: (b, i, 0)),
            pl.BlockSpec((1, D_MODEL), const2),
            pl.BlockSpec((IN_W, D_MODEL), const2),
            tab, tab, tab, tab,
            pl.BlockSpec((SGU_W, _LANES), const2),
        ],
        out_specs=[
            fm(Q_W),
            pl.BlockSpec((1, tm, KV_W), lambda b, i: (b, i, 0)),
            pl.BlockSpec((1, 1, KV_W, tm), lambda b, i: (b, i, 0, 0)),
            fm(SGU_W), fm(SGU_W), fm(D_MODEL), fm(D_MODEL),
        ],
        out_shape=[
            jax.ShapeDtypeStruct((B, Q_W, S), bf),
            jax.ShapeDtypeStruct((B, S, KV_W), bf),
            jax.ShapeDtypeStruct((B, nt, KV_W, tm), bf),
            jax.ShapeDtypeStruct((B, SGU_W, S), bf),
            jax.ShapeDtypeStruct((B, SGU_W, S), bf),
            jax.ShapeDtypeStruct((B, D_MODEL, S), bf),
            jax.ShapeDtypeStruct((B, D_MODEL, S), bf),
        ],
        compiler_params=_compiler_params(("arbitrary", "arbitrary"), 56),
        name="in_proj",
    )(x, gmix, w_inT, qa, qb, ka, kb, gs)


def _attention_kernel(qT_ref, k_ref, vT_ref, o_ref, m_ref, gap_ref, acc_ref):
    tq = qT_ref.shape[2]
    n_kt = vT_ref.shape[1]
    tk = vT_ref.shape[3]
    nq = GQA_GROUP * tq
    ones_rows = (lax.broadcasted_iota(jnp.int32, (_SUM_ROWS, tk), 0) == 0).astype(jnp.bfloat16)

    zeros = jnp.zeros((HEAD_DIM, nq), jnp.bfloat16)
    qp = []
    for kv in range(N_KV_HEADS):
        heads = [qT_ref[0, (kv * GQA_GROUP + g) * HEAD_DIM:(kv * GQA_GROUP + g + 1) * HEAD_DIM, :]
                 for g in range(GQA_GROUP)]
        qk = jnp.concatenate(heads, axis=1)
        qp.append(jnp.concatenate([qk, zeros] if kv == 0 else [zeros, qk], axis=0))

    def key_tile(ki):
        return k_ref[0, pl.ds(pl.multiple_of(ki * tk, tk), tk), :]

    def value_rows(ki, kv):
        vt = vT_ref[0, ki]
        return jnp.concatenate([vt[kv * HEAD_DIM:(kv + 1) * HEAD_DIM, :], ones_rows], axis=0)

    acc_ref[...] = jnp.zeros(acc_ref.shape, jnp.float32)
    gap_ref[...] = jnp.zeros(gap_ref.shape, jnp.float32)
    first_keys = k_ref[0, 0:_BF16_SUBLANES, :]
    for kv in range(N_KV_HEADS):
        s0 = jnp.dot(first_keys, qp[kv], preferred_element_type=jnp.float32)
        m_ref[kv] = jnp.max(s0, axis=0, keepdims=True)

    def fast_tile(ki):
        kt = key_tile(ki)
        for kv in range(N_KV_HEADS):
            s = jnp.dot(kt, qp[kv], preferred_element_type=jnp.float32)
            m_old = m_ref[kv]
            p = jnp.exp2(s - m_old).astype(jnp.bfloat16)
            mt = jnp.max(s, axis=0, keepdims=True)
            pv = jnp.dot(value_rows(ki, kv), p, preferred_element_type=jnp.float32)
            m_new = jnp.maximum(m_old, mt)
            acc_ref[kv] = (acc_ref[kv] + pv) * jnp.exp2(m_old - m_new)
            m_ref[kv] = m_new
            gap_ref[kv] = jnp.maximum(gap_ref[kv], mt - m_old)

    per_trip = math.gcd(n_kt, _KEY_TILES_PER_TRIP)

    def fast_group(j, carry):
        for t in range(per_trip):
            fast_tile(per_trip * j + t)
        return carry

    lax.fori_loop(0, n_kt // per_trip, fast_group, 0)

    @pl.when(jnp.max(gap_ref[...]) > _MAX_LAGGED_EXPONENT)
    def _():
        m_ref[...] = jnp.full(m_ref.shape, _NEG_BIG, jnp.float32)
        acc_ref[...] = jnp.zeros(acc_ref.shape, jnp.float32)

        def robust_tile(ki, carry):
            kt = key_tile(ki)
            for kv in range(N_KV_HEADS):
                s = jnp.dot(kt, qp[kv], preferred_element_type=jnp.float32)
                m_old = m_ref[kv]
                m_new = jnp.maximum(m_old, jnp.max(s, axis=0, keepdims=True))
                p = jnp.exp2(s - m_new).astype(jnp.bfloat16)
                pv = jnp.dot(value_rows(ki, kv), p, preferred_element_type=jnp.float32)
                acc_ref[kv] = jnp.exp2(m_old - m_new) * acc_ref[kv] + pv
                m_ref[kv] = m_new
            return carry

        lax.fori_loop(0, n_kt, robust_tile, 0)

    for kv in range(N_KV_HEADS):
        acc = acc_ref[kv]
        o = acc[:HEAD_DIM] / acc[HEAD_DIM:HEAD_DIM + 1]
        for g in range(GQA_GROUP):
            r0 = (kv * GQA_GROUP + g) * HEAD_DIM
            o_ref[0, r0:r0 + HEAD_DIM, :] = o[:, g * tq:(g + 1) * tq].astype(jnp.bfloat16)


def _attention(qT, k, vT):
    B, _, S = qT.shape
    tq = Q_TILE
    nq = GQA_GROUP * tq
    n_kt, tk = vT.shape[1], vT.shape[3]
    return pl.pallas_call(
        _attention_kernel,
        grid=(B, S // tq),
        in_specs=[
            pl.BlockSpec((1, Q_W, tq), lambda b, i: (b, 0, i)),
            pl.BlockSpec((1, S, KV_W), lambda b, i: (b, 0, 0)),
            pl.BlockSpec((1, n_kt, KV_W, tk), lambda b, i: (b, 0, 0, 0)),
        ],
        out_specs=pl.BlockSpec((1, Q_W, tq), lambda b, i: (b, 0, i)),
        out_shape=jax.ShapeDtypeStruct((B, Q_W, S), jnp.bfloat16),
        scratch_shapes=[
            pltpu.VMEM((N_KV_HEADS, 1, nq), jnp.float32),
            pltpu.VMEM((N_KV_HEADS, 1, nq), jnp.float32),
            pltpu.VMEM((N_KV_HEADS, HEAD_DIM + _SUM_ROWS, nq), jnp.float32),
        ],
        compiler_params=_compiler_params(("arbitrary", "arbitrary"), 56),
        name="attention",
    )(qT, k, vT)


def _post_kernel(attnT_ref, uT_ref, vsT_ref, gaT_ref, gbT_ref, x_ref,
                 wsT_ref, bs_ref, wpaT_ref, wpbT_ref, wout_ref, gffn_ref, wrT_ref, br_ref,
                 x1_ref, h2_ref, ids_ref, wts_ref, rank_ref, cnt_ref, carry_ref):
    tm = x_ref.shape[1]
    n_chunks = tm // CHUNK
    first = jnp.logical_and(pl.program_id(0) == 0, pl.program_id(1) == 0)

    @pl.when(first)
    def _():
        carry_ref[...] = jnp.zeros(carry_ref.shape, jnp.float32)

    gate_rows = []
    for g in range(SGU_GROUPS):
        r0 = g * SGU_GROUP_DIM
        vs_g = vsT_ref[0, r0:r0 + SGU_GROUP_DIM, :]
        lhs = jnp.concatenate(
            [vs_g[:, c * CHUNK:(c + 1) * CHUNK] for c in range(n_chunks)], axis=0)
        mixed = jnp.dot(lhs, wsT_ref[g], preferred_element_type=jnp.float32)
        mixed = mixed + bs_ref[g]
        mixedT = jnp.concatenate(
            [mixed[c * SGU_GROUP_DIM:(c + 1) * SGU_GROUP_DIM] for c in range(n_chunks)], axis=1)
        u_g = uT_ref[0, r0:r0 + SGU_GROUP_DIM, :].astype(jnp.float32)
        gate_rows.append((u_g * mixedT).astype(jnp.bfloat16))
    gateT = jnp.concatenate(gate_rows, axis=0)

    paT = jnp.dot(wpaT_ref[...], attnT_ref[0], preferred_element_type=jnp.float32)
    pbT = jnp.dot(wpbT_ref[...], gateT, preferred_element_type=jnp.float32)
    mT = (gaT_ref[0].astype(jnp.float32) * paT
          + gbT_ref[0].astype(jnp.float32) * pbT).astype(jnp.bfloat16)
    y = lax.dot_general(mT, wout_ref[...], _TN_DIMS, preferred_element_type=jnp.float32)
    x1 = x_ref[0] + y
    x1_ref[0] = x1

    ms = jnp.mean(x1 * x1, axis=-1, keepdims=True)
    h2 = x1 * lax.rsqrt(ms + EPS) * gffn_ref[...]
    h2_ref[0] = h2

    logits = lax.dot_general(wrT_ref[...], h2, _NT_DIMS, precision=lax.Precision.HIGHEST,
                             preferred_element_type=jnp.float32) + br_ref[...]
    eidx = lax.broadcasted_iota(jnp.int32, logits.shape, 0).astype(jnp.float32)
    work = logits
    vals, ids, sels = [], [], []
    for _ in range(TOP_K):
        mx = jnp.max(work, axis=0, keepdims=True)
        idx = jnp.min(jnp.where(work == mx, eidx, float(N_EXPERTS)), axis=0, keepdims=True)
        sel = eidx == idx
        vals.append(mx)
        ids.append(idx)
        sels.append(sel)
        work = jnp.where(sel, -jnp.inf, work)
    exps = [jnp.exp(v - vals[0]) for v in vals]
    denom = exps[0] + exps[1] + exps[2] + exps[3]
    ids_ref[...] = jnp.concatenate(ids, axis=0).astype(jnp.int32)
    wts_ref[...] = jnp.concatenate([e / denom for e in exps], axis=0)

    onehot = [s.astype(jnp.float32) for s in sels]
    hits = onehot[0] + onehot[1] + onehot[2] + onehot[3]
    ti = lax.broadcasted_iota(jnp.int32, (tm, tm), 0)
    tj = lax.broadcasted_iota(jnp.int32, (tm, tm), 1)
    upper = (ti < tj).astype(jnp.bfloat16)
    prefix = jnp.dot(hits.astype(jnp.bfloat16), upper, preferred_element_type=jnp.float32)
    base = prefix + carry_ref[...]
    ranks = [jnp.sum(oh * base, axis=0, keepdims=True) for oh in onehot]
    rank_ref[...] = jnp.concatenate(ranks, axis=0).astype(jnp.int32)
    carry_ref[...] = carry_ref[...] + jnp.sum(hits, axis=1, keepdims=True)
    cnt_ref[...] = carry_ref[...].astype(jnp.int32)


def _post(attnT, uT, vsT, gaT, gbT, x, wsT, bs, wpaT, wpbT, wout, gffn, wrT, br):
    B, S, _ = x.shape
    tm = TOKEN_TILE
    nt = S // tm
    T = B * S
    fm = lambda rows: pl.BlockSpec((1, rows, tm), lambda b, i: (b, 0, i))
    rowm = pl.BlockSpec((1, tm, D_MODEL), lambda b, i: (b, i, 0))
    c2 = lambda b, i: (0, 0)
    c3 = lambda b, i: (0, 0, 0)
    tokT = pl.BlockSpec((TOP_K, tm), lambda b, i: (0, b * nt + i))
    return pl.pallas_call(
        _post_kernel,
        grid=(B, nt),
        in_specs=[
            fm(Q_W), fm(SGU_W), fm(SGU_W), fm(D_MODEL), fm(D_MODEL), rowm,
            pl.BlockSpec((SGU_GROUPS, CHUNK, CHUNK), c3),
            pl.BlockSpec((SGU_GROUPS, 1, CHUNK), c3),
            pl.BlockSpec((D_MODEL, Q_W), c2),
            pl.BlockSpec((D_MODEL, SGU_W), c2),
            pl.BlockSpec((D_MODEL, D_MODEL), c2),
            pl.BlockSpec((1, D_MODEL), c2),
            pl.BlockSpec((N_EXPERTS, D_MODEL), c2),
            pl.BlockSpec((N_EXPERTS, 1), c2),
        ],
        out_specs=[rowm, rowm, tokT, tokT, tokT, pl.BlockSpec((N_EXPERTS, 1), c2)],
        out_shape=[
            jax.ShapeDtypeStruct((B, S, D_MODEL), jnp.float32),
            jax.ShapeDtypeStruct((B, S, D_MODEL), jnp.float32),
            jax.ShapeDtypeStruct((TOP_K, T), jnp.int32),
            jax.ShapeDtypeStruct((TOP_K, T), jnp.float32),
            jax.ShapeDtypeStruct((TOP_K, T), jnp.int32),
            jax.ShapeDtypeStruct((N_EXPERTS, 1), jnp.int32),
        ],
        scratch_shapes=[pltpu.VMEM((N_EXPERTS, 1), jnp.float32)],
        compiler_params=_compiler_params(("arbitrary", "arbitrary"), 56),
        name="post",
    )(attnT, uT, vsT, gaT, gbT, x, wsT, bs, wpaT, wpbT, wout, gffn, wrT, br)


def _row_copy(src, src_row, dst, dst_row, sem):
    return pltpu.make_async_copy(src.at[pl.ds(src_row, 1)], dst.at[pl.ds(dst_row, 1)], sem)


def _dispatch_kernel(dest_ref, h_ref, out_ref, sem):
    tt = h_ref.shape[0]

    def issue(r, carry):
        for k in range(TOP_K):
            _row_copy(h_ref, r, out_ref, dest_ref[0, 0, k * tt + r], sem).start(priority=k % 2)
        return carry

    lax.fori_loop(0, tt, issue, 0)
    for _ in range(TOP_K):
        pltpu.make_async_copy(h_ref, out_ref.at[pl.ds(0, tt)], sem).wait()


def _dispatch(h2, dest_tiles, n_rows):
    T = h2.shape[0]
    tt = DISPATCH_TILE
    return pl.pallas_call(
        _dispatch_kernel,
        grid=(T // tt,),
        in_specs=[
            pl.BlockSpec((1, 1, TOP_K * tt), lambda i: (i, 0, 0), memory_space=pltpu.SMEM),
            pl.BlockSpec((tt, D_MODEL), lambda i: (i, 0)),
        ],
        out_specs=pl.BlockSpec(memory_space=pl.ANY),
        out_shape=jax.ShapeDtypeStruct((n_rows, D_MODEL), jnp.float32),
        scratch_shapes=[pltpu.SemaphoreType.DMA],
        compiler_params=_compiler_params(("arbitrary",), 32),
        name="dispatch",
    )(dest_tiles, h2)


def _combine_kernel(dest_ref, w_ref, x1_ref, gfin_ref, y_ref, o_ref, ybuf, sem):
    tt = x1_ref.shape[0]

    def issue(r, carry):
        for k in range(TOP_K):
            _row_copy(y_ref, dest_ref[0, 0, k * tt + r], ybuf.at[k], r, sem).start(priority=k % 2)
        return carry

    lax.fori_loop(0, tt, issue, 0)
    for k in range(TOP_K):
        pltpu.make_async_copy(y_ref.at[pl.ds(0, tt)], ybuf.at[k], sem).wait()

    w = w_ref[...]
    moe = (w[:, 0:1] * ybuf[0] + w[:, 1:2] * ybuf[1]
           + w[:, 2:3] * ybuf[2] + w[:, 3:4] * ybuf[3])
    x2 = x1_ref[...] + moe
    ms = jnp.mean(x2 * x2, axis=-1, keepdims=True)
    o_ref[...] = x2 * lax.rsqrt(ms + EPS) * gfin_ref[...]


def _combine(dest_tiles, wts, x1, gfin, y_sorted):
    T = x1.shape[0]
    tt = COMBINE_TILE
    return pl.pallas_call(
        _combine_kernel,
        grid=(T // tt,),
        in_specs=[
            pl.BlockSpec((1, 1, TOP_K * tt), lambda i: (i, 0, 0), memory_space=pltpu.SMEM),
            pl.BlockSpec((tt, TOP_K), lambda i: (i, 0)),
            pl.BlockSpec((tt, D_MODEL), lambda i: (i, 0)),
            pl.BlockSpec((1, D_MODEL), lambda i: (0, 0)),
            pl.BlockSpec(memory_space=pl.ANY),
        ],
        out_specs=pl.BlockSpec((tt, D_MODEL), lambda i: (i, 0)),
        out_shape=jax.ShapeDtypeStruct((T, D_MODEL), jnp.float32),
        scratch_shapes=[pltpu.VMEM((TOP_K, tt, D_MODEL), jnp.float32),
                        pltpu.SemaphoreType.DMA],
        compiler_params=_compiler_params(("arbitrary",), 32),
        name="combine",
    )(dest_tiles, wts, x1, gfin, y_sorted)


def _sc_gather_rows(table, idx):
    n = idx.shape[0]
    d = table.shape[1]
    win = SC_GATHER_WINDOW
    mesh = plsc.VectorSubcoreMesh(core_axis_name="core", subcore_axis_name="subcore")

    info = plsc.get_sparse_core_info()
    n_workers = info.num_cores * info.num_subcores
    per_worker = n // n_workers
    n_chunks = per_worker // win

    @functools.partial(
        pl.kernel,
        out_type=jax.ShapeDtypeStruct((n, d), table.dtype),
        mesh=mesh,
        scratch_types=[
            pltpu.VMEM((win,), jnp.int32),
            pltpu.VMEM((win, d), table.dtype),
            pltpu.SemaphoreType.DMA,
        ],
        name="sc_gather_rows",
    )
    def gather(table_hbm, idx_hbm, out_hbm, idx_v, rows_v, sem):
        wid = lax.axis_index("subcore") * info.num_cores + lax.axis_index("core")
        base = wid * per_worker

        @pl.loop(0, n_chunks)
        def _(c):
            off = pl.multiple_of(base + c * win, win)
            pltpu.sync_copy(idx_hbm.at[pl.ds(off, win)], idx_v)
            pltpu.async_copy(table_hbm.at[idx_v], rows_v, sem).wait()
            pltpu.sync_copy(rows_v, out_hbm.at[pl.ds(off, win)])

    return gather(table, idx)


def _combine_dense_kernel(w_ref, x1_ref, gfin_ref, y_ref, o_ref):
    w = w_ref[...]
    moe = (w[:, 0:1] * y_ref[0] + w[:, 1:2] * y_ref[1]
           + w[:, 2:3] * y_ref[2] + w[:, 3:4] * y_ref[3])
    x2 = x1_ref[...] + moe
    ms = jnp.mean(x2 * x2, axis=-1, keepdims=True)
    o_ref[...] = x2 * lax.rsqrt(ms + EPS) * gfin_ref[...]


def _combine_dense(wts, x1, gfin, y_tok):
    T = x1.shape[0]
    tt = COMBINE_TILE
    return pl.pallas_call(
        _combine_dense_kernel,
        grid=(T // tt,),
        in_specs=[
            pl.BlockSpec((tt, TOP_K), lambda i: (i, 0)),
            pl.BlockSpec((tt, D_MODEL), lambda i: (i, 0)),
            pl.BlockSpec((1, D_MODEL), lambda i: (0, 0)),
            pl.BlockSpec((TOP_K, tt, D_MODEL), lambda i: (0, i, 0)),
        ],
        out_specs=pl.BlockSpec((tt, D_MODEL), lambda i: (i, 0)),
        out_shape=jax.ShapeDtypeStruct((T, D_MODEL), jnp.float32),
        compiler_params=_compiler_params(("arbitrary",), 32),
        name="combine_dense",
    )(wts, x1, gfin, y_tok)


def _gate_up_prep_kernel(w_ref, o_ref):
    half = _GU_BLOCK // 2
    i = lax.broadcasted_iota(jnp.int32, (_GU_BLOCK, _GU_BLOCK), 0)
    j = lax.broadcasted_iota(jnp.int32, (_GU_BLOCK, _GU_BLOCK), 1)
    perm = (i == jnp.where(j < half, 2 * j, 2 * (j - half) + 1)).astype(jnp.bfloat16)
    for c in range(2 * D_FF // _GU_BLOCK):
        cols = slice(c * _GU_BLOCK, (c + 1) * _GU_BLOCK)
        blk = w_ref[0, :, cols].astype(jnp.bfloat16)
        o_ref[0, :, cols] = jnp.dot(blk, perm, preferred_element_type=jnp.float32).astype(jnp.bfloat16)


def _gate_up_prep(w_gate_up):
    spec = pl.BlockSpec((1, D_MODEL, 2 * D_FF), lambda e: (e, 0, 0))
    return pl.pallas_call(
        _gate_up_prep_kernel,
        grid=(N_EXPERTS,),
        in_specs=[spec],
        out_specs=spec,
        out_shape=jax.ShapeDtypeStruct(w_gate_up.shape, jnp.bfloat16),
        compiler_params=_compiler_params(("arbitrary",), 48),
        name="gate_up_prep",
    )(w_gate_up)


def _split_gate_up(gu):
    half = _GU_BLOCK // 2
    n = gu.shape[1] // _GU_BLOCK
    glu = jnp.concatenate([gu[:, c * _GU_BLOCK:c * _GU_BLOCK + half] for c in range(n)], axis=1)
    lin = jnp.concatenate([gu[:, c * _GU_BLOCK + half:(c + 1) * _GU_BLOCK] for c in range(n)], axis=1)
    return glu, lin


def _experts_kernel(te_ref, nv_ref, x_ref, wgu_ref, bgu_ref, wd_ref, bd_ref, y_ref):
    @pl.when(pl.program_id(0) < nv_ref[0])
    def _():
        x = x_ref[...].astype(jnp.bfloat16)
        gu = jnp.dot(x, wgu_ref[0], preferred_element_type=jnp.float32) + bgu_ref[0]
        glu, lin = _split_gate_up(gu)
        glu = jnp.minimum(glu, SWIGLU_LIMIT)
        lin = jnp.clip(lin, -SWIGLU_LIMIT, SWIGLU_LIMIT)
        a = glu * jax.nn.sigmoid(SWIGLU_ALPHA * glu) * (lin + 1.0)
        y_ref[...] = jnp.dot(a.astype(jnp.bfloat16), wd_ref[0],
                             preferred_element_type=jnp.float32) + bd_ref[0]


def _experts(tile_expert, n_valid, x_sorted, wgu, bgu, wd, bd):
    n_rows = x_sorted.shape[0]
    tr = EXPERT_TILE
    n_tiles = n_rows // tr

    def row_map(j, te, nv):
        return (jnp.minimum(j, nv[0] - 1), 0)

    def exp_map(j, te, nv):
        return (te[j], 0, 0)

    grid_spec = pltpu.PrefetchScalarGridSpec(
        num_scalar_prefetch=2,
        grid=(n_tiles,),
        in_specs=[
            pl.BlockSpec((tr, D_MODEL), row_map),
            pl.BlockSpec((1, D_MODEL, 2 * D_FF), exp_map),
            pl.BlockSpec((1, 1, 2 * D_FF), exp_map),
            pl.BlockSpec((1, D_FF, D_MODEL), exp_map),
            pl.BlockSpec((1, 1, D_MODEL), exp_map),
        ],
        out_specs=pl.BlockSpec((tr, D_MODEL), row_map),
    )
    return pl.pallas_call(
        _experts_kernel,
        grid_spec=grid_spec,
        out_shape=jax.ShapeDtypeStruct((n_rows, D_MODEL), jnp.float32),
        compiler_params=_compiler_params(("arbitrary",), 56),
        name="experts",
    )(tile_expert, n_valid, x_sorted, wgu, bgu, wd, bd)


def _rope_tables(S, gain, scale):
    t = jnp.arange(S, dtype=jnp.int32)
    r = (t // GRID_W).astype(jnp.float32)
    c = (t % GRID_W).astype(jnp.float32)
    inv = jnp.float32(ROPE_THETA) ** (
        -jnp.arange(0, ROPE_AXIS_DIM, 2, dtype=jnp.float32) / ROPE_AXIS_DIM)
    ang = jnp.concatenate([r[None, :] * inv[:, None], c[None, :] * inv[:, None]], axis=0)
    cos = jnp.repeat(jnp.cos(ang), 2, axis=0)
    sin = jnp.repeat(jnp.sin(ang), 2, axis=0)
    sign = jnp.where(jnp.arange(HEAD_DIM) % 2 == 0, -1.0, 1.0).astype(jnp.float32)
    g = gain.astype(jnp.float32) * scale
    g_swapped = g.reshape(HEAD_DIM // 2, 2)[:, ::-1].reshape(HEAD_DIM)
    return g[:, None] * cos, (g_swapped * sign)[:, None] * sin


def _token_tiles(a, tile):
    T = a.shape[1]
    return a.reshape(TOP_K, T // tile, tile).transpose(1, 0, 2).reshape(T // tile, 1, TOP_K * tile)


def _prepare_weights(norm_mix_g, w_in, q_norm_g, k_norm_g, sgu_norm_g, w_spatial, b_spatial,
                     w_proj_attn, w_proj_sgu, w_out, norm_ffn_g, w_router, b_router,
                     w_gate_up, b_gate_up, w_down, b_down, norm_final_g):
    bf = jnp.bfloat16
    l = 0
    return dict(
        gmix=norm_mix_g[l][None, :],
        w_inT=w_in[l].T.astype(bf),
        q_gain=q_norm_g[l], k_gain=k_norm_g[l],
        gs=jnp.broadcast_to(sgu_norm_g[l][:, None], (SGU_W, _LANES)),
        wsT=jnp.swapaxes(w_spatial[l], 1, 2).astype(bf),
        bs=b_spatial[l][:, None, :],
        wpaT=w_proj_attn[l].T.astype(bf),
        wpbT=w_proj_sgu[l].T.astype(bf),
        wout=w_out[l].astype(bf),
        gffn=norm_ffn_g[l][None, :],
        wrT=w_router[l].T,
        br=b_router[l][:, None],
        wgu=_gate_up_prep(w_gate_up[l]),
        bgu=b_gate_up[l].reshape(N_EXPERTS, -1, _GU_BLOCK // 2, 2).transpose(0, 1, 3, 2)
        .reshape(N_EXPERTS, 1, 2 * D_FF),
        wd=w_down[l].astype(bf),
        bd=b_down[l][:, None, :],
        gfin=norm_final_g[None, :],
    )


def _trunk(x, w):
    B, S, _ = x.shape
    T = B * S
    qa, qb = _rope_tables(S, w["q_gain"], math.log2(math.e) / math.sqrt(HEAD_DIM))
    ka, kb = _rope_tables(S, w["k_gain"], 1.0)

    qT, k, vT, uT, vsT, gaT, gbT = _in_proj(x, w["gmix"], w["w_inT"], qa, qb, ka, kb, w["gs"])
    attnT = _attention(qT, k, vT)
    x1, h2, ids, wts, rank, counts = _post(
        attnT, uT, vsT, gaT, gbT, x, w["wsT"], w["bs"], w["wpaT"], w["wpbT"], w["wout"],
        w["gffn"], w["wrT"], w["br"])

    tr = EXPERT_TILE
    counts = counts[:, 0]
    padded = (counts + tr - 1) // tr * tr
    ends = jnp.cumsum(padded)
    starts = ends - padded
    n_tiles = (TOP_K * T) // tr + N_EXPERTS
    n_rows = n_tiles * tr
    dest = rank
    for e in range(N_EXPERTS):
        dest = dest + jnp.where(ids == e, starts[e], 0)
    tile_start = jnp.arange(n_tiles, dtype=jnp.int32) * tr
    tile_expert = jnp.minimum(
        jnp.sum((tile_start[:, None] >= ends[None, :]).astype(jnp.int32), axis=1), N_EXPERTS - 1)
    n_valid = (ends[-1] // tr).astype(jnp.int32)[None]

    x_sorted = _dispatch(h2.reshape(T, D_MODEL), _token_tiles(dest, DISPATCH_TILE), n_rows)
    y_sorted = _experts(tile_expert, n_valid, x_sorted, w["wgu"], w["bgu"], w["wd"], w["bd"])
    y_tok = _sc_gather_rows(y_sorted, dest.reshape(TOP_K * T)).reshape(TOP_K, T, D_MODEL)
    out = _combine_dense(wts.T, x1.reshape(T, D_MODEL), w["gfin"], y_tok)
    return out.reshape(B, S, D_MODEL)


def kernel(x_prompt, x_sample, norm_mix_g, w_in, q_norm_g, k_norm_g, sgu_norm_g, w_spatial,
           b_spatial, w_proj_attn, w_proj_sgu, w_out, norm_ffn_g, w_router, b_router,
           w_gate_up, b_gate_up, w_down, b_down, norm_final_g):
    w = _prepare_weights(norm_mix_g, w_in, q_norm_g, k_norm_g, sgu_norm_g, w_spatial,
                         b_spatial, w_proj_attn, w_proj_sgu, w_out, norm_ffn_g, w_router,
                         b_router, w_gate_up, b_gate_up, w_down, b_down, norm_final_g)
    return (_trunk(x_prompt, w), _trunk(x_sample, w))
```

```python
import functools
import math

import jax
import jax.numpy as jnp
from jax import lax
from jax.experimental import pallas as pl
from jax.experimental.pallas import tpu as pltpu
from jax.experimental.pallas import tpu_sc as plsc

D_MODEL = 1024
GRID_W = 64
N_HEADS = 8
N_KV_HEADS = 2
HEAD_DIM = 64
GQA_GROUP = N_HEADS // N_KV_HEADS
Q_W = N_HEADS * HEAD_DIM
KV_W = N_KV_HEADS * HEAD_DIM
ROPE_AXIS_DIM = HEAD_DIM // 2
ROPE_THETA = 10000.0
SGU_GROUPS = 8
SGU_W = D_MODEL // 2
SGU_GROUP_DIM = SGU_W // SGU_GROUPS
CHUNK = 128
IN_W = Q_W + 2 * KV_W + 2 * SGU_W + 2 * D_MODEL
N_EXPERTS = 32
TOP_K = 4
D_FF = D_MODEL
SWIGLU_LIMIT = 7.0
SWIGLU_ALPHA = 1.702
EPS = 1e-6

_Q0, _K0, _V0 = 0, Q_W, Q_W + KV_W
_U0 = Q_W + 2 * KV_W
_VS0 = _U0 + SGU_W
_GA0 = _VS0 + SGU_W
_GB0 = _GA0 + D_MODEL

TOKEN_TILE = 512
Q_TILE = 256
KEY_TILE = TOKEN_TILE
EXPERT_TILE = 512
DISPATCH_TILE = 256
COMBINE_TILE = 256
SC_GATHER_WINDOW = 32

_LANES = 128
_BF16_SUBLANES = 16
_SUM_ROWS = _BF16_SUBLANES
_GU_BLOCK = 256
_NEG_BIG = -1e30
_MAX_LAGGED_EXPONENT = 80.0
_KEY_TILES_PER_TRIP = 4
_MIB = 1024 * 1024

_NT_DIMS = (((1,), (1,)), ((), ()))
_TN_DIMS = (((0,), (0,)), ((), ()))


def _compiler_params(semantics, vmem_mib):
    return pltpu.CompilerParams(
        dimension_semantics=semantics, vmem_limit_bytes=vmem_mib * _MIB)


def _swap_adjacent_rows(x):
    n = x.shape[0]
    row = lax.broadcasted_iota(jnp.int32, x.shape, 0)
    nxt = pltpu.roll(x, n - 1, 0)
    prv = pltpu.roll(x, 1, 0)
    return jnp.where((row & 1) == 0, nxt, prv)


def _head_norm_rope(z, tab_a, tab_b, n_heads):
    tm = z.shape[1]
    z3 = z.reshape(n_heads, HEAD_DIM, tm)
    ms = jnp.mean(z3 * z3, axis=1, keepdims=True)
    r = lax.rsqrt(ms + EPS)
    zs = _swap_adjacent_rows(z).reshape(n_heads, HEAD_DIM, tm)
    out = (z3 * tab_a[None] + zs * tab_b[None]) * r
    return out.reshape(n_heads * HEAD_DIM, tm)


def _gelu(x):
    return 0.5 * x * (1.0 + lax.erf(x * (1.0 / math.sqrt(2.0))))


def _tile_lanes(x, reps):
    return jnp.concatenate([x] * reps, axis=1) if reps > 1 else x


def _in_proj_kernel(x_ref, gmix_ref, w_ref, qa_ref, qb_ref, ka_ref, kb_ref, gs_ref,
                    qT_ref, k_ref, vT_ref, uT_ref, vsT_ref, gaT_ref, gbT_ref):
    tm = x_ref.shape[1]
    x = x_ref[0]
    ms = jnp.mean(x * x, axis=-1, keepdims=True)
    h = (x * lax.rsqrt(ms + EPS) * gmix_ref[...]).astype(jnp.bfloat16)

    def proj(r0, rows):
        return lax.dot_general(w_ref[r0:r0 + rows, :], h, _NT_DIMS,
                               preferred_element_type=jnp.float32)

    zq = proj(_Q0, Q_W)
    qT_ref[0] = _head_norm_rope(zq, qa_ref[...], qb_ref[...], N_HEADS).astype(jnp.bfloat16)

    zkv = proj(_K0, 2 * KV_W)
    kT = _head_norm_rope(zkv[:KV_W], ka_ref[...], kb_ref[...], N_KV_HEADS)
    k_ref[0] = kT.T.astype(jnp.bfloat16)
    vT_ref[0, 0] = zkv[KV_W:].astype(jnp.bfloat16)

    uT_ref[0] = _gelu(proj(_U0, SGU_W)).astype(jnp.bfloat16)

    vs = _gelu(proj(_VS0, SGU_W))
    vms = jnp.mean(vs * vs, axis=0, keepdims=True)
    gs = _tile_lanes(gs_ref[...], tm // _LANES)
    vsT_ref[0] = (vs * lax.rsqrt(vms + EPS) * gs).astype(jnp.bfloat16)

    gaT_ref[0] = jax.nn.sigmoid(proj(_GA0, D_MODEL)).astype(jnp.bfloat16)
    gbT_ref[0] = jax.nn.sigmoid(proj(_GB0, D_MODEL)).astype(jnp.bfloat16)


def _in_proj(x, gmix, w_inT, qa, qb, ka, kb, gs):
    B, S, _ = x.shape
    tm = TOKEN_TILE
    nt = S // tm
    bf = jnp.bfloat16
    const2 = lambda b, i: (0, 0)
    tab = pl.BlockSpec((HEAD_DIM, tm), lambda b, i: (0, i))
    fm = lambda rows: pl.BlockSpec((1, rows, tm), lambda b, i: (b, 0, i))
    return pl.pallas_call(
        _in_proj_kernel,
        grid=(B, nt),
        in_specs=[
            pl.BlockSpec((1, tm, D_MODEL), lambda b, i: (b, i, 0)),
            pl.BlockSpec((1, D_MODEL), const2),
            pl.BlockSpec((IN_W, D_MODEL), const2),
            tab, tab, tab, tab,
            pl.BlockSpec((SGU_W, _LANES), const2),
        ],
        out_specs=[
            fm(Q_W),
            pl.BlockSpec((1, tm, KV_W), lambda b, i: (b, i, 0)),
            pl.BlockSpec((1, 1, KV_W, tm), lambda b, i: (b, i, 0, 0)),
            fm(SGU_W), fm(SGU_W), fm(D_MODEL), fm(D_MODEL),
        ],
        out_shape=[
            jax.ShapeDtypeStruct((B, Q_W, S), bf),
            jax.ShapeDtypeStruct((B, S, KV_W), bf),
            jax.ShapeDtypeStruct((B, nt, KV_W, tm), bf),
            jax.ShapeDtypeStruct((B, SGU_W, S), bf),
            jax.ShapeDtypeStruct((B, SGU_W, S), bf),
            jax.ShapeDtypeStruct((B, D_MODEL, S), bf),
            jax.ShapeDtypeStruct((B, D_MODEL, S), bf),
        ],
        compiler_params=_compiler_params(("arbitrary", "arbitrary"), 56),
        name="in_proj",
    )(x, gmix, w_inT, qa, qb, ka, kb, gs)


def _attention_kernel(qT_ref, k_ref, vT_ref, o_ref, m_ref, gap_ref, acc_ref):
    tq = qT_ref.shape[2]
    n_kt = vT_ref.shape[1]
    tk = vT_ref.shape[3]
    nq = GQA_GROUP * tq
    ones_rows = (lax.broadcasted_iota(jnp.int32, (_SUM_ROWS, tk), 0) == 0).astype(jnp.bfloat16)

    zeros = jnp.zeros((HEAD_DIM, nq), jnp.bfloat16)
    qp = []
    for kv in range(N_KV_HEADS):
        heads = [qT_ref[0, (kv * GQA_GROUP + g) * HEAD_DIM:(kv * GQA_GROUP + g + 1) * HEAD_DIM, :]
                 for g in range(GQA_GROUP)]
        qk = jnp.concatenate(heads, axis=1)
        qp.append(jnp.concatenate([qk, zeros] if kv == 0 else [zeros, qk], axis=0))

    def key_tile(ki):
        return k_ref[0, pl.ds(pl.multiple_of(ki * tk, tk), tk), :]

    def value_rows(ki, kv):
        vt = vT_ref[0, ki]
        return jnp.concatenate([vt[kv * HEAD_DIM:(kv + 1) * HEAD_DIM, :], ones_rows], axis=0)

    acc_ref[...] = jnp.zeros(acc_ref.shape, jnp.float32)
    gap_ref[...] = jnp.zeros(gap_ref.shape, jnp.float32)
    first_keys = k_ref[0, 0:_BF16_SUBLANES, :]
    for kv in range(N_KV_HEADS):
        s0 = jnp.dot(first_keys, qp[kv], preferred_element_type=jnp.float32)
        m_ref[kv] = jnp.max(s0, axis=0, keepdims=True)

    def fast_tile(ki):
        kt = key_tile(ki)
        for kv in range(N_KV_HEADS):
            s = jnp.dot(kt, qp[kv], preferred_element_type=jnp.float32)
            m_old = m_ref[kv]
            p = jnp.exp2(s - m_old).astype(jnp.bfloat16)
            mt = jnp.max(s, axis=0, keepdims=True)
            pv = jnp.dot(value_rows(ki, kv), p, preferred_element_type=jnp.float32)
            m_new = jnp.maximum(m_old, mt)
            acc_ref[kv] = (acc_ref[kv] + pv) * jnp.exp2(m_old - m_new)
            m_ref[kv] = m_new
            gap_ref[kv] = jnp.maximum(gap_ref[kv], mt - m_old)

    per_trip = math.gcd(n_kt, _KEY_TILES_PER_TRIP)

    def fast_group(j, carry):
        for t in range(per_trip):
            fast_tile(per_trip * j + t)
        return carry

    lax.fori_loop(0, n_kt // per_trip, fast_group, 0)

    @pl.when(jnp.max(gap_ref[...]) > _MAX_LAGGED_EXPONENT)
    def _():
        m_ref[...] = jnp.full(m_ref.shape, _NEG_BIG, jnp.float32)
        acc_ref[...] = jnp.zeros(acc_ref.shape, jnp.float32)

        def robust_tile(ki, carry):
            kt = key_tile(ki)
            for kv in range(N_KV_HEADS):
                s = jnp.dot(kt, qp[kv], preferred_element_type=jnp.float32)
                m_old = m_ref[kv]
                m_new = jnp.maximum(m_old, jnp.max(s, axis=0, keepdims=True))
                p = jnp.exp2(s - m_new).astype(jnp.bfloat16)
                pv = jnp.dot(value_rows(ki, kv), p, preferred_element_type=jnp.float32)
                acc_ref[kv] = jnp.exp2(m_old - m_new) * acc_ref[kv] + pv
                m_ref[kv] = m_new
            return carry

        lax.fori_loop(0, n_kt, robust_tile, 0)

    for kv in range(N_KV_HEADS):
        acc = acc_ref[kv]
        o = acc[:HEAD_DIM] / acc[HEAD_DIM:HEAD_DIM + 1]
        for g in range(GQA_GROUP):
            r0 = (kv * GQA_GROUP + g) * HEAD_DIM
            o_ref[0, r0:r0 + HEAD_DIM, :] = o[:, g * tq:(g + 1) * tq].astype(jnp.bfloat16)


def _attention(qT, k, vT):
    B, _, S = qT.shape
    tq = Q_TILE
    nq = GQA_GROUP * tq
    n_kt, tk = vT.shape[1], vT.shape[3]
    return pl.pallas_call(
        _attention_kernel,
        grid=(B, S // tq),
        in_specs=[
            pl.BlockSpec((1, Q_W, tq), lambda b, i: (b, 0, i)),
            pl.BlockSpec((1, S, KV_W), lambda b, i: (b, 0, 0)),
            pl.BlockSpec((1, n_kt, KV_W, tk), lambda b, i: (b, 0, 0, 0)),
        ],
        out_specs=pl.BlockSpec((1, Q_W, tq), lambda b, i: (b, 0, i)),
        out_shape=jax.ShapeDtypeStruct((B, Q_W, S), jnp.bfloat16),
        scratch_shapes=[
            pltpu.VMEM((N_KV_HEADS, 1, nq), jnp.float32),
            pltpu.VMEM((N_KV_HEADS, 1, nq), jnp.float32),
            pltpu.VMEM((N_KV_HEADS, HEAD_DIM + _SUM_ROWS, nq), jnp.float32),
        ],
        compiler_params=_compiler_params(("arbitrary", "arbitrary"), 56),
        name="attention",
    )(qT, k, vT)


def _post_kernel(attnT_ref, uT_ref, vsT_ref, gaT_ref, gbT_ref, x_ref,
                 wsT_ref, bs_ref, wpaT_ref, wpbT_ref, wout_ref, gffn_ref, wrT_ref, br_ref,
                 x1_ref, h2_ref, ids_ref, wts_ref, rank_ref, cnt_ref, carry_ref):
    tm = x_ref.shape[1]
    n_chunks = tm // CHUNK
    first = jnp.logical_and(pl.program_id(0) == 0, pl.program_id(1) == 0)

    @pl.when(first)
    def _():
        carry_ref[...] = jnp.zeros(carry_ref.shape, jnp.float32)

    gate_rows = []
    for g in range(SGU_GROUPS):
        r0 = g * SGU_GROUP_DIM
        vs_g = vsT_ref[0, r0:r0 + SGU_GROUP_DIM, :]
        lhs = jnp.concatenate(
            [vs_g[:, c * CHUNK:(c + 1) * CHUNK] for c in range(n_chunks)], axis=0)
        mixed = jnp.dot(lhs, wsT_ref[g], preferred_element_type=jnp.float32)
        mixed = mixed + bs_ref[g]
        mixedT = jnp.concatenate(
            [mixed[c * SGU_GROUP_DIM:(c + 1) * SGU_GROUP_DIM] for c in range(n_chunks)], axis=1)
        u_g = uT_ref[0, r0:r0 + SGU_GROUP_DIM, :].astype(jnp.float32)
        gate_rows.append((u_g * mixedT).astype(jnp.bfloat16))
    gateT = jnp.concatenate(gate_rows, axis=0)

    paT = jnp.dot(wpaT_ref[...], attnT_ref[0], preferred_element_type=jnp.float32)
    pbT = jnp.dot(wpbT_ref[...], gateT, preferred_element_type=jnp.float32)
    mT = (gaT_ref[0].astype(jnp.float32) * paT
          + gbT_ref[0].astype(jnp.float32) * pbT).astype(jnp.bfloat16)
    y = lax.dot_general(mT, wout_ref[...], _TN_DIMS, preferred_element_type=jnp.float32)
    x1 = x_ref[0] + y
    x1_ref[0] = x1

    ms = jnp.mean(x1 * x1, axis=-1, keepdims=True)
    h2 = x1 * lax.rsqrt(ms + EPS) * gffn_ref[...]
    h2_ref[0] = h2

    logits = lax.dot_general(wrT_ref[...], h2, _NT_DIMS, precision=lax.Precision.HIGHEST,
                             preferred_element_type=jnp.float32) + br_ref[...]
    eidx = lax.broadcasted_iota(jnp.int32, logits.shape, 0).astype(jnp.float32)
    work = logits
    vals, ids, sels = [], [], []
    for _ in range(TOP_K):
        mx = jnp.max(work, axis=0, keepdims=True)
        idx = jnp.min(jnp.where(work == mx, eidx, float(N_EXPERTS)), axis=0, keepdims=True)
        sel = eidx == idx
        vals.append(mx)
        ids.append(idx)
        sels.append(sel)
        work = jnp.where(sel, -jnp.inf, work)
    exps = [jnp.exp(v - vals[0]) for v in vals]
    denom = exps[0] + exps[1] + exps[2] + exps[3]
    ids_ref[...] = jnp.concatenate(ids, axis=0).astype(jnp.int32)
    wts_ref[...] = jnp.concatenate([e / denom for e in exps], axis=0)

    onehot = [s.astype(jnp.float32) for s in sels]
    hits = onehot[0] + onehot[1] + onehot[2] + onehot[3]
    ti = lax.broadcasted_iota(jnp.int32, (tm, tm), 0)
    tj = lax.broadcasted_iota(jnp.int32, (tm, tm), 1)
    upper = (ti < tj).astype(jnp.bfloat16)
    prefix = jnp.dot(hits.astype(jnp.bfloat16), upper, preferred_element_type=jnp.float32)
    base = prefix + carry_ref[...]
    ranks = [jnp.sum(oh * base, axis=0, keepdims=True) for oh in onehot]
    rank_ref[...] = jnp.concatenate(ranks, axis=0).astype(jnp.int32)
    carry_ref[...] = carry_ref[...] + jnp.sum(hits, axis=1, keepdims=True)
    cnt_ref[...] = carry_ref[...].astype(jnp.int32)


def _post(attnT, uT, vsT, gaT, gbT, x, wsT, bs, wpaT, wpbT, wout, gffn, wrT, br):
    B, S, _ = x.shape
    tm = TOKEN_TILE
    nt = S // tm
    T = B * S
    fm = lambda rows: pl.BlockSpec((1, rows, tm), lambda b, i: (b, 0, i))
    rowm = pl.BlockSpec((1, tm, D_MODEL), lambda b, i: (b, i, 0))
    c2 = lambda b, i: (0, 0)
    c3 = lambda b, i: (0, 0, 0)
    tokT = pl.BlockSpec((TOP_K, tm), lambda b, i: (0, b * nt + i))
    return pl.pallas_call(
        _post_kernel,
        grid=(B, nt),
        in_specs=[
            fm(Q_W), fm(SGU_W), fm(SGU_W), fm(D_MODEL), fm(D_MODEL), rowm,
            pl.BlockSpec((SGU_GROUPS, CHUNK, CHUNK), c3),
            pl.BlockSpec((SGU_GROUPS, 1, CHUNK), c3),
            pl.BlockSpec((D_MODEL, Q_W), c2),
            pl.BlockSpec((D_MODEL, SGU_W), c2),
            pl.BlockSpec((D_MODEL, D_MODEL), c2),
            pl.BlockSpec((1, D_MODEL), c2),
            pl.BlockSpec((N_EXPERTS, D_MODEL), c2),
            pl.BlockSpec((N_EXPERTS, 1), c2),
        ],
        out_specs=[rowm, rowm, tokT, tokT, tokT, pl.BlockSpec((N_EXPERTS, 1), c2)],
        out_shape=[
            jax.ShapeDtypeStruct((B, S, D_MODEL), jnp.float32),
            jax.ShapeDtypeStruct((B, S, D_MODEL), jnp.float32),
            jax.ShapeDtypeStruct((TOP_K, T), jnp.int32),
            jax.ShapeDtypeStruct((TOP_K, T), jnp.float32),
            jax.ShapeDtypeStruct((TOP_K, T), jnp.int32),
            jax.ShapeDtypeStruct((N_EXPERTS, 1), jnp.int32),
        ],
        scratch_shapes=[pltpu.VMEM((N_EXPERTS, 1), jnp.float32)],
        compiler_params=_compiler_params(("arbitrary", "arbitrary"), 56),
        name="post",
    )(attnT, uT, vsT, gaT, gbT, x, wsT, bs, wpaT, wpbT, wout, gffn, wrT, br)


def _row_copy(src, src_row, dst, dst_row, sem):
    return pltpu.make_async_copy(src.at[pl.ds(src_row, 1)], dst.at[pl.ds(dst_row, 1)], sem)


def _dispatch_kernel(dest_ref, h_ref, out_ref, sem):
    tt = h_ref.shape[0]

    def issue(r, carry):
        for k in range(TOP_K):
            _row_copy(h_ref, r, out_ref, dest_ref[0, 0, k * tt + r], sem).start(priority=k % 2)
        return carry

    lax.fori_loop(0, tt, issue, 0)
    for _ in range(TOP_K):
        pltpu.make_async_copy(h_ref, out_ref.at[pl.ds(0, tt)], sem).wait()


def _dispatch(h2, dest_tiles, n_rows):
    T = h2.shape[0]
    tt = DISPATCH_TILE
    return pl.pallas_call(
        _dispatch_kernel,
        grid=(T // tt,),
        in_specs=[
            pl.BlockSpec((1, 1, TOP_K * tt), lambda i: (i, 0, 0), memory_space=pltpu.SMEM),
            pl.BlockSpec((tt, D_MODEL), lambda i: (i, 0)),
        ],
        out_specs=pl.BlockSpec(memory_space=pl.ANY),
        out_shape=jax.ShapeDtypeStruct((n_rows, D_MODEL), jnp.float32),
        scratch_shapes=[pltpu.SemaphoreType.DMA],
        compiler_params=_compiler_params(("arbitrary",), 32),
        name="dispatch",
    )(dest_tiles, h2)


def _combine_kernel(dest_ref, w_ref, x1_ref, gfin_ref, y_ref, o_ref, ybuf, sem):
    tt = x1_ref.shape[0]

    def issue(r, carry):
        for k in range(TOP_K):
            _row_copy(y_ref, dest_ref[0, 0, k * tt + r], ybuf.at[k], r, sem).start(priority=k % 2)
        return carry

    lax.fori_loop(0, tt, issue, 0)
    for k in range(TOP_K):
        pltpu.make_async_copy(y_ref.at[pl.ds(0, tt)], ybuf.at[k], sem).wait()

    w = w_ref[...]
    moe = (w[:, 0:1] * ybuf[0] + w[:, 1:2] * ybuf[1]
           + w[:, 2:3] * ybuf[2] + w[:, 3:4] * ybuf[3])
    x2 = x1_ref[...] + moe
    ms = jnp.mean(x2 * x2, axis=-1, keepdims=True)
    o_ref[...] = x2 * lax.rsqrt(ms + EPS) * gfin_ref[...]


def _combine(dest_tiles, wts, x1, gfin, y_sorted):
    T = x1.shape[0]
    tt = COMBINE_TILE
    return pl.pallas_call(
        _combine_kernel,
        grid=(T // tt,),
        in_specs=[
            pl.BlockSpec((1, 1, TOP_K * tt), lambda i: (i, 0, 0), memory_space=pltpu.SMEM),
            pl.BlockSpec((tt, TOP_K), lambda i: (i, 0)),
            pl.BlockSpec((tt, D_MODEL), lambda i: (i, 0)),
            pl.BlockSpec((1, D_MODEL), lambda i: (0, 0)),
            pl.BlockSpec(memory_space=pl.ANY),
        ],
        out_specs=pl.BlockSpec((tt, D_MODEL), lambda i: (i, 0)),
        out_shape=jax.ShapeDtypeStruct((T, D_MODEL), jnp.float32),
        scratch_shapes=[pltpu.VMEM((TOP_K, tt, D_MODEL), jnp.float32),
                        pltpu.SemaphoreType.DMA],
        compiler_params=_compiler_params(("arbitrary",), 32),
        name="combine",
    )(dest_tiles, wts, x1, gfin, y_sorted)


def _sc_gather_rows(table, idx):
    n = idx.shape[0]
    d = table.shape[1]
    win = SC_GATHER_WINDOW
    mesh = plsc.VectorSubcoreMesh(core_axis_name="core", subcore_axis_name="subcore")

    info = plsc.get_sparse_core_info()
    n_workers = info.num_cores * info.num_subcores
    per_worker = n // n_workers
    n_chunks = per_worker // win

    @functools.partial(
        pl.kernel,
        out_type=jax.ShapeDtypeStruct((n, d), table.dtype),
        mesh=mesh,
        scratch_types=[
            pltpu.VMEM((win,), jnp.int32),
            pltpu.VMEM((win, d), table.dtype),
            pltpu.SemaphoreType.DMA,
        ],
        name="sc_gather_rows",
    )
    def gather(table_hbm, idx_hbm, out_hbm, idx_v, rows_v, sem):
        wid = lax.axis_index("subcore") * info.num_cores + lax.axis_index("core")
        base = wid * per_worker

        @pl.loop(0, n_chunks)
        def _(c):
            off = pl.multiple_of(base + c * win, win)
            pltpu.sync_copy(idx_hbm.at[pl.ds(off, win)], idx_v)
            pltpu.async_copy(table_hbm.at[idx_v], rows_v, sem).wait()
            pltpu.sync_copy(rows_v, out_hbm.at[pl.ds(off, win)])

    return gather(table, idx)


def _sc_scatter_rows(src, dest, n_rows):
    T, d = src.shape
    win = SC_GATHER_WINDOW
    mesh = plsc.VectorSubcoreMesh(core_axis_name="core", subcore_axis_name="subcore")
    info = plsc.get_sparse_core_info()
    n_workers = info.num_cores * info.num_subcores
    per_worker = T // n_workers
    n_chunks = per_worker // win

    @functools.partial(
        pl.kernel,
        out_type=jax.ShapeDtypeStruct((n_rows, d), src.dtype),
        mesh=mesh,
        scratch_types=[
            pltpu.VMEM((win,), jnp.int32),
            pltpu.VMEM((win, d), src.dtype),
        ],
        name="sc_scatter_rows",
    )
    def scatter(src_hbm, dest_hbm, out_hbm, idx_v, rows_v):
        wid = lax.axis_index("subcore") * info.num_cores + lax.axis_index("core")
        base = wid * per_worker

        @pl.loop(0, n_chunks)
        def _(c):
            off = pl.multiple_of(base + c * win, win)
            pltpu.sync_copy(src_hbm.at[pl.ds(off, win)], rows_v)
            for k in range(TOP_K):
                pltpu.sync_copy(dest_hbm.at[pl.ds(pl.multiple_of(k * T + off, win), win)], idx_v)
                pltpu.sync_copy(rows_v, out_hbm.at[idx_v])

    return scatter(src, dest.reshape(TOP_K * T))


def _combine_dense_kernel(w_ref, x1_ref, gfin_ref, y_ref, o_ref):
    w = w_ref[...]
    moe = (w[:, 0:1] * y_ref[0] + w[:, 1:2] * y_ref[1]
           + w[:, 2:3] * y_ref[2] + w[:, 3:4] * y_ref[3])
    x2 = x1_ref[...] + moe
    ms = jnp.mean(x2 * x2, axis=-1, keepdims=True)
    o_ref[...] = x2 * lax.rsqrt(ms + EPS) * gfin_ref[...]


def _combine_dense(wts, x1, gfin, y_tok):
    T = x1.shape[0]
    tt = COMBINE_TILE
    return pl.pallas_call(
        _combine_dense_kernel,
        grid=(T // tt,),
        in_specs=[
            pl.BlockSpec((tt, TOP_K), lambda i: (i, 0)),
            pl.BlockSpec((tt, D_MODEL), lambda i: (i, 0)),
            pl.BlockSpec((1, D_MODEL), lambda i: (0, 0)),
            pl.BlockSpec((TOP_K, tt, D_MODEL), lambda i: (0, i, 0)),
        ],
        out_specs=pl.BlockSpec((tt, D_MODEL), lambda i: (i, 0)),
        out_shape=jax.ShapeDtypeStruct((T, D_MODEL), jnp.float32),
        compiler_params=_compiler_params(("arbitrary",), 32),
        name="combine_dense",
    )(wts, x1, gfin, y_tok)


def _gate_up_prep_kernel(w_ref, o_ref):
    half = _GU_BLOCK // 2
    i = lax.broadcasted_iota(jnp.int32, (_GU_BLOCK, _GU_BLOCK), 0)
    j = lax.broadcasted_iota(jnp.int32, (_GU_BLOCK, _GU_BLOCK), 1)
    perm = (i == jnp.where(j < half, 2 * j, 2 * (j - half) + 1)).astype(jnp.bfloat16)
    for c in range(2 * D_FF // _GU_BLOCK):
        cols = slice(c * _GU_BLOCK, (c + 1) * _GU_BLOCK)
        blk = w_ref[0, :, cols].astype(jnp.bfloat16)
        o_ref[0, :, cols] = jnp.dot(blk, perm, preferred_element_type=jnp.float32).astype(jnp.bfloat16)


def _gate_up_prep(w_gate_up):
    spec = pl.BlockSpec((1, D_MODEL, 2 * D_FF), lambda e: (e, 0, 0))
    return pl.pallas_call(
        _gate_up_prep_kernel,
        grid=(N_EXPERTS,),
        in_specs=[spec],
        out_specs=spec,
        out_shape=jax.ShapeDtypeStruct(w_gate_up.shape, jnp.bfloat16),
        compiler_params=_compiler_params(("arbitrary",), 48),
        name="gate_up_prep",
    )(w_gate_up)


def _split_gate_up(gu):
    half = _GU_BLOCK // 2
    n = gu.shape[1] // _GU_BLOCK
    glu = jnp.concatenate([gu[:, c * _GU_BLOCK:c * _GU_BLOCK + half] for c in range(n)], axis=1)
    lin = jnp.concatenate([gu[:, c * _GU_BLOCK + half:(c + 1) * _GU_BLOCK] for c in range(n)], axis=1)
    return glu, lin


def _experts_kernel(te_ref, nv_ref, x_ref, wgu_ref, bgu_ref, wd_ref, bd_ref, y_ref):
    @pl.when(pl.program_id(0) < nv_ref[0])
    def _():
        x = x_ref[...].astype(jnp.bfloat16)
        gu = jnp.dot(x, wgu_ref[0], preferred_element_type=jnp.float32) + bgu_ref[0]
        glu, lin = _split_gate_up(gu)
        glu = jnp.minimum(glu, SWIGLU_LIMIT)
        lin = jnp.clip(lin, -SWIGLU_LIMIT, SWIGLU_LIMIT)
        a = glu * jax.nn.sigmoid(SWIGLU_ALPHA * glu) * (lin + 1.0)
        y_ref[...] = jnp.dot(a.astype(jnp.bfloat16), wd_ref[0],
                             preferred_element_type=jnp.float32) + bd_ref[0]


def _experts(tile_expert, n_valid, x_sorted, wgu, bgu, wd, bd):
    n_rows = x_sorted.shape[0]
    tr = EXPERT_TILE
    n_tiles = n_rows // tr

    def row_map(j, te, nv):
        return (jnp.minimum(j, nv[0] - 1), 0)

    def exp_map(j, te, nv):
        return (te[j], 0, 0)

    grid_spec = pltpu.PrefetchScalarGridSpec(
        num_scalar_prefetch=2,
        grid=(n_tiles,),
        in_specs=[
            pl.BlockSpec((tr, D_MODEL), row_map),
            pl.BlockSpec((1, D_MODEL, 2 * D_FF), exp_map),
            pl.BlockSpec((1, 1, 2 * D_FF), exp_map),
            pl.BlockSpec((1, D_FF, D_MODEL), exp_map),
            pl.BlockSpec((1, 1, D_MODEL), exp_map),
        ],
        out_specs=pl.BlockSpec((tr, D_MODEL), row_map),
    )
    return pl.pallas_call(
        _experts_kernel,
        grid_spec=grid_spec,
        out_shape=jax.ShapeDtypeStruct((n_rows, D_MODEL), jnp.float32),
        compiler_params=_compiler_params(("arbitrary",), 56),
        name="experts",
    )(tile_expert, n_valid, x_sorted, wgu, bgu, wd, bd)


def _rope_tables(S, gain, scale):
    t = jnp.arange(S, dtype=jnp.int32)
    r = (t // GRID_W).astype(jnp.float32)
    c = (t % GRID_W).astype(jnp.float32)
    inv = jnp.float32(ROPE_THETA) ** (
        -jnp.arange(0, ROPE_AXIS_DIM, 2, dtype=jnp.float32) / ROPE_AXIS_DIM)
    ang = jnp.concatenate([r[None, :] * inv[:, None], c[None, :] * inv[:, None]], axis=0)
    cos = jnp.repeat(jnp.cos(ang), 2, axis=0)
    sin = jnp.repeat(jnp.sin(ang), 2, axis=0)
    sign = jnp.where(jnp.arange(HEAD_DIM) % 2 == 0, -1.0, 1.0).astype(jnp.float32)
    g = gain.astype(jnp.float32) * scale
    g_swapped = g.reshape(HEAD_DIM // 2, 2)[:, ::-1].reshape(HEAD_DIM)
    return g[:, None] * cos, (g_swapped * sign)[:, None] * sin


def _token_tiles(a, tile):
    T = a.shape[1]
    return a.reshape(TOP_K, T // tile, tile).transpose(1, 0, 2).reshape(T // tile, 1, TOP_K * tile)


def _prepare_weights(norm_mix_g, w_in, q_norm_g, k_norm_g, sgu_norm_g, w_spatial, b_spatial,
                     w_proj_attn, w_proj_sgu, w_out, norm_ffn_g, w_router, b_router,
                     w_gate_up, b_gate_up, w_down, b_down, norm_final_g):
    bf = jnp.bfloat16
    l = 0
    return dict(
        gmix=norm_mix_g[l][None, :],
        w_inT=w_in[l].T.astype(bf),
        q_gain=q_norm_g[l], k_gain=k_norm_g[l],
        gs=jnp.broadcast_to(sgu_norm_g[l][:, None], (SGU_W, _LANES)),
        wsT=jnp.swapaxes(w_spatial[l], 1, 2).astype(bf),
        bs=b_spatial[l][:, None, :],
        wpaT=w_proj_attn[l].T.astype(bf),
        wpbT=w_proj_sgu[l].T.astype(bf),
        wout=w_out[l].astype(bf),
        gffn=norm_ffn_g[l][None, :],
        wrT=w_router[l].T,
        br=b_router[l][:, None],
        wgu=_gate_up_prep(w_gate_up[l]),
        bgu=b_gate_up[l].reshape(N_EXPERTS, -1, _GU_BLOCK // 2, 2).transpose(0, 1, 3, 2)
        .reshape(N_EXPERTS, 1, 2 * D_FF),
        wd=w_down[l].astype(bf),
        bd=b_down[l][:, None, :],
        gfin=norm_final_g[None, :],
    )


def _trunk(x, w):
    B, S, _ = x.shape
    T = B * S
    qa, qb = _rope_tables(S, w["q_gain"], math.log2(math.e) / math.sqrt(HEAD_DIM))
    ka, kb = _rope_tables(S, w["k_gain"], 1.0)

    qT, k, vT, uT, vsT, gaT, gbT = _in_proj(x, w["gmix"], w["w_inT"], qa, qb, ka, kb, w["gs"])
    attnT = _attention(qT, k, vT)
    x1, h2, ids, wts, rank, counts = _post(
        attnT, uT, vsT, gaT, gbT, x, w["wsT"], w["bs"], w["wpaT"], w["wpbT"], w["wout"],
        w["gffn"], w["wrT"], w["br"])

    tr = EXPERT_TILE
    counts = counts[:, 0]
    padded = (counts + tr - 1) // tr * tr
    ends = jnp.cumsum(padded)
    starts = ends - padded
    n_tiles = (TOP_K * T) // tr + N_EXPERTS
    n_rows = n_tiles * tr
    dest = rank
    for e in range(N_EXPERTS):
        dest = dest + jnp.where(ids == e, starts[e], 0)
    tile_start = jnp.arange(n_tiles, dtype=jnp.int32) * tr
    tile_expert = jnp.minimum(
        jnp.sum((tile_start[:, None] >= ends[None, :]).astype(jnp.int32), axis=1), N_EXPERTS - 1)
    n_valid = (ends[-1] // tr).astype(jnp.int32)[None]

    x_sorted = _sc_scatter_rows(h2.reshape(T, D_MODEL), dest, n_rows)
    y_sorted = _experts(tile_expert, n_valid, x_sorted, w["wgu"], w["bgu"], w["wd"], w["bd"])
    y_tok = _sc_gather_rows(y_sorted, dest.reshape(TOP_K * T)).reshape(TOP_K, T, D_MODEL)
    out = _combine_dense(wts.T, x1.reshape(T, D_MODEL), w["gfin"], y_tok)
    return out.reshape(B, S, D_MODEL)


def kernel(x_prompt, x_sample, norm_mix_g, w_in, q_norm_g, k_norm_g, sgu_norm_g, w_spatial,
           b_spatial, w_proj_attn, w_proj_sgu, w_out, norm_ffn_g, w_router, b_router,
           w_gate_up, b_gate_up, w_down, b_down, norm_final_g):
    w = _prepare_weights(norm_mix_g, w_in, q_norm_g, k_norm_g, sgu_norm_g, w_spatial,
                         b_spatial, w_proj_attn, w_proj_sgu, w_out, norm_ffn_g, w_router,
                         b_router, w_gate_up, b_gate_up, w_down, b_down, norm_final_g)
    return (_trunk(x_prompt, w), _trunk(x_sample, w))
```

```python
import functools
import math

import jax
import jax.numpy as jnp
from jax import lax
from jax.experimental import pallas as pl
from jax.experimental.pallas import tpu as pltpu
from jax.experimental.pallas import tpu_sc as plsc

D_MODEL = 1024
GRID_W = 64
N_HEADS = 8
N_KV_HEADS = 2
HEAD_DIM = 64
GQA_GROUP = N_HEADS // N_KV_HEADS
Q_W = N_HEADS * HEAD_DIM
KV_W = N_KV_HEADS * HEAD_DIM
ROPE_AXIS_DIM = HEAD_DIM // 2
ROPE_THETA = 10000.0
SGU_GROUPS = 8
SGU_W = D_MODEL // 2
SGU_GROUP_DIM = SGU_W // SGU_GROUPS
CHUNK = 128
IN_W = Q_W + 2 * KV_W + 2 * SGU_W + 2 * D_MODEL
N_EXPERTS = 32
TOP_K = 4
D_FF = D_MODEL
SWIGLU_LIMIT = 7.0
SWIGLU_ALPHA = 1.702
EPS = 1e-6

_Q0, _K0, _V0 = 0, Q_W, Q_W + KV_W
_U0 = Q_W + 2 * KV_W
_VS0 = _U0 + SGU_W
_GA0 = _VS0 + SGU_W
_GB0 = _GA0 + D_MODEL

TOKEN_TILE = 512
Q_TILE = 256
KEY_TILE = TOKEN_TILE
EXPERT_TILE = 512
COMBINE_TILE = 256
PACKED_W = D_MODEL // 2
SC_WINDOW = 64

_LANES = 128
_BF16_SUBLANES = 16
_SUM_ROWS = _BF16_SUBLANES
_GU_BLOCK = 256
_NEG_BIG = -1e30
_HIGH_HALF_MASK = 0xFFFF0000
_MAX_LAGGED_EXPONENT = 80.0
_KEY_TILES_PER_TRIP = 4
_MIB = 1024 * 1024

_NT_DIMS = (((1,), (1,)), ((), ()))
_TN_DIMS = (((0,), (0,)), ((), ()))


def _compiler_params(semantics, vmem_mib):
    return pltpu.CompilerParams(
        dimension_semantics=semantics, vmem_limit_bytes=vmem_mib * _MIB)


def _swap_adjacent_rows(x):
    n = x.shape[0]
    row = lax.broadcasted_iota(jnp.int32, x.shape, 0)
    nxt = pltpu.roll(x, n - 1, 0)
    prv = pltpu.roll(x, 1, 0)
    return jnp.where((row & 1) == 0, nxt, prv)


def _head_norm_rope(z, tab_a, tab_b, n_heads):
    tm = z.shape[1]
    z3 = z.reshape(n_heads, HEAD_DIM, tm)
    ms = jnp.mean(z3 * z3, axis=1, keepdims=True)
    r = lax.rsqrt(ms + EPS)
    zs = _swap_adjacent_rows(z).reshape(n_heads, HEAD_DIM, tm)
    out = (z3 * tab_a[None] + zs * tab_b[None]) * r
    return out.reshape(n_heads * HEAD_DIM, tm)


def _gelu(x):
    return 0.5 * x * (1.0 + lax.erf(x * (1.0 / math.sqrt(2.0))))


def _pack_row_halves(x):
    half = x.shape[1] // 2

    def bf16_bits(v):
        return lax.bitcast_convert_type(v.astype(jnp.bfloat16).astype(jnp.float32), jnp.uint32)

    word = (bf16_bits(x[:, :half]) >> 16) | (bf16_bits(x[:, half:]) & jnp.uint32(_HIGH_HALF_MASK))
    return lax.bitcast_convert_type(word, jnp.int32)


def _unpack_row_halves(p):
    word = lax.bitcast_convert_type(p, jnp.uint32)
    lo = lax.bitcast_convert_type(word << 16, jnp.float32)
    hi = lax.bitcast_convert_type(word & jnp.uint32(_HIGH_HALF_MASK), jnp.float32)
    return jnp.concatenate([lo, hi], axis=1)


def _tile_lanes(x, reps):
    return jnp.concatenate([x] * reps, axis=1) if reps > 1 else x


def _in_proj_kernel(x_ref, gmix_ref, w_ref, qa_ref, qb_ref, ka_ref, kb_ref, gs_ref,
                    qT_ref, k_ref, vT_ref, uT_ref, vsT_ref, gaT_ref, gbT_ref):
    tm = x_ref.shape[1]
    x = x_ref[0]
    ms = jnp.mean(x * x, axis=-1, keepdims=True)
    h = (x * lax.rsqrt(ms + EPS) * gmix_ref[...]).astype(jnp.bfloat16)

    def proj(r0, rows):
        return lax.dot_general(w_ref[r0:r0 + rows, :], h, _NT_DIMS,
                               preferred_element_type=jnp.float32)

    zq = proj(_Q0, Q_W)
    qT_ref[0] = _head_norm_rope(zq, qa_ref[...], qb_ref[...], N_HEADS).astype(jnp.bfloat16)

    zkv = proj(_K0, 2 * KV_W)
    kT = _head_norm_rope(zkv[:KV_W], ka_ref[...], kb_ref[...], N_KV_HEADS)
    k_ref[0] = kT.T.astype(jnp.bfloat16)
    vT_ref[0, 0] = zkv[KV_W:].astype(jnp.bfloat16)

    uT_ref[0] = _gelu(proj(_U0, SGU_W)).astype(jnp.bfloat16)

    vs = _gelu(proj(_VS0, SGU_W))
    vms = jnp.mean(vs * vs, axis=0, keepdims=True)
    gs = _tile_lanes(gs_ref[...], tm // _LANES)
    vsT_ref[0] = (vs * lax.rsqrt(vms + EPS) * gs).astype(jnp.bfloat16)

    gaT_ref[0] = jax.nn.sigmoid(proj(_GA0, D_MODEL)).astype(jnp.bfloat16)
    gbT_ref[0] = jax.nn.sigmoid(proj(_GB0, D_MODEL)).astype(jnp.bfloat16)


def _in_proj(x, gmix, w_inT, qa, qb, ka, kb, gs):
    B, S, _ = x.shape
    tm = TOKEN_TILE
    nt = S // tm
    bf = jnp.bfloat16
    const2 = lambda b, i: (0, 0)
    tab = pl.BlockSpec((HEAD_DIM, tm), lambda b, i: (0, i))
    fm = lambda rows: pl.BlockSpec((1, rows, tm), lambda b, i: (b, 0, i))
    return pl.pallas_call(
        _in_proj_kernel,
        grid=(B, nt),
        in_specs=[
            pl.BlockSpec((1, tm, D_MODEL), lambda b, i: (b, i, 0)),
            pl.BlockSpec((1, D_MODEL), const2),
            pl.BlockSpec((IN_W, D_MODEL), const2),
            tab, tab, tab, tab,
            pl.BlockSpec((SGU_W, _LANES), const2),
        ],
        out_specs=[
            fm(Q_W),
            pl.BlockSpec((1, tm, KV_W), lambda b, i: (b, i, 0)),
            pl.BlockSpec((1, 1, KV_W, tm), lambda b, i: (b, i, 0, 0)),
            fm(SGU_W), fm(SGU_W), fm(D_MODEL), fm(D_MODEL),
        ],
        out_shape=[
            jax.ShapeDtypeStruct((B, Q_W, S), bf),
            jax.ShapeDtypeStruct((B, S, KV_W), bf),
            jax.ShapeDtypeStruct((B, nt, KV_W, tm), bf),
            jax.ShapeDtypeStruct((B, SGU_W, S), bf),
            jax.ShapeDtypeStruct((B, SGU_W, S), bf),
            jax.ShapeDtypeStruct((B, D_MODEL, S), bf),
            jax.ShapeDtypeStruct((B, D_MODEL, S), bf),
        ],
        compiler_params=_compiler_params(("arbitrary", "arbitrary"), 56),
        name="in_proj",
    )(x, gmix, w_inT, qa, qb, ka, kb, gs)


def _attention_kernel(qT_ref, k_ref, vT_ref, o_ref, m_ref, gap_ref, acc_ref):
    tq = qT_ref.shape[2]
    n_kt = vT_ref.shape[1]
    tk = vT_ref.shape[3]
    nq = GQA_GROUP * tq
    ones_rows = (lax.broadcasted_iota(jnp.int32, (_SUM_ROWS, tk), 0) == 0).astype(jnp.bfloat16)

    zeros = jnp.zeros((HEAD_DIM, nq), jnp.bfloat16)
    qp = []
    for kv in range(N_KV_HEADS):
        heads = [qT_ref[0, (kv * GQA_GROUP + g) * HEAD_DIM:(kv * GQA_GROUP + g + 1) * HEAD_DIM, :]
                 for g in range(GQA_GROUP)]
        qk = jnp.concatenate(heads, axis=1)
        qp.append(jnp.concatenate([qk, zeros] if kv == 0 else [zeros, qk], axis=0))

    def key_tile(ki):
        return k_ref[0, pl.ds(pl.multiple_of(ki * tk, tk), tk), :]

    def value_rows(ki, kv):
        vt = vT_ref[0, ki]
        return jnp.concatenate([vt[kv * HEAD_DIM:(kv + 1) * HEAD_DIM, :], ones_rows], axis=0)

    acc_ref[...] = jnp.zeros(acc_ref.shape, jnp.float32)
    gap_ref[...] = jnp.zeros(gap_ref.shape, jnp.float32)
    first_keys = k_ref[0, 0:_BF16_SUBLANES, :]
    for kv in range(N_KV_HEADS):
        s0 = jnp.dot(first_keys, qp[kv], preferred_element_type=jnp.float32)
        m_ref[kv] = jnp.max(s0, axis=0, keepdims=True)

    def fast_tile(ki):
        kt = key_tile(ki)
        for kv in range(N_KV_HEADS):
            s = jnp.dot(kt, qp[kv], preferred_element_type=jnp.float32)
            m_old = m_ref[kv]
            p = jnp.exp2(s - m_old).astype(jnp.bfloat16)
            mt = jnp.max(s, axis=0, keepdims=True)
            pv = jnp.dot(value_rows(ki, kv), p, preferred_element_type=jnp.float32)
            m_new = jnp.maximum(m_old, mt)
            acc_ref[kv] = (acc_ref[kv] + pv) * jnp.exp2(m_old - m_new)
            m_ref[kv] = m_new
            gap_ref[kv] = jnp.maximum(gap_ref[kv], mt - m_old)

    per_trip = math.gcd(n_kt, _KEY_TILES_PER_TRIP)

    def fast_group(j, carry):
        for t in range(per_trip):
            fast_tile(per_trip * j + t)
        return carry

    lax.fori_loop(0, n_kt // per_trip, fast_group, 0)

    @pl.when(jnp.max(gap_ref[...]) > _MAX_LAGGED_EXPONENT)
    def _():
        m_ref[...] = jnp.full(m_ref.shape, _NEG_BIG, jnp.float32)
        acc_ref[...] = jnp.zeros(acc_ref.shape, jnp.float32)

        def robust_tile(ki, carry):
            kt = key_tile(ki)
            for kv in range(N_KV_HEADS):
                s = jnp.dot(kt, qp[kv], preferred_element_type=jnp.float32)
                m_old = m_ref[kv]
                m_new = jnp.maximum(m_old, jnp.max(s, axis=0, keepdims=True))
                p = jnp.exp2(s - m_new).astype(jnp.bfloat16)
                pv = jnp.dot(value_rows(ki, kv), p, preferred_element_type=jnp.float32)
                acc_ref[kv] = jnp.exp2(m_old - m_new) * acc_ref[kv] + pv
                m_ref[kv] = m_new
            return carry

        lax.fori_loop(0, n_kt, robust_tile, 0)

    for kv in range(N_KV_HEADS):
        acc = acc_ref[kv]
        o = acc[:HEAD_DIM] / acc[HEAD_DIM:HEAD_DIM + 1]
        for g in range(GQA_GROUP):
            r0 = (kv * GQA_GROUP + g) * HEAD_DIM
            o_ref[0, r0:r0 + HEAD_DIM, :] = o[:, g * tq:(g + 1) * tq].astype(jnp.bfloat16)


def _attention(qT, k, vT):
    B, _, S = qT.shape
    tq = Q_TILE
    nq = GQA_GROUP * tq
    n_kt, tk = vT.shape[1], vT.shape[3]
    return pl.pallas_call(
        _attention_kernel,
        grid=(B, S // tq),
        in_specs=[
            pl.BlockSpec((1, Q_W, tq), lambda b, i: (b, 0, i)),
            pl.BlockSpec((1, S, KV_W), lambda b, i: (b, 0, 0)),
            pl.BlockSpec((1, n_kt, KV_W, tk), lambda b, i: (b, 0, 0, 0)),
        ],
        out_specs=pl.BlockSpec((1, Q_W, tq), lambda b, i: (b, 0, i)),
        out_shape=jax.ShapeDtypeStruct((B, Q_W, S), jnp.bfloat16),
        scratch_shapes=[
            pltpu.VMEM((N_KV_HEADS, 1, nq), jnp.float32),
            pltpu.VMEM((N_KV_HEADS, 1, nq), jnp.float32),
            pltpu.VMEM((N_KV_HEADS, HEAD_DIM + _SUM_ROWS, nq), jnp.float32),
        ],
        compiler_params=_compiler_params(("arbitrary", "arbitrary"), 56),
        name="attention",
    )(qT, k, vT)


def _post_kernel(attnT_ref, uT_ref, vsT_ref, gaT_ref, gbT_ref, x_ref,
                 wsT_ref, bs_ref, wpaT_ref, wpbT_ref, wout_ref, gffn_ref, wrT_ref, br_ref,
                 x1_ref, h2_ref, ids_ref, wts_ref, rank_ref, cnt_ref, carry_ref):
    tm = x_ref.shape[1]
    n_chunks = tm // CHUNK
    first = jnp.logical_and(pl.program_id(0) == 0, pl.program_id(1) == 0)

    @pl.when(first)
    def _():
        carry_ref[...] = jnp.zeros(carry_ref.shape, jnp.float32)

    gate_rows = []
    for g in range(SGU_GROUPS):
        r0 = g * SGU_GROUP_DIM
        vs_g = vsT_ref[0, r0:r0 + SGU_GROUP_DIM, :]
        lhs = jnp.concatenate(
            [vs_g[:, c * CHUNK:(c + 1) * CHUNK] for c in range(n_chunks)], axis=0)
        mixed = jnp.dot(lhs, wsT_ref[g], preferred_element_type=jnp.float32)
        mixed = mixed + bs_ref[g]
        mixedT = jnp.concatenate(
            [mixed[c * SGU_GROUP_DIM:(c + 1) * SGU_GROUP_DIM] for c in range(n_chunks)], axis=1)
        u_g = uT_ref[0, r0:r0 + SGU_GROUP_DIM, :].astype(jnp.float32)
        gate_rows.append((u_g * mixedT).astype(jnp.bfloat16))
    gateT = jnp.concatenate(gate_rows, axis=0)

    paT = jnp.dot(wpaT_ref[...], attnT_ref[0], preferred_element_type=jnp.float32)
    pbT = jnp.dot(wpbT_ref[...], gateT, preferred_element_type=jnp.float32)
    mT = (gaT_ref[0].astype(jnp.float32) * paT
          + gbT_ref[0].astype(jnp.float32) * pbT).astype(jnp.bfloat16)
    y = lax.dot_general(mT, wout_ref[...], _TN_DIMS, preferred_element_type=jnp.float32)
    x1 = x_ref[0] + y
    x1_ref[0] = x1

    ms = jnp.mean(x1 * x1, axis=-1, keepdims=True)
    h2 = x1 * lax.rsqrt(ms + EPS) * gffn_ref[...]
    h2_ref[0] = _pack_row_halves(h2)

    logits = lax.dot_general(wrT_ref[...], h2, _NT_DIMS, precision=lax.Precision.HIGHEST,
                             preferred_element_type=jnp.float32) + br_ref[...]
    eidx = lax.broadcasted_iota(jnp.int32, logits.shape, 0).astype(jnp.float32)
    work = logits
    vals, ids, sels = [], [], []
    for _ in range(TOP_K):
        mx = jnp.max(work, axis=0, keepdims=True)
        idx = jnp.min(jnp.where(work == mx, eidx, float(N_EXPERTS)), axis=0, keepdims=True)
        sel = eidx == idx
        vals.append(mx)
        ids.append(idx)
        sels.append(sel)
        work = jnp.where(sel, -jnp.inf, work)
    exps = [jnp.exp(v - vals[0]) for v in vals]
    denom = exps[0] + exps[1] + exps[2] + exps[3]
    ids_ref[...] = jnp.concatenate(ids, axis=0).astype(jnp.int32)
    wts_ref[...] = jnp.concatenate([e / denom for e in exps], axis=0)

    onehot = [s.astype(jnp.float32) for s in sels]
    hits = onehot[0] + onehot[1] + onehot[2] + onehot[3]
    ti = lax.broadcasted_iota(jnp.int32, (tm, tm), 0)
    tj = lax.broadcasted_iota(jnp.int32, (tm, tm), 1)
    upper = (ti < tj).astype(jnp.bfloat16)
    prefix = jnp.dot(hits.astype(jnp.bfloat16), upper, preferred_element_type=jnp.float32)
    base = prefix + carry_ref[...]
    ranks = [jnp.sum(oh * base, axis=0, keepdims=True) for oh in onehot]
    rank_ref[...] = jnp.concatenate(ranks, axis=0).astype(jnp.int32)
    carry_ref[...] = carry_ref[...] + jnp.sum(hits, axis=1, keepdims=True)
    cnt_ref[...] = carry_ref[...].astype(jnp.int32)


def _post(attnT, uT, vsT, gaT, gbT, x, wsT, bs, wpaT, wpbT, wout, gffn, wrT, br):
    B, S, _ = x.shape
    tm = TOKEN_TILE
    nt = S // tm
    T = B * S
    fm = lambda rows: pl.BlockSpec((1, rows, tm), lambda b, i: (b, 0, i))
    rowm = pl.BlockSpec((1, tm, D_MODEL), lambda b, i: (b, i, 0))
    c2 = lambda b, i: (0, 0)
    c3 = lambda b, i: (0, 0, 0)
    tokT = pl.BlockSpec((TOP_K, tm), lambda b, i: (0, b * nt + i))
    return pl.pallas_call(
        _post_kernel,
        grid=(B, nt),
        in_specs=[
            fm(Q_W), fm(SGU_W), fm(SGU_W), fm(D_MODEL), fm(D_MODEL), rowm,
            pl.BlockSpec((SGU_GROUPS, CHUNK, CHUNK), c3),
            pl.BlockSpec((SGU_GROUPS, 1, CHUNK), c3),
            pl.BlockSpec((D_MODEL, Q_W), c2),
            pl.BlockSpec((D_MODEL, SGU_W), c2),
            pl.BlockSpec((D_MODEL, D_MODEL), c2),
            pl.BlockSpec((1, D_MODEL), c2),
            pl.BlockSpec((N_EXPERTS, D_MODEL), c2),
            pl.BlockSpec((N_EXPERTS, 1), c2),
        ],
        out_specs=[rowm, pl.BlockSpec((1, tm, PACKED_W), lambda b, i: (b, i, 0)),
                   tokT, tokT, tokT, pl.BlockSpec((N_EXPERTS, 1), c2)],
        out_shape=[
            jax.ShapeDtypeStruct((B, S, D_MODEL), jnp.float32),
            jax.ShapeDtypeStruct((B, S, PACKED_W), jnp.int32),
            jax.ShapeDtypeStruct((TOP_K, T), jnp.int32),
            jax.ShapeDtypeStruct((TOP_K, T), jnp.float32),
            jax.ShapeDtypeStruct((TOP_K, T), jnp.int32),
            jax.ShapeDtypeStruct((N_EXPERTS, 1), jnp.int32),
        ],
        scratch_shapes=[pltpu.VMEM((N_EXPERTS, 1), jnp.float32)],
        compiler_params=_compiler_params(("arbitrary", "arbitrary"), 56),
        name="post",
    )(attnT, uT, vsT, gaT, gbT, x, wsT, bs, wpaT, wpbT, wout, gffn, wrT, br)


def _sc_kernel(name, body, out_type, n_items, row_shape, row_dtype):
    info = plsc.get_sparse_core_info()
    n_workers = info.num_cores * info.num_subcores
    per_worker = n_items // n_workers
    n_windows = per_worker // SC_WINDOW
    assert per_worker * n_workers == n_items and n_windows * SC_WINDOW == per_worker
    assert n_windows % 2 == 0

    def wrapped(*refs):
        wid = lax.axis_index("subcore") * info.num_cores + lax.axis_index("core")
        body(wid * per_worker, n_windows, *refs)

    return pl.kernel(
        wrapped,
        name=name,
        out_type=out_type,
        mesh=plsc.VectorSubcoreMesh(core_axis_name="core", subcore_axis_name="subcore"),
        scratch_types=[
            pltpu.VMEM((SC_WINDOW,), jnp.int32),
            pltpu.VMEM((SC_WINDOW,), jnp.int32),
            pltpu.VMEM((SC_WINDOW,) + row_shape, row_dtype),
            pltpu.VMEM((SC_WINDOW,) + row_shape, row_dtype),
            pltpu.SemaphoreType.DMA,
            pltpu.SemaphoreType.DMA,
        ],
    )


def _sc_gather_rows(table, idx):
    n, d, win = idx.shape[0], table.shape[1], SC_WINDOW

    def body(base, n_windows, table_hbm, idx_hbm, out_hbm, idx0, idx1, rows0, rows1, sem0, sem1):
        def window(c):
            return pl.ds(pl.multiple_of(base + c * win, win), win)

        def fetch(c, idx_v, rows_v, sem):
            pltpu.sync_copy(idx_hbm.at[window(c)], idx_v)
            pltpu.async_copy(table_hbm.at[idx_v], rows_v, sem)

        def drain(c, idx_v, rows_v, sem):
            pltpu.make_async_copy(table_hbm.at[idx_v], rows_v, sem).wait()
            pltpu.sync_copy(rows_v, out_hbm.at[window(c)])

        fetch(0, idx0, rows0, sem0)

        @pl.loop(0, n_windows, step=2)
        def _(c):
            fetch(c + 1, idx1, rows1, sem1)
            drain(c, idx0, rows0, sem0)

            @pl.when(c + 2 < n_windows)
            def _():
                fetch(c + 2, idx0, rows0, sem0)

            drain(c + 1, idx1, rows1, sem1)

    out_type = jax.ShapeDtypeStruct((n, d), table.dtype)
    return _sc_kernel("sc_gather_rows", body, out_type, n, (d,), table.dtype)(table, idx)


def _sc_scatter_rows(src, dest, n_rows):
    T, d = src.shape
    win = SC_WINDOW

    def body(base, n_windows, src_hbm, dest_hbm, out_hbm, idx0, idx1, rows0, rows1, sem0, sem1):
        def window(c, slot=0):
            return pl.ds(pl.multiple_of(slot * T + base + c * win, win), win)

        def fetch(c, rows_v, sem):
            pltpu.async_copy(src_hbm.at[window(c)], rows_v, sem)

        def push(c, rows_v, sem):
            pltpu.make_async_copy(src_hbm.at[window(c)], rows_v, sem).wait()
            for k in range(TOP_K):
                idx_v = idx0 if k % 2 == 0 else idx1
                pltpu.sync_copy(dest_hbm.at[window(c, k)], idx_v)
                pltpu.sync_copy(rows_v, out_hbm.at[idx_v])

        fetch(0, rows0, sem0)

        @pl.loop(0, n_windows, step=2)
        def _(c):
            fetch(c + 1, rows1, sem1)
            push(c, rows0, sem0)

            @pl.when(c + 2 < n_windows)
            def _():
                fetch(c + 2, rows0, sem0)

            push(c + 1, rows1, sem1)

    out_type = jax.ShapeDtypeStruct((n_rows, d), src.dtype)
    return _sc_kernel("sc_scatter_rows", body, out_type, T, (d,), src.dtype)(
        src, dest.reshape(TOP_K * T))


def _combine_dense_kernel(w_ref, x1_ref, gfin_ref, y_ref, o_ref):
    w = w_ref[...]
    moe = w[:, 0:1] * _unpack_row_halves(y_ref[0])
    for k in range(1, TOP_K):
        moe = moe + w[:, k:k + 1] * _unpack_row_halves(y_ref[k])
    x2 = x1_ref[...] + moe
    ms = jnp.mean(x2 * x2, axis=-1, keepdims=True)
    o_ref[...] = x2 * lax.rsqrt(ms + EPS) * gfin_ref[...]


def _combine_dense(wts, x1, gfin, y_tok):
    T = x1.shape[0]
    tt = COMBINE_TILE
    return pl.pallas_call(
        _combine_dense_kernel,
        grid=(T // tt,),
        in_specs=[
            pl.BlockSpec((tt, TOP_K), lambda i: (i, 0)),
            pl.BlockSpec((tt, D_MODEL), lambda i: (i, 0)),
            pl.BlockSpec((1, D_MODEL), lambda i: (0, 0)),
            pl.BlockSpec((TOP_K, tt, PACKED_W), lambda i: (0, i, 0)),
        ],
        out_specs=pl.BlockSpec((tt, D_MODEL), lambda i: (i, 0)),
        out_shape=jax.ShapeDtypeStruct((T, D_MODEL), jnp.float32),
        compiler_params=_compiler_params(("arbitrary",), 32),
        name="combine_dense",
    )(wts, x1, gfin, y_tok)


def _gate_up_prep_kernel(w_ref, o_ref):
    half = _GU_BLOCK // 2
    i = lax.broadcasted_iota(jnp.int32, (_GU_BLOCK, _GU_BLOCK), 0)
    j = lax.broadcasted_iota(jnp.int32, (_GU_BLOCK, _GU_BLOCK), 1)
    perm = (i == jnp.where(j < half, 2 * j, 2 * (j - half) + 1)).astype(jnp.bfloat16)
    for c in range(2 * D_FF // _GU_BLOCK):
        cols = slice(c * _GU_BLOCK, (c + 1) * _GU_BLOCK)
        blk = w_ref[0, :, cols].astype(jnp.bfloat16)
        o_ref[0, :, cols] = jnp.dot(blk, perm, preferred_element_type=jnp.float32).astype(jnp.bfloat16)


def _gate_up_prep(w_gate_up):
    spec = pl.BlockSpec((1, D_MODEL, 2 * D_FF), lambda e: (e, 0, 0))
    return pl.pallas_call(
        _gate_up_prep_kernel,
        grid=(N_EXPERTS,),
        in_specs=[spec],
        out_specs=spec,
        out_shape=jax.ShapeDtypeStruct(w_gate_up.shape, jnp.bfloat16),
        compiler_params=_compiler_params(("arbitrary",), 48),
        name="gate_up_prep",
    )(w_gate_up)


def _split_gate_up(gu):
    half = _GU_BLOCK // 2
    n = gu.shape[1] // _GU_BLOCK
    glu = jnp.concatenate([gu[:, c * _GU_BLOCK:c * _GU_BLOCK + half] for c in range(n)], axis=1)
    lin = jnp.concatenate([gu[:, c * _GU_BLOCK + half:(c + 1) * _GU_BLOCK] for c in range(n)], axis=1)
    return glu, lin


def _experts_kernel(te_ref, nv_ref, x_ref, wgu_ref, bgu_ref, wd_ref, bd_ref, y_ref):
    @pl.when(pl.program_id(0) < nv_ref[0])
    def _():
        x = _unpack_row_halves(x_ref[...]).astype(jnp.bfloat16)
        gu = jnp.dot(x, wgu_ref[0], preferred_element_type=jnp.float32) + bgu_ref[0]
        glu, lin = _split_gate_up(gu)
        glu = jnp.minimum(glu, SWIGLU_LIMIT)
        lin = jnp.clip(lin, -SWIGLU_LIMIT, SWIGLU_LIMIT)
        a = glu * jax.nn.sigmoid(SWIGLU_ALPHA * glu) * (lin + 1.0)
        y = jnp.dot(a.astype(jnp.bfloat16), wd_ref[0],
                    preferred_element_type=jnp.float32) + bd_ref[0]
        y_ref[...] = _pack_row_halves(y)


def _experts(tile_expert, n_valid, x_sorted, wgu, bgu, wd, bd):
    n_rows = x_sorted.shape[0]
    tr = EXPERT_TILE
    n_tiles = n_rows // tr

    def row_map(j, te, nv):
        return (jnp.minimum(j, nv[0] - 1), 0)

    def exp_map(j, te, nv):
        return (te[j], 0, 0)

    grid_spec = pltpu.PrefetchScalarGridSpec(
        num_scalar_prefetch=2,
        grid=(n_tiles,),
        in_specs=[
            pl.BlockSpec((tr, PACKED_W), row_map),
            pl.BlockSpec((1, D_MODEL, 2 * D_FF), exp_map),
            pl.BlockSpec((1, 1, 2 * D_FF), exp_map),
            pl.BlockSpec((1, D_FF, D_MODEL), exp_map),
            pl.BlockSpec((1, 1, D_MODEL), exp_map),
        ],
        out_specs=pl.BlockSpec((tr, PACKED_W), row_map),
    )
    return pl.pallas_call(
        _experts_kernel,
        grid_spec=grid_spec,
        out_shape=jax.ShapeDtypeStruct((n_rows, PACKED_W), jnp.int32),
        compiler_params=_compiler_params(("arbitrary",), 56),
        name="experts",
    )(tile_expert, n_valid, x_sorted, wgu, bgu, wd, bd)


def _rope_tables(S, gain, scale):
    t = jnp.arange(S, dtype=jnp.int32)
    r = (t // GRID_W).astype(jnp.float32)
    c = (t % GRID_W).astype(jnp.float32)
    inv = jnp.float32(ROPE_THETA) ** (
        -jnp.arange(0, ROPE_AXIS_DIM, 2, dtype=jnp.float32) / ROPE_AXIS_DIM)
    ang = jnp.concatenate([r[None, :] * inv[:, None], c[None, :] * inv[:, None]], axis=0)
    cos = jnp.repeat(jnp.cos(ang), 2, axis=0)
    sin = jnp.repeat(jnp.sin(ang), 2, axis=0)
    sign = jnp.where(jnp.arange(HEAD_DIM) % 2 == 0, -1.0, 1.0).astype(jnp.float32)
    g = gain.astype(jnp.float32) * scale
    g_swapped = g.reshape(HEAD_DIM // 2, 2)[:, ::-1].reshape(HEAD_DIM)
    return g[:, None] * cos, (g_swapped * sign)[:, None] * sin


def _prepare_weights(norm_mix_g, w_in, q_norm_g, k_norm_g, sgu_norm_g, w_spatial, b_spatial,
                     w_proj_attn, w_proj_sgu, w_out, norm_ffn_g, w_router, b_router,
                     w_gate_up, b_gate_up, w_down, b_down, norm_final_g):
    bf = jnp.bfloat16
    l = 0
    return dict(
        gmix=norm_mix_g[l][None, :],
        w_inT=w_in[l].T.astype(bf),
        q_gain=q_norm_g[l], k_gain=k_norm_g[l],
        gs=jnp.broadcast_to(sgu_norm_g[l][:, None], (SGU_W, _LANES)),
        wsT=jnp.swapaxes(w_spatial[l], 1, 2).astype(bf),
        bs=b_spatial[l][:, None, :],
        wpaT=w_proj_attn[l].T.astype(bf),
        wpbT=w_proj_sgu[l].T.astype(bf),
        wout=w_out[l].astype(bf),
        gffn=norm_ffn_g[l][None, :],
        wrT=w_router[l].T,
        br=b_router[l][:, None],
        wgu=_gate_up_prep(w_gate_up[l]),
        bgu=b_gate_up[l].reshape(N_EXPERTS, -1, _GU_BLOCK // 2, 2).transpose(0, 1, 3, 2)
        .reshape(N_EXPERTS, 1, 2 * D_FF),
        wd=w_down[l].astype(bf),
        bd=b_down[l][:, None, :],
        gfin=norm_final_g[None, :],
    )


def _trunk(x, w):
    B, S, _ = x.shape
    T = B * S
    qa, qb = _rope_tables(S, w["q_gain"], math.log2(math.e) / math.sqrt(HEAD_DIM))
    ka, kb = _rope_tables(S, w["k_gain"], 1.0)

    qT, k, vT, uT, vsT, gaT, gbT = _in_proj(x, w["gmix"], w["w_inT"], qa, qb, ka, kb, w["gs"])
    attnT = _attention(qT, k, vT)
    x1, h2, ids, wts, rank, counts = _post(
        attnT, uT, vsT, gaT, gbT, x, w["wsT"], w["bs"], w["wpaT"], w["wpbT"], w["wout"],
        w["gffn"], w["wrT"], w["br"])

    tr = EXPERT_TILE
    counts = counts[:, 0]
    padded = (counts + tr - 1) // tr * tr
    ends = jnp.cumsum(padded)
    starts = ends - padded
    n_tiles = (TOP_K * T) // tr + N_EXPERTS
    n_rows = n_tiles * tr
    dest = rank
    for e in range(N_EXPERTS):
        dest = dest + jnp.where(ids == e, starts[e], 0)
    tile_start = jnp.arange(n_tiles, dtype=jnp.int32) * tr
    tile_expert = jnp.minimum(
        jnp.sum((tile_start[:, None] >= ends[None, :]).astype(jnp.int32), axis=1), N_EXPERTS - 1)
    n_valid = (ends[-1] // tr).astype(jnp.int32)[None]

    x_sorted = _sc_scatter_rows(h2.reshape(T, PACKED_W), dest, n_rows)
    y_sorted = _experts(tile_expert, n_valid, x_sorted, w["wgu"], w["bgu"], w["wd"], w["bd"])
    y_tok = _sc_gather_rows(y_sorted, dest.reshape(TOP_K * T)).reshape(TOP_K, T, PACKED_W)
    out = _combine_dense(wts.T, x1.reshape(T, D_MODEL), w["gfin"], y_tok)
    return out.reshape(B, S, D_MODEL)


def kernel(x_prompt, x_sample, norm_mix_g, w_in, q_norm_g, k_norm_g, sgu_norm_g, w_spatial,
           b_spatial, w_proj_attn, w_proj_sgu, w_out, norm_ffn_g, w_router, b_router,
           w_gate_up, b_gate_up, w_down, b_down, norm_final_g):
    w = _prepare_weights(norm_mix_g, w_in, q_norm_g, k_norm_g, sgu_norm_g, w_spatial,
                         b_spatial, w_proj_attn, w_proj_sgu, w_out, norm_ffn_g, w_router,
                         b_router, w_gate_up, b_gate_up, w_down, b_down, norm_final_g)
    return (_trunk(x_prompt, w), _trunk(x_sample, w))
```

```python
import functools
import math

import jax
import jax.numpy as jnp
from jax import lax
from jax.experimental import pallas as pl
from jax.experimental.pallas import tpu as pltpu
from jax.experimental.pallas import tpu_sc as plsc

D_MODEL = 1024
GRID_W = 64
N_HEADS = 8
N_KV_HEADS = 2
HEAD_DIM = 64
GQA_GROUP = N_HEADS // N_KV_HEADS
Q_W = N_HEADS * HEAD_DIM
KV_W = N_KV_HEADS * HEAD_DIM
ROPE_AXIS_DIM = HEAD_DIM // 2
ROPE_THETA = 10000.0
SGU_GROUPS = 8
SGU_W = D_MODEL // 2
SGU_GROUP_DIM = SGU_W // SGU_GROUPS
CHUNK = 128
IN_W = Q_W + 2 * KV_W + 2 * SGU_W + 2 * D_MODEL
N_EXPERTS = 32
TOP_K = 4
D_FF = D_MODEL
SWIGLU_LIMIT = 7.0
SWIGLU_ALPHA = 1.702
EPS = 1e-6

_Q0, _K0, _V0 = 0, Q_W, Q_W + KV_W
_U0 = Q_W + 2 * KV_W
_VS0 = _U0 + SGU_W
_GA0 = _VS0 + SGU_W
_GB0 = _GA0 + D_MODEL

TOKEN_TILE = 512
Q_TILE = 1024
KEY_TILE = TOKEN_TILE
EXPERT_TILE = 512
COMBINE_TILE = 512
PACKED_W = D_MODEL // 2
SC_WINDOW = 64

_LANES = 128
_BF16_SUBLANES = 16
_SUM_ROWS = _BF16_SUBLANES
_GU_BLOCK = 256
_NEG_BIG = -1e30
_HIGH_HALF_MASK = 0xFFFF0000
_MAX_LAGGED_EXPONENT = 80.0
_KEY_TILES_PER_TRIP = 2
_MIB = 1024 * 1024

_NT_DIMS = (((1,), (1,)), ((), ()))
_TN_DIMS = (((0,), (0,)), ((), ()))


def _compiler_params(semantics, vmem_mib):
    return pltpu.CompilerParams(
        dimension_semantics=semantics, vmem_limit_bytes=vmem_mib * _MIB)


def _swap_adjacent_rows(x):
    n = x.shape[0]
    row = lax.broadcasted_iota(jnp.int32, x.shape, 0)
    nxt = pltpu.roll(x, n - 1, 0)
    prv = pltpu.roll(x, 1, 0)
    return jnp.where((row & 1) == 0, nxt, prv)


def _head_norm_rope(z, tab_a, tab_b, n_heads):
    tm = z.shape[1]
    z3 = z.reshape(n_heads, HEAD_DIM, tm)
    ms = jnp.mean(z3 * z3, axis=1, keepdims=True)
    r = lax.rsqrt(ms + EPS)
    zs = _swap_adjacent_rows(z).reshape(n_heads, HEAD_DIM, tm)
    out = (z3 * tab_a[None] + zs * tab_b[None]) * r
    return out.reshape(n_heads * HEAD_DIM, tm)


def _gelu(x):
    return 0.5 * x * (1.0 + lax.erf(x * (1.0 / math.sqrt(2.0))))


def _pack_row_halves(x):
    half = x.shape[1] // 2

    def bf16_bits(v):
        return lax.bitcast_convert_type(v.astype(jnp.bfloat16).astype(jnp.float32), jnp.uint32)

    word = (bf16_bits(x[:, :half]) >> 16) | (bf16_bits(x[:, half:]) & jnp.uint32(_HIGH_HALF_MASK))
    return lax.bitcast_convert_type(word, jnp.int32)


def _unpack_row_halves(p):
    word = lax.bitcast_convert_type(p, jnp.uint32)
    lo = lax.bitcast_convert_type(word << 16, jnp.float32)
    hi = lax.bitcast_convert_type(word & jnp.uint32(_HIGH_HALF_MASK), jnp.float32)
    return jnp.concatenate([lo, hi], axis=1)


def _tile_lanes(x, reps):
    return jnp.concatenate([x] * reps, axis=1) if reps > 1 else x


def _in_proj_kernel(x_ref, gmix_ref, w_ref, qa_ref, qb_ref, ka_ref, kb_ref, gs_ref,
                    qT_ref, k_ref, vT_ref, uT_ref, vsT_ref, gaT_ref, gbT_ref):
    tm = x_ref.shape[1]
    x = x_ref[0]
    ms = jnp.mean(x * x, axis=-1, keepdims=True)
    h = (x * lax.rsqrt(ms + EPS) * gmix_ref[...]).astype(jnp.bfloat16)

    def proj(r0, rows):
        return lax.dot_general(w_ref[r0:r0 + rows, :], h, _NT_DIMS,
                               preferred_element_type=jnp.float32)

    zq = proj(_Q0, Q_W)
    qT_ref[0] = _head_norm_rope(zq, qa_ref[...], qb_ref[...], N_HEADS).astype(jnp.bfloat16)

    zkv = proj(_K0, 2 * KV_W)
    kT = _head_norm_rope(zkv[:KV_W], ka_ref[...], kb_ref[...], N_KV_HEADS)
    k_ref[0] = kT.T.astype(jnp.bfloat16)
    vT_ref[0, 0] = zkv[KV_W:].astype(jnp.bfloat16)

    uT_ref[0] = _gelu(proj(_U0, SGU_W)).astype(jnp.bfloat16)

    vs = _gelu(proj(_VS0, SGU_W))
    vms = jnp.mean(vs * vs, axis=0, keepdims=True)
    gs = _tile_lanes(gs_ref[...], tm // _LANES)
    vsT_ref[0] = (vs * lax.rsqrt(vms + EPS) * gs).astype(jnp.bfloat16)

    gaT_ref[0] = jax.nn.sigmoid(proj(_GA0, D_MODEL)).astype(jnp.bfloat16)
    gbT_ref[0] = jax.nn.sigmoid(proj(_GB0, D_MODEL)).astype(jnp.bfloat16)


def _in_proj(x, gmix, w_inT, qa, qb, ka, kb, gs):
    B, S, _ = x.shape
    tm = TOKEN_TILE
    nt = S // tm
    bf = jnp.bfloat16
    const2 = lambda b, i: (0, 0)
    tab = pl.BlockSpec((HEAD_DIM, tm), lambda b, i: (0, i))
    fm = lambda rows: pl.BlockSpec((1, rows, tm), lambda b, i: (b, 0, i))
    return pl.pallas_call(
        _in_proj_kernel,
        grid=(B, nt),
        in_specs=[
            pl.BlockSpec((1, tm, D_MODEL), lambda b, i: (b, i, 0)),
            pl.BlockSpec((1, D_MODEL), const2),
            pl.BlockSpec((IN_W, D_MODEL), const2),
            tab, tab, tab, tab,
            pl.BlockSpec((SGU_W, _LANES), const2),
        ],
        out_specs=[
            fm(Q_W),
            pl.BlockSpec((1, tm, KV_W), lambda b, i: (b, i, 0)),
            pl.BlockSpec((1, 1, KV_W, tm), lambda b, i: (b, i, 0, 0)),
            fm(SGU_W), fm(SGU_W), fm(D_MODEL), fm(D_MODEL),
        ],
        out_shape=[
            jax.ShapeDtypeStruct((B, Q_W, S), bf),
            jax.ShapeDtypeStruct((B, S, KV_W), bf),
            jax.ShapeDtypeStruct((B, nt, KV_W, tm), bf),
            jax.ShapeDtypeStruct((B, SGU_W, S), bf),
            jax.ShapeDtypeStruct((B, SGU_W, S), bf),
            jax.ShapeDtypeStruct((B, D_MODEL, S), bf),
            jax.ShapeDtypeStruct((B, D_MODEL, S), bf),
        ],
        compiler_params=_compiler_params(("arbitrary", "arbitrary"), 56),
        name="in_proj",
    )(x, gmix, w_inT, qa, qb, ka, kb, gs)


def _attention_kernel(qT_ref, k_ref, vT_ref, o_ref, m_ref, gap_ref, acc_ref):
    tq = qT_ref.shape[2]
    n_kt = vT_ref.shape[1]
    tk = vT_ref.shape[3]
    nq = GQA_GROUP * tq
    ones_rows = (lax.broadcasted_iota(jnp.int32, (_SUM_ROWS, tk), 0) == 0).astype(jnp.bfloat16)

    zeros = jnp.zeros((HEAD_DIM, nq), jnp.bfloat16)
    qp = []
    for kv in range(N_KV_HEADS):
        heads = [qT_ref[0, (kv * GQA_GROUP + g) * HEAD_DIM:(kv * GQA_GROUP + g + 1) * HEAD_DIM, :]
                 for g in range(GQA_GROUP)]
        qk = jnp.concatenate(heads, axis=1)
        qp.append(jnp.concatenate([qk, zeros] if kv == 0 else [zeros, qk], axis=0))

    def key_tile(ki):
        return k_ref[0, pl.ds(pl.multiple_of(ki * tk, tk), tk), :]

    def value_rows(ki, kv):
        vt = vT_ref[0, ki]
        return jnp.concatenate([vt[kv * HEAD_DIM:(kv + 1) * HEAD_DIM, :], ones_rows], axis=0)

    acc_ref[...] = jnp.zeros(acc_ref.shape, jnp.float32)
    gap_ref[...] = jnp.zeros(gap_ref.shape, jnp.float32)
    first_keys = k_ref[0, 0:_BF16_SUBLANES, :]
    for kv in range(N_KV_HEADS):
        s0 = jnp.dot(first_keys, qp[kv], preferred_element_type=jnp.float32)
        m_ref[kv] = jnp.max(s0, axis=0, keepdims=True)

    def fast_tile(ki):
        kt = key_tile(ki)
        for kv in range(N_KV_HEADS):
            s = jnp.dot(kt, qp[kv], preferred_element_type=jnp.float32)
            m_old = m_ref[kv]
            p = jnp.exp2(s - m_old).astype(jnp.bfloat16)
            mt = jnp.max(s, axis=0, keepdims=True)
            pv = jnp.dot(value_rows(ki, kv), p, preferred_element_type=jnp.float32)
            m_new = jnp.maximum(m_old, mt)
            acc_ref[kv] = (acc_ref[kv] + pv) * jnp.exp2(m_old - m_new)
            m_ref[kv] = m_new
            gap_ref[kv] = jnp.maximum(gap_ref[kv], mt - m_old)

    per_trip = math.gcd(n_kt, _KEY_TILES_PER_TRIP)

    def fast_group(j, carry):
        for t in range(per_trip):
            fast_tile(per_trip * j + t)
        return carry

    lax.fori_loop(0, n_kt // per_trip, fast_group, 0)

    @pl.when(jnp.max(gap_ref[...]) > _MAX_LAGGED_EXPONENT)
    def _():
        m_ref[...] = jnp.full(m_ref.shape, _NEG_BIG, jnp.float32)
        acc_ref[...] = jnp.zeros(acc_ref.shape, jnp.float32)

        def robust_tile(ki, carry):
            kt = key_tile(ki)
            for kv in range(N_KV_HEADS):
                s = jnp.dot(kt, qp[kv], preferred_element_type=jnp.float32)
                m_old = m_ref[kv]
                m_new = jnp.maximum(m_old, jnp.max(s, axis=0, keepdims=True))
                p = jnp.exp2(s - m_new).astype(jnp.bfloat16)
                pv = jnp.dot(value_rows(ki, kv), p, preferred_element_type=jnp.float32)
                acc_ref[kv] = jnp.exp2(m_old - m_new) * acc_ref[kv] + pv
                m_ref[kv] = m_new
            return carry

        lax.fori_loop(0, n_kt, robust_tile, 0)

    for kv in range(N_KV_HEADS):
        acc = acc_ref[kv]
        o = acc[:HEAD_DIM] / acc[HEAD_DIM:HEAD_DIM + 1]
        for g in range(GQA_GROUP):
            r0 = (kv * GQA_GROUP + g) * HEAD_DIM
            o_ref[0, r0:r0 + HEAD_DIM, :] = o[:, g * tq:(g + 1) * tq].astype(jnp.bfloat16)


def _attention(qT, k, vT):
    B, _, S = qT.shape
    tq = Q_TILE
    nq = GQA_GROUP * tq
    n_kt, tk = vT.shape[1], vT.shape[3]
    return pl.pallas_call(
        _attention_kernel,
        grid=(B, S // tq),
        in_specs=[
            pl.BlockSpec((1, Q_W, tq), lambda b, i: (b, 0, i)),
            pl.BlockSpec((1, S, KV_W), lambda b, i: (b, 0, 0)),
            pl.BlockSpec((1, n_kt, KV_W, tk), lambda b, i: (b, 0, 0, 0)),
        ],
        out_specs=pl.BlockSpec((1, Q_W, tq), lambda b, i: (b, 0, i)),
        out_shape=jax.ShapeDtypeStruct((B, Q_W, S), jnp.bfloat16),
        scratch_shapes=[
            pltpu.VMEM((N_KV_HEADS, 1, nq), jnp.float32),
            pltpu.VMEM((N_KV_HEADS, 1, nq), jnp.float32),
            pltpu.VMEM((N_KV_HEADS, HEAD_DIM + _SUM_ROWS, nq), jnp.float32),
        ],
        compiler_params=_compiler_params(("arbitrary", "arbitrary"), 56),
        name="attention",
    )(qT, k, vT)


def _post_kernel(attnT_ref, uT_ref, vsT_ref, gaT_ref, gbT_ref, x_ref,
                 wsT_ref, bs_ref, wpaT_ref, wpbT_ref, wout_ref, gffn_ref, wrT_ref, br_ref,
                 x1_ref, h2_ref, ids_ref, wts_ref, rank_ref, cnt_ref, carry_ref):
    tm = x_ref.shape[1]
    n_chunks = tm // CHUNK
    first = jnp.logical_and(pl.program_id(0) == 0, pl.program_id(1) == 0)

    @pl.when(first)
    def _():
        carry_ref[...] = jnp.zeros(carry_ref.shape, jnp.float32)

    gate_rows = []
    for g in range(SGU_GROUPS):
        r0 = g * SGU_GROUP_DIM
        vs_g = vsT_ref[0, r0:r0 + SGU_GROUP_DIM, :]
        lhs = jnp.concatenate(
            [vs_g[:, c * CHUNK:(c + 1) * CHUNK] for c in range(n_chunks)], axis=0)
        mixed = jnp.dot(lhs, wsT_ref[g], preferred_element_type=jnp.float32)
        mixed = mixed + bs_ref[g]
        mixedT = jnp.concatenate(
            [mixed[c * SGU_GROUP_DIM:(c + 1) * SGU_GROUP_DIM] for c in range(n_chunks)], axis=1)
        u_g = uT_ref[0, r0:r0 + SGU_GROUP_DIM, :].astype(jnp.float32)
        gate_rows.append((u_g * mixedT).astype(jnp.bfloat16))
    gateT = jnp.concatenate(gate_rows, axis=0)

    paT = jnp.dot(wpaT_ref[...], attnT_ref[0], preferred_element_type=jnp.float32)
    pbT = jnp.dot(wpbT_ref[...], gateT, preferred_element_type=jnp.float32)
    mT = (gaT_ref[0].astype(jnp.float32) * paT
          + gbT_ref[0].astype(jnp.float32) * pbT).astype(jnp.bfloat16)
    y = lax.dot_general(mT, wout_ref[...], _TN_DIMS, preferred_element_type=jnp.float32)
    x1 = x_ref[0] + y
    x1_ref[0] = x1

    ms = jnp.mean(x1 * x1, axis=-1, keepdims=True)
    h2 = x1 * lax.rsqrt(ms + EPS) * gffn_ref[...]
    h2_ref[0] = _pack_row_halves(h2)

    logits = lax.dot_general(wrT_ref[...], h2, _NT_DIMS, precision=lax.Precision.HIGHEST,
                             preferred_element_type=jnp.float32) + br_ref[...]
    eidx = lax.broadcasted_iota(jnp.int32, logits.shape, 0).astype(jnp.float32)
    work = logits
    vals, ids, sels = [], [], []
    for _ in range(TOP_K):
        mx = jnp.max(work, axis=0, keepdims=True)
        idx = jnp.min(jnp.where(work == mx, eidx, float(N_EXPERTS)), axis=0, keepdims=True)
        sel = eidx == idx
        vals.append(mx)
        ids.append(idx)
        sels.append(sel)
        work = jnp.where(sel, -jnp.inf, work)
    exps = [jnp.exp(v - vals[0]) for v in vals]
    denom = exps[0] + exps[1] + exps[2] + exps[3]
    ids_ref[...] = jnp.concatenate(ids, axis=0).astype(jnp.int32)
    wts_ref[...] = jnp.concatenate([e / denom for e in exps], axis=0)

    onehot = [s.astype(jnp.float32) for s in sels]
    hits = onehot[0] + onehot[1] + onehot[2] + onehot[3]
    ti = lax.broadcasted_iota(jnp.int32, (tm, tm), 0)
    tj = lax.broadcasted_iota(jnp.int32, (tm, tm), 1)
    upper = (ti < tj).astype(jnp.bfloat16)
    prefix = jnp.dot(hits.astype(jnp.bfloat16), upper, preferred_element_type=jnp.float32)
    base = prefix + carry_ref[...]
    ranks = [jnp.sum(oh * base, axis=0, keepdims=True) for oh in onehot]
    rank_ref[...] = jnp.concatenate(ranks, axis=0).astype(jnp.int32)
    carry_ref[...] = carry_ref[...] + jnp.sum(hits, axis=1, keepdims=True)
    cnt_ref[...] = carry_ref[...].astype(jnp.int32)


def _post(attnT, uT, vsT, gaT, gbT, x, wsT, bs, wpaT, wpbT, wout, gffn, wrT, br):
    B, S, _ = x.shape
    tm = TOKEN_TILE
    nt = S // tm
    T = B * S
    fm = lambda rows: pl.BlockSpec((1, rows, tm), lambda b, i: (b, 0, i))
    rowm = pl.BlockSpec((1, tm, D_MODEL), lambda b, i: (b, i, 0))
    c2 = lambda b, i: (0, 0)
    c3 = lambda b, i: (0, 0, 0)
    tokT = pl.BlockSpec((TOP_K, tm), lambda b, i: (0, b * nt + i))
    return pl.pallas_call(
        _post_kernel,
        grid=(B, nt),
        in_specs=[
            fm(Q_W), fm(SGU_W), fm(SGU_W), fm(D_MODEL), fm(D_MODEL), rowm,
            pl.BlockSpec((SGU_GROUPS, CHUNK, CHUNK), c3),
            pl.BlockSpec((SGU_GROUPS, 1, CHUNK), c3),
            pl.BlockSpec((D_MODEL, Q_W), c2),
            pl.BlockSpec((D_MODEL, SGU_W), c2),
            pl.BlockSpec((D_MODEL, D_MODEL), c2),
            pl.BlockSpec((1, D_MODEL), c2),
            pl.BlockSpec((N_EXPERTS, D_MODEL), c2),
            pl.BlockSpec((N_EXPERTS, 1), c2),
        ],
        out_specs=[rowm, pl.BlockSpec((1, tm, PACKED_W), lambda b, i: (b, i, 0)),
                   tokT, tokT, tokT, pl.BlockSpec((N_EXPERTS, 1), c2)],
        out_shape=[
            jax.ShapeDtypeStruct((B, S, D_MODEL), jnp.float32),
            jax.ShapeDtypeStruct((B, S, PACKED_W), jnp.int32),
            jax.ShapeDtypeStruct((TOP_K, T), jnp.int32),
            jax.ShapeDtypeStruct((TOP_K, T), jnp.float32),
            jax.ShapeDtypeStruct((TOP_K, T), jnp.int32),
            jax.ShapeDtypeStruct((N_EXPERTS, 1), jnp.int32),
        ],
        scratch_shapes=[pltpu.VMEM((N_EXPERTS, 1), jnp.float32)],
        compiler_params=_compiler_params(("arbitrary", "arbitrary"), 56),
        name="post",
    )(attnT, uT, vsT, gaT, gbT, x, wsT, bs, wpaT, wpbT, wout, gffn, wrT, br)


def _sc_kernel(name, body, out_type, n_items, row_shape, row_dtype):
    info = plsc.get_sparse_core_info()
    n_workers = info.num_cores * info.num_subcores
    per_worker = n_items // n_workers
    n_windows = per_worker // SC_WINDOW
    assert per_worker * n_workers == n_items and n_windows * SC_WINDOW == per_worker
    assert n_windows % 2 == 0

    def wrapped(*refs):
        wid = lax.axis_index("subcore") * info.num_cores + lax.axis_index("core")
        body(wid * per_worker, n_windows, *refs)

    return pl.kernel(
        wrapped,
        name=name,
        out_type=out_type,
        mesh=plsc.VectorSubcoreMesh(core_axis_name="core", subcore_axis_name="subcore"),
        scratch_types=[
            pltpu.VMEM((SC_WINDOW,), jnp.int32),
            pltpu.VMEM((SC_WINDOW,), jnp.int32),
            pltpu.VMEM((SC_WINDOW,) + row_shape, row_dtype),
            pltpu.VMEM((SC_WINDOW,) + row_shape, row_dtype),
            pltpu.SemaphoreType.DMA,
            pltpu.SemaphoreType.DMA,
        ],
    )


def _sc_gather_rows(table, idx):
    n, d, win = idx.shape[0], table.shape[1], SC_WINDOW

    def body(base, n_windows, table_hbm, idx_hbm, out_hbm, idx0, idx1, rows0, rows1, sem0, sem1):
        def window(c):
            return pl.ds(pl.multiple_of(base + c * win, win), win)

        def fetch(c, idx_v, rows_v, sem):
            pltpu.sync_copy(idx_hbm.at[window(c)], idx_v)
            pltpu.async_copy(table_hbm.at[idx_v], rows_v, sem)

        def drain(c, idx_v, rows_v, sem):
            pltpu.make_async_copy(table_hbm.at[idx_v], rows_v, sem).wait()
            pltpu.sync_copy(rows_v, out_hbm.at[window(c)])

        fetch(0, idx0, rows0, sem0)

        @pl.loop(0, n_windows, step=2)
        def _(c):
            fetch(c + 1, idx1, rows1, sem1)
            drain(c, idx0, rows0, sem0)

            @pl.when(c + 2 < n_windows)
            def _():
                fetch(c + 2, idx0, rows0, sem0)

            drain(c + 1, idx1, rows1, sem1)

    out_type = jax.ShapeDtypeStruct((n, d), table.dtype)
    return _sc_kernel("sc_gather_rows", body, out_type, n, (d,), table.dtype)(table, idx)


def _sc_scatter_rows(src, dest, n_rows):
    T, d = src.shape
    win = SC_WINDOW

    def body(base, n_windows, src_hbm, dest_hbm, out_hbm, idx0, idx1, rows0, rows1, sem0, sem1):
        def window(c, slot=0):
            return pl.ds(pl.multiple_of(slot * T + base + c * win, win), win)

        def fetch(c, rows_v, sem):
            pltpu.async_copy(src_hbm.at[window(c)], rows_v, sem)

        def push(c, rows_v, sem):
            pltpu.make_async_copy(src_hbm.at[window(c)], rows_v, sem).wait()
            for k in range(TOP_K):
                idx_v = idx0 if k % 2 == 0 else idx1
                pltpu.sync_copy(dest_hbm.at[window(c, k)], idx_v)
                pltpu.sync_copy(rows_v, out_hbm.at[idx_v])

        fetch(0, rows0, sem0)

        @pl.loop(0, n_windows, step=2)
        def _(c):
            fetch(c + 1, rows1, sem1)
            push(c, rows0, sem0)

            @pl.when(c + 2 < n_windows)
            def _():
                fetch(c + 2, rows0, sem0)

            push(c + 1, rows1, sem1)

    out_type = jax.ShapeDtypeStruct((n_rows, d), src.dtype)
    return _sc_kernel("sc_scatter_rows", body, out_type, T, (d,), src.dtype)(
        src, dest.reshape(TOP_K * T))


def _combine_dense_kernel(w_ref, x1_ref, gfin_ref, y_ref, o_ref):
    w = w_ref[...]
    moe = w[:, 0:1] * _unpack_row_halves(y_ref[0])
    for k in range(1, TOP_K):
        moe = moe + w[:, k:k + 1] * _unpack_row_halves(y_ref[k])
    x2 = x1_ref[...] + moe
    ms = jnp.mean(x2 * x2, axis=-1, keepdims=True)
    o_ref[...] = x2 * lax.rsqrt(ms + EPS) * gfin_ref[...]


def _combine_dense(wts, x1, gfin, y_tok):
    T = x1.shape[0]
    tt = COMBINE_TILE
    return pl.pallas_call(
        _combine_dense_kernel,
        grid=(T // tt,),
        in_specs=[
            pl.BlockSpec((tt, TOP_K), lambda i: (i, 0)),
            pl.BlockSpec((tt, D_MODEL), lambda i: (i, 0)),
            pl.BlockSpec((1, D_MODEL), lambda i: (0, 0)),
            pl.BlockSpec((TOP_K, tt, PACKED_W), lambda i: (0, i, 0)),
        ],
        out_specs=pl.BlockSpec((tt, D_MODEL), lambda i: (i, 0)),
        out_shape=jax.ShapeDtypeStruct((T, D_MODEL), jnp.float32),
        compiler_params=_compiler_params(("arbitrary",), 32),
        name="combine_dense",
    )(wts, x1, gfin, y_tok)


def _gate_up_prep_kernel(w_ref, o_ref):
    half = _GU_BLOCK // 2
    i = lax.broadcasted_iota(jnp.int32, (_GU_BLOCK, _GU_BLOCK), 0)
    j = lax.broadcasted_iota(jnp.int32, (_GU_BLOCK, _GU_BLOCK), 1)
    perm = (i == jnp.where(j < half, 2 * j, 2 * (j - half) + 1)).astype(jnp.bfloat16)
    for c in range(2 * D_FF // _GU_BLOCK):
        cols = slice(c * _GU_BLOCK, (c + 1) * _GU_BLOCK)
        blk = w_ref[0, :, cols].astype(jnp.bfloat16)
        o_ref[0, :, cols] = jnp.dot(blk, perm, preferred_element_type=jnp.float32).astype(jnp.bfloat16)


def _gate_up_prep(w_gate_up):
    spec = pl.BlockSpec((1, D_MODEL, 2 * D_FF), lambda e: (e, 0, 0))
    return pl.pallas_call(
        _gate_up_prep_kernel,
        grid=(N_EXPERTS,),
        in_specs=[spec],
        out_specs=spec,
        out_shape=jax.ShapeDtypeStruct(w_gate_up.shape, jnp.bfloat16),
        compiler_params=_compiler_params(("arbitrary",), 48),
        name="gate_up_prep",
    )(w_gate_up)


def _split_gate_up(gu):
    half = _GU_BLOCK // 2
    n = gu.shape[1] // _GU_BLOCK
    glu = jnp.concatenate([gu[:, c * _GU_BLOCK:c * _GU_BLOCK + half] for c in range(n)], axis=1)
    lin = jnp.concatenate([gu[:, c * _GU_BLOCK + half:(c + 1) * _GU_BLOCK] for c in range(n)], axis=1)
    return glu, lin


def _experts_kernel(te_ref, nv_ref, x_ref, wgu_ref, bgu_ref, wd_ref, bd_ref, y_ref):
    @pl.when(pl.program_id(0) < nv_ref[0])
    def _():
        x = _unpack_row_halves(x_ref[...]).astype(jnp.bfloat16)
        gu = jnp.dot(x, wgu_ref[0], preferred_element_type=jnp.float32) + bgu_ref[0]
        glu, lin = _split_gate_up(gu)
        glu = jnp.minimum(glu, SWIGLU_LIMIT)
        lin = jnp.clip(lin, -SWIGLU_LIMIT, SWIGLU_LIMIT)
        a = glu * jax.nn.sigmoid(SWIGLU_ALPHA * glu) * (lin + 1.0)
        y = jnp.dot(a.astype(jnp.bfloat16), wd_ref[0],
                    preferred_element_type=jnp.float32) + bd_ref[0]
        y_ref[...] = _pack_row_halves(y)


def _experts(tile_expert, n_valid, x_sorted, wgu, bgu, wd, bd):
    n_rows = x_sorted.shape[0]
    tr = EXPERT_TILE
    n_tiles = n_rows // tr

    def row_map(j, te, nv):
        return (jnp.minimum(j, nv[0] - 1), 0)

    def exp_map(j, te, nv):
        return (te[j], 0, 0)

    grid_spec = pltpu.PrefetchScalarGridSpec(
        num_scalar_prefetch=2,
        grid=(n_tiles,),
        in_specs=[
            pl.BlockSpec((tr, PACKED_W), row_map),
            pl.BlockSpec((1, D_MODEL, 2 * D_FF), exp_map),
            pl.BlockSpec((1, 1, 2 * D_FF), exp_map),
            pl.BlockSpec((1, D_FF, D_MODEL), exp_map),
            pl.BlockSpec((1, 1, D_MODEL), exp_map),
        ],
        out_specs=pl.BlockSpec((tr, PACKED_W), row_map),
    )
    return pl.pallas_call(
        _experts_kernel,
        grid_spec=grid_spec,
        out_shape=jax.ShapeDtypeStruct((n_rows, PACKED_W), jnp.int32),
        compiler_params=_compiler_params(("arbitrary",), 56),
        name="experts",
    )(tile_expert, n_valid, x_sorted, wgu, bgu, wd, bd)


def _rope_tables(S, gain, scale):
    t = jnp.arange(S, dtype=jnp.int32)
    r = (t // GRID_W).astype(jnp.float32)
    c = (t % GRID_W).astype(jnp.float32)
    inv = jnp.float32(ROPE_THETA) ** (
        -jnp.arange(0, ROPE_AXIS_DIM, 2, dtype=jnp.float32) / ROPE_AXIS_DIM)
    ang = jnp.concatenate([r[None, :] * inv[:, None], c[None, :] * inv[:, None]], axis=0)
    cos = jnp.repeat(jnp.cos(ang), 2, axis=0)
    sin = jnp.repeat(jnp.sin(ang), 2, axis=0)
    sign = jnp.where(jnp.arange(HEAD_DIM) % 2 == 0, -1.0, 1.0).astype(jnp.float32)
    g = gain.astype(jnp.float32) * scale
    g_swapped = g.reshape(HEAD_DIM // 2, 2)[:, ::-1].reshape(HEAD_DIM)
    return g[:, None] * cos, (g_swapped * sign)[:, None] * sin


def _prepare_weights(norm_mix_g, w_in, q_norm_g, k_norm_g, sgu_norm_g, w_spatial, b_spatial,
                     w_proj_attn, w_proj_sgu, w_out, norm_ffn_g, w_router, b_router,
                     w_gate_up, b_gate_up, w_down, b_down, norm_final_g):
    bf = jnp.bfloat16
    l = 0
    return dict(
        gmix=norm_mix_g[l][None, :],
        w_inT=w_in[l].T.astype(bf),
        q_gain=q_norm_g[l], k_gain=k_norm_g[l],
        gs=jnp.broadcast_to(sgu_norm_g[l][:, None], (SGU_W, _LANES)),
        wsT=jnp.swapaxes(w_spatial[l], 1, 2).astype(bf),
        bs=b_spatial[l][:, None, :],
        wpaT=w_proj_attn[l].T.astype(bf),
        wpbT=w_proj_sgu[l].T.astype(bf),
        wout=w_out[l].astype(bf),
        gffn=norm_ffn_g[l][None, :],
        wrT=w_router[l].T,
        br=b_router[l][:, None],
        wgu=_gate_up_prep(w_gate_up[l]),
        bgu=b_gate_up[l].reshape(N_EXPERTS, -1, _GU_BLOCK // 2, 2).transpose(0, 1, 3, 2)
        .reshape(N_EXPERTS, 1, 2 * D_FF),
        wd=w_down[l].astype(bf),
        bd=b_down[l][:, None, :],
        gfin=norm_final_g[None, :],
    )


def _trunk(x, w):
    B, S, _ = x.shape
    T = B * S
    qa, qb = _rope_tables(S, w["q_gain"], math.log2(math.e) / math.sqrt(HEAD_DIM))
    ka, kb = _rope_tables(S, w["k_gain"], 1.0)

    qT, k, vT, uT, vsT, gaT, gbT = _in_proj(x, w["gmix"], w["w_inT"], qa, qb, ka, kb, w["gs"])
    attnT = _attention(qT, k, vT)
    x1, h2, ids, wts, rank, counts = _post(
        attnT, uT, vsT, gaT, gbT, x, w["wsT"], w["bs"], w["wpaT"], w["wpbT"], w["wout"],
        w["gffn"], w["wrT"], w["br"])

    tr = EXPERT_TILE
    counts = counts[:, 0]
    padded = (counts + tr - 1) // tr * tr
    ends = jnp.cumsum(padded)
    starts = ends - padded
    n_tiles = (TOP_K * T) // tr + N_EXPERTS
    n_rows = n_tiles * tr
    dest = rank
    for e in range(N_EXPERTS):
        dest = dest + jnp.where(ids == e, starts[e], 0)
    tile_start = jnp.arange(n_tiles, dtype=jnp.int32) * tr
    tile_expert = jnp.minimum(
        jnp.sum((tile_start[:, None] >= ends[None, :]).astype(jnp.int32), axis=1), N_EXPERTS - 1)
    n_valid = (ends[-1] // tr).astype(jnp.int32)[None]

    x_sorted = _sc_scatter_rows(h2.reshape(T, PACKED_W), dest, n_rows)
    y_sorted = _experts(tile_expert, n_valid, x_sorted, w["wgu"], w["bgu"], w["wd"], w["bd"])
    y_tok = _sc_gather_rows(y_sorted, dest.reshape(TOP_K * T)).reshape(TOP_K, T, PACKED_W)
    out = _combine_dense(wts.T, x1.reshape(T, D_MODEL), w["gfin"], y_tok)
    return out.reshape(B, S, D_MODEL)


def kernel(x_prompt, x_sample, norm_mix_g, w_in, q_norm_g, k_norm_g, sgu_norm_g, w_spatial,
           b_spatial, w_proj_attn, w_proj_sgu, w_out, norm_ffn_g, w_router, b_router,
           w_gate_up, b_gate_up, w_down, b_down, norm_final_g):
    w = _prepare_weights(norm_mix_g, w_in, q_norm_g, k_norm_g, sgu_norm_g, w_spatial,
                         b_spatial, w_proj_attn, w_proj_sgu, w_out, norm_ffn_g, w_router,
                         b_router, w_gate_up, b_gate_up, w_down, b_down, norm_final_g)
    return (_trunk(x_prompt, w), _trunk(x_sample, w))
```

```python
import functools
import math

import jax
import jax.numpy as jnp
from jax import lax
from jax.experimental import pallas as pl
from jax.experimental.pallas import tpu as pltpu
from jax.experimental.pallas import tpu_sc as plsc

D_MODEL = 1024
GRID_W = 64
N_HEADS = 8
N_KV_HEADS = 2
HEAD_DIM = 64
GQA_GROUP = N_HEADS // N_KV_HEADS
Q_W = N_HEADS * HEAD_DIM
KV_W = N_KV_HEADS * HEAD_DIM
ROPE_AXIS_DIM = HEAD_DIM // 2
ROPE_THETA = 10000.0
SGU_GROUPS = 8
SGU_W = D_MODEL // 2
SGU_GROUP_DIM = SGU_W // SGU_GROUPS
CHUNK = 128
IN_W = Q_W + 2 * KV_W + 2 * SGU_W + 2 * D_MODEL
N_EXPERTS = 32
TOP_K = 4
D_FF = D_MODEL
SWIGLU_LIMIT = 7.0
SWIGLU_ALPHA = 1.702
EPS = 1e-6

_Q0, _K0, _V0 = 0, Q_W, Q_W + KV_W
_U0 = Q_W + 2 * KV_W
_VS0 = _U0 + SGU_W
_GA0 = _VS0 + SGU_W
_GB0 = _GA0 + D_MODEL

TOKEN_TILE = 512
Q_TILE = 512
KEY_TILE = TOKEN_TILE
EXPERT_TILE = 512
COMBINE_TILE = 512
PACKED_W = D_MODEL // 2
SC_WINDOW = 64

_LANES = 128
_BF16_SUBLANES = 16
_SUM_ROWS = _BF16_SUBLANES
_GU_BLOCK = 256
_NEG_BIG = -1e30
_HIGH_HALF_MASK = 0xFFFF0000
_MAX_LAGGED_EXPONENT = 80.0
_KEY_TILES_PER_TRIP = 4
_MIB = 1024 * 1024

_NT_DIMS = (((1,), (1,)), ((), ()))
_TN_DIMS = (((0,), (0,)), ((), ()))


def _compiler_params(semantics, vmem_mib):
    return pltpu.CompilerParams(
        dimension_semantics=semantics, vmem_limit_bytes=vmem_mib * _MIB)


def _swap_adjacent_rows(x):
    n = x.shape[0]
    row = lax.broadcasted_iota(jnp.int32, x.shape, 0)
    nxt = pltpu.roll(x, n - 1, 0)
    prv = pltpu.roll(x, 1, 0)
    return jnp.where((row & 1) == 0, nxt, prv)


def _head_norm_rope(z, tab_a, tab_b, n_heads):
    tm = z.shape[1]
    z3 = z.reshape(n_heads, HEAD_DIM, tm)
    ms = jnp.mean(z3 * z3, axis=1, keepdims=True)
    r = lax.rsqrt(ms + EPS)
    zs = _swap_adjacent_rows(z).reshape(n_heads, HEAD_DIM, tm)
    out = (z3 * tab_a[None] + zs * tab_b[None]) * r
    return out.reshape(n_heads * HEAD_DIM, tm)


def _gelu(x):
    return 0.5 * x * (1.0 + lax.erf(x * (1.0 / math.sqrt(2.0))))


def _pack_row_halves(x):
    half = x.shape[1] // 2

    def bf16_bits(v):
        return lax.bitcast_convert_type(v.astype(jnp.bfloat16).astype(jnp.float32), jnp.uint32)

    word = (bf16_bits(x[:, :half]) >> 16) | (bf16_bits(x[:, half:]) & jnp.uint32(_HIGH_HALF_MASK))
    return lax.bitcast_convert_type(word, jnp.int32)


def _unpack_row_halves(p):
    word = lax.bitcast_convert_type(p, jnp.uint32)
    lo = lax.bitcast_convert_type(word << 16, jnp.float32)
    hi = lax.bitcast_convert_type(word & jnp.uint32(_HIGH_HALF_MASK), jnp.float32)
    return jnp.concatenate([lo, hi], axis=1)


def _tile_lanes(x, reps):
    return jnp.concatenate([x] * reps, axis=1) if reps > 1 else x


def _in_proj_kernel(x_ref, gmix_ref, w_ref, qa_ref, qb_ref, ka_ref, kb_ref, gs_ref,
                    qT_ref, k_ref, vT_ref, uT_ref, vsT_ref, gaT_ref, gbT_ref):
    tm = x_ref.shape[1]
    x = x_ref[0]
    ms = jnp.mean(x * x, axis=-1, keepdims=True)
    h = (x * lax.rsqrt(ms + EPS) * gmix_ref[...]).astype(jnp.bfloat16)

    def proj(r0, rows):
        return lax.dot_general(w_ref[r0:r0 + rows, :], h, _NT_DIMS,
                               preferred_element_type=jnp.float32)

    zq = proj(_Q0, Q_W)
    qT_ref[0] = _head_norm_rope(zq, qa_ref[...], qb_ref[...], N_HEADS).astype(jnp.bfloat16)

    zkv = proj(_K0, 2 * KV_W)
    kT = _head_norm_rope(zkv[:KV_W], ka_ref[...], kb_ref[...], N_KV_HEADS)
    k_ref[0] = kT.T.astype(jnp.bfloat16)
    vT_ref[0, 0] = zkv[KV_W:].astype(jnp.bfloat16)

    uT_ref[0] = _gelu(proj(_U0, SGU_W)).astype(jnp.bfloat16)

    vs = _gelu(proj(_VS0, SGU_W))
    vms = jnp.mean(vs * vs, axis=0, keepdims=True)
    gs = _tile_lanes(gs_ref[...], tm // _LANES)
    vsT_ref[0] = (vs * lax.rsqrt(vms + EPS) * gs).astype(jnp.bfloat16)

    gaT_ref[0] = jax.nn.sigmoid(proj(_GA0, D_MODEL)).astype(jnp.bfloat16)
    gbT_ref[0] = jax.nn.sigmoid(proj(_GB0, D_MODEL)).astype(jnp.bfloat16)


def _in_proj(x, gmix, w_inT, qa, qb, ka, kb, gs):
    B, S, _ = x.shape
    tm = TOKEN_TILE
    nt = S // tm
    bf = jnp.bfloat16
    const2 = lambda b, i: (0, 0)
    tab = pl.BlockSpec((HEAD_DIM, tm), lambda b, i: (0, i))
    fm = lambda rows: pl.BlockSpec((1, rows, tm), lambda b, i: (b, 0, i))
    return pl.pallas_call(
        _in_proj_kernel,
        grid=(B, nt),
        in_specs=[
            pl.BlockSpec((1, tm, D_MODEL), lambda b, i: (b, i, 0)),
            pl.BlockSpec((1, D_MODEL), const2),
            pl.BlockSpec((IN_W, D_MODEL), const2),
            tab, tab, tab, tab,
            pl.BlockSpec((SGU_W, _LANES), const2),
        ],
        out_specs=[
            fm(Q_W),
            pl.BlockSpec((1, tm, KV_W), lambda b, i: (b, i, 0)),
            pl.BlockSpec((1, 1, KV_W, tm), lambda b, i: (b, i, 0, 0)),
            fm(SGU_W), fm(SGU_W), fm(D_MODEL), fm(D_MODEL),
        ],
        out_shape=[
            jax.ShapeDtypeStruct((B, Q_W, S), bf),
            jax.ShapeDtypeStruct((B, S, KV_W), bf),
            jax.ShapeDtypeStruct((B, nt, KV_W, tm), bf),
            jax.ShapeDtypeStruct((B, SGU_W, S), bf),
            jax.ShapeDtypeStruct((B, SGU_W, S), bf),
            jax.ShapeDtypeStruct((B, D_MODEL, S), bf),
            jax.ShapeDtypeStruct((B, D_MODEL, S), bf),
        ],
        compiler_params=_compiler_params(("arbitrary", "arbitrary"), 56),
        name="in_proj",
    )(x, gmix, w_inT, qa, qb, ka, kb, gs)


def _attention_kernel(qT_ref, k_ref, vT_ref, o_ref, m_ref, gap_ref, acc_ref):
    tq = qT_ref.shape[2]
    n_kt = vT_ref.shape[1]
    tk = vT_ref.shape[3]
    nq = GQA_GROUP * tq
    ones_rows = (lax.broadcasted_iota(jnp.int32, (_SUM_ROWS, tk), 0) == 0).astype(jnp.bfloat16)

    zeros = jnp.zeros((HEAD_DIM, nq), jnp.bfloat16)
    qp = []
    for kv in range(N_KV_HEADS):
        heads = [qT_ref[0, (kv * GQA_GROUP + g) * HEAD_DIM:(kv * GQA_GROUP + g + 1) * HEAD_DIM, :]
                 for g in range(GQA_GROUP)]
        qk = jnp.concatenate(heads, axis=1)
        qp.append(jnp.concatenate([qk, zeros] if kv == 0 else [zeros, qk], axis=0))

    def key_tile(ki):
        return k_ref[0, pl.ds(pl.multiple_of(ki * tk, tk), tk), :]

    def value_rows(ki, kv):
        vt = vT_ref[0, ki]
        return jnp.concatenate([vt[kv * HEAD_DIM:(kv + 1) * HEAD_DIM, :], ones_rows], axis=0)

    acc_ref[...] = jnp.zeros(acc_ref.shape, jnp.float32)
    gap_ref[...] = jnp.zeros(gap_ref.shape, jnp.float32)
    def as_reference(m):
        return m.astype(jnp.bfloat16).astype(jnp.float32)

    first_keys = k_ref[0, 0:_BF16_SUBLANES, :]
    for kv in range(N_KV_HEADS):
        s0 = jnp.dot(first_keys, qp[kv], preferred_element_type=jnp.float32)
        m_ref[kv] = as_reference(jnp.max(s0, axis=0, keepdims=True))

    ones_cols = (lax.broadcasted_iota(jnp.int32, (tk, _LANES), 1) == 0).astype(jnp.bfloat16)
    ref_row = lax.broadcasted_iota(jnp.int32, (_BF16_SUBLANES, nq), 0) == 0
    pad_rows = jnp.zeros((_LANES - _BF16_SUBLANES, nq), jnp.bfloat16)

    def fast_tile(ki):
        kt = jnp.concatenate([key_tile(ki), ones_cols], axis=1)
        for kv in range(N_KV_HEADS):
            m_old = m_ref[kv]
            neg_ref = jnp.where(ref_row, -m_old, 0.0).astype(jnp.bfloat16)
            q_aug = jnp.concatenate([qp[kv], neg_ref, pad_rows], axis=0)
            s = jnp.dot(kt, q_aug, preferred_element_type=jnp.float32)
            p = jnp.exp2(s).astype(jnp.bfloat16)
            over = jnp.max(s, axis=0, keepdims=True)
            pv = jnp.dot(value_rows(ki, kv), p, preferred_element_type=jnp.float32)
            m_new = as_reference(m_old + jnp.maximum(over, 0.0))
            acc_ref[kv] = (acc_ref[kv] + pv) * jnp.exp2(m_old - m_new)
            m_ref[kv] = m_new
            gap_ref[kv] = jnp.maximum(gap_ref[kv], over)

    per_trip = math.gcd(n_kt, _KEY_TILES_PER_TRIP)

    def fast_group(j, carry):
        for t in range(per_trip):
            fast_tile(per_trip * j + t)
        return carry

    lax.fori_loop(0, n_kt // per_trip, fast_group, 0)

    @pl.when(jnp.max(gap_ref[...]) > _MAX_LAGGED_EXPONENT)
    def _():
        m_ref[...] = jnp.full(m_ref.shape, _NEG_BIG, jnp.float32)
        acc_ref[...] = jnp.zeros(acc_ref.shape, jnp.float32)

        def robust_tile(ki, carry):
            kt = key_tile(ki)
            for kv in range(N_KV_HEADS):
                s = jnp.dot(kt, qp[kv], preferred_element_type=jnp.float32)
                m_old = m_ref[kv]
                m_new = jnp.maximum(m_old, jnp.max(s, axis=0, keepdims=True))
                p = jnp.exp2(s - m_new).astype(jnp.bfloat16)
                pv = jnp.dot(value_rows(ki, kv), p, preferred_element_type=jnp.float32)
                acc_ref[kv] = jnp.exp2(m_old - m_new) * acc_ref[kv] + pv
                m_ref[kv] = m_new
            return carry

        lax.fori_loop(0, n_kt, robust_tile, 0)

    for kv in range(N_KV_HEADS):
        acc = acc_ref[kv]
        o = acc[:HEAD_DIM] / acc[HEAD_DIM:HEAD_DIM + 1]
        for g in range(GQA_GROUP):
            r0 = (kv * GQA_GROUP + g) * HEAD_DIM
            o_ref[0, r0:r0 + HEAD_DIM, :] = o[:, g * tq:(g + 1) * tq].astype(jnp.bfloat16)


def _attention(qT, k, vT):
    B, _, S = qT.shape
    tq = Q_TILE
    nq = GQA_GROUP * tq
    n_kt, tk = vT.shape[1], vT.shape[3]
    return pl.pallas_call(
        _attention_kernel,
        grid=(B, S // tq),
        in_specs=[
            pl.BlockSpec((1, Q_W, tq), lambda b, i: (b, 0, i)),
            pl.BlockSpec((1, S, KV_W), lambda b, i: (b, 0, 0)),
            pl.BlockSpec((1, n_kt, KV_W, tk), lambda b, i: (b, 0, 0, 0)),
        ],
        out_specs=pl.BlockSpec((1, Q_W, tq), lambda b, i: (b, 0, i)),
        out_shape=jax.ShapeDtypeStruct((B, Q_W, S), jnp.bfloat16),
        scratch_shapes=[
            pltpu.VMEM((N_KV_HEADS, 1, nq), jnp.float32),
            pltpu.VMEM((N_KV_HEADS, 1, nq), jnp.float32),
            pltpu.VMEM((N_KV_HEADS, HEAD_DIM + _SUM_ROWS, nq), jnp.float32),
        ],
        compiler_params=_compiler_params(("arbitrary", "arbitrary"), 56),
        name="attention",
    )(qT, k, vT)


def _post_kernel(attnT_ref, uT_ref, vsT_ref, gaT_ref, gbT_ref, x_ref,
                 wsT_ref, bs_ref, wpaT_ref, wpbT_ref, wout_ref, gffn_ref, wrT_ref, br_ref,
                 x1_ref, h2_ref, ids_ref, wts_ref, rank_ref, cnt_ref, carry_ref):
    tm = x_ref.shape[1]
    n_chunks = tm // CHUNK
    first = jnp.logical_and(pl.program_id(0) == 0, pl.program_id(1) == 0)

    @pl.when(first)
    def _():
        carry_ref[...] = jnp.zeros(carry_ref.shape, jnp.float32)

    gate_rows = []
    for g in range(SGU_GROUPS):
        r0 = g * SGU_GROUP_DIM
        vs_g = vsT_ref[0, r0:r0 + SGU_GROUP_DIM, :]
        lhs = jnp.concatenate(
            [vs_g[:, c * CHUNK:(c + 1) * CHUNK] for c in range(n_chunks)], axis=0)
        mixed = jnp.dot(lhs, wsT_ref[g], preferred_element_type=jnp.float32)
        mixed = mixed + bs_ref[g]
        mixedT = jnp.concatenate(
            [mixed[c * SGU_GROUP_DIM:(c + 1) * SGU_GROUP_DIM] for c in range(n_chunks)], axis=1)
        u_g = uT_ref[0, r0:r0 + SGU_GROUP_DIM, :].astype(jnp.float32)
        gate_rows.append((u_g * mixedT).astype(jnp.bfloat16))
    gateT = jnp.concatenate(gate_rows, axis=0)

    paT = jnp.dot(wpaT_ref[...], attnT_ref[0], preferred_element_type=jnp.float32)
    pbT = jnp.dot(wpbT_ref[...], gateT, preferred_element_type=jnp.float32)
    mT = (gaT_ref[0].astype(jnp.float32) * paT
          + gbT_ref[0].astype(jnp.float32) * pbT).astype(jnp.bfloat16)
    y = lax.dot_general(mT, wout_ref[...], _TN_DIMS, preferred_element_type=jnp.float32)
    x1 = x_ref[0] + y
    x1_ref[0] = x1

    ms = jnp.mean(x1 * x1, axis=-1, keepdims=True)
    h2 = x1 * lax.rsqrt(ms + EPS) * gffn_ref[...]
    h2_ref[0] = _pack_row_halves(h2)

    logits = lax.dot_general(wrT_ref[...], h2, _NT_DIMS, precision=lax.Precision.HIGHEST,
                             preferred_element_type=jnp.float32) + br_ref[...]
    eidx = lax.broadcasted_iota(jnp.int32, logits.shape, 0).astype(jnp.float32)
    work = logits
    vals, ids, sels = [], [], []
    for _ in range(TOP_K):
        mx = jnp.max(work, axis=0, keepdims=True)
        idx = jnp.min(jnp.where(work == mx, eidx, float(N_EXPERTS)), axis=0, keepdims=True)
        sel = eidx == idx
        vals.append(mx)
        ids.append(idx)
        sels.append(sel)
        work = jnp.where(sel, -jnp.inf, work)
    exps = [jnp.exp(v - vals[0]) for v in vals]
    denom = exps[0] + exps[1] + exps[2] + exps[3]
    ids_ref[...] = jnp.concatenate(ids, axis=0).astype(jnp.int32)
    wts_ref[...] = jnp.concatenate([e / denom for e in exps], axis=0)

    onehot = [s.astype(jnp.float32) for s in sels]
    hits = onehot[0] + onehot[1] + onehot[2] + onehot[3]
    ti = lax.broadcasted_iota(jnp.int32, (tm, tm), 0)
    tj = lax.broadcasted_iota(jnp.int32, (tm, tm), 1)
    upper = (ti < tj).astype(jnp.bfloat16)
    prefix = jnp.dot(hits.astype(jnp.bfloat16), upper, preferred_element_type=jnp.float32)
    base = prefix + carry_ref[...]
    ranks = [jnp.sum(oh * base, axis=0, keepdims=True) for oh in onehot]
    rank_ref[...] = jnp.concatenate(ranks, axis=0).astype(jnp.int32)
    carry_ref[...] = carry_ref[...] + jnp.sum(hits, axis=1, keepdims=True)
    cnt_ref[...] = carry_ref[...].astype(jnp.int32)


def _post(attnT, uT, vsT, gaT, gbT, x, wsT, bs, wpaT, wpbT, wout, gffn, wrT, br):
    B, S, _ = x.shape
    tm = TOKEN_TILE
    nt = S // tm
    T = B * S
    fm = lambda rows: pl.BlockSpec((1, rows, tm), lambda b, i: (b, 0, i))
    rowm = pl.BlockSpec((1, tm, D_MODEL), lambda b, i: (b, i, 0))
    c2 = lambda b, i: (0, 0)
    c3 = lambda b, i: (0, 0, 0)
    tokT = pl.BlockSpec((TOP_K, tm), lambda b, i: (0, b * nt + i))
    return pl.pallas_call(
        _post_kernel,
        grid=(B, nt),
        in_specs=[
            fm(Q_W), fm(SGU_W), fm(SGU_W), fm(D_MODEL), fm(D_MODEL), rowm,
            pl.BlockSpec((SGU_GROUPS, CHUNK, CHUNK), c3),
            pl.BlockSpec((SGU_GROUPS, 1, CHUNK), c3),
            pl.BlockSpec((D_MODEL, Q_W), c2),
            pl.BlockSpec((D_MODEL, SGU_W), c2),
            pl.BlockSpec((D_MODEL, D_MODEL), c2),
            pl.BlockSpec((1, D_MODEL), c2),
            pl.BlockSpec((N_EXPERTS, D_MODEL), c2),
            pl.BlockSpec((N_EXPERTS, 1), c2),
        ],
        out_specs=[rowm, pl.BlockSpec((1, tm, PACKED_W), lambda b, i: (b, i, 0)),
                   tokT, tokT, tokT, pl.BlockSpec((N_EXPERTS, 1), c2)],
        out_shape=[
            jax.ShapeDtypeStruct((B, S, D_MODEL), jnp.float32),
            jax.ShapeDtypeStruct((B, S, PACKED_W), jnp.int32),
            jax.ShapeDtypeStruct((TOP_K, T), jnp.int32),
            jax.ShapeDtypeStruct((TOP_K, T), jnp.float32),
            jax.ShapeDtypeStruct((TOP_K, T), jnp.int32),
            jax.ShapeDtypeStruct((N_EXPERTS, 1), jnp.int32),
        ],
        scratch_shapes=[pltpu.VMEM((N_EXPERTS, 1), jnp.float32)],
        compiler_params=_compiler_params(("arbitrary", "arbitrary"), 56),
        name="post",
    )(attnT, uT, vsT, gaT, gbT, x, wsT, bs, wpaT, wpbT, wout, gffn, wrT, br)


def _sc_kernel(name, body, out_type, n_items, row_shape, row_dtype):
    info = plsc.get_sparse_core_info()
    n_workers = info.num_cores * info.num_subcores
    per_worker = n_items // n_workers
    n_windows = per_worker // SC_WINDOW
    assert per_worker * n_workers == n_items and n_windows * SC_WINDOW == per_worker
    assert n_windows % 2 == 0

    def wrapped(*refs):
        wid = lax.axis_index("subcore") * info.num_cores + lax.axis_index("core")
        body(wid * per_worker, n_windows, *refs)

    return pl.kernel(
        wrapped,
        name=name,
        out_type=out_type,
        mesh=plsc.VectorSubcoreMesh(core_axis_name="core", subcore_axis_name="subcore"),
        scratch_types=[
            pltpu.VMEM((SC_WINDOW,), jnp.int32),
            pltpu.VMEM((SC_WINDOW,), jnp.int32),
            pltpu.VMEM((SC_WINDOW,) + row_shape, row_dtype),
            pltpu.VMEM((SC_WINDOW,) + row_shape, row_dtype),
            pltpu.SemaphoreType.DMA,
            pltpu.SemaphoreType.DMA,
        ],
    )


def _sc_gather_rows(table, idx):
    n, d, win = idx.shape[0], table.shape[1], SC_WINDOW

    def body(base, n_windows, table_hbm, idx_hbm, out_hbm, idx0, idx1, rows0, rows1, sem0, sem1):
        def window(c):
            return pl.ds(pl.multiple_of(base + c * win, win), win)

        def fetch(c, idx_v, rows_v, sem):
            pltpu.sync_copy(idx_hbm.at[window(c)], idx_v)
            pltpu.async_copy(table_hbm.at[idx_v], rows_v, sem)

        def drain(c, idx_v, rows_v, sem):
            pltpu.make_async_copy(table_hbm.at[idx_v], rows_v, sem).wait()
            pltpu.sync_copy(rows_v, out_hbm.at[window(c)])

        fetch(0, idx0, rows0, sem0)

        @pl.loop(0, n_windows, step=2)
        def _(c):
            fetch(c + 1, idx1, rows1, sem1)
            drain(c, idx0, rows0, sem0)

            @pl.when(c + 2 < n_windows)
            def _():
                fetch(c + 2, idx0, rows0, sem0)

            drain(c + 1, idx1, rows1, sem1)

    out_type = jax.ShapeDtypeStruct((n, d), table.dtype)
    return _sc_kernel("sc_gather_rows", body, out_type, n, (d,), table.dtype)(table, idx)


def _sc_scatter_rows(src, dest, n_rows):
    T, d = src.shape
    win = SC_WINDOW

    def body(base, n_windows, src_hbm, dest_hbm, out_hbm, idx0, idx1, rows0, rows1, sem0, sem1):
        def window(c, slot=0):
            return pl.ds(pl.multiple_of(slot * T + base + c * win, win), win)

        def fetch(c, rows_v, sem):
            pltpu.async_copy(src_hbm.at[window(c)], rows_v, sem)

        def push(c, rows_v, sem):
            pltpu.make_async_copy(src_hbm.at[window(c)], rows_v, sem).wait()
            for k in range(TOP_K):
                idx_v = idx0 if k % 2 == 0 else idx1
                pltpu.sync_copy(dest_hbm.at[window(c, k)], idx_v)
                pltpu.sync_copy(rows_v, out_hbm.at[idx_v])

        fetch(0, rows0, sem0)

        @pl.loop(0, n_windows, step=2)
        def _(c):
            fetch(c + 1, rows1, sem1)
            push(c, rows0, sem0)

            @pl.when(c + 2 < n_windows)
            def _():
                fetch(c + 2, rows0, sem0)

            push(c + 1, rows1, sem1)

    out_type = jax.ShapeDtypeStruct((n_rows, d), src.dtype)
    return _sc_kernel("sc_scatter_rows", body, out_type, T, (d,), src.dtype)(
        src, dest.reshape(TOP_K * T))


def _combine_dense_kernel(w_ref, x1_ref, gfin_ref, y_ref, o_ref):
    w = w_ref[...]
    moe = w[:, 0:1] * _unpack_row_halves(y_ref[0])
    for k in range(1, TOP_K):
        moe = moe + w[:, k:k + 1] * _unpack_row_halves(y_ref[k])
    x2 = x1_ref[...] + moe
    ms = jnp.mean(x2 * x2, axis=-1, keepdims=True)
    o_ref[...] = x2 * lax.rsqrt(ms + EPS) * gfin_ref[...]


def _combine_dense(wts, x1, gfin, y_tok):
    T = x1.shape[0]
    tt = COMBINE_TILE
    return pl.pallas_call(
        _combine_dense_kernel,
        grid=(T // tt,),
        in_specs=[
            pl.BlockSpec((tt, TOP_K), lambda i: (i, 0)),
            pl.BlockSpec((tt, D_MODEL), lambda i: (i, 0)),
            pl.BlockSpec((1, D_MODEL), lambda i: (0, 0)),
            pl.BlockSpec((TOP_K, tt, PACKED_W), lambda i: (0, i, 0)),
        ],
        out_specs=pl.BlockSpec((tt, D_MODEL), lambda i: (i, 0)),
        out_shape=jax.ShapeDtypeStruct((T, D_MODEL), jnp.float32),
        compiler_params=_compiler_params(("arbitrary",), 32),
        name="combine_dense",
    )(wts, x1, gfin, y_tok)


def _gate_up_prep_kernel(w_ref, o_ref):
    half = _GU_BLOCK // 2
    i = lax.broadcasted_iota(jnp.int32, (_GU_BLOCK, _GU_BLOCK), 0)
    j = lax.broadcasted_iota(jnp.int32, (_GU_BLOCK, _GU_BLOCK), 1)
    perm = (i == jnp.where(j < half, 2 * j, 2 * (j - half) + 1)).astype(jnp.bfloat16)
    for c in range(2 * D_FF // _GU_BLOCK):
        cols = slice(c * _GU_BLOCK, (c + 1) * _GU_BLOCK)
        blk = w_ref[0, :, cols].astype(jnp.bfloat16)
        o_ref[0, :, cols] = jnp.dot(blk, perm, preferred_element_type=jnp.float32).astype(jnp.bfloat16)


def _gate_up_prep(w_gate_up):
    spec = pl.BlockSpec((1, D_MODEL, 2 * D_FF), lambda e: (e, 0, 0))
    return pl.pallas_call(
        _gate_up_prep_kernel,
        grid=(N_EXPERTS,),
        in_specs=[spec],
        out_specs=spec,
        out_shape=jax.ShapeDtypeStruct(w_gate_up.shape, jnp.bfloat16),
        compiler_params=_compiler_params(("arbitrary",), 48),
        name="gate_up_prep",
    )(w_gate_up)


def _split_gate_up(gu):
    half = _GU_BLOCK // 2
    n = gu.shape[1] // _GU_BLOCK
    glu = jnp.concatenate([gu[:, c * _GU_BLOCK:c * _GU_BLOCK + half] for c in range(n)], axis=1)
    lin = jnp.concatenate([gu[:, c * _GU_BLOCK + half:(c + 1) * _GU_BLOCK] for c in range(n)], axis=1)
    return glu, lin


def _experts_kernel(te_ref, nv_ref, x_ref, wgu_ref, bgu_ref, wd_ref, bd_ref, y_ref):
    @pl.when(pl.program_id(0) < nv_ref[0])
    def _():
        x = _unpack_row_halves(x_ref[...]).astype(jnp.bfloat16)
        gu = jnp.dot(x, wgu_ref[0], preferred_element_type=jnp.float32) + bgu_ref[0]
        glu, lin = _split_gate_up(gu)
        glu = jnp.minimum(glu, SWIGLU_LIMIT)
        lin = jnp.clip(lin, -SWIGLU_LIMIT, SWIGLU_LIMIT)
        a = glu * jax.nn.sigmoid(SWIGLU_ALPHA * glu) * (lin + 1.0)
        y = jnp.dot(a.astype(jnp.bfloat16), wd_ref[0],
                    preferred_element_type=jnp.float32) + bd_ref[0]
        y_ref[...] = _pack_row_halves(y)


def _experts(tile_expert, n_valid, x_sorted, wgu, bgu, wd, bd):
    n_rows = x_sorted.shape[0]
    tr = EXPERT_TILE
    n_tiles = n_rows // tr

    def row_map(j, te, nv):
        return (jnp.minimum(j, nv[0] - 1), 0)

    def exp_map(j, te, nv):
        return (te[j], 0, 0)

    grid_spec = pltpu.PrefetchScalarGridSpec(
        num_scalar_prefetch=2,
        grid=(n_tiles,),
        in_specs=[
            pl.BlockSpec((tr, PACKED_W), row_map),
            pl.BlockSpec((1, D_MODEL, 2 * D_FF), exp_map),
            pl.BlockSpec((1, 1, 2 * D_FF), exp_map),
            pl.BlockSpec((1, D_FF, D_MODEL), exp_map),
            pl.BlockSpec((1, 1, D_MODEL), exp_map),
        ],
        out_specs=pl.BlockSpec((tr, PACKED_W), row_map),
    )
    return pl.pallas_call(
        _experts_kernel,
        grid_spec=grid_spec,
        out_shape=jax.ShapeDtypeStruct((n_rows, PACKED_W), jnp.int32),
        compiler_params=_compiler_params(("arbitrary",), 56),
        name="experts",
    )(tile_expert, n_valid, x_sorted, wgu, bgu, wd, bd)


def _rope_tables(S, gain, scale):
    t = jnp.arange(S, dtype=jnp.int32)
    r = (t // GRID_W).astype(jnp.float32)
    c = (t % GRID_W).astype(jnp.float32)
    inv = jnp.float32(ROPE_THETA) ** (
        -jnp.arange(0, ROPE_AXIS_DIM, 2, dtype=jnp.float32) / ROPE_AXIS_DIM)
    ang = jnp.concatenate([r[None, :] * inv[:, None], c[None, :] * inv[:, None]], axis=0)
    cos = jnp.repeat(jnp.cos(ang), 2, axis=0)
    sin = jnp.repeat(jnp.sin(ang), 2, axis=0)
    sign = jnp.where(jnp.arange(HEAD_DIM) % 2 == 0, -1.0, 1.0).astype(jnp.float32)
    g = gain.astype(jnp.float32) * scale
    g_swapped = g.reshape(HEAD_DIM // 2, 2)[:, ::-1].reshape(HEAD_DIM)
    return g[:, None] * cos, (g_swapped * sign)[:, None] * sin


def _prepare_weights(norm_mix_g, w_in, q_norm_g, k_norm_g, sgu_norm_g, w_spatial, b_spatial,
                     w_proj_attn, w_proj_sgu, w_out, norm_ffn_g, w_router, b_router,
                     w_gate_up, b_gate_up, w_down, b_down, norm_final_g):
    bf = jnp.bfloat16
    l = 0
    return dict(
        gmix=norm_mix_g[l][None, :],
        w_inT=w_in[l].T.astype(bf),
        q_gain=q_norm_g[l], k_gain=k_norm_g[l],
        gs=jnp.broadcast_to(sgu_norm_g[l][:, None], (SGU_W, _LANES)),
        wsT=jnp.swapaxes(w_spatial[l], 1, 2).astype(bf),
        bs=b_spatial[l][:, None, :],
        wpaT=w_proj_attn[l].T.astype(bf),
        wpbT=w_proj_sgu[l].T.astype(bf),
        wout=w_out[l].astype(bf),
        gffn=norm_ffn_g[l][None, :],
        wrT=w_router[l].T,
        br=b_router[l][:, None],
        wgu=_gate_up_prep(w_gate_up[l]),
        bgu=b_gate_up[l].reshape(N_EXPERTS, -1, _GU_BLOCK // 2, 2).transpose(0, 1, 3, 2)
        .reshape(N_EXPERTS, 1, 2 * D_FF),
        wd=w_down[l].astype(bf),
        bd=b_down[l][:, None, :],
        gfin=norm_final_g[None, :],
    )


def _trunk(x, w, rope):
    B, S, _ = x.shape
    T = B * S
    qT, k, vT, uT, vsT, gaT, gbT = _in_proj(x, w["gmix"], w["w_inT"], *rope, w["gs"])
    attnT = _attention(qT, k, vT)
    x1, h2, ids, wts, rank, counts = _post(
        attnT, uT, vsT, gaT, gbT, x, w["wsT"], w["bs"], w["wpaT"], w["wpbT"], w["wout"],
        w["gffn"], w["wrT"], w["br"])

    tr = EXPERT_TILE
    counts = counts[:, 0]
    padded = (counts + tr - 1) // tr * tr
    ends = jnp.cumsum(padded)
    starts = ends - padded
    n_tiles = (TOP_K * T) // tr + N_EXPERTS
    n_rows = n_tiles * tr
    dest = rank
    for e in range(N_EXPERTS):
        dest = dest + jnp.where(ids == e, starts[e], 0)
    tile_start = jnp.arange(n_tiles, dtype=jnp.int32) * tr
    tile_expert = jnp.minimum(
        jnp.sum((tile_start[:, None] >= ends[None, :]).astype(jnp.int32), axis=1), N_EXPERTS - 1)
    n_valid = (ends[-1] // tr).astype(jnp.int32)[None]

    x_sorted = _sc_scatter_rows(h2.reshape(T, PACKED_W), dest, n_rows)
    y_sorted = _experts(tile_expert, n_valid, x_sorted, w["wgu"], w["bgu"], w["wd"], w["bd"])
    y_tok = _sc_gather_rows(y_sorted, dest.reshape(TOP_K * T)).reshape(TOP_K, T, PACKED_W)
    out = _combine_dense(wts.T, x1.reshape(T, D_MODEL), w["gfin"], y_tok)
    return out.reshape(B, S, D_MODEL)


def kernel(x_prompt, x_sample, norm_mix_g, w_in, q_norm_g, k_norm_g, sgu_norm_g, w_spatial,
           b_spatial, w_proj_attn, w_proj_sgu, w_out, norm_ffn_g, w_router, b_router,
           w_gate_up, b_gate_up, w_down, b_down, norm_final_g):
    w = _prepare_weights(norm_mix_g, w_in, q_norm_g, k_norm_g, sgu_norm_g, w_spatial,
                         b_spatial, w_proj_attn, w_proj_sgu, w_out, norm_ffn_g, w_router,
                         b_router, w_gate_up, b_gate_up, w_down, b_down, norm_final_g)
    s_max = max(x_prompt.shape[1], x_sample.shape[1])
    rope = (*_rope_tables(s_max, w["q_gain"], math.log2(math.e) / math.sqrt(HEAD_DIM)),
            *_rope_tables(s_max, w["k_gain"], 1.0))
    return (_trunk(x_prompt, w, rope), _trunk(x_sample, w, rope))
```

```python
import functools
import math

import jax
import jax.numpy as jnp
from jax import lax
from jax.experimental import pallas as pl
from jax.experimental.pallas import tpu as pltpu
from jax.experimental.pallas import tpu_sc as plsc

D_MODEL = 1024
GRID_W = 64
N_HEADS = 8
N_KV_HEADS = 2
HEAD_DIM = 64
GQA_GROUP = N_HEADS // N_KV_HEADS
Q_W = N_HEADS * HEAD_DIM
KV_W = N_KV_HEADS * HEAD_DIM
ROPE_AXIS_DIM = HEAD_DIM // 2
ROPE_THETA = 10000.0
SGU_GROUPS = 8
SGU_W = D_MODEL // 2
SGU_GROUP_DIM = SGU_W // SGU_GROUPS
CHUNK = 128
IN_W = Q_W + 2 * KV_W + 2 * SGU_W + 2 * D_MODEL
N_EXPERTS = 32
TOP_K = 4
D_FF = D_MODEL
SWIGLU_LIMIT = 7.0
SWIGLU_ALPHA = 1.702
EPS = 1e-6

_Q0, _K0, _V0 = 0, Q_W, Q_W + KV_W
_U0 = Q_W + 2 * KV_W
_VS0 = _U0 + SGU_W
_GA0 = _VS0 + SGU_W
_GB0 = _GA0 + D_MODEL

TOKEN_TILE = 512
Q_TILE = 512
KEY_TILE = TOKEN_TILE
EXPERT_TILE = 512
COMBINE_TILE = 512
PACKED_W = D_MODEL // 2
SC_WINDOW = 64

_LANES = 128
_BF16_SUBLANES = 16
_SUM_ROWS = _BF16_SUBLANES
_GU_BLOCK = 256
_NEG_BIG = -1e30
_HIGH_HALF_MASK = 0xFFFF0000
_MAX_LAGGED_EXPONENT = 80.0
_KEY_TILES_PER_TRIP = 4
_MIB = 1024 * 1024

_NT_DIMS = (((1,), (1,)), ((), ()))
_TN_DIMS = (((0,), (0,)), ((), ()))


def _compiler_params(semantics, vmem_mib):
    return pltpu.CompilerParams(
        dimension_semantics=semantics, vmem_limit_bytes=vmem_mib * _MIB)


def _swap_adjacent_rows(x):
    n = x.shape[0]
    row = lax.broadcasted_iota(jnp.int32, x.shape, 0)
    nxt = pltpu.roll(x, n - 1, 0)
    prv = pltpu.roll(x, 1, 0)
    return jnp.where((row & 1) == 0, nxt, prv)


def _head_norm_rope(z, tab_a, tab_b, n_heads):
    tm = z.shape[1]
    z3 = z.reshape(n_heads, HEAD_DIM, tm)
    ms = jnp.mean(z3 * z3, axis=1, keepdims=True)
    r = lax.rsqrt(ms + EPS)
    zs = _swap_adjacent_rows(z).reshape(n_heads, HEAD_DIM, tm)
    out = (z3 * tab_a[None] + zs * tab_b[None]) * r
    return out.reshape(n_heads * HEAD_DIM, tm)


def _gelu(x):
    return 0.5 * x * (1.0 + lax.erf(x * (1.0 / math.sqrt(2.0))))


def _pack_row_halves(x):
    half = x.shape[1] // 2

    def bf16_bits(v):
        return lax.bitcast_convert_type(v.astype(jnp.bfloat16).astype(jnp.float32), jnp.uint32)

    word = (bf16_bits(x[:, :half]) >> 16) | (bf16_bits(x[:, half:]) & jnp.uint32(_HIGH_HALF_MASK))
    return lax.bitcast_convert_type(word, jnp.int32)


def _unpack_row_halves(p):
    word = lax.bitcast_convert_type(p, jnp.uint32)
    lo = lax.bitcast_convert_type(word << 16, jnp.float32)
    hi = lax.bitcast_convert_type(word & jnp.uint32(_HIGH_HALF_MASK), jnp.float32)
    return jnp.concatenate([lo, hi], axis=1)


def _tile_lanes(x, reps):
    return jnp.concatenate([x] * reps, axis=1) if reps > 1 else x


def _in_proj_kernel(x_ref, gmix_ref, w_ref, qa_ref, qb_ref, ka_ref, kb_ref, gs_ref,
                    qT_ref, k_ref, vT_ref, uT_ref, vsT_ref, gaT_ref, gbT_ref):
    tm = x_ref.shape[1]
    x = x_ref[0]
    ms = jnp.mean(x * x, axis=-1, keepdims=True)
    h = (x * lax.rsqrt(ms + EPS) * gmix_ref[...]).astype(jnp.bfloat16)

    def proj(r0, rows):
        return lax.dot_general(w_ref[r0:r0 + rows, :], h, _NT_DIMS,
                               preferred_element_type=jnp.float32)

    zq = proj(_Q0, Q_W)
    qT_ref[0] = _head_norm_rope(zq, qa_ref[...], qb_ref[...], N_HEADS).astype(jnp.bfloat16)

    zkv = proj(_K0, 2 * KV_W)
    kT = _head_norm_rope(zkv[:KV_W], ka_ref[...], kb_ref[...], N_KV_HEADS)
    k_ref[0] = kT.T.astype(jnp.bfloat16)
    vT_ref[0, 0] = zkv[KV_W:].astype(jnp.bfloat16)

    uT_ref[0] = _gelu(proj(_U0, SGU_W)).astype(jnp.bfloat16)

    vs = _gelu(proj(_VS0, SGU_W))
    vms = jnp.mean(vs * vs, axis=0, keepdims=True)
    gs = _tile_lanes(gs_ref[...], tm // _LANES)
    vsT_ref[0] = (vs * lax.rsqrt(vms + EPS) * gs).astype(jnp.bfloat16)

    gaT_ref[0] = jax.nn.sigmoid(proj(_GA0, D_MODEL)).astype(jnp.bfloat16)
    gbT_ref[0] = jax.nn.sigmoid(proj(_GB0, D_MODEL)).astype(jnp.bfloat16)


def _in_proj(x, gmix, w_inT, qa, qb, ka, kb, gs):
    B, S, _ = x.shape
    tm = TOKEN_TILE
    nt = S // tm
    bf = jnp.bfloat16
    const2 = lambda b, i: (0, 0)
    tab = pl.BlockSpec((HEAD_DIM, tm), lambda b, i: (0, i))
    fm = lambda rows: pl.BlockSpec((1, rows, tm), lambda b, i: (b, 0, i))
    return pl.pallas_call(
        _in_proj_kernel,
        grid=(B, nt),
        in_specs=[
            pl.BlockSpec((1, tm, D_MODEL), lambda b, i: (b, i, 0)),
            pl.BlockSpec((1, D_MODEL), const2),
            pl.BlockSpec((IN_W, D_MODEL), const2),
            tab, tab, tab, tab,
            pl.BlockSpec((SGU_W, _LANES), const2),
        ],
        out_specs=[
            fm(Q_W),
            pl.BlockSpec((1, tm, KV_W), lambda b, i: (b, i, 0)),
            pl.BlockSpec((1, 1, KV_W, tm), lambda b, i: (b, i, 0, 0)),
            fm(SGU_W), fm(SGU_W), fm(D_MODEL), fm(D_MODEL),
        ],
        out_shape=[
            jax.ShapeDtypeStruct((B, Q_W, S), bf),
            jax.ShapeDtypeStruct((B, S, KV_W), bf),
            jax.ShapeDtypeStruct((B, nt, KV_W, tm), bf),
            jax.ShapeDtypeStruct((B, SGU_W, S), bf),
            jax.ShapeDtypeStruct((B, SGU_W, S), bf),
            jax.ShapeDtypeStruct((B, D_MODEL, S), bf),
            jax.ShapeDtypeStruct((B, D_MODEL, S), bf),
        ],
        compiler_params=_compiler_params(("arbitrary", "arbitrary"), 56),
        name="in_proj",
    )(x, gmix, w_inT, qa, qb, ka, kb, gs)


def _attention_kernel(qT_ref, k_ref, vT_ref, o_ref, m_ref, gap_ref, acc_ref):
    tq = qT_ref.shape[2]
    n_kt = vT_ref.shape[1]
    tk = vT_ref.shape[3]
    nq = GQA_GROUP * tq
    ones_rows = (lax.broadcasted_iota(jnp.int32, (_SUM_ROWS, tk), 0) == 0).astype(jnp.bfloat16)

    zeros = jnp.zeros((HEAD_DIM, nq), jnp.bfloat16)
    qp = []
    for kv in range(N_KV_HEADS):
        heads = [qT_ref[0, (kv * GQA_GROUP + g) * HEAD_DIM:(kv * GQA_GROUP + g + 1) * HEAD_DIM, :]
                 for g in range(GQA_GROUP)]
        qk = jnp.concatenate(heads, axis=1)
        qp.append(jnp.concatenate([qk, zeros] if kv == 0 else [zeros, qk], axis=0))

    def key_tile(ki):
        return k_ref[0, pl.ds(pl.multiple_of(ki * tk, tk), tk), :]

    def value_rows(ki, kv):
        vt = vT_ref[0, ki]
        return jnp.concatenate([vt[kv * HEAD_DIM:(kv + 1) * HEAD_DIM, :], ones_rows], axis=0)

    acc_ref[...] = jnp.zeros(acc_ref.shape, jnp.float32)
    gap_ref[...] = jnp.zeros(gap_ref.shape, jnp.float32)
    def as_reference(m):
        return m.astype(jnp.bfloat16).astype(jnp.float32)

    first_keys = k_ref[0, 0:_BF16_SUBLANES, :]
    for kv in range(N_KV_HEADS):
        s0 = jnp.dot(first_keys, qp[kv], preferred_element_type=jnp.float32)
        m_ref[kv] = as_reference(jnp.max(s0, axis=0, keepdims=True))

    ones_cols = (lax.broadcasted_iota(jnp.int32, (tk, _LANES), 1) == 0).astype(jnp.bfloat16)
    ref_row = lax.broadcasted_iota(jnp.int32, (_BF16_SUBLANES, nq), 0) == 0
    pad_rows = jnp.zeros((_LANES - _BF16_SUBLANES, nq), jnp.bfloat16)

    def fast_tile(ki):
        kt = jnp.concatenate([key_tile(ki), ones_cols], axis=1)
        for kv in range(N_KV_HEADS):
            m_old = m_ref[kv]
            neg_ref = jnp.where(ref_row, -m_old, 0.0).astype(jnp.bfloat16)
            q_aug = jnp.concatenate([qp[kv], neg_ref, pad_rows], axis=0)
            s = jnp.dot(kt, q_aug, preferred_element_type=jnp.float32)
            p = jnp.exp2(s).astype(jnp.bfloat16)
            over = jnp.max(s, axis=0, keepdims=True)
            pv = jnp.dot(value_rows(ki, kv), p, preferred_element_type=jnp.float32)
            m_new = as_reference(m_old + jnp.maximum(over, 0.0))
            acc_ref[kv] = (acc_ref[kv] + pv) * jnp.exp2(m_old - m_new)
            m_ref[kv] = m_new
            gap_ref[kv] = jnp.maximum(gap_ref[kv], over)

    per_trip = math.gcd(n_kt, _KEY_TILES_PER_TRIP)

    def fast_group(j, carry):
        for t in range(per_trip):
            fast_tile(per_trip * j + t)
        return carry

    lax.fori_loop(0, n_kt // per_trip, fast_group, 0)

    @pl.when(jnp.max(gap_ref[...]) > _MAX_LAGGED_EXPONENT)
    def _():
        m_ref[...] = jnp.full(m_ref.shape, _NEG_BIG, jnp.float32)
        acc_ref[...] = jnp.zeros(acc_ref.shape, jnp.float32)

        def robust_tile(ki, carry):
            kt = key_tile(ki)
            for kv in range(N_KV_HEADS):
                s = jnp.dot(kt, qp[kv], preferred_element_type=jnp.float32)
                m_old = m_ref[kv]
                m_new = jnp.maximum(m_old, jnp.max(s, axis=0, keepdims=True))
                p = jnp.exp2(s - m_new).astype(jnp.bfloat16)
                pv = jnp.dot(value_rows(ki, kv), p, preferred_element_type=jnp.float32)
                acc_ref[kv] = jnp.exp2(m_old - m_new) * acc_ref[kv] + pv
                m_ref[kv] = m_new
            return carry

        lax.fori_loop(0, n_kt, robust_tile, 0)

    for kv in range(N_KV_HEADS):
        acc = acc_ref[kv]
        o = acc[:HEAD_DIM] / acc[HEAD_DIM:HEAD_DIM + 1]
        for g in range(GQA_GROUP):
            r0 = (kv * GQA_GROUP + g) * HEAD_DIM
            o_ref[0, r0:r0 + HEAD_DIM, :] = o[:, g * tq:(g + 1) * tq].astype(jnp.bfloat16)


def _attention(qT, k, vT):
    B, _, S = qT.shape
    tq = Q_TILE
    nq = GQA_GROUP * tq
    n_kt, tk = vT.shape[1], vT.shape[3]
    return pl.pallas_call(
        _attention_kernel,
        grid=(B, S // tq),
        in_specs=[
            pl.BlockSpec((1, Q_W, tq), lambda b, i: (b, 0, i)),
            pl.BlockSpec((1, S, KV_W), lambda b, i: (b, 0, 0)),
            pl.BlockSpec((1, n_kt, KV_W, tk), lambda b, i: (b, 0, 0, 0)),
        ],
        out_specs=pl.BlockSpec((1, Q_W, tq), lambda b, i: (b, 0, i)),
        out_shape=jax.ShapeDtypeStruct((B, Q_W, S), jnp.bfloat16),
        scratch_shapes=[
            pltpu.VMEM((N_KV_HEADS, 1, nq), jnp.float32),
            pltpu.VMEM((N_KV_HEADS, 1, nq), jnp.float32),
            pltpu.VMEM((N_KV_HEADS, HEAD_DIM + _SUM_ROWS, nq), jnp.float32),
        ],
        compiler_params=_compiler_params(("arbitrary", "arbitrary"), 56),
        name="attention",
    )(qT, k, vT)


def _post_kernel(attnT_ref, uT_ref, vsT_ref, gaT_ref, gbT_ref, x_ref,
                 wsT_ref, bs_ref, wpaT_ref, wpbT_ref, wout_ref, gffn_ref, wrT_ref, br_ref,
                 x1_ref, h2_ref, ids_ref, wts_ref, rank_ref, cnt_ref, carry_ref):
    tm = x_ref.shape[1]
    n_chunks = tm // CHUNK
    first = jnp.logical_and(pl.program_id(0) == 0, pl.program_id(1) == 0)

    @pl.when(first)
    def _():
        carry_ref[...] = jnp.zeros(carry_ref.shape, jnp.float32)

    gate_rows = []
    for g in range(SGU_GROUPS):
        r0 = g * SGU_GROUP_DIM
        vs_g = vsT_ref[0, r0:r0 + SGU_GROUP_DIM, :]
        lhs = jnp.concatenate(
            [vs_g[:, c * CHUNK:(c + 1) * CHUNK] for c in range(n_chunks)], axis=0)
        mixed = jnp.dot(lhs, wsT_ref[g], preferred_element_type=jnp.float32)
        mixed = mixed + bs_ref[g]
        mixedT = jnp.concatenate(
            [mixed[c * SGU_GROUP_DIM:(c + 1) * SGU_GROUP_DIM] for c in range(n_chunks)], axis=1)
        u_g = uT_ref[0, r0:r0 + SGU_GROUP_DIM, :].astype(jnp.float32)
        gate_rows.append((u_g * mixedT).astype(jnp.bfloat16))
    gateT = jnp.concatenate(gate_rows, axis=0)

    paT = jnp.dot(wpaT_ref[...], attnT_ref[0], preferred_element_type=jnp.float32)
    pbT = jnp.dot(wpbT_ref[...], gateT, preferred_element_type=jnp.float32)
    mT = (gaT_ref[0].astype(jnp.float32) * paT
          + gbT_ref[0].astype(jnp.float32) * pbT).astype(jnp.bfloat16)
    y = lax.dot_general(mT, wout_ref[...], _TN_DIMS, preferred_element_type=jnp.float32)
    x1 = x_ref[0] + y
    x1_ref[0] = x1

    ms = jnp.mean(x1 * x1, axis=-1, keepdims=True)
    h2 = x1 * lax.rsqrt(ms + EPS) * gffn_ref[...]
    h2_ref[0] = _pack_row_halves(h2)

    logits = lax.dot_general(wrT_ref[...], h2, _NT_DIMS, precision=lax.Precision.HIGHEST,
                             preferred_element_type=jnp.float32) + br_ref[...]
    eidx = lax.broadcasted_iota(jnp.int32, logits.shape, 0).astype(jnp.float32)
    work = logits
    vals, ids, sels = [], [], []
    for _ in range(TOP_K):
        mx = jnp.max(work, axis=0, keepdims=True)
        idx = jnp.min(jnp.where(work == mx, eidx, float(N_EXPERTS)), axis=0, keepdims=True)
        sel = eidx == idx
        vals.append(mx)
        ids.append(idx)
        sels.append(sel)
        work = jnp.where(sel, -jnp.inf, work)
    exps = [jnp.exp(v - vals[0]) for v in vals]
    denom = exps[0] + exps[1] + exps[2] + exps[3]
    ids_ref[...] = jnp.concatenate(ids, axis=0).astype(jnp.int32)
    wts_ref[...] = jnp.concatenate([e / denom for e in exps], axis=0)

    onehot = [s.astype(jnp.float32) for s in sels]
    hits = onehot[0] + onehot[1] + onehot[2] + onehot[3]
    ti = lax.broadcasted_iota(jnp.int32, (tm, tm), 0)
    tj = lax.broadcasted_iota(jnp.int32, (tm, tm), 1)
    upper = (ti < tj).astype(jnp.bfloat16)
    prefix = jnp.dot(hits.astype(jnp.bfloat16), upper, preferred_element_type=jnp.float32)
    base = prefix + carry_ref[...]
    ranks = [jnp.sum(oh * base, axis=0, keepdims=True) for oh in onehot]
    rank_ref[...] = jnp.concatenate(ranks, axis=0).astype(jnp.int32)
    carry_ref[...] = carry_ref[...] + jnp.sum(hits, axis=1, keepdims=True)
    cnt_ref[...] = carry_ref[...].astype(jnp.int32)


def _post(attnT, uT, vsT, gaT, gbT, x, wsT, bs, wpaT, wpbT, wout, gffn, wrT, br):
    B, S, _ = x.shape
    tm = TOKEN_TILE
    nt = S // tm
    T = B * S
    fm = lambda rows: pl.BlockSpec((1, rows, tm), lambda b, i: (b, 0, i))
    rowm = pl.BlockSpec((1, tm, D_MODEL), lambda b, i: (b, i, 0))
    c2 = lambda b, i: (0, 0)
    c3 = lambda b, i: (0, 0, 0)
    tokT = pl.BlockSpec((TOP_K, tm), lambda b, i: (0, b * nt + i))
    return pl.pallas_call(
        _post_kernel,
        grid=(B, nt),
        in_specs=[
            fm(Q_W), fm(SGU_W), fm(SGU_W), fm(D_MODEL), fm(D_MODEL), rowm,
            pl.BlockSpec((SGU_GROUPS, CHUNK, CHUNK), c3),
            pl.BlockSpec((SGU_GROUPS, 1, CHUNK), c3),
            pl.BlockSpec((D_MODEL, Q_W), c2),
            pl.BlockSpec((D_MODEL, SGU_W), c2),
            pl.BlockSpec((D_MODEL, D_MODEL), c2),
            pl.BlockSpec((1, D_MODEL), c2),
            pl.BlockSpec((N_EXPERTS, D_MODEL), c2),
            pl.BlockSpec((N_EXPERTS, 1), c2),
        ],
        out_specs=[rowm, pl.BlockSpec((1, tm, PACKED_W), lambda b, i: (b, i, 0)),
                   tokT, tokT, tokT, pl.BlockSpec((N_EXPERTS, 1), c2)],
        out_shape=[
            jax.ShapeDtypeStruct((B, S, D_MODEL), jnp.float32),
            jax.ShapeDtypeStruct((B, S, PACKED_W), jnp.int32),
            jax.ShapeDtypeStruct((TOP_K, T), jnp.int32),
            jax.ShapeDtypeStruct((TOP_K, T), jnp.float32),
            jax.ShapeDtypeStruct((TOP_K, T), jnp.int32),
            jax.ShapeDtypeStruct((N_EXPERTS, 1), jnp.int32),
        ],
        scratch_shapes=[pltpu.VMEM((N_EXPERTS, 1), jnp.float32)],
        compiler_params=_compiler_params(("arbitrary", "arbitrary"), 56),
        name="post",
    )(attnT, uT, vsT, gaT, gbT, x, wsT, bs, wpaT, wpbT, wout, gffn, wrT, br)


def _sc_kernel(name, body, out_type, n_items, row_shape, row_dtype):
    info = plsc.get_sparse_core_info()
    n_workers = info.num_cores * info.num_subcores
    per_worker = n_items // n_workers
    n_windows = per_worker // SC_WINDOW
    assert per_worker * n_workers == n_items and n_windows * SC_WINDOW == per_worker
    assert n_windows % 2 == 0

    def wrapped(*refs):
        wid = lax.axis_index("subcore") * info.num_cores + lax.axis_index("core")
        body(wid * per_worker, n_windows, *refs)

    return pl.kernel(
        wrapped,
        name=name,
        out_type=out_type,
        mesh=plsc.VectorSubcoreMesh(core_axis_name="core", subcore_axis_name="subcore"),
        scratch_types=[
            pltpu.VMEM((SC_WINDOW,), jnp.int32),
            pltpu.VMEM((SC_WINDOW,), jnp.int32),
            pltpu.VMEM((SC_WINDOW,) + row_shape, row_dtype),
            pltpu.VMEM((SC_WINDOW,) + row_shape, row_dtype),
            pltpu.SemaphoreType.DMA,
            pltpu.SemaphoreType.DMA,
        ],
    )


def _sc_gather_rows(table, idx):
    n, d, win = idx.shape[0], table.shape[1], SC_WINDOW

    def body(base, n_windows, table_hbm, idx_hbm, out_hbm, idx0, idx1, rows0, rows1, sem0, sem1):
        def window(c):
            return pl.ds(pl.multiple_of(base + c * win, win), win)

        def fetch(c, idx_v, rows_v, sem):
            pltpu.sync_copy(idx_hbm.at[window(c)], idx_v)
            pltpu.async_copy(table_hbm.at[idx_v], rows_v, sem)

        def drain(c, idx_v, rows_v, sem):
            pltpu.make_async_copy(table_hbm.at[idx_v], rows_v, sem).wait()
            pltpu.sync_copy(rows_v, out_hbm.at[window(c)])

        fetch(0, idx0, rows0, sem0)

        @pl.loop(0, n_windows, step=2)
        def _(c):
            fetch(c + 1, idx1, rows1, sem1)
            drain(c, idx0, rows0, sem0)

            @pl.when(c + 2 < n_windows)
            def _():
                fetch(c + 2, idx0, rows0, sem0)

            drain(c + 1, idx1, rows1, sem1)

    out_type = jax.ShapeDtypeStruct((n, d), table.dtype)
    return _sc_kernel("sc_gather_rows", body, out_type, n, (d,), table.dtype)(table, idx)


def _sc_scatter_rows(src, dest, n_rows):
    T, d = src.shape
    win = SC_WINDOW

    def body(base, n_windows, src_hbm, dest_hbm, out_hbm, idx0, idx1, rows0, rows1, sem0, sem1):
        def window(c, slot=0):
            return pl.ds(pl.multiple_of(slot * T + base + c * win, win), win)

        def fetch(c, rows_v, sem):
            pltpu.async_copy(src_hbm.at[window(c)], rows_v, sem)

        def push(c, rows_v, sem):
            pltpu.make_async_copy(src_hbm.at[window(c)], rows_v, sem).wait()
            for k in range(TOP_K):
                idx_v = idx0 if k % 2 == 0 else idx1
                pltpu.sync_copy(dest_hbm.at[window(c, k)], idx_v)
                pltpu.sync_copy(rows_v, out_hbm.at[idx_v])

        fetch(0, rows0, sem0)

        @pl.loop(0, n_windows, step=2)
        def _(c):
            fetch(c + 1, rows1, sem1)
            push(c, rows0, sem0)

            @pl.when(c + 2 < n_windows)
            def _():
                fetch(c + 2, rows0, sem0)

            push(c + 1, rows1, sem1)

    out_type = jax.ShapeDtypeStruct((n_rows, d), src.dtype)
    return _sc_kernel("sc_scatter_rows", body, out_type, T, (d,), src.dtype)(
        src, dest.reshape(TOP_K * T))


def _combine_dense_kernel(w_ref, x1_ref, gfin_ref, y_ref, o_ref):
    w = w_ref[...]
    moe = w[:, 0:1] * _unpack_row_halves(y_ref[0])
    for k in range(1, TOP_K):
        moe = moe + w[:, k:k + 1] * _unpack_row_halves(y_ref[k])
    x2 = x1_ref[...] + moe
    ms = jnp.mean(x2 * x2, axis=-1, keepdims=True)
    o_ref[...] = x2 * lax.rsqrt(ms + EPS) * gfin_ref[...]


def _combine_dense(wts, x1, gfin, y_tok):
    T = x1.shape[0]
    tt = COMBINE_TILE
    return pl.pallas_call(
        _combine_dense_kernel,
        grid=(T // tt,),
        in_specs=[
            pl.BlockSpec((tt, TOP_K), lambda i: (i, 0)),
            pl.BlockSpec((tt, D_MODEL), lambda i: (i, 0)),
            pl.BlockSpec((1, D_MODEL), lambda i: (0, 0)),
            pl.BlockSpec((TOP_K, tt, PACKED_W), lambda i: (0, i, 0)),
        ],
        out_specs=pl.BlockSpec((tt, D_MODEL), lambda i: (i, 0)),
        out_shape=jax.ShapeDtypeStruct((T, D_MODEL), jnp.float32),
        compiler_params=_compiler_params(("arbitrary",), 32),
        name="combine_dense",
    )(wts, x1, gfin, y_tok)


def _gate_up_prep_kernel(w_ref, o_ref):
    half = _GU_BLOCK // 2
    i = lax.broadcasted_iota(jnp.int32, (_GU_BLOCK, _GU_BLOCK), 0)
    j = lax.broadcasted_iota(jnp.int32, (_GU_BLOCK, _GU_BLOCK), 1)
    perm = (i == jnp.where(j < half, 2 * j, 2 * (j - half) + 1)).astype(jnp.bfloat16)
    for c in range(2 * D_FF // _GU_BLOCK):
        cols = slice(c * _GU_BLOCK, (c + 1) * _GU_BLOCK)
        blk = w_ref[0, :, cols].astype(jnp.bfloat16)
        o_ref[0, :, cols] = jnp.dot(blk, perm, preferred_element_type=jnp.float32).astype(jnp.bfloat16)


def _gate_up_prep(w_gate_up):
    spec = pl.BlockSpec((1, D_MODEL, 2 * D_FF), lambda e: (e, 0, 0))
    return pl.pallas_call(
        _gate_up_prep_kernel,
        grid=(N_EXPERTS,),
        in_specs=[spec],
        out_specs=spec,
        out_shape=jax.ShapeDtypeStruct(w_gate_up.shape, jnp.bfloat16),
        compiler_params=_compiler_params(("arbitrary",), 48),
        name="gate_up_prep",
    )(w_gate_up)


def _split_gate_up(gu):
    half = _GU_BLOCK // 2
    n = gu.shape[1] // _GU_BLOCK
    glu = jnp.concatenate([gu[:, c * _GU_BLOCK:c * _GU_BLOCK + half] for c in range(n)], axis=1)
    lin = jnp.concatenate([gu[:, c * _GU_BLOCK + half:(c + 1) * _GU_BLOCK] for c in range(n)], axis=1)
    return glu, lin


def _experts_kernel(te_ref, nv_ref, x_ref, wgu_ref, bgu_ref, wd_ref, bd_ref, y_ref):
    @pl.when(pl.program_id(0) < nv_ref[0])
    def _():
        x = _unpack_row_halves(x_ref[...]).astype(jnp.bfloat16)
        gu = jnp.dot(x, wgu_ref[0], preferred_element_type=jnp.float32) + bgu_ref[0]
        glu, lin = _split_gate_up(gu)
        glu = jnp.minimum(glu, SWIGLU_LIMIT)
        lin = jnp.clip(lin, -SWIGLU_LIMIT, SWIGLU_LIMIT)
        a = glu * jax.nn.sigmoid(SWIGLU_ALPHA * glu) * (lin + 1.0)
        y = jnp.dot(a.astype(jnp.bfloat16), wd_ref[0].astype(jnp.bfloat16),
                    preferred_element_type=jnp.float32) + bd_ref[0]
        y_ref[...] = _pack_row_halves(y)


def _experts(tile_expert, n_valid, x_sorted, wgu, bgu, wd, bd):
    n_rows = x_sorted.shape[0]
    tr = EXPERT_TILE
    n_tiles = n_rows // tr

    def row_map(j, te, nv):
        return (jnp.minimum(j, nv[0] - 1), 0)

    def exp_map(j, te, nv):
        return (te[j], 0, 0)

    grid_spec = pltpu.PrefetchScalarGridSpec(
        num_scalar_prefetch=2,
        grid=(n_tiles,),
        in_specs=[
            pl.BlockSpec((tr, PACKED_W), row_map),
            pl.BlockSpec((1, D_MODEL, 2 * D_FF), exp_map),
            pl.BlockSpec((1, 1, 2 * D_FF), exp_map),
            pl.BlockSpec((1, D_FF, D_MODEL), exp_map),
            pl.BlockSpec((1, 1, D_MODEL), exp_map),
        ],
        out_specs=pl.BlockSpec((tr, PACKED_W), row_map),
    )
    return pl.pallas_call(
        _experts_kernel,
        grid_spec=grid_spec,
        out_shape=jax.ShapeDtypeStruct((n_rows, PACKED_W), jnp.int32),
        compiler_params=_compiler_params(("arbitrary",), 56),
        name="experts",
    )(tile_expert, n_valid, x_sorted, wgu, bgu, wd, bd)


def _rope_tables(S, gain, scale):
    t = jnp.arange(S, dtype=jnp.int32)
    r = (t // GRID_W).astype(jnp.float32)
    c = (t % GRID_W).astype(jnp.float32)
    inv = jnp.float32(ROPE_THETA) ** (
        -jnp.arange(0, ROPE_AXIS_DIM, 2, dtype=jnp.float32) / ROPE_AXIS_DIM)
    ang = jnp.concatenate([r[None, :] * inv[:, None], c[None, :] * inv[:, None]], axis=0)
    cos = jnp.repeat(jnp.cos(ang), 2, axis=0)
    sin = jnp.repeat(jnp.sin(ang), 2, axis=0)
    sign = jnp.where(jnp.arange(HEAD_DIM) % 2 == 0, -1.0, 1.0).astype(jnp.float32)
    g = gain.astype(jnp.float32) * scale
    g_swapped = g.reshape(HEAD_DIM // 2, 2)[:, ::-1].reshape(HEAD_DIM)
    return g[:, None] * cos, (g_swapped * sign)[:, None] * sin


def _prepare_weights(norm_mix_g, w_in, q_norm_g, k_norm_g, sgu_norm_g, w_spatial, b_spatial,
                     w_proj_attn, w_proj_sgu, w_out, norm_ffn_g, w_router, b_router,
                     w_gate_up, b_gate_up, w_down, b_down, norm_final_g):
    bf = jnp.bfloat16
    l = 0
    return dict(
        gmix=norm_mix_g[l][None, :],
        w_inT=w_in[l].T.astype(bf),
        q_gain=q_norm_g[l], k_gain=k_norm_g[l],
        gs=jnp.broadcast_to(sgu_norm_g[l][:, None], (SGU_W, _LANES)),
        wsT=jnp.swapaxes(w_spatial[l], 1, 2).astype(bf),
        bs=b_spatial[l][:, None, :],
        wpaT=w_proj_attn[l].T.astype(bf),
        wpbT=w_proj_sgu[l].T.astype(bf),
        wout=w_out[l].astype(bf),
        gffn=norm_ffn_g[l][None, :],
        wrT=w_router[l].T,
        br=b_router[l][:, None],
        wgu=_gate_up_prep(w_gate_up[l]),
        bgu=b_gate_up[l].reshape(N_EXPERTS, -1, _GU_BLOCK // 2, 2).transpose(0, 1, 3, 2)
        .reshape(N_EXPERTS, 1, 2 * D_FF),
        wd=w_down[l],
        bd=b_down[l][:, None, :],
        gfin=norm_final_g[None, :],
    )


def _trunk(x, w, rope):
    B, S, _ = x.shape
    T = B * S
    qT, k, vT, uT, vsT, gaT, gbT = _in_proj(x, w["gmix"], w["w_inT"], *rope, w["gs"])
    attnT = _attention(qT, k, vT)
    x1, h2, ids, wts, rank, counts = _post(
        attnT, uT, vsT, gaT, gbT, x, w["wsT"], w["bs"], w["wpaT"], w["wpbT"], w["wout"],
        w["gffn"], w["wrT"], w["br"])

    tr = EXPERT_TILE
    counts = counts[:, 0]
    padded = (counts + tr - 1) // tr * tr
    ends = jnp.cumsum(padded)
    starts = ends - padded
    n_tiles = (TOP_K * T) // tr + N_EXPERTS
    n_rows = n_tiles * tr
    dest = rank
    for e in range(N_EXPERTS):
        dest = dest + jnp.where(ids == e, starts[e], 0)
    tile_start = jnp.arange(n_tiles, dtype=jnp.int32) * tr
    tile_expert = jnp.minimum(
        jnp.sum((tile_start[:, None] >= ends[None, :]).astype(jnp.int32), axis=1), N_EXPERTS - 1)
    n_valid = (ends[-1] // tr).astype(jnp.int32)[None]

    x_sorted = _sc_scatter_rows(h2.reshape(T, PACKED_W), dest, n_rows)
    y_sorted = _experts(tile_expert, n_valid, x_sorted, w["wgu"], w["bgu"], w["wd"], w["bd"])
    y_tok = _sc_gather_rows(y_sorted, dest.reshape(TOP_K * T)).reshape(TOP_K, T, PACKED_W)
    out = _combine_dense(wts.T, x1.reshape(T, D_MODEL), w["gfin"], y_tok)
    return out.reshape(B, S, D_MODEL)


def kernel(x_prompt, x_sample, norm_mix_g, w_in, q_norm_g, k_norm_g, sgu_norm_g, w_spatial,
           b_spatial, w_proj_attn, w_proj_sgu, w_out, norm_ffn_g, w_router, b_router,
           w_gate_up, b_gate_up, w_down, b_down, norm_final_g):
    w = _prepare_weights(norm_mix_g, w_in, q_norm_g, k_norm_g, sgu_norm_g, w_spatial,
                         b_spatial, w_proj_attn, w_proj_sgu, w_out, norm_ffn_g, w_router,
                         b_router, w_gate_up, b_gate_up, w_down, b_down, norm_final_g)
    s_max = max(x_prompt.shape[1], x_sample.shape[1])
    rope = (*_rope_tables(s_max, w["q_gain"], math.log2(math.e) / math.sqrt(HEAD_DIM)),
            *_rope_tables(s_max, w["k_gain"], 1.0))
    return (_trunk(x_prompt, w, rope), _trunk(x_sample, w, rope))
```

```python
import functools
import math

import jax
import jax.numpy as jnp
from jax import lax
from jax.experimental import pallas as pl
from jax.experimental.pallas import tpu as pltpu
from jax.experimental.pallas import tpu_sc as plsc

D_MODEL = 1024
GRID_W = 64
N_HEADS = 8
N_KV_HEADS = 2
HEAD_DIM = 64
GQA_GROUP = N_HEADS // N_KV_HEADS
Q_W = N_HEADS * HEAD_DIM
KV_W = N_KV_HEADS * HEAD_DIM
ROPE_AXIS_DIM = HEAD_DIM // 2
ROPE_THETA = 10000.0
SGU_GROUPS = 8
SGU_W = D_MODEL // 2
SGU_GROUP_DIM = SGU_W // SGU_GROUPS
CHUNK = 128
IN_W = Q_W + 2 * KV_W + 2 * SGU_W + 2 * D_MODEL
N_EXPERTS = 32
TOP_K = 4
D_FF = D_MODEL
SWIGLU_LIMIT = 7.0
SWIGLU_ALPHA = 1.702
EPS = 1e-6

_Q0, _K0, _V0 = 0, Q_W, Q_W + KV_W
_U0 = Q_W + 2 * KV_W
_VS0 = _U0 + SGU_W
_GA0 = _VS0 + SGU_W
_GB0 = _GA0 + D_MODEL

TOKEN_TILE = 512
Q_TILE = 512
KEY_TILE = TOKEN_TILE
EXPERT_TILE = 512
COMBINE_TILE = 512
PACKED_W = D_MODEL // 2
SC_WINDOW = 64

_LANES = 128
_BF16_SUBLANES = 16
_SUM_ROWS = _BF16_SUBLANES
_GU_BLOCK = 256
_NEG_BIG = -1e30
_HIGH_HALF_MASK = 0xFFFF0000
_MAX_LAGGED_EXPONENT = 80.0
_KEY_TILES_PER_TRIP = 4
_MIB = 1024 * 1024

_NT_DIMS = (((1,), (1,)), ((), ()))
_TN_DIMS = (((0,), (0,)), ((), ()))


def _compiler_params(semantics, vmem_mib):
    return pltpu.CompilerParams(
        dimension_semantics=semantics, vmem_limit_bytes=vmem_mib * _MIB)


def _swap_adjacent_rows(x):
    n = x.shape[0]
    row = lax.broadcasted_iota(jnp.int32, x.shape, 0)
    nxt = pltpu.roll(x, n - 1, 0)
    prv = pltpu.roll(x, 1, 0)
    return jnp.where((row & 1) == 0, nxt, prv)


def _head_norm_rope(z, tab_a, tab_b, n_heads):
    tm = z.shape[1]
    z3 = z.reshape(n_heads, HEAD_DIM, tm)
    ms = jnp.mean(z3 * z3, axis=1, keepdims=True)
    r = lax.rsqrt(ms + EPS)
    zs = _swap_adjacent_rows(z).reshape(n_heads, HEAD_DIM, tm)
    out = (z3 * tab_a[None] + zs * tab_b[None]) * r
    return out.reshape(n_heads * HEAD_DIM, tm)


def _gelu(x):
    return 0.5 * x * (1.0 + lax.erf(x * (1.0 / math.sqrt(2.0))))


def _pack_row_halves(x):
    half = x.shape[1] // 2

    def bf16_bits(v):
        return lax.bitcast_convert_type(v.astype(jnp.bfloat16).astype(jnp.float32), jnp.uint32)

    word = (bf16_bits(x[:, :half]) >> 16) | (bf16_bits(x[:, half:]) & jnp.uint32(_HIGH_HALF_MASK))
    return lax.bitcast_convert_type(word, jnp.int32)


def _unpack_row_halves(p):
    word = lax.bitcast_convert_type(p, jnp.uint32)
    lo = lax.bitcast_convert_type(word << 16, jnp.float32)
    hi = lax.bitcast_convert_type(word & jnp.uint32(_HIGH_HALF_MASK), jnp.float32)
    return jnp.concatenate([lo, hi], axis=1)


def _tile_lanes(x, reps):
    return jnp.concatenate([x] * reps, axis=1) if reps > 1 else x


def _in_proj_kernel(x_ref, gmix_ref, w_ref, qa_ref, qb_ref, ka_ref, kb_ref, gs_ref,
                    qT_ref, k_ref, vT_ref, uT_ref, vsT_ref, gaT_ref, gbT_ref):
    tm = x_ref.shape[1]
    x = x_ref[0]
    ms = jnp.mean(x * x, axis=-1, keepdims=True)
    h = (x * lax.rsqrt(ms + EPS) * gmix_ref[...]).astype(jnp.bfloat16)

    def proj(r0, rows):
        return lax.dot_general(w_ref[r0:r0 + rows, :], h, _NT_DIMS,
                               preferred_element_type=jnp.float32)

    zq = proj(_Q0, Q_W)
    qT_ref[0] = _head_norm_rope(zq, qa_ref[...], qb_ref[...], N_HEADS).astype(jnp.bfloat16)

    zkv = proj(_K0, 2 * KV_W)
    kT = _head_norm_rope(zkv[:KV_W], ka_ref[...], kb_ref[...], N_KV_HEADS)
    k_ref[0] = kT.T.astype(jnp.bfloat16)
    vT_ref[0, 0] = zkv[KV_W:].astype(jnp.bfloat16)

    uT_ref[0] = _gelu(proj(_U0, SGU_W)).astype(jnp.bfloat16)

    vs = _gelu(proj(_VS0, SGU_W))
    vms = jnp.mean(vs * vs, axis=0, keepdims=True)
    gs = _tile_lanes(gs_ref[...], tm // _LANES)
    vsT_ref[0] = (vs * lax.rsqrt(vms + EPS) * gs).astype(jnp.bfloat16)

    gaT_ref[0] = jax.nn.sigmoid(proj(_GA0, D_MODEL)).astype(jnp.bfloat16)
    gbT_ref[0] = jax.nn.sigmoid(proj(_GB0, D_MODEL)).astype(jnp.bfloat16)


def _in_proj(x, gmix, w_inT, qa, qb, ka, kb, gs):
    B, S, _ = x.shape
    tm = TOKEN_TILE
    nt = S // tm
    bf = jnp.bfloat16
    const2 = lambda b, i: (0, 0)
    tab = pl.BlockSpec((HEAD_DIM, tm), lambda b, i: (0, i))
    fm = lambda rows: pl.BlockSpec((1, rows, tm), lambda b, i: (b, 0, i))
    return pl.pallas_call(
        _in_proj_kernel,
        grid=(B, nt),
        in_specs=[
            pl.BlockSpec((1, tm, D_MODEL), lambda b, i: (b, i, 0)),
            pl.BlockSpec((1, D_MODEL), const2),
            pl.BlockSpec((IN_W, D_MODEL), const2),
            tab, tab, tab, tab,
            pl.BlockSpec((SGU_W, _LANES), const2),
        ],
        out_specs=[
            fm(Q_W),
            pl.BlockSpec((1, tm, KV_W), lambda b, i: (b, i, 0)),
            pl.BlockSpec((1, 1, KV_W, tm), lambda b, i: (b, i, 0, 0)),
            fm(SGU_W), fm(SGU_W), fm(D_MODEL), fm(D_MODEL),
        ],
        out_shape=[
            jax.ShapeDtypeStruct((B, Q_W, S), bf),
            jax.ShapeDtypeStruct((B, S, KV_W), bf),
            jax.ShapeDtypeStruct((B, nt, KV_W, tm), bf),
            jax.ShapeDtypeStruct((B, SGU_W, S), bf),
            jax.ShapeDtypeStruct((B, SGU_W, S), bf),
            jax.ShapeDtypeStruct((B, D_MODEL, S), bf),
            jax.ShapeDtypeStruct((B, D_MODEL, S), bf),
        ],
        compiler_params=_compiler_params(("arbitrary", "arbitrary"), 56),
        name="in_proj",
    )(x, gmix, w_inT, qa, qb, ka, kb, gs)


def _attention_kernel(qT_ref, k_ref, vT_ref, o_ref, m_ref, gap_ref, acc_ref):
    tq = qT_ref.shape[2]
    n_kt = vT_ref.shape[1]
    tk = vT_ref.shape[3]
    nq = GQA_GROUP * tq
    ones_rows = (lax.broadcasted_iota(jnp.int32, (_SUM_ROWS, tk), 0) == 0).astype(jnp.bfloat16)

    zeros = jnp.zeros((HEAD_DIM, nq), jnp.bfloat16)
    qp = []
    for kv in range(N_KV_HEADS):
        heads = [qT_ref[0, (kv * GQA_GROUP + g) * HEAD_DIM:(kv * GQA_GROUP + g + 1) * HEAD_DIM, :]
                 for g in range(GQA_GROUP)]
        qk = jnp.concatenate(heads, axis=1)
        qp.append(jnp.concatenate([qk, zeros] if kv == 0 else [zeros, qk], axis=0))

    def key_tile(ki):
        return k_ref[0, pl.ds(pl.multiple_of(ki * tk, tk), tk), :]

    def value_rows(ki, kv):
        vt = vT_ref[0, ki]
        return jnp.concatenate([vt[kv * HEAD_DIM:(kv + 1) * HEAD_DIM, :], ones_rows], axis=0)

    acc_ref[...] = jnp.zeros(acc_ref.shape, jnp.float32)
    gap_ref[...] = jnp.zeros(gap_ref.shape, jnp.float32)
    def as_reference(m):
        return m.astype(jnp.bfloat16).astype(jnp.float32)

    first_keys = k_ref[0, 0:_BF16_SUBLANES, :]
    for kv in range(N_KV_HEADS):
        s0 = jnp.dot(first_keys, qp[kv], preferred_element_type=jnp.float32)
        m_ref[kv] = as_reference(jnp.max(s0, axis=0, keepdims=True))

    ones_cols = (lax.broadcasted_iota(jnp.int32, (tk, _LANES), 1) == 0).astype(jnp.bfloat16)
    ref_row = lax.broadcasted_iota(jnp.int32, (_BF16_SUBLANES, nq), 0) == 0
    pad_rows = jnp.zeros((_LANES - _BF16_SUBLANES, nq), jnp.bfloat16)

    def fast_tile(ki):
        kt = jnp.concatenate([key_tile(ki), ones_cols], axis=1)
        for kv in range(N_KV_HEADS):
            m_old = m_ref[kv]
            neg_ref = jnp.where(ref_row, -m_old, 0.0).astype(jnp.bfloat16)
            q_aug = jnp.concatenate([qp[kv], neg_ref, pad_rows], axis=0)
            s = jnp.dot(kt, q_aug, preferred_element_type=jnp.float32)
            p = jnp.exp2(s).astype(jnp.bfloat16)
            over = jnp.max(s, axis=0, keepdims=True)
            pv = jnp.dot(value_rows(ki, kv), p, preferred_element_type=jnp.float32)
            m_new = as_reference(m_old + jnp.maximum(over, 0.0))
            acc_ref[kv] = (acc_ref[kv] + pv) * jnp.exp2(m_old - m_new)
            m_ref[kv] = m_new
            gap_ref[kv] = jnp.maximum(gap_ref[kv], over)

    per_trip = math.gcd(n_kt, _KEY_TILES_PER_TRIP)

    def fast_group(j, carry):
        for t in range(per_trip):
            fast_tile(per_trip * j + t)
        return carry

    lax.fori_loop(0, n_kt // per_trip, fast_group, 0)

    @pl.when(jnp.max(gap_ref[...]) > _MAX_LAGGED_EXPONENT)
    def _():
        m_ref[...] = jnp.full(m_ref.shape, _NEG_BIG, jnp.float32)
        acc_ref[...] = jnp.zeros(acc_ref.shape, jnp.float32)

        def robust_tile(ki, carry):
            kt = key_tile(ki)
            for kv in range(N_KV_HEADS):
                s = jnp.dot(kt, qp[kv], preferred_element_type=jnp.float32)
                m_old = m_ref[kv]
                m_new = jnp.maximum(m_old, jnp.max(s, axis=0, keepdims=True))
                p = jnp.exp2(s - m_new).astype(jnp.bfloat16)
                pv = jnp.dot(value_rows(ki, kv), p, preferred_element_type=jnp.float32)
                acc_ref[kv] = jnp.exp2(m_old - m_new) * acc_ref[kv] + pv
                m_ref[kv] = m_new
            return carry

        lax.fori_loop(0, n_kt, robust_tile, 0)

    for kv in range(N_KV_HEADS):
        acc = acc_ref[kv]
        o = acc[:HEAD_DIM] / acc[HEAD_DIM:HEAD_DIM + 1]
        for g in range(GQA_GROUP):
            r0 = (kv * GQA_GROUP + g) * HEAD_DIM
            o_ref[0, r0:r0 + HEAD_DIM, :] = o[:, g * tq:(g + 1) * tq].astype(jnp.bfloat16)


def _attention(qT, k, vT):
    B, _, S = qT.shape
    tq = Q_TILE
    nq = GQA_GROUP * tq
    n_kt, tk = vT.shape[1], vT.shape[3]
    return pl.pallas_call(
        _attention_kernel,
        grid=(B, S // tq),
        in_specs=[
            pl.BlockSpec((1, Q_W, tq), lambda b, i: (b, 0, i)),
            pl.BlockSpec((1, S, KV_W), lambda b, i: (b, 0, 0)),
            pl.BlockSpec((1, n_kt, KV_W, tk), lambda b, i: (b, 0, 0, 0)),
        ],
        out_specs=pl.BlockSpec((1, Q_W, tq), lambda b, i: (b, 0, i)),
        out_shape=jax.ShapeDtypeStruct((B, Q_W, S), jnp.bfloat16),
        scratch_shapes=[
            pltpu.VMEM((N_KV_HEADS, 1, nq), jnp.float32),
            pltpu.VMEM((N_KV_HEADS, 1, nq), jnp.float32),
            pltpu.VMEM((N_KV_HEADS, HEAD_DIM + _SUM_ROWS, nq), jnp.float32),
        ],
        compiler_params=_compiler_params(("arbitrary", "arbitrary"), 56),
        name="attention",
    )(qT, k, vT)


def _post_kernel(attnT_ref, uT_ref, vsT_ref, gaT_ref, gbT_ref, x_ref,
                 wsT_ref, bs_ref, wpaT_ref, wpbT_ref, wout_ref, gffn_ref, wrT_ref, br_ref,
                 x1_ref, h2_ref, ids_ref, wts_ref, rank_ref, cnt_ref, carry_ref):
    tm = x_ref.shape[1]
    n_chunks = tm // CHUNK
    first = jnp.logical_and(pl.program_id(0) == 0, pl.program_id(1) == 0)

    @pl.when(first)
    def _():
        carry_ref[...] = jnp.zeros(carry_ref.shape, jnp.float32)

    gate_rows = []
    for g in range(SGU_GROUPS):
        r0 = g * SGU_GROUP_DIM
        vs_g = vsT_ref[0, r0:r0 + SGU_GROUP_DIM, :]
        lhs = jnp.concatenate(
            [vs_g[:, c * CHUNK:(c + 1) * CHUNK] for c in range(n_chunks)], axis=0)
        mixed = jnp.dot(lhs, wsT_ref[g], preferred_element_type=jnp.float32)
        mixed = mixed + bs_ref[g]
        mixedT = jnp.concatenate(
            [mixed[c * SGU_GROUP_DIM:(c + 1) * SGU_GROUP_DIM] for c in range(n_chunks)], axis=1)
        u_g = uT_ref[0, r0:r0 + SGU_GROUP_DIM, :].astype(jnp.float32)
        gate_rows.append((u_g * mixedT).astype(jnp.bfloat16))
    gateT = jnp.concatenate(gate_rows, axis=0)

    paT = jnp.dot(wpaT_ref[...], attnT_ref[0], preferred_element_type=jnp.float32)
    pbT = jnp.dot(wpbT_ref[...], gateT, preferred_element_type=jnp.float32)
    mT = (gaT_ref[0].astype(jnp.float32) * paT
          + gbT_ref[0].astype(jnp.float32) * pbT).astype(jnp.bfloat16)
    y = lax.dot_general(mT, wout_ref[...], _TN_DIMS, preferred_element_type=jnp.float32)
    x1 = x_ref[0] + y
    x1_ref[0] = x1

    ms = jnp.mean(x1 * x1, axis=-1, keepdims=True)
    h2 = x1 * lax.rsqrt(ms + EPS) * gffn_ref[...]
    h2_ref[0] = _pack_row_halves(h2)

    h_hi = h2.astype(jnp.bfloat16)
    h_lo = (h2 - h_hi.astype(jnp.float32)).astype(jnp.bfloat16)
    by_hi = lax.dot_general(wrT_ref[...], h_hi, _NT_DIMS, preferred_element_type=jnp.float32)
    by_lo = lax.dot_general(wrT_ref[:N_EXPERTS, :], h_lo, _NT_DIMS,
                            preferred_element_type=jnp.float32)
    logits = by_hi[:N_EXPERTS] + by_hi[N_EXPERTS:] + by_lo + br_ref[...]
    eidx = lax.broadcasted_iota(jnp.int32, logits.shape, 0).astype(jnp.float32)
    work = logits
    vals, ids, sels = [], [], []
    for _ in range(TOP_K):
        mx = jnp.max(work, axis=0, keepdims=True)
        idx = jnp.min(jnp.where(work == mx, eidx, float(N_EXPERTS)), axis=0, keepdims=True)
        sel = eidx == idx
        vals.append(mx)
        ids.append(idx)
        sels.append(sel)
        work = jnp.where(sel, -jnp.inf, work)
    exps = [jnp.exp(v - vals[0]) for v in vals]
    denom = exps[0] + exps[1] + exps[2] + exps[3]
    ids_ref[...] = jnp.concatenate(ids, axis=0).astype(jnp.int32)
    wts_ref[...] = jnp.concatenate([e / denom for e in exps], axis=0)

    onehot = [s.astype(jnp.float32) for s in sels]
    hits = onehot[0] + onehot[1] + onehot[2] + onehot[3]
    ti = lax.broadcasted_iota(jnp.int32, (tm, tm), 0)
    tj = lax.broadcasted_iota(jnp.int32, (tm, tm), 1)
    upper = (ti < tj).astype(jnp.bfloat16)
    prefix = jnp.dot(hits.astype(jnp.bfloat16), upper, preferred_element_type=jnp.float32)
    base = prefix + carry_ref[...]
    ranks = [jnp.sum(oh * base, axis=0, keepdims=True) for oh in onehot]
    rank_ref[...] = jnp.concatenate(ranks, axis=0).astype(jnp.int32)
    carry_ref[...] = carry_ref[...] + jnp.sum(hits, axis=1, keepdims=True)
    cnt_ref[...] = carry_ref[...].astype(jnp.int32)


def _post(attnT, uT, vsT, gaT, gbT, x, wsT, bs, wpaT, wpbT, wout, gffn, wrT, br):
    B, S, _ = x.shape
    tm = TOKEN_TILE
    nt = S // tm
    T = B * S
    fm = lambda rows: pl.BlockSpec((1, rows, tm), lambda b, i: (b, 0, i))
    rowm = pl.BlockSpec((1, tm, D_MODEL), lambda b, i: (b, i, 0))
    c2 = lambda b, i: (0, 0)
    c3 = lambda b, i: (0, 0, 0)
    tokT = pl.BlockSpec((TOP_K, tm), lambda b, i: (0, b * nt + i))
    return pl.pallas_call(
        _post_kernel,
        grid=(B, nt),
        in_specs=[
            fm(Q_W), fm(SGU_W), fm(SGU_W), fm(D_MODEL), fm(D_MODEL), rowm,
            pl.BlockSpec((SGU_GROUPS, CHUNK, CHUNK), c3),
            pl.BlockSpec((SGU_GROUPS, 1, CHUNK), c3),
            pl.BlockSpec((D_MODEL, Q_W), c2),
            pl.BlockSpec((D_MODEL, SGU_W), c2),
            pl.BlockSpec((D_MODEL, D_MODEL), c2),
            pl.BlockSpec((1, D_MODEL), c2),
            pl.BlockSpec((2 * N_EXPERTS, D_MODEL), c2),
            pl.BlockSpec((N_EXPERTS, 1), c2),
        ],
        out_specs=[rowm, pl.BlockSpec((1, tm, PACKED_W), lambda b, i: (b, i, 0)),
                   tokT, tokT, tokT, pl.BlockSpec((N_EXPERTS, 1), c2)],
        out_shape=[
            jax.ShapeDtypeStruct((B, S, D_MODEL), jnp.float32),
            jax.ShapeDtypeStruct((B, S, PACKED_W), jnp.int32),
            jax.ShapeDtypeStruct((TOP_K, T), jnp.int32),
            jax.ShapeDtypeStruct((TOP_K, T), jnp.float32),
            jax.ShapeDtypeStruct((TOP_K, T), jnp.int32),
            jax.ShapeDtypeStruct((N_EXPERTS, 1), jnp.int32),
        ],
        scratch_shapes=[pltpu.VMEM((N_EXPERTS, 1), jnp.float32)],
        compiler_params=_compiler_params(("arbitrary", "arbitrary"), 56),
        name="post",
    )(attnT, uT, vsT, gaT, gbT, x, wsT, bs, wpaT, wpbT, wout, gffn, wrT, br)


def _sc_kernel(name, body, out_type, n_items, row_shape, row_dtype):
    info = plsc.get_sparse_core_info()
    n_workers = info.num_cores * info.num_subcores
    per_worker = n_items // n_workers
    n_windows = per_worker // SC_WINDOW
    assert per_worker * n_workers == n_items and n_windows * SC_WINDOW == per_worker
    assert n_windows % 2 == 0

    def wrapped(*refs):
        wid = lax.axis_index("subcore") * info.num_cores + lax.axis_index("core")
        body(wid * per_worker, n_windows, *refs)

    return pl.kernel(
        wrapped,
        name=name,
        out_type=out_type,
        mesh=plsc.VectorSubcoreMesh(core_axis_name="core", subcore_axis_name="subcore"),
        scratch_types=[
            pltpu.VMEM((SC_WINDOW,), jnp.int32),
            pltpu.VMEM((SC_WINDOW,), jnp.int32),
            pltpu.VMEM((SC_WINDOW,) + row_shape, row_dtype),
            pltpu.VMEM((SC_WINDOW,) + row_shape, row_dtype),
            pltpu.SemaphoreType.DMA,
            pltpu.SemaphoreType.DMA,
        ],
    )


def _sc_gather_rows(table, idx):
    n, d, win = idx.shape[0], table.shape[1], SC_WINDOW

    def body(base, n_windows, table_hbm, idx_hbm, out_hbm, idx0, idx1, rows0, rows1, sem0, sem1):
        def window(c):
            return pl.ds(pl.multiple_of(base + c * win, win), win)

        def fetch(c, idx_v, rows_v, sem):
            pltpu.sync_copy(idx_hbm.at[window(c)], idx_v)
            pltpu.async_copy(table_hbm.at[idx_v], rows_v, sem)

        def drain(c, idx_v, rows_v, sem):
            pltpu.make_async_copy(table_hbm.at[idx_v], rows_v, sem).wait()
            pltpu.sync_copy(rows_v, out_hbm.at[window(c)])

        fetch(0, idx0, rows0, sem0)

        @pl.loop(0, n_windows, step=2)
        def _(c):
            fetch(c + 1, idx1, rows1, sem1)
            drain(c, idx0, rows0, sem0)

            @pl.when(c + 2 < n_windows)
            def _():
                fetch(c + 2, idx0, rows0, sem0)

            drain(c + 1, idx1, rows1, sem1)

    out_type = jax.ShapeDtypeStruct((n, d), table.dtype)
    return _sc_kernel("sc_gather_rows", body, out_type, n, (d,), table.dtype)(table, idx)


def _sc_scatter_rows(src, dest, n_rows):
    T, d = src.shape
    win = SC_WINDOW

    def body(base, n_windows, src_hbm, dest_hbm, out_hbm, idx0, idx1, rows0, rows1, sem0, sem1):
        def window(c, slot=0):
            return pl.ds(pl.multiple_of(slot * T + base + c * win, win), win)

        def fetch(c, rows_v, sem):
            pltpu.async_copy(src_hbm.at[window(c)], rows_v, sem)

        def push(c, rows_v, sem):
            pltpu.make_async_copy(src_hbm.at[window(c)], rows_v, sem).wait()
            for k in range(TOP_K):
                idx_v = idx0 if k % 2 == 0 else idx1
                pltpu.sync_copy(dest_hbm.at[window(c, k)], idx_v)
                pltpu.sync_copy(rows_v, out_hbm.at[idx_v])

        fetch(0, rows0, sem0)

        @pl.loop(0, n_windows, step=2)
        def _(c):
            fetch(c + 1, rows1, sem1)
            push(c, rows0, sem0)

            @pl.when(c + 2 < n_windows)
            def _():
                fetch(c + 2, rows0, sem0)

            push(c + 1, rows1, sem1)

    out_type = jax.ShapeDtypeStruct((n_rows, d), src.dtype)
    return _sc_kernel("sc_scatter_rows", body, out_type, T, (d,), src.dtype)(
        src, dest.reshape(TOP_K * T))


def _combine_dense_kernel(w_ref, x1_ref, gfin_ref, y_ref, o_ref):
    w = w_ref[...]
    moe = w[:, 0:1] * _unpack_row_halves(y_ref[0])
    for k in range(1, TOP_K):
        moe = moe + w[:, k:k + 1] * _unpack_row_halves(y_ref[k])
    x2 = x1_ref[...] + moe
    ms = jnp.mean(x2 * x2, axis=-1, keepdims=True)
    o_ref[...] = x2 * lax.rsqrt(ms + EPS) * gfin_ref[...]


def _combine_dense(wts, x1, gfin, y_tok):
    T = x1.shape[0]
    tt = COMBINE_TILE
    return pl.pallas_call(
        _combine_dense_kernel,
        grid=(T // tt,),
        in_specs=[
            pl.BlockSpec((tt, TOP_K), lambda i: (i, 0)),
            pl.BlockSpec((tt, D_MODEL), lambda i: (i, 0)),
            pl.BlockSpec((1, D_MODEL), lambda i: (0, 0)),
            pl.BlockSpec((TOP_K, tt, PACKED_W), lambda i: (0, i, 0)),
        ],
        out_specs=pl.BlockSpec((tt, D_MODEL), lambda i: (i, 0)),
        out_shape=jax.ShapeDtypeStruct((T, D_MODEL), jnp.float32),
        compiler_params=_compiler_params(("arbitrary",), 32),
        name="combine_dense",
    )(wts, x1, gfin, y_tok)


def _gate_up_prep_kernel(w_ref, o_ref):
    half = _GU_BLOCK // 2
    i = lax.broadcasted_iota(jnp.int32, (_GU_BLOCK, _GU_BLOCK), 0)
    j = lax.broadcasted_iota(jnp.int32, (_GU_BLOCK, _GU_BLOCK), 1)
    perm = (i == jnp.where(j < half, 2 * j, 2 * (j - half) + 1)).astype(jnp.bfloat16)
    for c in range(2 * D_FF // _GU_BLOCK):
        cols = slice(c * _GU_BLOCK, (c + 1) * _GU_BLOCK)
        blk = w_ref[0, :, cols].astype(jnp.bfloat16)
        o_ref[0, :, cols] = jnp.dot(blk, perm, preferred_element_type=jnp.float32).astype(jnp.bfloat16)


def _gate_up_prep(w_gate_up):
    spec = pl.BlockSpec((1, D_MODEL, 2 * D_FF), lambda e: (e, 0, 0))
    return pl.pallas_call(
        _gate_up_prep_kernel,
        grid=(N_EXPERTS,),
        in_specs=[spec],
        out_specs=spec,
        out_shape=jax.ShapeDtypeStruct(w_gate_up.shape, jnp.bfloat16),
        compiler_params=_compiler_params(("arbitrary",), 48),
        name="gate_up_prep",
    )(w_gate_up)


def _split_gate_up(gu):
    half = _GU_BLOCK // 2
    n = gu.shape[1] // _GU_BLOCK
    glu = jnp.concatenate([gu[:, c * _GU_BLOCK:c * _GU_BLOCK + half] for c in range(n)], axis=1)
    lin = jnp.concatenate([gu[:, c * _GU_BLOCK + half:(c + 1) * _GU_BLOCK] for c in range(n)], axis=1)
    return glu, lin


def _experts_kernel(te_ref, nv_ref, x_ref, wgu_ref, bgu_ref, wd_ref, bd_ref, y_ref):
    @pl.when(pl.program_id(0) < nv_ref[0])
    def _():
        x = _unpack_row_halves(x_ref[...]).astype(jnp.bfloat16)
        gu = jnp.dot(x, wgu_ref[0], preferred_element_type=jnp.float32) + bgu_ref[0]
        glu, lin = _split_gate_up(gu)
        glu = jnp.minimum(glu, SWIGLU_LIMIT)
        lin = jnp.clip(lin, -SWIGLU_LIMIT, SWIGLU_LIMIT)
        a = glu * jax.nn.sigmoid(SWIGLU_ALPHA * glu) * (lin + 1.0)
        y = jnp.dot(a.astype(jnp.bfloat16), wd_ref[0],
                    preferred_element_type=jnp.float32) + bd_ref[0]
        y_ref[...] = _pack_row_halves(y)


def _experts(tile_expert, n_valid, x_sorted, wgu, bgu, wd, bd):
    n_rows = x_sorted.shape[0]
    tr = EXPERT_TILE
    n_tiles = n_rows // tr

    def row_map(j, te, nv):
        return (jnp.minimum(j, nv[0] - 1), 0)

    def exp_map(j, te, nv):
        return (te[j], 0, 0)

    grid_spec = pltpu.PrefetchScalarGridSpec(
        num_scalar_prefetch=2,
        grid=(n_tiles,),
        in_specs=[
            pl.BlockSpec((tr, PACKED_W), row_map),
            pl.BlockSpec((1, D_MODEL, 2 * D_FF), exp_map),
            pl.BlockSpec((1, 1, 2 * D_FF), exp_map),
            pl.BlockSpec((1, D_FF, D_MODEL), exp_map),
            pl.BlockSpec((1, 1, D_MODEL), exp_map),
        ],
        out_specs=pl.BlockSpec((tr, PACKED_W), row_map),
    )
    return pl.pallas_call(
        _experts_kernel,
        grid_spec=grid_spec,
        out_shape=jax.ShapeDtypeStruct((n_rows, PACKED_W), jnp.int32),
        compiler_params=_compiler_params(("arbitrary",), 56),
        name="experts",
    )(tile_expert, n_valid, x_sorted, wgu, bgu, wd, bd)


def _rope_tables(S, gain, scale):
    t = jnp.arange(S, dtype=jnp.int32)
    r = (t // GRID_W).astype(jnp.float32)
    c = (t % GRID_W).astype(jnp.float32)
    inv = jnp.float32(ROPE_THETA) ** (
        -jnp.arange(0, ROPE_AXIS_DIM, 2, dtype=jnp.float32) / ROPE_AXIS_DIM)
    ang = jnp.concatenate([r[None, :] * inv[:, None], c[None, :] * inv[:, None]], axis=0)
    cos = jnp.repeat(jnp.cos(ang), 2, axis=0)
    sin = jnp.repeat(jnp.sin(ang), 2, axis=0)
    sign = jnp.where(jnp.arange(HEAD_DIM) % 2 == 0, -1.0, 1.0).astype(jnp.float32)
    g = gain.astype(jnp.float32) * scale
    g_swapped = g.reshape(HEAD_DIM // 2, 2)[:, ::-1].reshape(HEAD_DIM)
    return g[:, None] * cos, (g_swapped * sign)[:, None] * sin


def _split_hi_lo(w):
    hi = w.astype(jnp.bfloat16)
    lo = (w - hi.astype(jnp.float32)).astype(jnp.bfloat16)
    return jnp.concatenate([hi, lo], axis=0)


def _prepare_weights(norm_mix_g, w_in, q_norm_g, k_norm_g, sgu_norm_g, w_spatial, b_spatial,
                     w_proj_attn, w_proj_sgu, w_out, norm_ffn_g, w_router, b_router,
                     w_gate_up, b_gate_up, w_down, b_down, norm_final_g):
    bf = jnp.bfloat16
    l = 0
    return dict(
        gmix=norm_mix_g[l][None, :],
        w_inT=w_in[l].T.astype(bf),
        q_gain=q_norm_g[l], k_gain=k_norm_g[l],
        gs=jnp.broadcast_to(sgu_norm_g[l][:, None], (SGU_W, _LANES)),
        wsT=jnp.swapaxes(w_spatial[l], 1, 2).astype(bf),
        bs=b_spatial[l][:, None, :],
        wpaT=w_proj_attn[l].T.astype(bf),
        wpbT=w_proj_sgu[l].T.astype(bf),
        wout=w_out[l].astype(bf),
        gffn=norm_ffn_g[l][None, :],
        wrT=_split_hi_lo(w_router[l].T),
        br=b_router[l][:, None],
        wgu=_gate_up_prep(w_gate_up[l]),
        bgu=b_gate_up[l].reshape(N_EXPERTS, -1, _GU_BLOCK // 2, 2).transpose(0, 1, 3, 2)
        .reshape(N_EXPERTS, 1, 2 * D_FF),
        wd=w_down[l].astype(bf),
        bd=b_down[l][:, None, :],
        gfin=norm_final_g[None, :],
    )


def _trunk(x, w, rope):
    B, S, _ = x.shape
    T = B * S
    qT, k, vT, uT, vsT, gaT, gbT = _in_proj(x, w["gmix"], w["w_inT"], *rope, w["gs"])
    attnT = _attention(qT, k, vT)
    x1, h2, ids, wts, rank, counts = _post(
        attnT, uT, vsT, gaT, gbT, x, w["wsT"], w["bs"], w["wpaT"], w["wpbT"], w["wout"],
        w["gffn"], w["wrT"], w["br"])

    tr = EXPERT_TILE
    counts = counts[:, 0]
    padded = (counts + tr - 1) // tr * tr
    ends = jnp.cumsum(padded)
    starts = ends - padded
    n_tiles = (TOP_K * T) // tr + N_EXPERTS
    n_rows = n_tiles * tr
    dest = rank
    for e in range(N_EXPERTS):
        dest = dest + jnp.where(ids == e, starts[e], 0)
    tile_start = jnp.arange(n_tiles, dtype=jnp.int32) * tr
    tile_expert = jnp.minimum(
        jnp.sum((tile_start[:, None] >= ends[None, :]).astype(jnp.int32), axis=1), N_EXPERTS - 1)
    n_valid = (ends[-1] // tr).astype(jnp.int32)[None]

    x_sorted = _sc_scatter_rows(h2.reshape(T, PACKED_W), dest, n_rows)
    y_sorted = _experts(tile_expert, n_valid, x_sorted, w["wgu"], w["bgu"], w["wd"], w["bd"])
    y_tok = _sc_gather_rows(y_sorted, dest.reshape(TOP_K * T)).reshape(TOP_K, T, PACKED_W)
    out = _combine_dense(wts.T, x1.reshape(T, D_MODEL), w["gfin"], y_tok)
    return out.reshape(B, S, D_MODEL)


def kernel(x_prompt, x_sample, norm_mix_g, w_in, q_norm_g, k_norm_g, sgu_norm_g, w_spatial,
           b_spatial, w_proj_attn, w_proj_sgu, w_out, norm_ffn_g, w_router, b_router,
           w_gate_up, b_gate_up, w_down, b_down, norm_final_g):
    w = _prepare_weights(norm_mix_g, w_in, q_norm_g, k_norm_g, sgu_norm_g, w_spatial,
                         b_spatial, w_proj_attn, w_proj_sgu, w_out, norm_ffn_g, w_router,
                         b_router, w_gate_up, b_gate_up, w_down, b_down, norm_final_g)
    s_max = max(x_prompt.shape[1], x_sample.shape[1])
    rope = (*_rope_tables(s_max, w["q_gain"], math.log2(math.e) / math.sqrt(HEAD_DIM)),
            *_rope_tables(s_max, w["k_gain"], 1.0))
    return (_trunk(x_prompt, w, rope), _trunk(x_sample, w, rope))
```

```python
import math

import jax
import jax.numpy as jnp
from jax import lax
from jax.experimental import pallas as pl
from jax.experimental.pallas import tpu as pltpu
from jax.experimental.pallas import tpu_sc as plsc

D_MODEL = 1024
GRID_W = 64
N_HEADS = 8
N_KV_HEADS = 2
HEAD_DIM = 64
GQA_GROUP = N_HEADS // N_KV_HEADS
Q_W = N_HEADS * HEAD_DIM
KV_W = N_KV_HEADS * HEAD_DIM
ROPE_AXIS_DIM = HEAD_DIM // 2
ROPE_THETA = 10000.0
SGU_GROUPS = 8
SGU_W = D_MODEL // 2
SGU_GROUP_DIM = SGU_W // SGU_GROUPS
CHUNK = 128
IN_W = Q_W + 2 * KV_W + 2 * SGU_W + 2 * D_MODEL
N_EXPERTS = 32
TOP_K = 4
D_FF = D_MODEL
SWIGLU_LIMIT = 7.0
SWIGLU_ALPHA = 1.702
EPS = 1e-6

_Q0, _K0, _V0 = 0, Q_W, Q_W + KV_W
_U0 = Q_W + 2 * KV_W
_VS0 = _U0 + SGU_W
_GA0 = _VS0 + SGU_W
_GB0 = _GA0 + D_MODEL

TOKEN_TILE = 512
Q_TILE = 512
EXPERT_TILE = 512
COMBINE_TILE = 512
PACKED_W = D_MODEL // 2
SC_WINDOW = 64

_LANES = 128
_BF16_SUBLANES = 16
_SUM_ROWS = _BF16_SUBLANES
_GU_BLOCK = 256
_NEG_BIG = -1e30
_HIGH_HALF_MASK = 0xFFFF0000
_MAX_LAGGED_EXPONENT = 80.0
_KEY_TILES_PER_TRIP = 4
_MIB = 1024 * 1024
_VMEM_LIMIT_MIB = 40

_NT_DIMS = (((1,), (1,)), ((), ()))
_TN_DIMS = (((0,), (0,)), ((), ()))


def _compiler_params(*semantics):
    return pltpu.CompilerParams(
        dimension_semantics=semantics, vmem_limit_bytes=_VMEM_LIMIT_MIB * _MIB)


def _swap_adjacent_rows(x):
    n = x.shape[0]
    row = lax.broadcasted_iota(jnp.int32, x.shape, 0)
    nxt = pltpu.roll(x, n - 1, 0)
    prv = pltpu.roll(x, 1, 0)
    return jnp.where((row & 1) == 0, nxt, prv)


def _head_norm_rope(z, tab_a, tab_b, n_heads):
    tm = z.shape[1]
    z3 = z.reshape(n_heads, HEAD_DIM, tm)
    ms = jnp.mean(z3 * z3, axis=1, keepdims=True)
    r = lax.rsqrt(ms + EPS)
    zs = _swap_adjacent_rows(z).reshape(n_heads, HEAD_DIM, tm)
    out = (z3 * tab_a[None] + zs * tab_b[None]) * r
    return out.reshape(n_heads * HEAD_DIM, tm)


def _gelu(x):
    return 0.5 * x * (1.0 + lax.erf(x * (1.0 / math.sqrt(2.0))))


def _pack_row_halves(x):
    half = x.shape[1] // 2

    def bf16_bits(v):
        return lax.bitcast_convert_type(v.astype(jnp.bfloat16).astype(jnp.float32), jnp.uint32)

    word = (bf16_bits(x[:, :half]) >> 16) | (bf16_bits(x[:, half:]) & jnp.uint32(_HIGH_HALF_MASK))
    return lax.bitcast_convert_type(word, jnp.int32)


def _unpack_row_halves(p):
    word = lax.bitcast_convert_type(p, jnp.uint32)
    lo = lax.bitcast_convert_type(word << 16, jnp.float32)
    hi = lax.bitcast_convert_type(word & jnp.uint32(_HIGH_HALF_MASK), jnp.float32)
    return jnp.concatenate([lo, hi], axis=1)


def _tile_lanes(x, reps):
    return jnp.concatenate([x] * reps, axis=1) if reps > 1 else x


def _in_proj_kernel(x_ref, gmix_ref, w_ref, qa_ref, qb_ref, ka_ref, kb_ref, gs_ref,
                    qT_ref, k_ref, vT_ref, uT_ref, vsT_ref, gaT_ref, gbT_ref):
    tm = x_ref.shape[1]
    x = x_ref[0]
    ms = jnp.mean(x * x, axis=-1, keepdims=True)
    h = (x * lax.rsqrt(ms + EPS) * gmix_ref[...]).astype(jnp.bfloat16)

    def proj(r0, rows):
        return lax.dot_general(w_ref[r0:r0 + rows, :], h, _NT_DIMS,
                               preferred_element_type=jnp.float32)

    zq = proj(_Q0, Q_W)
    qT_ref[0] = _head_norm_rope(zq, qa_ref[...], qb_ref[...], N_HEADS).astype(jnp.bfloat16)

    zkv = proj(_K0, 2 * KV_W)
    kT = _head_norm_rope(zkv[:KV_W], ka_ref[...], kb_ref[...], N_KV_HEADS)
    k_ref[0] = kT.T.astype(jnp.bfloat16)
    vT_ref[0, 0] = zkv[KV_W:].astype(jnp.bfloat16)

    uT_ref[0] = _gelu(proj(_U0, SGU_W)).astype(jnp.bfloat16)

    vs = _gelu(proj(_VS0, SGU_W))
    vms = jnp.mean(vs * vs, axis=0, keepdims=True)
    gs = _tile_lanes(gs_ref[...], tm // _LANES)
    vsT_ref[0] = (vs * lax.rsqrt(vms + EPS) * gs).astype(jnp.bfloat16)

    gaT_ref[0] = jax.nn.sigmoid(proj(_GA0, D_MODEL)).astype(jnp.bfloat16)
    gbT_ref[0] = jax.nn.sigmoid(proj(_GB0, D_MODEL)).astype(jnp.bfloat16)


def _in_proj(x, gmix, w_inT, qa, qb, ka, kb, gs):
    B, S, _ = x.shape
    tm = TOKEN_TILE
    nt = S // tm
    bf = jnp.bfloat16
    const2 = lambda b, i: (0, 0)
    tab = pl.BlockSpec((HEAD_DIM, tm), lambda b, i: (0, i))
    fm = lambda rows: pl.BlockSpec((1, rows, tm), lambda b, i: (b, 0, i))
    return pl.pallas_call(
        _in_proj_kernel,
        grid=(B, nt),
        in_specs=[
            pl.BlockSpec((1, tm, D_MODEL), lambda b, i: (b, i, 0)),
            pl.BlockSpec((1, D_MODEL), const2),
            pl.BlockSpec((IN_W, D_MODEL), const2),
            tab, tab, tab, tab,
            pl.BlockSpec((SGU_W, _LANES), const2),
        ],
        out_specs=[
            fm(Q_W),
            pl.BlockSpec((1, tm, KV_W), lambda b, i: (b, i, 0)),
            pl.BlockSpec((1, 1, KV_W, tm), lambda b, i: (b, i, 0, 0)),
            fm(SGU_W), fm(SGU_W), fm(D_MODEL), fm(D_MODEL),
        ],
        out_shape=[
            jax.ShapeDtypeStruct((B, Q_W, S), bf),
            jax.ShapeDtypeStruct((B, S, KV_W), bf),
            jax.ShapeDtypeStruct((B, nt, KV_W, tm), bf),
            jax.ShapeDtypeStruct((B, SGU_W, S), bf),
            jax.ShapeDtypeStruct((B, SGU_W, S), bf),
            jax.ShapeDtypeStruct((B, D_MODEL, S), bf),
            jax.ShapeDtypeStruct((B, D_MODEL, S), bf),
        ],
        compiler_params=_compiler_params("arbitrary", "arbitrary"),
        name="in_proj",
    )(x, gmix, w_inT, qa, qb, ka, kb, gs)


def _attention_kernel(qT_ref, k_ref, vT_ref, o_ref, m_ref, gap_ref, acc_ref):
    tq = qT_ref.shape[2]
    n_kt = vT_ref.shape[1]
    tk = vT_ref.shape[3]
    nq = GQA_GROUP * tq
    ones_rows = (lax.broadcasted_iota(jnp.int32, (_SUM_ROWS, tk), 0) == 0).astype(jnp.bfloat16)

    zeros = jnp.zeros((HEAD_DIM, nq), jnp.bfloat16)
    qp = []
    for kv in range(N_KV_HEADS):
        heads = [qT_ref[0, (kv * GQA_GROUP + g) * HEAD_DIM:(kv * GQA_GROUP + g + 1) * HEAD_DIM, :]
                 for g in range(GQA_GROUP)]
        qk = jnp.concatenate(heads, axis=1)
        qp.append(jnp.concatenate([qk, zeros] if kv == 0 else [zeros, qk], axis=0))

    def key_tile(ki):
        return k_ref[0, pl.ds(pl.multiple_of(ki * tk, tk), tk), :]

    def value_rows(ki, kv):
        vt = vT_ref[0, ki]
        return jnp.concatenate([vt[kv * HEAD_DIM:(kv + 1) * HEAD_DIM, :], ones_rows], axis=0)

    acc_ref[...] = jnp.zeros(acc_ref.shape, jnp.float32)
    gap_ref[...] = jnp.zeros(gap_ref.shape, jnp.float32)
    def as_reference(m):
        return m.astype(jnp.bfloat16).astype(jnp.float32)

    first_keys = k_ref[0, 0:_BF16_SUBLANES, :]
    for kv in range(N_KV_HEADS):
        s0 = jnp.dot(first_keys, qp[kv], preferred_element_type=jnp.float32)
        m_ref[kv] = as_reference(jnp.max(s0, axis=0, keepdims=True))

    ones_cols = (lax.broadcasted_iota(jnp.int32, (tk, _LANES), 1) == 0).astype(jnp.bfloat16)
    ref_row = lax.broadcasted_iota(jnp.int32, (_BF16_SUBLANES, nq), 0) == 0
    pad_rows = jnp.zeros((_LANES - _BF16_SUBLANES, nq), jnp.bfloat16)

    def fast_tile(ki):
        kt = jnp.concatenate([key_tile(ki), ones_cols], axis=1)
        for kv in range(N_KV_HEADS):
            m_old = m_ref[kv]
            neg_ref = jnp.where(ref_row, -m_old, 0.0).astype(jnp.bfloat16)
            q_aug = jnp.concatenate([qp[kv], neg_ref, pad_rows], axis=0)
            s = jnp.dot(kt, q_aug, preferred_element_type=jnp.float32)
            p = jnp.exp2(s).astype(jnp.bfloat16)
            over = jnp.max(s, axis=0, keepdims=True)
            pv = jnp.dot(value_rows(ki, kv), p, preferred_element_type=jnp.float32)
            m_new = as_reference(m_old + jnp.maximum(over, 0.0))
            acc_ref[kv] = (acc_ref[kv] + pv) * jnp.exp2(m_old - m_new)
            m_ref[kv] = m_new
            gap_ref[kv] = jnp.maximum(gap_ref[kv], over)

    per_trip = math.gcd(n_kt, _KEY_TILES_PER_TRIP)

    def fast_group(j, carry):
        for t in range(per_trip):
            fast_tile(per_trip * j + t)
        return carry

    lax.fori_loop(0, n_kt // per_trip, fast_group, 0)

    @pl.when(jnp.max(gap_ref[...]) > _MAX_LAGGED_EXPONENT)
    def _():
        m_ref[...] = jnp.full(m_ref.shape, _NEG_BIG, jnp.float32)
        acc_ref[...] = jnp.zeros(acc_ref.shape, jnp.float32)

        def robust_tile(ki, carry):
            kt = key_tile(ki)
            for kv in range(N_KV_HEADS):
                s = jnp.dot(kt, qp[kv], preferred_element_type=jnp.float32)
                m_old = m_ref[kv]
                m_new = jnp.maximum(m_old, jnp.max(s, axis=0, keepdims=True))
                p = jnp.exp2(s - m_new).astype(jnp.bfloat16)
                pv = jnp.dot(value_rows(ki, kv), p, preferred_element_type=jnp.float32)
                acc_ref[kv] = jnp.exp2(m_old - m_new) * acc_ref[kv] + pv
                m_ref[kv] = m_new
            return carry

        lax.fori_loop(0, n_kt, robust_tile, 0)

    for kv in range(N_KV_HEADS):
        acc = acc_ref[kv]
        o = acc[:HEAD_DIM] / acc[HEAD_DIM:HEAD_DIM + 1]
        for g in range(GQA_GROUP):
            r0 = (kv * GQA_GROUP + g) * HEAD_DIM
            o_ref[0, r0:r0 + HEAD_DIM, :] = o[:, g * tq:(g + 1) * tq].astype(jnp.bfloat16)


def _attention(qT, k, vT):
    B, _, S = qT.shape
    tq = Q_TILE
    nq = GQA_GROUP * tq
    n_kt, tk = vT.shape[1], vT.shape[3]
    return pl.pallas_call(
        _attention_kernel,
        grid=(B, S // tq),
        in_specs=[
            pl.BlockSpec((1, Q_W, tq), lambda b, i: (b, 0, i)),
            pl.BlockSpec((1, S, KV_W), lambda b, i: (b, 0, 0)),
            pl.BlockSpec((1, n_kt, KV_W, tk), lambda b, i: (b, 0, 0, 0)),
        ],
        out_specs=pl.BlockSpec((1, Q_W, tq), lambda b, i: (b, 0, i)),
        out_shape=jax.ShapeDtypeStruct((B, Q_W, S), jnp.bfloat16),
        scratch_shapes=[
            pltpu.VMEM((N_KV_HEADS, 1, nq), jnp.float32),
            pltpu.VMEM((N_KV_HEADS, 1, nq), jnp.float32),
            pltpu.VMEM((N_KV_HEADS, HEAD_DIM + _SUM_ROWS, nq), jnp.float32),
        ],
        compiler_params=_compiler_params("arbitrary", "arbitrary"),
        name="attention",
    )(qT, k, vT)


def _post_kernel(attnT_ref, uT_ref, vsT_ref, gaT_ref, gbT_ref, x_ref,
                 wsT_ref, bs_ref, wpaT_ref, wpbT_ref, wout_ref, gffn_ref, wrT_ref, br_ref,
                 x1_ref, h2_ref, ids_ref, wts_ref, rank_ref, cnt_ref, carry_ref):
    tm = x_ref.shape[1]
    n_chunks = tm // CHUNK
    first = jnp.logical_and(pl.program_id(0) == 0, pl.program_id(1) == 0)

    @pl.when(first)
    def _():
        carry_ref[...] = jnp.zeros(carry_ref.shape, jnp.float32)

    gate_rows = []
    for g in range(SGU_GROUPS):
        r0 = g * SGU_GROUP_DIM
        vs_g = vsT_ref[0, r0:r0 + SGU_GROUP_DIM, :]
        lhs = jnp.concatenate(
            [vs_g[:, c * CHUNK:(c + 1) * CHUNK] for c in range(n_chunks)], axis=0)
        mixed = jnp.dot(lhs, wsT_ref[g], preferred_element_type=jnp.float32)
        mixed = mixed + bs_ref[g]
        mixedT = jnp.concatenate(
            [mixed[c * SGU_GROUP_DIM:(c + 1) * SGU_GROUP_DIM] for c in range(n_chunks)], axis=1)
        u_g = uT_ref[0, r0:r0 + SGU_GROUP_DIM, :].astype(jnp.float32)
        gate_rows.append((u_g * mixedT).astype(jnp.bfloat16))
    gateT = jnp.concatenate(gate_rows, axis=0)

    paT = jnp.dot(wpaT_ref[...], attnT_ref[0], preferred_element_type=jnp.float32)
    pbT = jnp.dot(wpbT_ref[...], gateT, preferred_element_type=jnp.float32)
    mT = (gaT_ref[0].astype(jnp.float32) * paT
          + gbT_ref[0].astype(jnp.float32) * pbT).astype(jnp.bfloat16)
    y = lax.dot_general(mT, wout_ref[...], _TN_DIMS, preferred_element_type=jnp.float32)
    x1 = x_ref[0] + y
    x1_ref[0] = x1

    ms = jnp.mean(x1 * x1, axis=-1, keepdims=True)
    h2 = x1 * lax.rsqrt(ms + EPS) * gffn_ref[...]
    h2_ref[0] = _pack_row_halves(h2)

    h_hi = h2.astype(jnp.bfloat16)
    h_lo = (h2 - h_hi.astype(jnp.float32)).astype(jnp.bfloat16)
    by_hi = lax.dot_general(wrT_ref[...], h_hi, _NT_DIMS, preferred_element_type=jnp.float32)
    by_lo = lax.dot_general(wrT_ref[:N_EXPERTS, :], h_lo, _NT_DIMS,
                            preferred_element_type=jnp.float32)
    logits = by_hi[:N_EXPERTS] + by_hi[N_EXPERTS:] + by_lo + br_ref[...]
    eidx = lax.broadcasted_iota(jnp.int32, logits.shape, 0).astype(jnp.float32)
    work = logits
    vals, ids, sels = [], [], []
    for _ in range(TOP_K):
        mx = jnp.max(work, axis=0, keepdims=True)
        idx = jnp.min(jnp.where(work == mx, eidx, float(N_EXPERTS)), axis=0, keepdims=True)
        sel = eidx == idx
        vals.append(mx)
        ids.append(idx)
        sels.append(sel)
        work = jnp.where(sel, -jnp.inf, work)
    exps = [jnp.exp(v - vals[0]) for v in vals]
    denom = exps[0] + exps[1] + exps[2] + exps[3]
    ids_ref[...] = jnp.concatenate(ids, axis=0).astype(jnp.int32)
    wts_ref[...] = jnp.concatenate([e / denom for e in exps], axis=0)

    onehot = [s.astype(jnp.float32) for s in sels]
    hits = onehot[0] + onehot[1] + onehot[2] + onehot[3]
    ti = lax.broadcasted_iota(jnp.int32, (tm, tm), 0)
    tj = lax.broadcasted_iota(jnp.int32, (tm, tm), 1)
    upper = (ti < tj).astype(jnp.bfloat16)
    prefix = jnp.dot(hits.astype(jnp.bfloat16), upper, preferred_element_type=jnp.float32)
    base = prefix + carry_ref[...]
    ranks = [jnp.sum(oh * base, axis=0, keepdims=True) for oh in onehot]
    rank_ref[...] = jnp.concatenate(ranks, axis=0).astype(jnp.int32)
    carry_ref[...] = carry_ref[...] + jnp.sum(hits, axis=1, keepdims=True)
    cnt_ref[...] = carry_ref[...].astype(jnp.int32)


def _post(attnT, uT, vsT, gaT, gbT, x, wsT, bs, wpaT, wpbT, wout, gffn, wrT, br):
    B, S, _ = x.shape
    tm = TOKEN_TILE
    nt = S // tm
    T = B * S
    fm = lambda rows: pl.BlockSpec((1, rows, tm), lambda b, i: (b, 0, i))
    rowm = pl.BlockSpec((1, tm, D_MODEL), lambda b, i: (b, i, 0))
    c2 = lambda b, i: (0, 0)
    c3 = lambda b, i: (0, 0, 0)
    tokT = pl.BlockSpec((TOP_K, tm), lambda b, i: (0, b * nt + i))
    return pl.pallas_call(
        _post_kernel,
        grid=(B, nt),
        in_specs=[
            fm(Q_W), fm(SGU_W), fm(SGU_W), fm(D_MODEL), fm(D_MODEL), rowm,
            pl.BlockSpec((SGU_GROUPS, CHUNK, CHUNK), c3),
            pl.BlockSpec((SGU_GROUPS, 1, CHUNK), c3),
            pl.BlockSpec((D_MODEL, Q_W), c2),
            pl.BlockSpec((D_MODEL, SGU_W), c2),
            pl.BlockSpec((D_MODEL, D_MODEL), c2),
            pl.BlockSpec((1, D_MODEL), c2),
            pl.BlockSpec((2 * N_EXPERTS, D_MODEL), c2),
            pl.BlockSpec((N_EXPERTS, 1), c2),
        ],
        out_specs=[rowm, pl.BlockSpec((1, tm, PACKED_W), lambda b, i: (b, i, 0)),
                   tokT, tokT, tokT, pl.BlockSpec((N_EXPERTS, 1), c2)],
        out_shape=[
            jax.ShapeDtypeStruct((B, S, D_MODEL), jnp.float32),
            jax.ShapeDtypeStruct((B, S, PACKED_W), jnp.int32),
            jax.ShapeDtypeStruct((TOP_K, T), jnp.int32),
            jax.ShapeDtypeStruct((TOP_K, T), jnp.float32),
            jax.ShapeDtypeStruct((TOP_K, T), jnp.int32),
            jax.ShapeDtypeStruct((N_EXPERTS, 1), jnp.int32),
        ],
        scratch_shapes=[pltpu.VMEM((N_EXPERTS, 1), jnp.float32)],
        compiler_params=_compiler_params("arbitrary", "arbitrary"),
        name="post",
    )(attnT, uT, vsT, gaT, gbT, x, wsT, bs, wpaT, wpbT, wout, gffn, wrT, br)


def _sc_kernel(name, body, out_type, n_items, row_shape, row_dtype):
    info = plsc.get_sparse_core_info()
    n_workers = info.num_cores * info.num_subcores
    per_worker = n_items // n_workers
    n_windows = per_worker // SC_WINDOW
    assert per_worker * n_workers == n_items and n_windows * SC_WINDOW == per_worker
    assert n_windows % 2 == 0

    def wrapped(*refs):
        wid = lax.axis_index("subcore") * info.num_cores + lax.axis_index("core")
        body(wid * per_worker, n_windows, *refs)

    return pl.kernel(
        wrapped,
        name=name,
        out_type=out_type,
        mesh=plsc.VectorSubcoreMesh(core_axis_name="core", subcore_axis_name="subcore"),
        scratch_types=[
            pltpu.VMEM((SC_WINDOW,), jnp.int32),
            pltpu.VMEM((SC_WINDOW,), jnp.int32),
            pltpu.VMEM((SC_WINDOW,) + row_shape, row_dtype),
            pltpu.VMEM((SC_WINDOW,) + row_shape, row_dtype),
            pltpu.SemaphoreType.DMA,
            pltpu.SemaphoreType.DMA,
        ],
    )


def _sc_gather_rows(table, idx):
    n, d, win = idx.shape[0], table.shape[1], SC_WINDOW

    def body(base, n_windows, table_hbm, idx_hbm, out_hbm, idx0, idx1, rows0, rows1, sem0, sem1):
        def window(c):
            return pl.ds(pl.multiple_of(base + c * win, win), win)

        def fetch(c, idx_v, rows_v, sem):
            pltpu.sync_copy(idx_hbm.at[window(c)], idx_v)
            pltpu.async_copy(table_hbm.at[idx_v], rows_v, sem)

        def drain(c, idx_v, rows_v, sem):
            pltpu.make_async_copy(table_hbm.at[idx_v], rows_v, sem).wait()
            pltpu.sync_copy(rows_v, out_hbm.at[window(c)])

        fetch(0, idx0, rows0, sem0)

        @pl.loop(0, n_windows, step=2)
        def _(c):
            fetch(c + 1, idx1, rows1, sem1)
            drain(c, idx0, rows0, sem0)

            @pl.when(c + 2 < n_windows)
            def _():
                fetch(c + 2, idx0, rows0, sem0)

            drain(c + 1, idx1, rows1, sem1)

    out_type = jax.ShapeDtypeStruct((n, d), table.dtype)
    return _sc_kernel("sc_gather_rows", body, out_type, n, (d,), table.dtype)(table, idx)


def _sc_scatter_rows(src, dest, n_rows):
    T, d = src.shape
    win = SC_WINDOW

    def body(base, n_windows, src_hbm, dest_hbm, out_hbm, idx0, idx1, rows0, rows1, sem0, sem1):
        def window(c, slot=0):
            return pl.ds(pl.multiple_of(slot * T + base + c * win, win), win)

        def fetch(c, rows_v, sem):
            pltpu.async_copy(src_hbm.at[window(c)], rows_v, sem)

        def push(c, rows_v, sem):
            pltpu.make_async_copy(src_hbm.at[window(c)], rows_v, sem).wait()
            for k in range(TOP_K):
                idx_v = idx0 if k % 2 == 0 else idx1
                pltpu.sync_copy(dest_hbm.at[window(c, k)], idx_v)
                pltpu.sync_copy(rows_v, out_hbm.at[idx_v])

        fetch(0, rows0, sem0)

        @pl.loop(0, n_windows, step=2)
        def _(c):
            fetch(c + 1, rows1, sem1)
            push(c, rows0, sem0)

            @pl.when(c + 2 < n_windows)
            def _():
                fetch(c + 2, rows0, sem0)

            push(c + 1, rows1, sem1)

    out_type = jax.ShapeDtypeStruct((n_rows, d), src.dtype)
    return _sc_kernel("sc_scatter_rows", body, out_type, T, (d,), src.dtype)(
        src, dest.reshape(TOP_K * T))


def _combine_dense_kernel(w_ref, x1_ref, gfin_ref, y_ref, o_ref):
    w = w_ref[...]
    moe = w[:, 0:1] * _unpack_row_halves(y_ref[0])
    for k in range(1, TOP_K):
        moe = moe + w[:, k:k + 1] * _unpack_row_halves(y_ref[k])
    x2 = x1_ref[...] + moe
    ms = jnp.mean(x2 * x2, axis=-1, keepdims=True)
    o_ref[...] = x2 * lax.rsqrt(ms + EPS) * gfin_ref[...]


def _combine_dense(wts, x1, gfin, y_tok):
    T = x1.shape[0]
    tt = COMBINE_TILE
    return pl.pallas_call(
        _combine_dense_kernel,
        grid=(T // tt,),
        in_specs=[
            pl.BlockSpec((tt, TOP_K), lambda i: (i, 0)),
            pl.BlockSpec((tt, D_MODEL), lambda i: (i, 0)),
            pl.BlockSpec((1, D_MODEL), lambda i: (0, 0)),
            pl.BlockSpec((TOP_K, tt, PACKED_W), lambda i: (0, i, 0)),
        ],
        out_specs=pl.BlockSpec((tt, D_MODEL), lambda i: (i, 0)),
        out_shape=jax.ShapeDtypeStruct((T, D_MODEL), jnp.float32),
        compiler_params=_compiler_params("arbitrary"),
        name="combine_dense",
    )(wts, x1, gfin, y_tok)


def _gate_up_prep_kernel(w_ref, o_ref):
    half = _GU_BLOCK // 2
    i = lax.broadcasted_iota(jnp.int32, (_GU_BLOCK, _GU_BLOCK), 0)
    j = lax.broadcasted_iota(jnp.int32, (_GU_BLOCK, _GU_BLOCK), 1)
    perm = (i == jnp.where(j < half, 2 * j, 2 * (j - half) + 1)).astype(jnp.bfloat16)
    for c in range(2 * D_FF // _GU_BLOCK):
        cols = slice(c * _GU_BLOCK, (c + 1) * _GU_BLOCK)
        blk = w_ref[0, :, cols].astype(jnp.bfloat16)
        o_ref[0, :, cols] = jnp.dot(blk, perm, preferred_element_type=jnp.float32).astype(jnp.bfloat16)


def _gate_up_prep(w_gate_up):
    spec = pl.BlockSpec((1, D_MODEL, 2 * D_FF), lambda e: (e, 0, 0))
    return pl.pallas_call(
        _gate_up_prep_kernel,
        grid=(N_EXPERTS,),
        in_specs=[spec],
        out_specs=spec,
        out_shape=jax.ShapeDtypeStruct(w_gate_up.shape, jnp.bfloat16),
        compiler_params=_compiler_params("arbitrary"),
        name="gate_up_prep",
    )(w_gate_up)


def _split_gate_up(gu):
    half = _GU_BLOCK // 2
    n = gu.shape[1] // _GU_BLOCK
    glu = jnp.concatenate([gu[:, c * _GU_BLOCK:c * _GU_BLOCK + half] for c in range(n)], axis=1)
    lin = jnp.concatenate([gu[:, c * _GU_BLOCK + half:(c + 1) * _GU_BLOCK] for c in range(n)], axis=1)
    return glu, lin


def _experts_kernel(te_ref, nv_ref, x_ref, wgu_ref, bgu_ref, wd_ref, bd_ref, y_ref):
    @pl.when(pl.program_id(0) < nv_ref[0])
    def _():
        x = _unpack_row_halves(x_ref[...]).astype(jnp.bfloat16)
        gu = jnp.dot(x, wgu_ref[0], preferred_element_type=jnp.float32) + bgu_ref[0]
        glu, lin = _split_gate_up(gu)
        glu = jnp.minimum(glu, SWIGLU_LIMIT)
        lin = jnp.clip(lin, -SWIGLU_LIMIT, SWIGLU_LIMIT)
        a = glu * jax.nn.sigmoid(SWIGLU_ALPHA * glu) * (lin + 1.0)
        y = jnp.dot(a.astype(jnp.bfloat16), wd_ref[0],
                    preferred_element_type=jnp.float32) + bd_ref[0]
        y_ref[...] = _pack_row_halves(y)


def _experts(tile_expert, n_valid, x_sorted, wgu, bgu, wd, bd):
    n_rows = x_sorted.shape[0]
    tr = EXPERT_TILE
    n_tiles = n_rows // tr

    def row_map(j, te, nv):
        return (jnp.minimum(j, nv[0] - 1), 0)

    def exp_map(j, te, nv):
        return (te[j], 0, 0)

    grid_spec = pltpu.PrefetchScalarGridSpec(
        num_scalar_prefetch=2,
        grid=(n_tiles,),
        in_specs=[
            pl.BlockSpec((tr, PACKED_W), row_map),
            pl.BlockSpec((1, D_MODEL, 2 * D_FF), exp_map),
            pl.BlockSpec((1, 1, 2 * D_FF), exp_map),
            pl.BlockSpec((1, D_FF, D_MODEL), exp_map),
            pl.BlockSpec((1, 1, D_MODEL), exp_map),
        ],
        out_specs=pl.BlockSpec((tr, PACKED_W), row_map),
    )
    return pl.pallas_call(
        _experts_kernel,
        grid_spec=grid_spec,
        out_shape=jax.ShapeDtypeStruct((n_rows, PACKED_W), jnp.int32),
        compiler_params=_compiler_params("arbitrary"),
        name="experts",
    )(tile_expert, n_valid, x_sorted, wgu, bgu, wd, bd)


def _rope_tables(S, gain, scale):
    t = jnp.arange(S, dtype=jnp.int32)
    r = (t // GRID_W).astype(jnp.float32)
    c = (t % GRID_W).astype(jnp.float32)
    inv = jnp.float32(ROPE_THETA) ** (
        -jnp.arange(0, ROPE_AXIS_DIM, 2, dtype=jnp.float32) / ROPE_AXIS_DIM)
    ang = jnp.concatenate([r[None, :] * inv[:, None], c[None, :] * inv[:, None]], axis=0)
    cos = jnp.repeat(jnp.cos(ang), 2, axis=0)
    sin = jnp.repeat(jnp.sin(ang), 2, axis=0)
    sign = jnp.where(jnp.arange(HEAD_DIM) % 2 == 0, -1.0, 1.0).astype(jnp.float32)
    g = gain.astype(jnp.float32) * scale
    g_swapped = g.reshape(HEAD_DIM // 2, 2)[:, ::-1].reshape(HEAD_DIM)
    return g[:, None] * cos, (g_swapped * sign)[:, None] * sin


def _split_hi_lo(w):
    hi = w.astype(jnp.bfloat16)
    lo = (w - hi.astype(jnp.float32)).astype(jnp.bfloat16)
    return jnp.concatenate([hi, lo], axis=0)


def _prepare_weights(norm_mix_g, w_in, q_norm_g, k_norm_g, sgu_norm_g, w_spatial, b_spatial,
                     w_proj_attn, w_proj_sgu, w_out, norm_ffn_g, w_router, b_router,
                     w_gate_up, b_gate_up, w_down, b_down, norm_final_g):
    bf = jnp.bfloat16
    l = 0
    return dict(
        gmix=norm_mix_g[l][None, :],
        w_inT=w_in[l].T.astype(bf),
        q_gain=q_norm_g[l], k_gain=k_norm_g[l],
        gs=jnp.broadcast_to(sgu_norm_g[l][:, None], (SGU_W, _LANES)),
        wsT=jnp.swapaxes(w_spatial[l], 1, 2).astype(bf),
        bs=b_spatial[l][:, None, :],
        wpaT=w_proj_attn[l].T.astype(bf),
        wpbT=w_proj_sgu[l].T.astype(bf),
        wout=w_out[l].astype(bf),
        gffn=norm_ffn_g[l][None, :],
        wrT=_split_hi_lo(w_router[l].T),
        br=b_router[l][:, None],
        wgu=_gate_up_prep(w_gate_up[l]),
        bgu=b_gate_up[l].reshape(N_EXPERTS, -1, _GU_BLOCK // 2, 2).transpose(0, 1, 3, 2)
        .reshape(N_EXPERTS, 1, 2 * D_FF),
        wd=w_down[l].astype(bf),
        bd=b_down[l][:, None, :],
        gfin=norm_final_g[None, :],
    )


def _trunk(x, w, rope):
    B, S, _ = x.shape
    T = B * S
    qT, k, vT, uT, vsT, gaT, gbT = _in_proj(x, w["gmix"], w["w_inT"], *rope, w["gs"])
    attnT = _attention(qT, k, vT)
    x1, h2, ids, wts, rank, counts = _post(
        attnT, uT, vsT, gaT, gbT, x, w["wsT"], w["bs"], w["wpaT"], w["wpbT"], w["wout"],
        w["gffn"], w["wrT"], w["br"])

    tr = EXPERT_TILE
    counts = counts[:, 0]
    padded = (counts + tr - 1) // tr * tr
    ends = jnp.cumsum(padded)
    starts = ends - padded
    n_tiles = (TOP_K * T) // tr + N_EXPERTS
    n_rows = n_tiles * tr
    dest = rank
    for e in range(N_EXPERTS):
        dest = dest + jnp.where(ids == e, starts[e], 0)
    tile_start = jnp.arange(n_tiles, dtype=jnp.int32) * tr
    tile_expert = jnp.minimum(
        jnp.sum((tile_start[:, None] >= ends[None, :]).astype(jnp.int32), axis=1), N_EXPERTS - 1)
    n_valid = (ends[-1] // tr).astype(jnp.int32)[None]

    x_sorted = _sc_scatter_rows(h2.reshape(T, PACKED_W), dest, n_rows)
    y_sorted = _experts(tile_expert, n_valid, x_sorted, w["wgu"], w["bgu"], w["wd"], w["bd"])
    y_tok = _sc_gather_rows(y_sorted, dest.reshape(TOP_K * T)).reshape(TOP_K, T, PACKED_W)
    out = _combine_dense(wts.T, x1.reshape(T, D_MODEL), w["gfin"], y_tok)
    return out.reshape(B, S, D_MODEL)


def kernel(x_prompt, x_sample, norm_mix_g, w_in, q_norm_g, k_norm_g, sgu_norm_g, w_spatial,
           b_spatial, w_proj_attn, w_proj_sgu, w_out, norm_ffn_g, w_router, b_router,
           w_gate_up, b_gate_up, w_down, b_down, norm_final_g):
    w = _prepare_weights(norm_mix_g, w_in, q_norm_g, k_norm_g, sgu_norm_g, w_spatial,
                         b_spatial, w_proj_attn, w_proj_sgu, w_out, norm_ffn_g, w_router,
                         b_router, w_gate_up, b_gate_up, w_down, b_down, norm_final_g)
    s_max = max(x_prompt.shape[1], x_sample.shape[1])
    rope = (*_rope_tables(s_max, w["q_gain"], math.log2(math.e) / math.sqrt(HEAD_DIM)),
            *_rope_tables(s_max, w["k_gain"], 1.0))
    return (_trunk(x_prompt, w, rope), _trunk(x_sample, w, rope))
```

```python
import math

import jax
import jax.numpy as jnp
from jax import lax
from jax.experimental import pallas as pl
from jax.experimental.pallas import tpu as pltpu
from jax.experimental.pallas import tpu_sc as plsc

D_MODEL = 1024
GRID_W = 64
N_HEADS = 8
N_KV_HEADS = 2
HEAD_DIM = 64
GQA_GROUP = N_HEADS // N_KV_HEADS
Q_W = N_HEADS * HEAD_DIM
KV_W = N_KV_HEADS * HEAD_DIM
ROPE_AXIS_DIM = HEAD_DIM // 2
ROPE_THETA = 10000.0
SGU_GROUPS = 8
SGU_W = D_MODEL // 2
SGU_GROUP_DIM = SGU_W // SGU_GROUPS
CHUNK = 128
IN_W = Q_W + 2 * KV_W + 2 * SGU_W + 2 * D_MODEL
N_EXPERTS = 32
TOP_K = 4
D_FF = D_MODEL
SWIGLU_LIMIT = 7.0
SWIGLU_ALPHA = 1.702
EPS = 1e-6

_Q0, _K0, _V0 = 0, Q_W, Q_W + KV_W
_U0 = Q_W + 2 * KV_W
_VS0 = _U0 + SGU_W
_GA0 = _VS0 + SGU_W
_GB0 = _GA0 + D_MODEL

TOKEN_TILE = 512
Q_TILE = 512
EXPERT_TILE = 512
COMBINE_TILE = 512
PACKED_W = D_MODEL // 2
SC_WINDOW = 64

_LANES = 128
_BF16_SUBLANES = 16
_SUM_ROWS = _BF16_SUBLANES
_GU_BLOCK = 256
_NEG_BIG = -1e30
_HIGH_HALF_MASK = 0xFFFF0000
_MAX_LAGGED_EXPONENT = 80.0
_KEY_TILES_PER_TRIP = 8
_MIB = 1024 * 1024
_VMEM_LIMIT_MIB = 40

_NT_DIMS = (((1,), (1,)), ((), ()))
_TN_DIMS = (((0,), (0,)), ((), ()))


def _compiler_params(*semantics):
    return pltpu.CompilerParams(
        dimension_semantics=semantics, vmem_limit_bytes=_VMEM_LIMIT_MIB * _MIB)


def _swap_adjacent_rows(x):
    n = x.shape[0]
    row = lax.broadcasted_iota(jnp.int32, x.shape, 0)
    nxt = pltpu.roll(x, n - 1, 0)
    prv = pltpu.roll(x, 1, 0)
    return jnp.where((row & 1) == 0, nxt, prv)


def _head_norm_rope(z, tab_a, tab_b, n_heads):
    tm = z.shape[1]
    z3 = z.reshape(n_heads, HEAD_DIM, tm)
    ms = jnp.mean(z3 * z3, axis=1, keepdims=True)
    r = lax.rsqrt(ms + EPS)
    zs = _swap_adjacent_rows(z).reshape(n_heads, HEAD_DIM, tm)
    out = (z3 * tab_a[None] + zs * tab_b[None]) * r
    return out.reshape(n_heads * HEAD_DIM, tm)


def _gelu(x):
    return 0.5 * x * (1.0 + lax.erf(x * (1.0 / math.sqrt(2.0))))


def _pack_row_halves(x):
    half = x.shape[1] // 2

    def bf16_bits(v):
        return lax.bitcast_convert_type(v.astype(jnp.bfloat16).astype(jnp.float32), jnp.uint32)

    word = (bf16_bits(x[:, :half]) >> 16) | (bf16_bits(x[:, half:]) & jnp.uint32(_HIGH_HALF_MASK))
    return lax.bitcast_convert_type(word, jnp.int32)


def _unpack_row_halves(p):
    word = lax.bitcast_convert_type(p, jnp.uint32)
    lo = lax.bitcast_convert_type(word << 16, jnp.float32)
    hi = lax.bitcast_convert_type(word & jnp.uint32(_HIGH_HALF_MASK), jnp.float32)
    return jnp.concatenate([lo, hi], axis=1)


def _tile_lanes(x, reps):
    return jnp.concatenate([x] * reps, axis=1) if reps > 1 else x


def _in_proj_kernel(x_ref, gmix_ref, w_ref, qa_ref, qb_ref, ka_ref, kb_ref, gs_ref,
                    qT_ref, k_ref, vT_ref, uT_ref, vsT_ref, gaT_ref, gbT_ref):
    tm = x_ref.shape[1]
    x = x_ref[0]
    ms = jnp.mean(x * x, axis=-1, keepdims=True)
    h = (x * lax.rsqrt(ms + EPS) * gmix_ref[...]).astype(jnp.bfloat16)

    def proj(r0, rows):
        return lax.dot_general(w_ref[r0:r0 + rows, :], h, _NT_DIMS,
                               preferred_element_type=jnp.float32)

    zq = proj(_Q0, Q_W)
    qT_ref[0] = _head_norm_rope(zq, qa_ref[...], qb_ref[...], N_HEADS).astype(jnp.bfloat16)

    zkv = proj(_K0, 2 * KV_W)
    kT = _head_norm_rope(zkv[:KV_W], ka_ref[...], kb_ref[...], N_KV_HEADS)
    k_ref[0] = kT.T.astype(jnp.bfloat16)
    vT_ref[0, 0] = zkv[KV_W:].astype(jnp.bfloat16)

    uT_ref[0] = _gelu(proj(_U0, SGU_W)).astype(jnp.bfloat16)

    vs = _gelu(proj(_VS0, SGU_W))
    vms = jnp.mean(vs * vs, axis=0, keepdims=True)
    gs = _tile_lanes(gs_ref[...], tm // _LANES)
    vsT_ref[0] = (vs * lax.rsqrt(vms + EPS) * gs).astype(jnp.bfloat16)

    gaT_ref[0] = jax.nn.sigmoid(proj(_GA0, D_MODEL)).astype(jnp.bfloat16)
    gbT_ref[0] = jax.nn.sigmoid(proj(_GB0, D_MODEL)).astype(jnp.bfloat16)


def _in_proj(x, gmix, w_inT, qa, qb, ka, kb, gs):
    B, S, _ = x.shape
    tm = TOKEN_TILE
    nt = S // tm
    bf = jnp.bfloat16
    const2 = lambda b, i: (0, 0)
    tab = pl.BlockSpec((HEAD_DIM, tm), lambda b, i: (0, i))
    fm = lambda rows: pl.BlockSpec((1, rows, tm), lambda b, i: (b, 0, i))
    return pl.pallas_call(
        _in_proj_kernel,
        grid=(B, nt),
        in_specs=[
            pl.BlockSpec((1, tm, D_MODEL), lambda b, i: (b, i, 0)),
            pl.BlockSpec((1, D_MODEL), const2),
            pl.BlockSpec((IN_W, D_MODEL), const2),
            tab, tab, tab, tab,
            pl.BlockSpec((SGU_W, _LANES), const2),
        ],
        out_specs=[
            fm(Q_W),
            pl.BlockSpec((1, tm, KV_W), lambda b, i: (b, i, 0)),
            pl.BlockSpec((1, 1, KV_W, tm), lambda b, i: (b, i, 0, 0)),
            fm(SGU_W), fm(SGU_W), fm(D_MODEL), fm(D_MODEL),
        ],
        out_shape=[
            jax.ShapeDtypeStruct((B, Q_W, S), bf),
            jax.ShapeDtypeStruct((B, S, KV_W), bf),
            jax.ShapeDtypeStruct((B, nt, KV_W, tm), bf),
            jax.ShapeDtypeStruct((B, SGU_W, S), bf),
            jax.ShapeDtypeStruct((B, SGU_W, S), bf),
            jax.ShapeDtypeStruct((B, D_MODEL, S), bf),
            jax.ShapeDtypeStruct((B, D_MODEL, S), bf),
        ],
        compiler_params=_compiler_params("arbitrary", "arbitrary"),
        name="in_proj",
    )(x, gmix, w_inT, qa, qb, ka, kb, gs)


def _attention_kernel(qT_ref, k_ref, vT_ref, o_ref, m_ref, gap_ref, acc_ref):
    tq = qT_ref.shape[2]
    n_kt = vT_ref.shape[1]
    tk = vT_ref.shape[3]
    nq = GQA_GROUP * tq
    ones_rows = (lax.broadcasted_iota(jnp.int32, (_SUM_ROWS, tk), 0) == 0).astype(jnp.bfloat16)

    zeros = jnp.zeros((HEAD_DIM, nq), jnp.bfloat16)
    qp = []
    for kv in range(N_KV_HEADS):
        heads = [qT_ref[0, (kv * GQA_GROUP + g) * HEAD_DIM:(kv * GQA_GROUP + g + 1) * HEAD_DIM, :]
                 for g in range(GQA_GROUP)]
        qk = jnp.concatenate(heads, axis=1)
        qp.append(jnp.concatenate([qk, zeros] if kv == 0 else [zeros, qk], axis=0))

    def key_tile(ki):
        return k_ref[0, pl.ds(pl.multiple_of(ki * tk, tk), tk), :]

    def value_rows(ki, kv):
        vt = vT_ref[0, ki]
        return jnp.concatenate([vt[kv * HEAD_DIM:(kv + 1) * HEAD_DIM, :], ones_rows], axis=0)

    acc_ref[...] = jnp.zeros(acc_ref.shape, jnp.float32)
    gap_ref[...] = jnp.zeros(gap_ref.shape, jnp.float32)
    def as_reference(m):
        return m.astype(jnp.bfloat16).astype(jnp.float32)

    first_keys = k_ref[0, 0:_BF16_SUBLANES, :]
    for kv in range(N_KV_HEADS):
        s0 = jnp.dot(first_keys, qp[kv], preferred_element_type=jnp.float32)
        m_ref[kv] = as_reference(jnp.max(s0, axis=0, keepdims=True))

    ones_cols = (lax.broadcasted_iota(jnp.int32, (tk, _LANES), 1) == 0).astype(jnp.bfloat16)
    ref_row = lax.broadcasted_iota(jnp.int32, (_BF16_SUBLANES, nq), 0) == 0
    pad_rows = jnp.zeros((_LANES - _BF16_SUBLANES, nq), jnp.bfloat16)

    def fast_tile(ki):
        kt = jnp.concatenate([key_tile(ki), ones_cols], axis=1)
        for kv in range(N_KV_HEADS):
            m_old = m_ref[kv]
            neg_ref = jnp.where(ref_row, -m_old, 0.0).astype(jnp.bfloat16)
            q_aug = jnp.concatenate([qp[kv], neg_ref, pad_rows], axis=0)
            s = jnp.dot(kt, q_aug, preferred_element_type=jnp.float32)
            p = jnp.exp2(s).astype(jnp.bfloat16)
            over = jnp.max(s, axis=0, keepdims=True)
            pv = jnp.dot(value_rows(ki, kv), p, preferred_element_type=jnp.float32)
            m_new = as_reference(m_old + jnp.maximum(over, 0.0))
            acc_ref[kv] = (acc_ref[kv] + pv) * jnp.exp2(m_old - m_new)
            m_ref[kv] = m_new
            gap_ref[kv] = jnp.maximum(gap_ref[kv], over)

    per_trip = math.gcd(n_kt, _KEY_TILES_PER_TRIP)

    def fast_group(j, carry):
        for t in range(per_trip):
            fast_tile(per_trip * j + t)
        return carry

    lax.fori_loop(0, n_kt // per_trip, fast_group, 0)

    @pl.when(jnp.max(gap_ref[...]) > _MAX_LAGGED_EXPONENT)
    def _():
        m_ref[...] = jnp.full(m_ref.shape, _NEG_BIG, jnp.float32)
        acc_ref[...] = jnp.zeros(acc_ref.shape, jnp.float32)

        def robust_tile(ki, carry):
            kt = key_tile(ki)
            for kv in range(N_KV_HEADS):
                s = jnp.dot(kt, qp[kv], preferred_element_type=jnp.float32)
                m_old = m_ref[kv]
                m_new = jnp.maximum(m_old, jnp.max(s, axis=0, keepdims=True))
                p = jnp.exp2(s - m_new).astype(jnp.bfloat16)
                pv = jnp.dot(value_rows(ki, kv), p, preferred_element_type=jnp.float32)
                acc_ref[kv] = jnp.exp2(m_old - m_new) * acc_ref[kv] + pv
                m_ref[kv] = m_new
            return carry

        lax.fori_loop(0, n_kt, robust_tile, 0)

    for kv in range(N_KV_HEADS):
        acc = acc_ref[kv]
        o = acc[:HEAD_DIM] / acc[HEAD_DIM:HEAD_DIM + 1]
        for g in range(GQA_GROUP):
            r0 = (kv * GQA_GROUP + g) * HEAD_DIM
            o_ref[0, r0:r0 + HEAD_DIM, :] = o[:, g * tq:(g + 1) * tq].astype(jnp.bfloat16)


def _attention(qT, k, vT):
    B, _, S = qT.shape
    tq = Q_TILE
    nq = GQA_GROUP * tq
    n_kt, tk = vT.shape[1], vT.shape[3]
    return pl.pallas_call(
        _attention_kernel,
        grid=(B, S // tq),
        in_specs=[
            pl.BlockSpec((1, Q_W, tq), lambda b, i: (b, 0, i)),
            pl.BlockSpec((1, S, KV_W), lambda b, i: (b, 0, 0)),
            pl.BlockSpec((1, n_kt, KV_W, tk), lambda b, i: (b, 0, 0, 0)),
        ],
        out_specs=pl.BlockSpec((1, Q_W, tq), lambda b, i: (b, 0, i)),
        out_shape=jax.ShapeDtypeStruct((B, Q_W, S), jnp.bfloat16),
        scratch_shapes=[
            pltpu.VMEM((N_KV_HEADS, 1, nq), jnp.float32),
            pltpu.VMEM((N_KV_HEADS, 1, nq), jnp.float32),
            pltpu.VMEM((N_KV_HEADS, HEAD_DIM + _SUM_ROWS, nq), jnp.float32),
        ],
        compiler_params=_compiler_params("arbitrary", "arbitrary"),
        name="attention",
    )(qT, k, vT)


def _post_kernel(attnT_ref, uT_ref, vsT_ref, gaT_ref, gbT_ref, x_ref,
                 wsT_ref, bs_ref, wpaT_ref, wpbT_ref, wout_ref, gffn_ref, wrT_ref, br_ref,
                 x1_ref, h2_ref, ids_ref, wts_ref, rank_ref, cnt_ref, carry_ref):
    tm = x_ref.shape[1]
    n_chunks = tm // CHUNK
    first = jnp.logical_and(pl.program_id(0) == 0, pl.program_id(1) == 0)

    @pl.when(first)
    def _():
        carry_ref[...] = jnp.zeros(carry_ref.shape, jnp.float32)

    gate_rows = []
    for g in range(SGU_GROUPS):
        r0 = g * SGU_GROUP_DIM
        vs_g = vsT_ref[0, r0:r0 + SGU_GROUP_DIM, :]
        lhs = jnp.concatenate(
            [vs_g[:, c * CHUNK:(c + 1) * CHUNK] for c in range(n_chunks)], axis=0)
        mixed = jnp.dot(lhs, wsT_ref[g], preferred_element_type=jnp.float32)
        mixed = mixed + bs_ref[g]
        mixedT = jnp.concatenate(
            [mixed[c * SGU_GROUP_DIM:(c + 1) * SGU_GROUP_DIM] for c in range(n_chunks)], axis=1)
        u_g = uT_ref[0, r0:r0 + SGU_GROUP_DIM, :].astype(jnp.float32)
        gate_rows.append((u_g * mixedT).astype(jnp.bfloat16))
    gateT = jnp.concatenate(gate_rows, axis=0)

    paT = jnp.dot(wpaT_ref[...], attnT_ref[0], preferred_element_type=jnp.float32)
    pbT = jnp.dot(wpbT_ref[...], gateT, preferred_element_type=jnp.float32)
    mT = (gaT_ref[0].astype(jnp.float32) * paT
          + gbT_ref[0].astype(jnp.float32) * pbT).astype(jnp.bfloat16)
    y = lax.dot_general(mT, wout_ref[...], _TN_DIMS, preferred_element_type=jnp.float32)
    x1 = x_ref[0] + y
    x1_ref[0] = x1

    ms = jnp.mean(x1 * x1, axis=-1, keepdims=True)
    h2 = x1 * lax.rsqrt(ms + EPS) * gffn_ref[...]
    h2_ref[0] = _pack_row_halves(h2)

    h_hi = h2.astype(jnp.bfloat16)
    h_lo = (h2 - h_hi.astype(jnp.float32)).astype(jnp.bfloat16)
    by_hi = lax.dot_general(wrT_ref[...], h_hi, _NT_DIMS, preferred_element_type=jnp.float32)
    by_lo = lax.dot_general(wrT_ref[:N_EXPERTS, :], h_lo, _NT_DIMS,
                            preferred_element_type=jnp.float32)
    logits = by_hi[:N_EXPERTS] + by_hi[N_EXPERTS:] + by_lo + br_ref[...]
    eidx = lax.broadcasted_iota(jnp.int32, logits.shape, 0).astype(jnp.float32)
    work = logits
    vals, ids, sels = [], [], []
    for _ in range(TOP_K):
        mx = jnp.max(work, axis=0, keepdims=True)
        idx = jnp.min(jnp.where(work == mx, eidx, float(N_EXPERTS)), axis=0, keepdims=True)
        sel = eidx == idx
        vals.append(mx)
        ids.append(idx)
        sels.append(sel)
        work = jnp.where(sel, -jnp.inf, work)
    exps = [jnp.exp(v - vals[0]) for v in vals]
    denom = exps[0] + exps[1] + exps[2] + exps[3]
    ids_ref[...] = jnp.concatenate(ids, axis=0).astype(jnp.int32)
    wts_ref[...] = jnp.concatenate([e / denom for e in exps], axis=0)

    onehot = [s.astype(jnp.float32) for s in sels]
    hits = onehot[0] + onehot[1] + onehot[2] + onehot[3]
    ti = lax.broadcasted_iota(jnp.int32, (tm, tm), 0)
    tj = lax.broadcasted_iota(jnp.int32, (tm, tm), 1)
    upper = (ti < tj).astype(jnp.bfloat16)
    prefix = jnp.dot(hits.astype(jnp.bfloat16), upper, preferred_element_type=jnp.float32)
    base = prefix + carry_ref[...]
    ranks = [jnp.sum(oh * base, axis=0, keepdims=True) for oh in onehot]
    rank_ref[...] = jnp.concatenate(ranks, axis=0).astype(jnp.int32)
    carry_ref[...] = carry_ref[...] + jnp.sum(hits, axis=1, keepdims=True)
    cnt_ref[...] = carry_ref[...].astype(jnp.int32)


def _post(attnT, uT, vsT, gaT, gbT, x, wsT, bs, wpaT, wpbT, wout, gffn, wrT, br):
    B, S, _ = x.shape
    tm = TOKEN_TILE
    nt = S // tm
    T = B * S
    fm = lambda rows: pl.BlockSpec((1, rows, tm), lambda b, i: (b, 0, i))
    rowm = pl.BlockSpec((1, tm, D_MODEL), lambda b, i: (b, i, 0))
    c2 = lambda b, i: (0, 0)
    c3 = lambda b, i: (0, 0, 0)
    tokT = pl.BlockSpec((TOP_K, tm), lambda b, i: (0, b * nt + i))
    return pl.pallas_call(
        _post_kernel,
        grid=(B, nt),
        in_specs=[
            fm(Q_W), fm(SGU_W), fm(SGU_W), fm(D_MODEL), fm(D_MODEL), rowm,
            pl.BlockSpec((SGU_GROUPS, CHUNK, CHUNK), c3),
            pl.BlockSpec((SGU_GROUPS, 1, CHUNK), c3),
            pl.BlockSpec((D_MODEL, Q_W), c2),
            pl.BlockSpec((D_MODEL, SGU_W), c2),
            pl.BlockSpec((D_MODEL, D_MODEL), c2),
            pl.BlockSpec((1, D_MODEL), c2),
            pl.BlockSpec((2 * N_EXPERTS, D_MODEL), c2),
            pl.BlockSpec((N_EXPERTS, 1), c2),
        ],
        out_specs=[rowm, pl.BlockSpec((1, tm, PACKED_W), lambda b, i: (b, i, 0)),
                   tokT, tokT, tokT, pl.BlockSpec((N_EXPERTS, 1), c2)],
        out_shape=[
            jax.ShapeDtypeStruct((B, S, D_MODEL), jnp.float32),
            jax.ShapeDtypeStruct((B, S, PACKED_W), jnp.int32),
            jax.ShapeDtypeStruct((TOP_K, T), jnp.int32),
            jax.ShapeDtypeStruct((TOP_K, T), jnp.float32),
            jax.ShapeDtypeStruct((TOP_K, T), jnp.int32),
            jax.ShapeDtypeStruct((N_EXPERTS, 1), jnp.int32),
        ],
        scratch_shapes=[pltpu.VMEM((N_EXPERTS, 1), jnp.float32)],
        compiler_params=_compiler_params("arbitrary", "arbitrary"),
        name="post",
    )(attnT, uT, vsT, gaT, gbT, x, wsT, bs, wpaT, wpbT, wout, gffn, wrT, br)


def _sc_kernel(name, body, out_type, n_items, row_shape, row_dtype):
    info = plsc.get_sparse_core_info()
    n_workers = info.num_cores * info.num_subcores
    per_worker = n_items // n_workers
    n_windows = per_worker // SC_WINDOW
    assert per_worker * n_workers == n_items and n_windows * SC_WINDOW == per_worker
    assert n_windows % 2 == 0

    def wrapped(*refs):
        wid = lax.axis_index("subcore") * info.num_cores + lax.axis_index("core")
        body(wid * per_worker, n_windows, *refs)

    return pl.kernel(
        wrapped,
        name=name,
        out_type=out_type,
        mesh=plsc.VectorSubcoreMesh(core_axis_name="core", subcore_axis_name="subcore"),
        scratch_types=[
            pltpu.VMEM((SC_WINDOW,), jnp.int32),
            pltpu.VMEM((SC_WINDOW,), jnp.int32),
            pltpu.VMEM((SC_WINDOW,) + row_shape, row_dtype),
            pltpu.VMEM((SC_WINDOW,) + row_shape, row_dtype),
            pltpu.SemaphoreType.DMA,
            pltpu.SemaphoreType.DMA,
        ],
    )


def _sc_gather_rows(table, idx):
    n, d, win = idx.shape[0], table.shape[1], SC_WINDOW

    def body(base, n_windows, table_hbm, idx_hbm, out_hbm, idx0, idx1, rows0, rows1, sem0, sem1):
        def window(c):
            return pl.ds(pl.multiple_of(base + c * win, win), win)

        def fetch(c, idx_v, rows_v, sem):
            pltpu.sync_copy(idx_hbm.at[window(c)], idx_v)
            pltpu.async_copy(table_hbm.at[idx_v], rows_v, sem)

        def drain(c, idx_v, rows_v, sem):
            pltpu.make_async_copy(table_hbm.at[idx_v], rows_v, sem).wait()
            pltpu.sync_copy(rows_v, out_hbm.at[window(c)])

        fetch(0, idx0, rows0, sem0)

        @pl.loop(0, n_windows, step=2)
        def _(c):
            fetch(c + 1, idx1, rows1, sem1)
            drain(c, idx0, rows0, sem0)

            @pl.when(c + 2 < n_windows)
            def _():
                fetch(c + 2, idx0, rows0, sem0)

            drain(c + 1, idx1, rows1, sem1)

    out_type = jax.ShapeDtypeStruct((n, d), table.dtype)
    return _sc_kernel("sc_gather_rows", body, out_type, n, (d,), table.dtype)(table, idx)


def _sc_scatter_rows(src, dest, n_rows):
    T, d = src.shape
    win = SC_WINDOW

    def body(base, n_windows, src_hbm, dest_hbm, out_hbm, idx0, idx1, rows0, rows1, sem0, sem1):
        def window(c, slot=0):
            return pl.ds(pl.multiple_of(slot * T + base + c * win, win), win)

        def fetch(c, rows_v, sem):
            pltpu.async_copy(src_hbm.at[window(c)], rows_v, sem)

        def push(c, rows_v, sem):
            pltpu.make_async_copy(src_hbm.at[window(c)], rows_v, sem).wait()
            for k in range(TOP_K):
                idx_v = idx0 if k % 2 == 0 else idx1
                pltpu.sync_copy(dest_hbm.at[window(c, k)], idx_v)
                pltpu.sync_copy(rows_v, out_hbm.at[idx_v])

        fetch(0, rows0, sem0)

        @pl.loop(0, n_windows, step=2)
        def _(c):
            fetch(c + 1, rows1, sem1)
            push(c, rows0, sem0)

            @pl.when(c + 2 < n_windows)
            def _():
                fetch(c + 2, rows0, sem0)

            push(c + 1, rows1, sem1)

    out_type = jax.ShapeDtypeStruct((n_rows, d), src.dtype)
    return _sc_kernel("sc_scatter_rows", body, out_type, T, (d,), src.dtype)(
        src, dest.reshape(TOP_K * T))


def _combine_dense_kernel(w_ref, x1_ref, gfin_ref, y_ref, o_ref):
    w = w_ref[...].T
    moe = w[:, 0:1] * _unpack_row_halves(y_ref[0])
    for k in range(1, TOP_K):
        moe = moe + w[:, k:k + 1] * _unpack_row_halves(y_ref[k])
    x2 = x1_ref[...] + moe
    ms = jnp.mean(x2 * x2, axis=-1, keepdims=True)
    o_ref[...] = x2 * lax.rsqrt(ms + EPS) * gfin_ref[...]


def _combine_dense(wts, x1, gfin, y_tok):
    T = x1.shape[0]
    tt = COMBINE_TILE
    return pl.pallas_call(
        _combine_dense_kernel,
        grid=(T // tt,),
        in_specs=[
            pl.BlockSpec((TOP_K, tt), lambda i: (0, i)),
            pl.BlockSpec((tt, D_MODEL), lambda i: (i, 0)),
            pl.BlockSpec((1, D_MODEL), lambda i: (0, 0)),
            pl.BlockSpec((TOP_K, tt, PACKED_W), lambda i: (0, i, 0)),
        ],
        out_specs=pl.BlockSpec((tt, D_MODEL), lambda i: (i, 0)),
        out_shape=jax.ShapeDtypeStruct((T, D_MODEL), jnp.float32),
        compiler_params=_compiler_params("arbitrary"),
        name="combine_dense",
    )(wts, x1, gfin, y_tok)


def _gate_up_prep_kernel(w_ref, o_ref):
    half = _GU_BLOCK // 2
    i = lax.broadcasted_iota(jnp.int32, (_GU_BLOCK, _GU_BLOCK), 0)
    j = lax.broadcasted_iota(jnp.int32, (_GU_BLOCK, _GU_BLOCK), 1)
    perm = (i == jnp.where(j < half, 2 * j, 2 * (j - half) + 1)).astype(jnp.bfloat16)
    for c in range(2 * D_FF // _GU_BLOCK):
        cols = slice(c * _GU_BLOCK, (c + 1) * _GU_BLOCK)
        blk = w_ref[0, :, cols].astype(jnp.bfloat16)
        o_ref[0, :, cols] = jnp.dot(blk, perm, preferred_element_type=jnp.float32).astype(jnp.bfloat16)


def _gate_up_prep(w_gate_up):
    spec = pl.BlockSpec((1, D_MODEL, 2 * D_FF), lambda e: (e, 0, 0))
    return pl.pallas_call(
        _gate_up_prep_kernel,
        grid=(N_EXPERTS,),
        in_specs=[spec],
        out_specs=spec,
        out_shape=jax.ShapeDtypeStruct(w_gate_up.shape, jnp.bfloat16),
        compiler_params=_compiler_params("arbitrary"),
        name="gate_up_prep",
    )(w_gate_up)


def _split_gate_up(gu):
    half = _GU_BLOCK // 2
    n = gu.shape[1] // _GU_BLOCK
    glu = jnp.concatenate([gu[:, c * _GU_BLOCK:c * _GU_BLOCK + half] for c in range(n)], axis=1)
    lin = jnp.concatenate([gu[:, c * _GU_BLOCK + half:(c + 1) * _GU_BLOCK] for c in range(n)], axis=1)
    return glu, lin


def _experts_kernel(te_ref, nv_ref, x_ref, wgu_ref, bgu_ref, wd_ref, bd_ref, y_ref):
    @pl.when(pl.program_id(0) < nv_ref[0])
    def _():
        x = _unpack_row_halves(x_ref[...]).astype(jnp.bfloat16)
        gu = jnp.dot(x, wgu_ref[0], preferred_element_type=jnp.float32) + bgu_ref[0]
        glu, lin = _split_gate_up(gu)
        glu = jnp.minimum(glu, SWIGLU_LIMIT)
        lin = jnp.clip(lin, -SWIGLU_LIMIT, SWIGLU_LIMIT)
        a = glu * jax.nn.sigmoid(SWIGLU_ALPHA * glu) * (lin + 1.0)
        y = jnp.dot(a.astype(jnp.bfloat16), wd_ref[0],
                    preferred_element_type=jnp.float32) + bd_ref[0]
        y_ref[...] = _pack_row_halves(y)


def _experts(tile_expert, n_valid, x_sorted, wgu, bgu, wd, bd):
    n_rows = x_sorted.shape[0]
    tr = EXPERT_TILE
    n_tiles = n_rows // tr

    def row_map(j, te, nv):
        return (jnp.minimum(j, nv[0] - 1), 0)

    def exp_map(j, te, nv):
        return (te[j], 0, 0)

    grid_spec = pltpu.PrefetchScalarGridSpec(
        num_scalar_prefetch=2,
        grid=(n_tiles,),
        in_specs=[
            pl.BlockSpec((tr, PACKED_W), row_map),
            pl.BlockSpec((1, D_MODEL, 2 * D_FF), exp_map),
            pl.BlockSpec((1, 1, 2 * D_FF), exp_map),
            pl.BlockSpec((1, D_FF, D_MODEL), exp_map),
            pl.BlockSpec((1, 1, D_MODEL), exp_map),
        ],
        out_specs=pl.BlockSpec((tr, PACKED_W), row_map),
    )
    return pl.pallas_call(
        _experts_kernel,
        grid_spec=grid_spec,
        out_shape=jax.ShapeDtypeStruct((n_rows, PACKED_W), jnp.int32),
        compiler_params=_compiler_params("arbitrary"),
        name="experts",
    )(tile_expert, n_valid, x_sorted, wgu, bgu, wd, bd)


def _rope_tables(S, gain, scale):
    t = jnp.arange(S, dtype=jnp.int32)
    r = (t // GRID_W).astype(jnp.float32)
    c = (t % GRID_W).astype(jnp.float32)
    inv = jnp.float32(ROPE_THETA) ** (
        -jnp.arange(0, ROPE_AXIS_DIM, 2, dtype=jnp.float32) / ROPE_AXIS_DIM)
    ang = jnp.concatenate([r[None, :] * inv[:, None], c[None, :] * inv[:, None]], axis=0)
    cos = jnp.repeat(jnp.cos(ang), 2, axis=0)
    sin = jnp.repeat(jnp.sin(ang), 2, axis=0)
    sign = jnp.where(jnp.arange(HEAD_DIM) % 2 == 0, -1.0, 1.0).astype(jnp.float32)
    g = gain.astype(jnp.float32) * scale
    g_swapped = g.reshape(HEAD_DIM // 2, 2)[:, ::-1].reshape(HEAD_DIM)
    return g[:, None] * cos, (g_swapped * sign)[:, None] * sin


def _split_hi_lo(w):
    hi = w.astype(jnp.bfloat16)
    lo = (w - hi.astype(jnp.float32)).astype(jnp.bfloat16)
    return jnp.concatenate([hi, lo], axis=0)


def _prepare_weights(norm_mix_g, w_in, q_norm_g, k_norm_g, sgu_norm_g, w_spatial, b_spatial,
                     w_proj_attn, w_proj_sgu, w_out, norm_ffn_g, w_router, b_router,
                     w_gate_up, b_gate_up, w_down, b_down, norm_final_g):
    bf = jnp.bfloat16
    l = 0
    return dict(
        gmix=norm_mix_g[l][None, :],
        w_inT=w_in[l].T.astype(bf),
        q_gain=q_norm_g[l], k_gain=k_norm_g[l],
        gs=jnp.broadcast_to(sgu_norm_g[l][:, None], (SGU_W, _LANES)),
        wsT=jnp.swapaxes(w_spatial[l], 1, 2).astype(bf),
        bs=b_spatial[l][:, None, :],
        wpaT=w_proj_attn[l].T.astype(bf),
        wpbT=w_proj_sgu[l].T.astype(bf),
        wout=w_out[l].astype(bf),
        gffn=norm_ffn_g[l][None, :],
        wrT=_split_hi_lo(w_router[l].T),
        br=b_router[l][:, None],
        wgu=_gate_up_prep(w_gate_up[l]),
        bgu=b_gate_up[l].reshape(N_EXPERTS, -1, _GU_BLOCK // 2, 2).transpose(0, 1, 3, 2)
        .reshape(N_EXPERTS, 1, 2 * D_FF),
        wd=w_down[l].astype(bf),
        bd=b_down[l][:, None, :],
        gfin=norm_final_g[None, :],
    )


def _trunk(x, w, rope):
    B, S, _ = x.shape
    T = B * S
    qT, k, vT, uT, vsT, gaT, gbT = _in_proj(x, w["gmix"], w["w_inT"], *rope, w["gs"])
    attnT = _attention(qT, k, vT)
    x1, h2, ids, wts, rank, counts = _post(
        attnT, uT, vsT, gaT, gbT, x, w["wsT"], w["bs"], w["wpaT"], w["wpbT"], w["wout"],
        w["gffn"], w["wrT"], w["br"])

    tr = EXPERT_TILE
    counts = counts[:, 0]
    padded = (counts + tr - 1) // tr * tr
    ends = jnp.cumsum(padded)
    starts = ends - padded
    n_tiles = (TOP_K * T) // tr + N_EXPERTS
    n_rows = n_tiles * tr
    dest = rank
    for e in range(N_EXPERTS):
        dest = dest + jnp.where(ids == e, starts[e], 0)
    tile_start = jnp.arange(n_tiles, dtype=jnp.int32) * tr
    tile_expert = jnp.minimum(
        jnp.sum((tile_start[:, None] >= ends[None, :]).astype(jnp.int32), axis=1), N_EXPERTS - 1)
    n_valid = (ends[-1] // tr).astype(jnp.int32)[None]

    x_sorted = _sc_scatter_rows(h2.reshape(T, PACKED_W), dest, n_rows)
    y_sorted = _experts(tile_expert, n_valid, x_sorted, w["wgu"], w["bgu"], w["wd"], w["bd"])
    y_tok = _sc_gather_rows(y_sorted, dest.reshape(TOP_K * T)).reshape(TOP_K, T, PACKED_W)
    out = _combine_dense(wts, x1.reshape(T, D_MODEL), w["gfin"], y_tok)
    return out.reshape(B, S, D_MODEL)


def kernel(x_prompt, x_sample, norm_mix_g, w_in, q_norm_g, k_norm_g, sgu_norm_g, w_spatial,
           b_spatial, w_proj_attn, w_proj_sgu, w_out, norm_ffn_g, w_router, b_router,
           w_gate_up, b_gate_up, w_down, b_down, norm_final_g):
    w = _prepare_weights(norm_mix_g, w_in, q_norm_g, k_norm_g, sgu_norm_g, w_spatial,
                         b_spatial, w_proj_attn, w_proj_sgu, w_out, norm_ffn_g, w_router,
                         b_router, w_gate_up, b_gate_up, w_down, b_down, norm_final_g)
    s_max = max(x_prompt.shape[1], x_sample.shape[1])
    rope = (*_rope_tables(s_max, w["q_gain"], math.log2(math.e) / math.sqrt(HEAD_DIM)),
            *_rope_tables(s_max, w["k_gain"], 1.0))
    return (_trunk(x_prompt, w, rope), _trunk(x_sample, w, rope))
```

```python
import math

import jax
import jax.numpy as jnp
from jax import lax
from jax.experimental import pallas as pl
from jax.experimental.pallas import tpu as pltpu
from jax.experimental.pallas import tpu_sc as plsc

D_MODEL = 1024
GRID_W = 64
N_HEADS = 8
N_KV_HEADS = 2
HEAD_DIM = 64
GQA_GROUP = N_HEADS // N_KV_HEADS
Q_W = N_HEADS * HEAD_DIM
KV_W = N_KV_HEADS * HEAD_DIM
ROPE_AXIS_DIM = HEAD_DIM // 2
ROPE_THETA = 10000.0
SGU_GROUPS = 8
SGU_W = D_MODEL // 2
SGU_GROUP_DIM = SGU_W // SGU_GROUPS
CHUNK = 128
IN_W = Q_W + 2 * KV_W + 2 * SGU_W + 2 * D_MODEL
N_EXPERTS = 32
TOP_K = 4
D_FF = D_MODEL
SWIGLU_LIMIT = 7.0
SWIGLU_ALPHA = 1.702
EPS = 1e-6

_Q0, _K0, _V0 = 0, Q_W, Q_W + KV_W
_U0 = Q_W + 2 * KV_W
_VS0 = _U0 + SGU_W
_GA0 = _VS0 + SGU_W
_GB0 = _GA0 + D_MODEL

TOKEN_TILE = 512
_IN_PROJ_SUBTILES = 2
Q_TILE = 512
EXPERT_TILE = 512
COMBINE_TILE = 512
PACKED_W = D_MODEL // 2
SC_WINDOW = 64

_LANES = 128
_BF16_SUBLANES = 16
_SUM_ROWS = _BF16_SUBLANES
_GU_BLOCK = 256
_NEG_BIG = -1e30
_HIGH_HALF_MASK = 0xFFFF0000
_MAX_LAGGED_EXPONENT = 80.0
_KEY_TILES_PER_TRIP = 4
_MIB = 1024 * 1024
_VMEM_LIMIT_MIB = 40

_NT_DIMS = (((1,), (1,)), ((), ()))
_TN_DIMS = (((0,), (0,)), ((), ()))


def _compiler_params(*semantics):
    return pltpu.CompilerParams(
        dimension_semantics=semantics, vmem_limit_bytes=_VMEM_LIMIT_MIB * _MIB)


def _swap_adjacent_rows(x):
    n = x.shape[0]
    row = lax.broadcasted_iota(jnp.int32, x.shape, 0)
    nxt = pltpu.roll(x, n - 1, 0)
    prv = pltpu.roll(x, 1, 0)
    return jnp.where((row & 1) == 0, nxt, prv)


def _head_norm_rope(z, tab_a, tab_b, n_heads):
    tm = z.shape[1]
    z3 = z.reshape(n_heads, HEAD_DIM, tm)
    ms = jnp.mean(z3 * z3, axis=1, keepdims=True)
    r = lax.rsqrt(ms + EPS)
    zs = _swap_adjacent_rows(z).reshape(n_heads, HEAD_DIM, tm)
    out = (z3 * tab_a[None] + zs * tab_b[None]) * r
    return out.reshape(n_heads * HEAD_DIM, tm)


def _gelu(x):
    return 0.5 * x * (1.0 + lax.erf(x * (1.0 / math.sqrt(2.0))))


def _pack_row_halves(x):
    half = x.shape[1] // 2

    def bf16_bits(v):
        return lax.bitcast_convert_type(v.astype(jnp.bfloat16).astype(jnp.float32), jnp.uint32)

    word = (bf16_bits(x[:, :half]) >> 16) | (bf16_bits(x[:, half:]) & jnp.uint32(_HIGH_HALF_MASK))
    return lax.bitcast_convert_type(word, jnp.int32)


def _unpack_row_halves(p):
    word = lax.bitcast_convert_type(p, jnp.uint32)
    lo = lax.bitcast_convert_type(word << 16, jnp.float32)
    hi = lax.bitcast_convert_type(word & jnp.uint32(_HIGH_HALF_MASK), jnp.float32)
    return jnp.concatenate([lo, hi], axis=1)


def _tile_lanes(x, reps):
    return jnp.concatenate([x] * reps, axis=1) if reps > 1 else x


def _in_proj_kernel(x_ref, gmix_ref, w_ref, qa_ref, qb_ref, ka_ref, kb_ref, gs_ref,
                    qT_ref, k_ref, vT_ref, uT_ref, vsT_ref, gaT_ref, gbT_ref):
    tm = x_ref.shape[1]
    sub = tm // _IN_PROJ_SUBTILES
    for i in range(_IN_PROJ_SUBTILES):
        tok = slice(i * sub, (i + 1) * sub)
        x = x_ref[0, tok, :]
        ms = jnp.mean(x * x, axis=-1, keepdims=True)
        h = (x * lax.rsqrt(ms + EPS) * gmix_ref[...]).astype(jnp.bfloat16)

        def proj(r0, rows, h=h):
            return lax.dot_general(w_ref[r0:r0 + rows, :], h, _NT_DIMS,
                                   preferred_element_type=jnp.float32)

        zq = proj(_Q0, Q_W)
        qT_ref[0, :, tok] = _head_norm_rope(
            zq, qa_ref[:, tok], qb_ref[:, tok], N_HEADS).astype(jnp.bfloat16)

        zkv = proj(_K0, 2 * KV_W)
        kT = _head_norm_rope(zkv[:KV_W], ka_ref[:, tok], kb_ref[:, tok], N_KV_HEADS)
        k_ref[0, tok, :] = kT.T.astype(jnp.bfloat16)
        vT_ref[0, 0, :, tok] = zkv[KV_W:].astype(jnp.bfloat16)

        uT_ref[0, :, tok] = _gelu(proj(_U0, SGU_W)).astype(jnp.bfloat16)

        vs = _gelu(proj(_VS0, SGU_W))
        vms = jnp.mean(vs * vs, axis=0, keepdims=True)
        gs = _tile_lanes(gs_ref[...], sub // _LANES)
        vsT_ref[0, :, tok] = (vs * lax.rsqrt(vms + EPS) * gs).astype(jnp.bfloat16)

        gaT_ref[0, :, tok] = jax.nn.sigmoid(proj(_GA0, D_MODEL)).astype(jnp.bfloat16)
        gbT_ref[0, :, tok] = jax.nn.sigmoid(proj(_GB0, D_MODEL)).astype(jnp.bfloat16)


def _in_proj(x, gmix, w_inT, qa, qb, ka, kb, gs):
    B, S, _ = x.shape
    tm = TOKEN_TILE
    nt = S // tm
    bf = jnp.bfloat16
    const2 = lambda b, i: (0, 0)
    tab = pl.BlockSpec((HEAD_DIM, tm), lambda b, i: (0, i))
    fm = lambda rows: pl.BlockSpec((1, rows, tm), lambda b, i: (b, 0, i))
    return pl.pallas_call(
        _in_proj_kernel,
        grid=(B, nt),
        in_specs=[
            pl.BlockSpec((1, tm, D_MODEL), lambda b, i: (b, i, 0)),
            pl.BlockSpec((1, D_MODEL), const2),
            pl.BlockSpec((IN_W, D_MODEL), const2),
            tab, tab, tab, tab,
            pl.BlockSpec((SGU_W, _LANES), const2),
        ],
        out_specs=[
            fm(Q_W),
            pl.BlockSpec((1, tm, KV_W), lambda b, i: (b, i, 0)),
            pl.BlockSpec((1, 1, KV_W, tm), lambda b, i: (b, i, 0, 0)),
            fm(SGU_W), fm(SGU_W), fm(D_MODEL), fm(D_MODEL),
        ],
        out_shape=[
            jax.ShapeDtypeStruct((B, Q_W, S), bf),
            jax.ShapeDtypeStruct((B, S, KV_W), bf),
            jax.ShapeDtypeStruct((B, nt, KV_W, tm), bf),
            jax.ShapeDtypeStruct((B, SGU_W, S), bf),
            jax.ShapeDtypeStruct((B, SGU_W, S), bf),
            jax.ShapeDtypeStruct((B, D_MODEL, S), bf),
            jax.ShapeDtypeStruct((B, D_MODEL, S), bf),
        ],
        compiler_params=_compiler_params("arbitrary", "arbitrary"),
        name="in_proj",
    )(x, gmix, w_inT, qa, qb, ka, kb, gs)


def _attention_kernel(qT_ref, k_ref, vT_ref, o_ref, m_ref, gap_ref, acc_ref):
    tq = qT_ref.shape[2]
    n_kt = vT_ref.shape[1]
    tk = vT_ref.shape[3]
    nq = GQA_GROUP * tq
    ones_rows = (lax.broadcasted_iota(jnp.int32, (_SUM_ROWS, tk), 0) == 0).astype(jnp.bfloat16)

    zeros = jnp.zeros((HEAD_DIM, nq), jnp.bfloat16)
    qp = []
    for kv in range(N_KV_HEADS):
        heads = [qT_ref[0, (kv * GQA_GROUP + g) * HEAD_DIM:(kv * GQA_GROUP + g + 1) * HEAD_DIM, :]
                 for g in range(GQA_GROUP)]
        qk = jnp.concatenate(heads, axis=1)
        qp.append(jnp.concatenate([qk, zeros] if kv == 0 else [zeros, qk], axis=0))

    def key_tile(ki):
        return k_ref[0, pl.ds(pl.multiple_of(ki * tk, tk), tk), :]

    def value_rows(ki, kv):
        vt = vT_ref[0, ki]
        return jnp.concatenate([vt[kv * HEAD_DIM:(kv + 1) * HEAD_DIM, :], ones_rows], axis=0)

    acc_ref[...] = jnp.zeros(acc_ref.shape, jnp.float32)
    gap_ref[...] = jnp.zeros(gap_ref.shape, jnp.float32)
    def as_reference(m):
        return m.astype(jnp.bfloat16).astype(jnp.float32)

    first_keys = k_ref[0, 0:_BF16_SUBLANES, :]
    for kv in range(N_KV_HEADS):
        s0 = jnp.dot(first_keys, qp[kv], preferred_element_type=jnp.float32)
        m_ref[kv] = as_reference(jnp.max(s0, axis=0, keepdims=True))

    ones_cols = (lax.broadcasted_iota(jnp.int32, (tk, _LANES), 1) == 0).astype(jnp.bfloat16)
    ref_row = lax.broadcasted_iota(jnp.int32, (_BF16_SUBLANES, nq), 0) == 0
    pad_rows = jnp.zeros((_LANES - _BF16_SUBLANES, nq), jnp.bfloat16)

    def fast_tile(ki):
        kt = jnp.concatenate([key_tile(ki), ones_cols], axis=1)
        for kv in range(N_KV_HEADS):
            m_old = m_ref[kv]
            neg_ref = jnp.where(ref_row, -m_old, 0.0).astype(jnp.bfloat16)
            q_aug = jnp.concatenate([qp[kv], neg_ref, pad_rows], axis=0)
            s = jnp.dot(kt, q_aug, preferred_element_type=jnp.float32)
            p = jnp.exp2(s).astype(jnp.bfloat16)
            over = jnp.max(s, axis=0, keepdims=True)
            pv = jnp.dot(value_rows(ki, kv), p, preferred_element_type=jnp.float32)
            m_new = as_reference(m_old + jnp.maximum(over, 0.0))
            acc_ref[kv] = (acc_ref[kv] + pv) * jnp.exp2(m_old - m_new)
            m_ref[kv] = m_new
            gap_ref[kv] = jnp.maximum(gap_ref[kv], over)

    per_trip = math.gcd(n_kt, _KEY_TILES_PER_TRIP)

    def fast_group(j, carry):
        for t in range(per_trip):
            fast_tile(per_trip * j + t)
        return carry

    lax.fori_loop(0, n_kt // per_trip, fast_group, 0)

    @pl.when(jnp.max(gap_ref[...]) > _MAX_LAGGED_EXPONENT)
    def _():
        m_ref[...] = jnp.full(m_ref.shape, _NEG_BIG, jnp.float32)
        acc_ref[...] = jnp.zeros(acc_ref.shape, jnp.float32)

        def robust_tile(ki, carry):
            kt = key_tile(ki)
            for kv in range(N_KV_HEADS):
                s = jnp.dot(kt, qp[kv], preferred_element_type=jnp.float32)
                m_old = m_ref[kv]
                m_new = jnp.maximum(m_old, jnp.max(s, axis=0, keepdims=True))
                p = jnp.exp2(s - m_new).astype(jnp.bfloat16)
                pv = jnp.dot(value_rows(ki, kv), p, preferred_element_type=jnp.float32)
                acc_ref[kv] = jnp.exp2(m_old - m_new) * acc_ref[kv] + pv
                m_ref[kv] = m_new
            return carry

        lax.fori_loop(0, n_kt, robust_tile, 0)

    for kv in range(N_KV_HEADS):
        acc = acc_ref[kv]
        o = acc[:HEAD_DIM] / acc[HEAD_DIM:HEAD_DIM + 1]
        for g in range(GQA_GROUP):
            r0 = (kv * GQA_GROUP + g) * HEAD_DIM
            o_ref[0, r0:r0 + HEAD_DIM, :] = o[:, g * tq:(g + 1) * tq].astype(jnp.bfloat16)


def _attention(qT, k, vT):
    B, _, S = qT.shape
    tq = Q_TILE
    nq = GQA_GROUP * tq
    n_kt, tk = vT.shape[1], vT.shape[3]
    return pl.pallas_call(
        _attention_kernel,
        grid=(B, S // tq),
        in_specs=[
            pl.BlockSpec((1, Q_W, tq), lambda b, i: (b, 0, i)),
            pl.BlockSpec((1, S, KV_W), lambda b, i: (b, 0, 0)),
            pl.BlockSpec((1, n_kt, KV_W, tk), lambda b, i: (b, 0, 0, 0)),
        ],
        out_specs=pl.BlockSpec((1, Q_W, tq), lambda b, i: (b, 0, i)),
        out_shape=jax.ShapeDtypeStruct((B, Q_W, S), jnp.bfloat16),
        scratch_shapes=[
            pltpu.VMEM((N_KV_HEADS, 1, nq), jnp.float32),
            pltpu.VMEM((N_KV_HEADS, 1, nq), jnp.float32),
            pltpu.VMEM((N_KV_HEADS, HEAD_DIM + _SUM_ROWS, nq), jnp.float32),
        ],
        compiler_params=_compiler_params("arbitrary", "arbitrary"),
        name="attention",
    )(qT, k, vT)


def _post_kernel(attnT_ref, uT_ref, vsT_ref, gaT_ref, gbT_ref, x_ref,
                 wsT_ref, bs_ref, wpaT_ref, wpbT_ref, wout_ref, gffn_ref, wrT_ref, br_ref,
                 x1_ref, h2_ref, ids_ref, wts_ref, rank_ref, cnt_ref, carry_ref):
    tm = x_ref.shape[1]
    n_chunks = tm // CHUNK
    first = jnp.logical_and(pl.program_id(0) == 0, pl.program_id(1) == 0)

    @pl.when(first)
    def _():
        carry_ref[...] = jnp.zeros(carry_ref.shape, jnp.float32)

    gate_rows = []
    for g in range(SGU_GROUPS):
        r0 = g * SGU_GROUP_DIM
        vs_g = vsT_ref[0, r0:r0 + SGU_GROUP_DIM, :]
        lhs = jnp.concatenate(
            [vs_g[:, c * CHUNK:(c + 1) * CHUNK] for c in range(n_chunks)], axis=0)
        mixed = jnp.dot(lhs, wsT_ref[g], preferred_element_type=jnp.float32)
        mixed = mixed + bs_ref[g]
        mixedT = jnp.concatenate(
            [mixed[c * SGU_GROUP_DIM:(c + 1) * SGU_GROUP_DIM] for c in range(n_chunks)], axis=1)
        u_g = uT_ref[0, r0:r0 + SGU_GROUP_DIM, :].astype(jnp.float32)
        gate_rows.append((u_g * mixedT).astype(jnp.bfloat16))
    gateT = jnp.concatenate(gate_rows, axis=0)

    paT = jnp.dot(wpaT_ref[...], attnT_ref[0], preferred_element_type=jnp.float32)
    pbT = jnp.dot(wpbT_ref[...], gateT, preferred_element_type=jnp.float32)
    mT = (gaT_ref[0].astype(jnp.float32) * paT
          + gbT_ref[0].astype(jnp.float32) * pbT).astype(jnp.bfloat16)
    y = lax.dot_general(mT, wout_ref[...], _TN_DIMS, preferred_element_type=jnp.float32)
    x1 = x_ref[0] + y
    x1_ref[0] = x1

    ms = jnp.mean(x1 * x1, axis=-1, keepdims=True)
    h2 = x1 * lax.rsqrt(ms + EPS) * gffn_ref[...]
    h2_ref[0] = _pack_row_halves(h2)

    h_hi = h2.astype(jnp.bfloat16)
    h_lo = (h2 - h_hi.astype(jnp.float32)).astype(jnp.bfloat16)
    by_hi = lax.dot_general(wrT_ref[...], h_hi, _NT_DIMS, preferred_element_type=jnp.float32)
    by_lo = lax.dot_general(wrT_ref[:N_EXPERTS, :], h_lo, _NT_DIMS,
                            preferred_element_type=jnp.float32)
    logits = by_hi[:N_EXPERTS] + by_hi[N_EXPERTS:] + by_lo + br_ref[...]
    eidx = lax.broadcasted_iota(jnp.int32, logits.shape, 0).astype(jnp.float32)
    work = logits
    vals, ids, sels = [], [], []
    for _ in range(TOP_K):
        mx = jnp.max(work, axis=0, keepdims=True)
        idx = jnp.min(jnp.where(work == mx, eidx, float(N_EXPERTS)), axis=0, keepdims=True)
        sel = eidx == idx
        vals.append(mx)
        ids.append(idx)
        sels.append(sel)
        work = jnp.where(sel, -jnp.inf, work)
    exps = [jnp.exp(v - vals[0]) for v in vals]
    denom = exps[0] + exps[1] + exps[2] + exps[3]
    ids_ref[...] = jnp.concatenate(ids, axis=0).astype(jnp.int32)
    wts_ref[...] = jnp.concatenate([e / denom for e in exps], axis=0)

    onehot = [s.astype(jnp.float32) for s in sels]
    hits = onehot[0] + onehot[1] + onehot[2] + onehot[3]
    ti = lax.broadcasted_iota(jnp.int32, (tm, tm), 0)
    tj = lax.broadcasted_iota(jnp.int32, (tm, tm), 1)
    upper = (ti < tj).astype(jnp.bfloat16)
    prefix = jnp.dot(hits.astype(jnp.bfloat16), upper, preferred_element_type=jnp.float32)
    base = prefix + carry_ref[...]
    ranks = [jnp.sum(oh * base, axis=0, keepdims=True) for oh in onehot]
    rank_ref[...] = jnp.concatenate(ranks, axis=0).astype(jnp.int32)
    carry_ref[...] = carry_ref[...] + jnp.sum(hits, axis=1, keepdims=True)
    cnt_ref[...] = carry_ref[...].astype(jnp.int32)


def _post(attnT, uT, vsT, gaT, gbT, x, wsT, bs, wpaT, wpbT, wout, gffn, wrT, br):
    B, S, _ = x.shape
    tm = TOKEN_TILE
    nt = S // tm
    T = B * S
    fm = lambda rows: pl.BlockSpec((1, rows, tm), lambda b, i: (b, 0, i))
    rowm = pl.BlockSpec((1, tm, D_MODEL), lambda b, i: (b, i, 0))
    c2 = lambda b, i: (0, 0)
    c3 = lambda b, i: (0, 0, 0)
    tokT = pl.BlockSpec((TOP_K, tm), lambda b, i: (0, b * nt + i))
    return pl.pallas_call(
        _post_kernel,
        grid=(B, nt),
        in_specs=[
            fm(Q_W), fm(SGU_W), fm(SGU_W), fm(D_MODEL), fm(D_MODEL), rowm,
            pl.BlockSpec((SGU_GROUPS, CHUNK, CHUNK), c3),
            pl.BlockSpec((SGU_GROUPS, 1, CHUNK), c3),
            pl.BlockSpec((D_MODEL, Q_W), c2),
            pl.BlockSpec((D_MODEL, SGU_W), c2),
            pl.BlockSpec((D_MODEL, D_MODEL), c2),
            pl.BlockSpec((1, D_MODEL), c2),
            pl.BlockSpec((2 * N_EXPERTS, D_MODEL), c2),
            pl.BlockSpec((N_EXPERTS, 1), c2),
        ],
        out_specs=[rowm, pl.BlockSpec((1, tm, PACKED_W), lambda b, i: (b, i, 0)),
                   tokT, tokT, tokT, pl.BlockSpec((N_EXPERTS, 1), c2)],
        out_shape=[
            jax.ShapeDtypeStruct((B, S, D_MODEL), jnp.float32),
            jax.ShapeDtypeStruct((B, S, PACKED_W), jnp.int32),
            jax.ShapeDtypeStruct((TOP_K, T), jnp.int32),
            jax.ShapeDtypeStruct((TOP_K, T), jnp.float32),
            jax.ShapeDtypeStruct((TOP_K, T), jnp.int32),
            jax.ShapeDtypeStruct((N_EXPERTS, 1), jnp.int32),
        ],
        scratch_shapes=[pltpu.VMEM((N_EXPERTS, 1), jnp.float32)],
        compiler_params=_compiler_params("arbitrary", "arbitrary"),
        name="post",
    )(attnT, uT, vsT, gaT, gbT, x, wsT, bs, wpaT, wpbT, wout, gffn, wrT, br)


def _sc_kernel(name, body, out_type, n_items, row_shape, row_dtype):
    info = plsc.get_sparse_core_info()
    n_workers = info.num_cores * info.num_subcores
    per_worker = n_items // n_workers
    n_windows = per_worker // SC_WINDOW
    assert per_worker * n_workers == n_items and n_windows * SC_WINDOW == per_worker
    assert n_windows % 2 == 0

    def wrapped(*refs):
        wid = lax.axis_index("subcore") * info.num_cores + lax.axis_index("core")
        body(wid * per_worker, n_windows, *refs)

    return pl.kernel(
        wrapped,
        name=name,
        out_type=out_type,
        mesh=plsc.VectorSubcoreMesh(core_axis_name="core", subcore_axis_name="subcore"),
        scratch_types=[
            pltpu.VMEM((SC_WINDOW,), jnp.int32),
            pltpu.VMEM((SC_WINDOW,), jnp.int32),
            pltpu.VMEM((SC_WINDOW,) + row_shape, row_dtype),
            pltpu.VMEM((SC_WINDOW,) + row_shape, row_dtype),
            pltpu.SemaphoreType.DMA,
            pltpu.SemaphoreType.DMA,
        ],
    )


def _sc_gather_rows(table, idx):
    n, d, win = idx.shape[0], table.shape[1], SC_WINDOW

    def body(base, n_windows, table_hbm, idx_hbm, out_hbm, idx0, idx1, rows0, rows1, sem0, sem1):
        def window(c):
            return pl.ds(pl.multiple_of(base + c * win, win), win)

        def fetch(c, idx_v, rows_v, sem):
            pltpu.sync_copy(idx_hbm.at[window(c)], idx_v)
            pltpu.async_copy(table_hbm.at[idx_v], rows_v, sem)

        def drain(c, idx_v, rows_v, sem):
            pltpu.make_async_copy(table_hbm.at[idx_v], rows_v, sem).wait()
            pltpu.sync_copy(rows_v, out_hbm.at[window(c)])

        fetch(0, idx0, rows0, sem0)

        @pl.loop(0, n_windows, step=2)
        def _(c):
            fetch(c + 1, idx1, rows1, sem1)
            drain(c, idx0, rows0, sem0)

            @pl.when(c + 2 < n_windows)
            def _():
                fetch(c + 2, idx0, rows0, sem0)

            drain(c + 1, idx1, rows1, sem1)

    out_type = jax.ShapeDtypeStruct((n, d), table.dtype)
    return _sc_kernel("sc_gather_rows", body, out_type, n, (d,), table.dtype)(table, idx)


def _sc_scatter_rows(src, dest, n_rows):
    T, d = src.shape
    win = SC_WINDOW

    def body(base, n_windows, src_hbm, dest_hbm, out_hbm, idx0, idx1, rows0, rows1, sem0, sem1):
        def window(c, slot=0):
            return pl.ds(pl.multiple_of(slot * T + base + c * win, win), win)

        def fetch(c, rows_v, sem):
            pltpu.async_copy(src_hbm.at[window(c)], rows_v, sem)

        def push(c, rows_v, sem):
            pltpu.make_async_copy(src_hbm.at[window(c)], rows_v, sem).wait()
            for k in range(TOP_K):
                idx_v = idx0 if k % 2 == 0 else idx1
                pltpu.sync_copy(dest_hbm.at[window(c, k)], idx_v)
                pltpu.sync_copy(rows_v, out_hbm.at[idx_v])

        fetch(0, rows0, sem0)

        @pl.loop(0, n_windows, step=2)
        def _(c):
            fetch(c + 1, rows1, sem1)
            push(c, rows0, sem0)

            @pl.when(c + 2 < n_windows)
            def _():
                fetch(c + 2, rows0, sem0)

            push(c + 1, rows1, sem1)

    out_type = jax.ShapeDtypeStruct((n_rows, d), src.dtype)
    return _sc_kernel("sc_scatter_rows", body, out_type, T, (d,), src.dtype)(
        src, dest.reshape(TOP_K * T))


def _combine_dense_kernel(w_ref, x1_ref, gfin_ref, y_ref, o_ref):
    w = w_ref[...]
    moe = w[:, 0:1] * _unpack_row_halves(y_ref[0])
    for k in range(1, TOP_K):
        moe = moe + w[:, k:k + 1] * _unpack_row_halves(y_ref[k])
    x2 = x1_ref[...] + moe
    ms = jnp.mean(x2 * x2, axis=-1, keepdims=True)
    o_ref[...] = x2 * lax.rsqrt(ms + EPS) * gfin_ref[...]


def _combine_dense(wts, x1, gfin, y_tok):
    T = x1.shape[0]
    tt = COMBINE_TILE
    return pl.pallas_call(
        _combine_dense_kernel,
        grid=(T // tt,),
        in_specs=[
            pl.BlockSpec((tt, TOP_K), lambda i: (i, 0)),
            pl.BlockSpec((tt, D_MODEL), lambda i: (i, 0)),
            pl.BlockSpec((1, D_MODEL), lambda i: (0, 0)),
            pl.BlockSpec((TOP_K, tt, PACKED_W), lambda i: (0, i, 0)),
        ],
        out_specs=pl.BlockSpec((tt, D_MODEL), lambda i: (i, 0)),
        out_shape=jax.ShapeDtypeStruct((T, D_MODEL), jnp.float32),
        compiler_params=_compiler_params("arbitrary"),
        name="combine_dense",
    )(wts, x1, gfin, y_tok)


def _gate_up_prep_kernel(w_ref, o_ref):
    half = _GU_BLOCK // 2
    i = lax.broadcasted_iota(jnp.int32, (_GU_BLOCK, _GU_BLOCK), 0)
    j = lax.broadcasted_iota(jnp.int32, (_GU_BLOCK, _GU_BLOCK), 1)
    perm = (i == jnp.where(j < half, 2 * j, 2 * (j - half) + 1)).astype(jnp.bfloat16)
    for c in range(2 * D_FF // _GU_BLOCK):
        cols = slice(c * _GU_BLOCK, (c + 1) * _GU_BLOCK)
        blk = w_ref[0, :, cols].astype(jnp.bfloat16)
        o_ref[0, :, cols] = jnp.dot(blk, perm, preferred_element_type=jnp.float32).astype(jnp.bfloat16)


def _gate_up_prep(w_gate_up):
    spec = pl.BlockSpec((1, D_MODEL, 2 * D_FF), lambda e: (e, 0, 0))
    return pl.pallas_call(
        _gate_up_prep_kernel,
        grid=(N_EXPERTS,),
        in_specs=[spec],
        out_specs=spec,
        out_shape=jax.ShapeDtypeStruct(w_gate_up.shape, jnp.bfloat16),
        compiler_params=_compiler_params("arbitrary"),
        name="gate_up_prep",
    )(w_gate_up)


def _split_gate_up(gu):
    half = _GU_BLOCK // 2
    n = gu.shape[1] // _GU_BLOCK
    glu = jnp.concatenate([gu[:, c * _GU_BLOCK:c * _GU_BLOCK + half] for c in range(n)], axis=1)
    lin = jnp.concatenate([gu[:, c * _GU_BLOCK + half:(c + 1) * _GU_BLOCK] for c in range(n)], axis=1)
    return glu, lin


def _experts_kernel(te_ref, nv_ref, x_ref, wgu_ref, bgu_ref, wd_ref, bd_ref, y_ref):
    @pl.when(pl.program_id(0) < nv_ref[0])
    def _():
        x = _unpack_row_halves(x_ref[...]).astype(jnp.bfloat16)
        gu = jnp.dot(x, wgu_ref[0], preferred_element_type=jnp.float32) + bgu_ref[0]
        glu, lin = _split_gate_up(gu)
        glu = jnp.minimum(glu, SWIGLU_LIMIT)
        lin = jnp.clip(lin, -SWIGLU_LIMIT, SWIGLU_LIMIT)
        a = glu * jax.nn.sigmoid(SWIGLU_ALPHA * glu) * (lin + 1.0)
        y = jnp.dot(a.astype(jnp.bfloat16), wd_ref[0],
                    preferred_element_type=jnp.float32) + bd_ref[0]
        y_ref[...] = _pack_row_halves(y)


def _experts(tile_expert, n_valid, x_sorted, wgu, bgu, wd, bd):
    n_rows = x_sorted.shape[0]
    tr = EXPERT_TILE
    n_tiles = n_rows // tr

    def row_map(j, te, nv):
        return (jnp.minimum(j, nv[0] - 1), 0)

    def exp_map(j, te, nv):
        return (te[j], 0, 0)

    grid_spec = pltpu.PrefetchScalarGridSpec(
        num_scalar_prefetch=2,
        grid=(n_tiles,),
        in_specs=[
            pl.BlockSpec((tr, PACKED_W), row_map),
            pl.BlockSpec((1, D_MODEL, 2 * D_FF), exp_map),
            pl.BlockSpec((1, 1, 2 * D_FF), exp_map),
            pl.BlockSpec((1, D_FF, D_MODEL), exp_map),
            pl.BlockSpec((1, 1, D_MODEL), exp_map),
        ],
        out_specs=pl.BlockSpec((tr, PACKED_W), row_map),
    )
    return pl.pallas_call(
        _experts_kernel,
        grid_spec=grid_spec,
        out_shape=jax.ShapeDtypeStruct((n_rows, PACKED_W), jnp.int32),
        compiler_params=_compiler_params("arbitrary"),
        name="experts",
    )(tile_expert, n_valid, x_sorted, wgu, bgu, wd, bd)


def _rope_tables(S, gain, scale):
    t = jnp.arange(S, dtype=jnp.int32)
    r = (t // GRID_W).astype(jnp.float32)
    c = (t % GRID_W).astype(jnp.float32)
    inv = jnp.float32(ROPE_THETA) ** (
        -jnp.arange(0, ROPE_AXIS_DIM, 2, dtype=jnp.float32) / ROPE_AXIS_DIM)
    ang = jnp.concatenate([r[None, :] * inv[:, None], c[None, :] * inv[:, None]], axis=0)
    cos = jnp.repeat(jnp.cos(ang), 2, axis=0)
    sin = jnp.repeat(jnp.sin(ang), 2, axis=0)
    sign = jnp.where(jnp.arange(HEAD_DIM) % 2 == 0, -1.0, 1.0).astype(jnp.float32)
    g = gain.astype(jnp.float32) * scale
    g_swapped = g.reshape(HEAD_DIM // 2, 2)[:, ::-1].reshape(HEAD_DIM)
    return g[:, None] * cos, (g_swapped * sign)[:, None] * sin


def _split_hi_lo(w):
    hi = w.astype(jnp.bfloat16)
    lo = (w - hi.astype(jnp.float32)).astype(jnp.bfloat16)
    return jnp.concatenate([hi, lo], axis=0)


def _prepare_weights(norm_mix_g, w_in, q_norm_g, k_norm_g, sgu_norm_g, w_spatial, b_spatial,
                     w_proj_attn, w_proj_sgu, w_out, norm_ffn_g, w_router, b_router,
                     w_gate_up, b_gate_up, w_down, b_down, norm_final_g):
    bf = jnp.bfloat16
    l = 0
    return dict(
        gmix=norm_mix_g[l][None, :],
        w_inT=w_in[l].T.astype(bf),
        q_gain=q_norm_g[l], k_gain=k_norm_g[l],
        gs=jnp.broadcast_to(sgu_norm_g[l][:, None], (SGU_W, _LANES)),
        wsT=jnp.swapaxes(w_spatial[l], 1, 2).astype(bf),
        bs=b_spatial[l][:, None, :],
        wpaT=w_proj_attn[l].T.astype(bf),
        wpbT=w_proj_sgu[l].T.astype(bf),
        wout=w_out[l].astype(bf),
        gffn=norm_ffn_g[l][None, :],
        wrT=_split_hi_lo(w_router[l].T),
        br=b_router[l][:, None],
        wgu=_gate_up_prep(w_gate_up[l]),
        bgu=b_gate_up[l].reshape(N_EXPERTS, -1, _GU_BLOCK // 2, 2).transpose(0, 1, 3, 2)
        .reshape(N_EXPERTS, 1, 2 * D_FF),
        wd=w_down[l].astype(bf),
        bd=b_down[l][:, None, :],
        gfin=norm_final_g[None, :],
    )


def _trunk(x, w, rope):
    B, S, _ = x.shape
    T = B * S
    qT, k, vT, uT, vsT, gaT, gbT = _in_proj(x, w["gmix"], w["w_inT"], *rope, w["gs"])
    attnT = _attention(qT, k, vT)
    x1, h2, ids, wts, rank, counts = _post(
        attnT, uT, vsT, gaT, gbT, x, w["wsT"], w["bs"], w["wpaT"], w["wpbT"], w["wout"],
        w["gffn"], w["wrT"], w["br"])

    tr = EXPERT_TILE
    counts = counts[:, 0]
    padded = (counts + tr - 1) // tr * tr
    ends = jnp.cumsum(padded)
    starts = ends - padded
    n_tiles = (TOP_K * T) // tr + N_EXPERTS
    n_rows = n_tiles * tr
    dest = rank
    for e in range(N_EXPERTS):
        dest = dest + jnp.where(ids == e, starts[e], 0)
    tile_start = jnp.arange(n_tiles, dtype=jnp.int32) * tr
    tile_expert = jnp.minimum(
        jnp.sum((tile_start[:, None] >= ends[None, :]).astype(jnp.int32), axis=1), N_EXPERTS - 1)
    n_valid = (ends[-1] // tr).astype(jnp.int32)[None]

    x_sorted = _sc_scatter_rows(h2.reshape(T, PACKED_W), dest, n_rows)
    y_sorted = _experts(tile_expert, n_valid, x_sorted, w["wgu"], w["bgu"], w["wd"], w["bd"])
    y_tok = _sc_gather_rows(y_sorted, dest.reshape(TOP_K * T)).reshape(TOP_K, T, PACKED_W)
    out = _combine_dense(wts.T, x1.reshape(T, D_MODEL), w["gfin"], y_tok)
    return out.reshape(B, S, D_MODEL)


def kernel(x_prompt, x_sample, norm_mix_g, w_in, q_norm_g, k_norm_g, sgu_norm_g, w_spatial,
           b_spatial, w_proj_attn, w_proj_sgu, w_out, norm_ffn_g, w_router, b_router,
           w_gate_up, b_gate_up, w_down, b_down, norm_final_g):
    w = _prepare_weights(norm_mix_g, w_in, q_norm_g, k_norm_g, sgu_norm_g, w_spatial,
                         b_spatial, w_proj_attn, w_proj_sgu, w_out, norm_ffn_g, w_router,
                         b_router, w_gate_up, b_gate_up, w_down, b_down, norm_final_g)
    s_max = max(x_prompt.shape[1], x_sample.shape[1])
    rope = (*_rope_tables(s_max, w["q_gain"], math.log2(math.e) / math.sqrt(HEAD_DIM)),
            *_rope_tables(s_max, w["k_gain"], 1.0))
    return (_trunk(x_prompt, w, rope), _trunk(x_sample, w, rope))
```

```python
import math

import jax
import jax.numpy as jnp
from jax import lax
from jax.experimental import pallas as pl
from jax.experimental.pallas import tpu as pltpu
from jax.experimental.pallas import tpu_sc as plsc

D_MODEL = 1024
GRID_W = 64
N_HEADS = 8
N_KV_HEADS = 2
HEAD_DIM = 64
GQA_GROUP = N_HEADS // N_KV_HEADS
Q_W = N_HEADS * HEAD_DIM
KV_W = N_KV_HEADS * HEAD_DIM
ROPE_AXIS_DIM = HEAD_DIM // 2
ROPE_THETA = 10000.0
SGU_GROUPS = 8
SGU_W = D_MODEL // 2
SGU_GROUP_DIM = SGU_W // SGU_GROUPS
CHUNK = 128
IN_W = Q_W + 2 * KV_W + 2 * SGU_W + 2 * D_MODEL
N_EXPERTS = 32
TOP_K = 4
D_FF = D_MODEL
SWIGLU_LIMIT = 7.0
SWIGLU_ALPHA = 1.702
EPS = 1e-6

_Q0, _K0, _V0 = 0, Q_W, Q_W + KV_W
_U0 = Q_W + 2 * KV_W
_VS0 = _U0 + SGU_W
_GA0 = _VS0 + SGU_W
_GB0 = _GA0 + D_MODEL

TOKEN_TILE = 512
_IN_PROJ_SUBTILES = 2
Q_TILE = 512
EXPERT_TILE = 512
COMBINE_TILE = 512
PACKED_W = D_MODEL // 2
SC_WINDOW = 64

_LANES = 128
_BF16_SUBLANES = 16
_SUM_ROWS = _BF16_SUBLANES
_GU_BLOCK = 256
_NEG_BIG = -1e30
_HIGH_HALF_MASK = 0xFFFF0000
_MAX_LAGGED_EXPONENT = 80.0
_KEY_TILES_PER_TRIP = 4
_MIB = 1024 * 1024
_VMEM_LIMIT_MIB = 40

_NT_DIMS = (((1,), (1,)), ((), ()))
_TN_DIMS = (((0,), (0,)), ((), ()))


def _compiler_params(*semantics):
    return pltpu.CompilerParams(
        dimension_semantics=semantics, vmem_limit_bytes=_VMEM_LIMIT_MIB * _MIB)


def _swap_adjacent_rows(x):
    n = x.shape[0]
    row = lax.broadcasted_iota(jnp.int32, x.shape, 0)
    nxt = pltpu.roll(x, n - 1, 0)
    prv = pltpu.roll(x, 1, 0)
    return jnp.where((row & 1) == 0, nxt, prv)


def _head_norm_rope(z, tab_a, tab_b, n_heads):
    tm = z.shape[1]
    z3 = z.reshape(n_heads, HEAD_DIM, tm)
    ms = jnp.mean(z3 * z3, axis=1, keepdims=True)
    r = lax.rsqrt(ms + EPS)
    zs = _swap_adjacent_rows(z).reshape(n_heads, HEAD_DIM, tm)
    out = (z3 * tab_a[None] + zs * tab_b[None]) * r
    return out.reshape(n_heads * HEAD_DIM, tm)


def _gelu(x):
    return 0.5 * x * (1.0 + lax.erf(x * (1.0 / math.sqrt(2.0))))


def _pack_row_halves(x):
    half = x.shape[1] // 2

    def bf16_bits(v):
        return lax.bitcast_convert_type(v.astype(jnp.bfloat16).astype(jnp.float32), jnp.uint32)

    word = (bf16_bits(x[:, :half]) >> 16) | (bf16_bits(x[:, half:]) & jnp.uint32(_HIGH_HALF_MASK))
    return lax.bitcast_convert_type(word, jnp.int32)


def _unpack_row_halves(p):
    word = lax.bitcast_convert_type(p, jnp.uint32)
    lo = lax.bitcast_convert_type(word << 16, jnp.float32)
    hi = lax.bitcast_convert_type(word & jnp.uint32(_HIGH_HALF_MASK), jnp.float32)
    return jnp.concatenate([lo, hi], axis=1)


def _tile_lanes(x, reps):
    return jnp.concatenate([x] * reps, axis=1) if reps > 1 else x


def _in_proj_kernel(x_ref, gmix_ref, w_ref, qa_ref, qb_ref, ka_ref, kb_ref, gs_ref,
                    qT_ref, k_ref, vT_ref, uT_ref, vsT_ref, gaT_ref, gbT_ref):
    tm = x_ref.shape[1]
    sub = tm // _IN_PROJ_SUBTILES
    for i in range(_IN_PROJ_SUBTILES):
        tok = slice(i * sub, (i + 1) * sub)
        x = x_ref[0, tok, :]
        ms = jnp.mean(x * x, axis=-1, keepdims=True)
        h = (x * lax.rsqrt(ms + EPS) * gmix_ref[...]).astype(jnp.bfloat16)

        def proj(r0, rows, h=h):
            return lax.dot_general(w_ref[r0:r0 + rows, :], h, _NT_DIMS,
                                   preferred_element_type=jnp.float32)

        zq = proj(_Q0, Q_W)
        qT_ref[0, :, tok] = _head_norm_rope(
            zq, qa_ref[:, tok], qb_ref[:, tok], N_HEADS).astype(jnp.bfloat16)

        zkv = proj(_K0, 2 * KV_W)
        kT = _head_norm_rope(zkv[:KV_W], ka_ref[:, tok], kb_ref[:, tok], N_KV_HEADS)
        k_ref[0, tok, :] = kT.T.astype(jnp.bfloat16)
        vT_ref[0, 0, :, tok] = zkv[KV_W:].astype(jnp.bfloat16)

        uT_ref[0, :, tok] = _gelu(proj(_U0, SGU_W)).astype(jnp.bfloat16)

        vs = _gelu(proj(_VS0, SGU_W))
        vms = jnp.mean(vs * vs, axis=0, keepdims=True)
        gs = _tile_lanes(gs_ref[...], sub // _LANES)
        vsT_ref[0, :, tok] = (vs * lax.rsqrt(vms + EPS) * gs).astype(jnp.bfloat16)

        gaT_ref[0, :, tok] = jax.nn.sigmoid(proj(_GA0, D_MODEL)).astype(jnp.bfloat16)
        gbT_ref[0, :, tok] = jax.nn.sigmoid(proj(_GB0, D_MODEL)).astype(jnp.bfloat16)


def _in_proj(x, gmix, w_inT, qa, qb, ka, kb, gs):
    B, S, _ = x.shape
    tm = TOKEN_TILE
    nt = S // tm
    bf = jnp.bfloat16
    const2 = lambda b, i: (0, 0)
    tab = pl.BlockSpec((HEAD_DIM, tm), lambda b, i: (0, i))
    fm = lambda rows: pl.BlockSpec((1, rows, tm), lambda b, i: (b, 0, i))
    return pl.pallas_call(
        _in_proj_kernel,
        grid=(B, nt),
        in_specs=[
            pl.BlockSpec((1, tm, D_MODEL), lambda b, i: (b, i, 0)),
            pl.BlockSpec((1, D_MODEL), const2),
            pl.BlockSpec((IN_W, D_MODEL), const2),
            tab, tab, tab, tab,
            pl.BlockSpec((SGU_W, _LANES), const2),
        ],
        out_specs=[
            fm(Q_W),
            pl.BlockSpec((1, tm, KV_W), lambda b, i: (b, i, 0)),
            pl.BlockSpec((1, 1, KV_W, tm), lambda b, i: (b, i, 0, 0)),
            fm(SGU_W), fm(SGU_W), fm(D_MODEL), fm(D_MODEL),
        ],
        out_shape=[
            jax.ShapeDtypeStruct((B, Q_W, S), bf),
            jax.ShapeDtypeStruct((B, S, KV_W), bf),
            jax.ShapeDtypeStruct((B, nt, KV_W, tm), bf),
            jax.ShapeDtypeStruct((B, SGU_W, S), bf),
            jax.ShapeDtypeStruct((B, SGU_W, S), bf),
            jax.ShapeDtypeStruct((B, D_MODEL, S), bf),
            jax.ShapeDtypeStruct((B, D_MODEL, S), bf),
        ],
        compiler_params=_compiler_params("arbitrary", "arbitrary"),
        name="in_proj",
    )(x, gmix, w_inT, qa, qb, ka, kb, gs)


def _attention_kernel(qT_ref, k_ref, vT_ref, o_ref, m_ref, gap_ref, acc_ref):
    tq = qT_ref.shape[2]
    n_kt = vT_ref.shape[1]
    tk = vT_ref.shape[3]
    nq = GQA_GROUP * tq
    ones_rows = (lax.broadcasted_iota(jnp.int32, (_SUM_ROWS, tk), 0) == 0).astype(jnp.bfloat16)

    zeros = jnp.zeros((HEAD_DIM, nq), jnp.bfloat16)
    qp = []
    for kv in range(N_KV_HEADS):
        heads = [qT_ref[0, (kv * GQA_GROUP + g) * HEAD_DIM:(kv * GQA_GROUP + g + 1) * HEAD_DIM, :]
                 for g in range(GQA_GROUP)]
        qk = jnp.concatenate(heads, axis=1)
        qp.append(jnp.concatenate([qk, zeros] if kv == 0 else [zeros, qk], axis=0))

    def key_tile(ki):
        return k_ref[0, pl.ds(pl.multiple_of(ki * tk, tk), tk), :]

    def value_rows(ki, kv):
        vt = vT_ref[0, ki]
        return jnp.concatenate([vt[kv * HEAD_DIM:(kv + 1) * HEAD_DIM, :], ones_rows], axis=0)

    acc_ref[...] = jnp.zeros(acc_ref.shape, jnp.float32)
    gap_ref[...] = jnp.zeros(gap_ref.shape, jnp.float32)
    def as_reference(m):
        return m.astype(jnp.bfloat16).astype(jnp.float32)

    first_keys = k_ref[0, 0:_BF16_SUBLANES, :]
    for kv in range(N_KV_HEADS):
        s0 = jnp.dot(first_keys, qp[kv], preferred_element_type=jnp.float32)
        m_ref[kv] = as_reference(jnp.max(s0, axis=0, keepdims=True))

    ones_cols = (lax.broadcasted_iota(jnp.int32, (tk, _LANES), 1) == 0).astype(jnp.bfloat16)
    ref_row = lax.broadcasted_iota(jnp.int32, (_BF16_SUBLANES, nq), 0) == 0
    pad_rows = jnp.zeros((_LANES - _BF16_SUBLANES, nq), jnp.bfloat16)

    def fast_tile(ki):
        kt = jnp.concatenate([key_tile(ki), ones_cols], axis=1)
        for kv in range(N_KV_HEADS):
            m_old = m_ref[kv]
            neg_ref = jnp.where(ref_row, -m_old, 0.0).astype(jnp.bfloat16)
            q_aug = jnp.concatenate([qp[kv], neg_ref, pad_rows], axis=0)
            s = jnp.dot(kt, q_aug, preferred_element_type=jnp.float32)
            p = jnp.exp2(s).astype(jnp.bfloat16)
            over = jnp.max(s, axis=0, keepdims=True)
            pv = jnp.dot(value_rows(ki, kv), p, preferred_element_type=jnp.float32)
            m_new = as_reference(m_old + jnp.maximum(over, 0.0))
            acc_ref[kv] = (acc_ref[kv] + pv) * jnp.exp2(m_old - m_new)
            m_ref[kv] = m_new
            gap_ref[kv] = jnp.maximum(gap_ref[kv], over)

    per_trip = math.gcd(n_kt, _KEY_TILES_PER_TRIP)

    def fast_group(j, carry):
        for t in range(per_trip):
            fast_tile(per_trip * j + t)
        return carry

    lax.fori_loop(0, n_kt // per_trip, fast_group, 0)

    @pl.when(jnp.max(gap_ref[...]) > _MAX_LAGGED_EXPONENT)
    def _():
        m_ref[...] = jnp.full(m_ref.shape, _NEG_BIG, jnp.float32)
        acc_ref[...] = jnp.zeros(acc_ref.shape, jnp.float32)

        def robust_tile(ki, carry):
            kt = key_tile(ki)
            for kv in range(N_KV_HEADS):
                s = jnp.dot(kt, qp[kv], preferred_element_type=jnp.float32)
                m_old = m_ref[kv]
                m_new = jnp.maximum(m_old, jnp.max(s, axis=0, keepdims=True))
                p = jnp.exp2(s - m_new).astype(jnp.bfloat16)
                pv = jnp.dot(value_rows(ki, kv), p, preferred_element_type=jnp.float32)
                acc_ref[kv] = jnp.exp2(m_old - m_new) * acc_ref[kv] + pv
                m_ref[kv] = m_new
            return carry

        lax.fori_loop(0, n_kt, robust_tile, 0)

    for kv in range(N_KV_HEADS):
        acc = acc_ref[kv]
        o = acc[:HEAD_DIM] / acc[HEAD_DIM:HEAD_DIM + 1]
        for g in range(GQA_GROUP):
            r0 = (kv * GQA_GROUP + g) * HEAD_DIM
            o_ref[0, r0:r0 + HEAD_DIM, :] = o[:, g * tq:(g + 1) * tq].astype(jnp.bfloat16)


def _attention(qT, k, vT):
    B, _, S = qT.shape
    tq = Q_TILE
    nq = GQA_GROUP * tq
    n_kt, tk = vT.shape[1], vT.shape[3]
    return pl.pallas_call(
        _attention_kernel,
        grid=(B, S // tq),
        in_specs=[
            pl.BlockSpec((1, Q_W, tq), lambda b, i: (b, 0, i)),
            pl.BlockSpec((1, S, KV_W), lambda b, i: (b, 0, 0)),
            pl.BlockSpec((1, n_kt, KV_W, tk), lambda b, i: (b, 0, 0, 0)),
        ],
        out_specs=pl.BlockSpec((1, Q_W, tq), lambda b, i: (b, 0, i)),
        out_shape=jax.ShapeDtypeStruct((B, Q_W, S), jnp.bfloat16),
        scratch_shapes=[
            pltpu.VMEM((N_KV_HEADS, 1, nq), jnp.float32),
            pltpu.VMEM((N_KV_HEADS, 1, nq), jnp.float32),
            pltpu.VMEM((N_KV_HEADS, HEAD_DIM + _SUM_ROWS, nq), jnp.float32),
        ],
        compiler_params=_compiler_params("arbitrary", "arbitrary"),
        name="attention",
    )(qT, k, vT)


def _post_kernel(attnT_ref, uT_ref, vsT_ref, gaT_ref, gbT_ref, x_ref,
                 wsT_ref, bs_ref, wpaT_ref, wpbT_ref, wout_ref, gffn_ref, wrT_ref, br_ref,
                 x1_ref, h2_ref, ids_ref, wts_ref, rank_ref, cnt_ref, carry_ref):
    tm = x_ref.shape[1]
    n_chunks = tm // CHUNK
    first = jnp.logical_and(pl.program_id(0) == 0, pl.program_id(1) == 0)

    @pl.when(first)
    def _():
        carry_ref[...] = jnp.zeros(carry_ref.shape, jnp.float32)

    gate_rows = []
    for g in range(SGU_GROUPS):
        r0 = g * SGU_GROUP_DIM
        vs_g = vsT_ref[0, r0:r0 + SGU_GROUP_DIM, :]
        lhs = jnp.concatenate(
            [vs_g[:, c * CHUNK:(c + 1) * CHUNK] for c in range(n_chunks)], axis=0)
        mixed = jnp.dot(lhs, wsT_ref[g], preferred_element_type=jnp.float32)
        mixed = mixed + bs_ref[g]
        mixedT = jnp.concatenate(
            [mixed[c * SGU_GROUP_DIM:(c + 1) * SGU_GROUP_DIM] for c in range(n_chunks)], axis=1)
        u_g = uT_ref[0, r0:r0 + SGU_GROUP_DIM, :].astype(jnp.float32)
        gate_rows.append((u_g * mixedT).astype(jnp.bfloat16))
    gateT = jnp.concatenate(gate_rows, axis=0)

    paT = jnp.dot(wpaT_ref[...], attnT_ref[0], preferred_element_type=jnp.float32)
    pbT = jnp.dot(wpbT_ref[...], gateT, preferred_element_type=jnp.float32)
    mT = (gaT_ref[0].astype(jnp.float32) * paT
          + gbT_ref[0].astype(jnp.float32) * pbT).astype(jnp.bfloat16)
    y = lax.dot_general(mT, wout_ref[...], _TN_DIMS, preferred_element_type=jnp.float32)
    x1 = x_ref[0] + y
    x1_ref[0] = x1

    ms = jnp.mean(x1 * x1, axis=-1, keepdims=True)
    h2 = x1 * lax.rsqrt(ms + EPS) * gffn_ref[...]
    h2_ref[0] = _pack_row_halves(h2)

    h_hi = h2.astype(jnp.bfloat16)
    h_lo = (h2 - h_hi.astype(jnp.float32)).astype(jnp.bfloat16)
    by_hi = lax.dot_general(wrT_ref[...], h_hi, _NT_DIMS, preferred_element_type=jnp.float32)
    by_lo = lax.dot_general(wrT_ref[:N_EXPERTS, :], h_lo, _NT_DIMS,
                            preferred_element_type=jnp.float32)
    logits = by_hi[:N_EXPERTS] + by_hi[N_EXPERTS:] + by_lo + br_ref[...]
    eidx = lax.broadcasted_iota(jnp.int32, logits.shape, 0).astype(jnp.float32)
    work = logits
    vals, ids, sels = [], [], []
    for _ in range(TOP_K):
        mx = jnp.max(work, axis=0, keepdims=True)
        idx = jnp.min(jnp.where(work == mx, eidx, float(N_EXPERTS)), axis=0, keepdims=True)
        sel = eidx == idx
        vals.append(mx)
        ids.append(idx)
        sels.append(sel)
        work = jnp.where(sel, -jnp.inf, work)
    exps = [jnp.exp(v - vals[0]) for v in vals]
    denom = exps[0] + exps[1] + exps[2] + exps[3]
    ids_ref[...] = jnp.concatenate(ids, axis=0).astype(jnp.int32)
    wts_ref[...] = jnp.concatenate([e / denom for e in exps], axis=0)

    onehot = [s.astype(jnp.float32) for s in sels]
    hits = onehot[0] + onehot[1] + onehot[2] + onehot[3]
    ti = lax.broadcasted_iota(jnp.int32, (tm, tm), 0)
    tj = lax.broadcasted_iota(jnp.int32, (tm, tm), 1)
    upper = (ti < tj).astype(jnp.bfloat16)
    prefix = jnp.dot(hits.astype(jnp.bfloat16), upper, preferred_element_type=jnp.float32)
    base = prefix + carry_ref[...]
    ranks = [jnp.sum(oh * base, axis=0, keepdims=True) for oh in onehot]
    rank_ref[...] = jnp.concatenate(ranks, axis=0).astype(jnp.int32)
    carry_ref[...] = carry_ref[...] + jnp.sum(hits, axis=1, keepdims=True)
    cnt_ref[...] = carry_ref[...].astype(jnp.int32)


def _post(attnT, uT, vsT, gaT, gbT, x, wsT, bs, wpaT, wpbT, wout, gffn, wrT, br):
    B, S, _ = x.shape
    tm = TOKEN_TILE
    nt = S // tm
    T = B * S
    fm = lambda rows: pl.BlockSpec((1, rows, tm), lambda b, i: (b, 0, i))
    rowm = pl.BlockSpec((1, tm, D_MODEL), lambda b, i: (b, i, 0))
    c2 = lambda b, i: (0, 0)
    c3 = lambda b, i: (0, 0, 0)
    tokT = pl.BlockSpec((TOP_K, tm), lambda b, i: (0, b * nt + i))
    return pl.pallas_call(
        _post_kernel,
        grid=(B, nt),
        in_specs=[
            fm(Q_W), fm(SGU_W), fm(SGU_W), fm(D_MODEL), fm(D_MODEL), rowm,
            pl.BlockSpec((SGU_GROUPS, CHUNK, CHUNK), c3),
            pl.BlockSpec((SGU_GROUPS, 1, CHUNK), c3),
            pl.BlockSpec((D_MODEL, Q_W), c2),
            pl.BlockSpec((D_MODEL, SGU_W), c2),
            pl.BlockSpec((D_MODEL, D_MODEL), c2),
            pl.BlockSpec((1, D_MODEL), c2),
            pl.BlockSpec((2 * N_EXPERTS, D_MODEL), c2),
            pl.BlockSpec((N_EXPERTS, 1), c2),
        ],
        out_specs=[rowm, pl.BlockSpec((1, tm, PACKED_W), lambda b, i: (b, i, 0)),
                   tokT, tokT, tokT, pl.BlockSpec((N_EXPERTS, 1), c2)],
        out_shape=[
            jax.ShapeDtypeStruct((B, S, D_MODEL), jnp.float32),
            jax.ShapeDtypeStruct((B, S, PACKED_W), jnp.int32),
            jax.ShapeDtypeStruct((TOP_K, T), jnp.int32),
            jax.ShapeDtypeStruct((TOP_K, T), jnp.float32),
            jax.ShapeDtypeStruct((TOP_K, T), jnp.int32),
            jax.ShapeDtypeStruct((N_EXPERTS, 1), jnp.int32),
        ],
        scratch_shapes=[pltpu.VMEM((N_EXPERTS, 1), jnp.float32)],
        compiler_params=_compiler_params("arbitrary", "arbitrary"),
        name="post",
    )(attnT, uT, vsT, gaT, gbT, x, wsT, bs, wpaT, wpbT, wout, gffn, wrT, br)


def _sc_kernel(name, body, out_type, n_items, row_shape, row_dtype):
    info = plsc.get_sparse_core_info()
    n_workers = info.num_cores * info.num_subcores
    per_worker = n_items // n_workers
    n_windows = per_worker // SC_WINDOW
    assert per_worker * n_workers == n_items and n_windows * SC_WINDOW == per_worker
    assert n_windows % 2 == 0

    def wrapped(*refs):
        wid = lax.axis_index("subcore") * info.num_cores + lax.axis_index("core")
        body(wid * per_worker, n_windows, *refs)

    return pl.kernel(
        wrapped,
        name=name,
        out_type=out_type,
        mesh=plsc.VectorSubcoreMesh(core_axis_name="core", subcore_axis_name="subcore"),
        scratch_types=[
            pltpu.VMEM((SC_WINDOW,), jnp.int32),
            pltpu.VMEM((SC_WINDOW,), jnp.int32),
            pltpu.VMEM((SC_WINDOW,) + row_shape, row_dtype),
            pltpu.VMEM((SC_WINDOW,) + row_shape, row_dtype),
            pltpu.SemaphoreType.DMA,
            pltpu.SemaphoreType.DMA,
        ],
    )


def _sc_gather_rows(table, idx):
    n, d, win = idx.shape[0], table.shape[1], SC_WINDOW

    def body(base, n_windows, table_hbm, idx_hbm, out_hbm, idx0, idx1, rows0, rows1, sem0, sem1):
        def window(c):
            return pl.ds(pl.multiple_of(base + c * win, win), win)

        def fetch(c, idx_v, rows_v, sem):
            pltpu.sync_copy(idx_hbm.at[window(c)], idx_v)
            pltpu.async_copy(table_hbm.at[idx_v], rows_v, sem)

        def drain(c, idx_v, rows_v, sem):
            pltpu.make_async_copy(table_hbm.at[idx_v], rows_v, sem).wait()
            pltpu.sync_copy(rows_v, out_hbm.at[window(c)])

        fetch(0, idx0, rows0, sem0)

        @pl.loop(0, n_windows, step=2)
        def _(c):
            fetch(c + 1, idx1, rows1, sem1)
            drain(c, idx0, rows0, sem0)

            @pl.when(c + 2 < n_windows)
            def _():
                fetch(c + 2, idx0, rows0, sem0)

            drain(c + 1, idx1, rows1, sem1)

    out_type = jax.ShapeDtypeStruct((n, d), table.dtype)
    return _sc_kernel("sc_gather_rows", body, out_type, n, (d,), table.dtype)(table, idx)


def _sc_scatter_rows(src, dest, n_rows):
    T, d = src.shape
    win = SC_WINDOW

    def body(base, n_windows, src_hbm, dest_hbm, out_hbm, idx0, idx1, rows0, rows1, sem0, sem1):
        def window(c, slot=0):
            return pl.ds(pl.multiple_of(slot * T + base + c * win, win), win)

        def fetch(c, rows_v, sem):
            pltpu.async_copy(src_hbm.at[window(c)], rows_v, sem)

        def push(c, rows_v, sem):
            pltpu.make_async_copy(src_hbm.at[window(c)], rows_v, sem).wait()
            for k in range(TOP_K):
                idx_v = idx0 if k % 2 == 0 else idx1
                pltpu.sync_copy(dest_hbm.at[window(c, k)], idx_v)
                pltpu.sync_copy(rows_v, out_hbm.at[idx_v])

        fetch(0, rows0, sem0)

        @pl.loop(0, n_windows, step=2)
        def _(c):
            fetch(c + 1, rows1, sem1)
            push(c, rows0, sem0)

            @pl.when(c + 2 < n_windows)
            def _():
                fetch(c + 2, rows0, sem0)

            push(c + 1, rows1, sem1)

    out_type = jax.ShapeDtypeStruct((n_rows, d), src.dtype)
    return _sc_kernel("sc_scatter_rows", body, out_type, T, (d,), src.dtype)(
        src, dest.reshape(TOP_K * T))


def _combine_dense_kernel(w_ref, x1_ref, gfin_ref, y_ref, o_ref):
    w = w_ref[...]
    moe = w[:, 0:1] * _unpack_row_halves(y_ref[0])
    for k in range(1, TOP_K):
        moe = moe + w[:, k:k + 1] * _unpack_row_halves(y_ref[k])
    x2 = x1_ref[...] + moe
    ms = jnp.mean(x2 * x2, axis=-1, keepdims=True)
    o_ref[...] = x2 * lax.rsqrt(ms + EPS) * gfin_ref[...]


def _combine_dense(wts, x1, gfin, y_tok):
    T = x1.shape[0]
    tt = COMBINE_TILE
    return pl.pallas_call(
        _combine_dense_kernel,
        grid=(T // tt,),
        in_specs=[
            pl.BlockSpec((tt, TOP_K), lambda i: (i, 0)),
            pl.BlockSpec((tt, D_MODEL), lambda i: (i, 0)),
            pl.BlockSpec((1, D_MODEL), lambda i: (0, 0)),
            pl.BlockSpec((TOP_K, tt, PACKED_W), lambda i: (0, i, 0)),
        ],
        out_specs=pl.BlockSpec((tt, D_MODEL), lambda i: (i, 0)),
        out_shape=jax.ShapeDtypeStruct((T, D_MODEL), jnp.float32),
        compiler_params=_compiler_params("arbitrary"),
        name="combine_dense",
    )(wts, x1, gfin, y_tok)


def _gate_up_prep_kernel(w_ref, o_ref):
    half = _GU_BLOCK // 2
    i = lax.broadcasted_iota(jnp.int32, (_GU_BLOCK, _GU_BLOCK), 0)
    j = lax.broadcasted_iota(jnp.int32, (_GU_BLOCK, _GU_BLOCK), 1)
    perm = (i == jnp.where(j < half, 2 * j, 2 * (j - half) + 1)).astype(jnp.bfloat16)
    for c in range(2 * D_FF // _GU_BLOCK):
        cols = slice(c * _GU_BLOCK, (c + 1) * _GU_BLOCK)
        blk = w_ref[0, :, cols].astype(jnp.bfloat16)
        o_ref[0, :, cols] = jnp.dot(blk, perm, preferred_element_type=jnp.float32).astype(jnp.bfloat16)


def _gate_up_prep(w_gate_up):
    spec = pl.BlockSpec((1, D_MODEL, 2 * D_FF), lambda e: (e, 0, 0))
    return pl.pallas_call(
        _gate_up_prep_kernel,
        grid=(N_EXPERTS,),
        in_specs=[spec],
        out_specs=spec,
        out_shape=jax.ShapeDtypeStruct(w_gate_up.shape, jnp.bfloat16),
        compiler_params=_compiler_params("arbitrary"),
        name="gate_up_prep",
    )(w_gate_up)


def _split_gate_up(gu):
    half = _GU_BLOCK // 2
    n = gu.shape[1] // _GU_BLOCK
    glu = jnp.concatenate([gu[:, c * _GU_BLOCK:c * _GU_BLOCK + half] for c in range(n)], axis=1)
    lin = jnp.concatenate([gu[:, c * _GU_BLOCK + half:(c + 1) * _GU_BLOCK] for c in range(n)], axis=1)
    return glu, lin


def _experts_kernel(te_ref, nv_ref, x_ref, wgu_ref, bgu_ref, wd_ref, bd_ref, y_ref):
    @pl.when(pl.program_id(0) < nv_ref[0])
    def _():
        x = _unpack_row_halves(x_ref[...]).astype(jnp.bfloat16)
        gu = jnp.dot(x, wgu_ref[0], preferred_element_type=jnp.float32) + bgu_ref[0]
        glu, lin = _split_gate_up(gu)
        glu = jnp.minimum(glu, SWIGLU_LIMIT)
        lin = jnp.clip(lin, -SWIGLU_LIMIT, SWIGLU_LIMIT)
        a = glu * jax.nn.sigmoid(SWIGLU_ALPHA * glu) * (lin + 1.0)
        y = jnp.dot(a.astype(jnp.bfloat16), wd_ref[0],
                    preferred_element_type=jnp.float32) + bd_ref[0]
        y_ref[...] = _pack_row_halves(y)


def _experts(tile_expert, n_valid, x_sorted, wgu, bgu, wd, bd):
    n_rows = x_sorted.shape[0]
    tr = EXPERT_TILE
    n_tiles = n_rows // tr

    def row_map(j, te, nv):
        return (jnp.minimum(j, nv[0] - 1), 0)

    def exp_map(j, te, nv):
        return (te[j], 0, 0)

    grid_spec = pltpu.PrefetchScalarGridSpec(
        num_scalar_prefetch=2,
        grid=(n_tiles,),
        in_specs=[
            pl.BlockSpec((tr, PACKED_W), row_map),
            pl.BlockSpec((1, D_MODEL, 2 * D_FF), exp_map),
            pl.BlockSpec((1, 1, 2 * D_FF), exp_map),
            pl.BlockSpec((1, D_FF, D_MODEL), exp_map),
            pl.BlockSpec((1, 1, D_MODEL), exp_map),
        ],
        out_specs=pl.BlockSpec((tr, PACKED_W), row_map),
    )
    return pl.pallas_call(
        _experts_kernel,
        grid_spec=grid_spec,
        out_shape=jax.ShapeDtypeStruct((n_rows, PACKED_W), jnp.int32),
        compiler_params=_compiler_params("arbitrary"),
        name="experts",
    )(tile_expert, n_valid, x_sorted, wgu, bgu, wd, bd)


def _rope_tables(S, gain, scale):
    t = jnp.arange(S, dtype=jnp.int32)
    r = (t // GRID_W).astype(jnp.float32)
    c = (t % GRID_W).astype(jnp.float32)
    inv = jnp.float32(ROPE_THETA) ** (
        -jnp.arange(0, ROPE_AXIS_DIM, 2, dtype=jnp.float32) / ROPE_AXIS_DIM)
    ang = jnp.concatenate([r[None, :] * inv[:, None], c[None, :] * inv[:, None]], axis=0)
    cos = jnp.repeat(jnp.cos(ang), 2, axis=0)
    sin = jnp.repeat(jnp.sin(ang), 2, axis=0)
    sign = jnp.where(jnp.arange(HEAD_DIM) % 2 == 0, -1.0, 1.0).astype(jnp.float32)
    g = gain.astype(jnp.float32) * scale
    g_swapped = g.reshape(HEAD_DIM // 2, 2)[:, ::-1].reshape(HEAD_DIM)
    return g[:, None] * cos, (g_swapped * sign)[:, None] * sin


def _split_hi_lo(w):
    hi = w.astype(jnp.bfloat16)
    lo = (w - hi.astype(jnp.float32)).astype(jnp.bfloat16)
    return jnp.concatenate([hi, lo], axis=0)


def _prepare_weights(norm_mix_g, w_in, q_norm_g, k_norm_g, sgu_norm_g, w_spatial, b_spatial,
                     w_proj_attn, w_proj_sgu, w_out, norm_ffn_g, w_router, b_router,
                     w_gate_up, b_gate_up, w_down, b_down, norm_final_g):
    bf = jnp.bfloat16
    l = 0
    return dict(
        gmix=norm_mix_g[l][None, :],
        w_inT=w_in[l].T.astype(bf),
        q_gain=q_norm_g[l], k_gain=k_norm_g[l],
        gs=jnp.broadcast_to(sgu_norm_g[l][:, None], (SGU_W, _LANES)),
        wsT=jnp.swapaxes(w_spatial[l], 1, 2).astype(bf),
        bs=b_spatial[l][:, None, :],
        wpaT=w_proj_attn[l].T.astype(bf),
        wpbT=w_proj_sgu[l].T.astype(bf),
        wout=w_out[l].astype(bf),
        gffn=norm_ffn_g[l][None, :],
        wrT=_split_hi_lo(w_router[l].T),
        br=b_router[l][:, None],
        wgu=_gate_up_prep(w_gate_up[l]),
        bgu=b_gate_up[l].reshape(N_EXPERTS, -1, _GU_BLOCK // 2, 2).transpose(0, 1, 3, 2)
        .reshape(N_EXPERTS, 1, 2 * D_FF),
        wd=w_down[l].astype(bf),
        bd=b_down[l][:, None, :],
        gfin=norm_final_g[None, :],
    )


def _trunk(x, w, rope):
    B, S, _ = x.shape
    T = B * S
    qT, k, vT, uT, vsT, gaT, gbT = _in_proj(x, w["gmix"], w["w_inT"], *rope, w["gs"])
    attnT = _attention(qT, k, vT)
    x1, h2, ids, wts, rank, counts = _post(
        attnT, uT, vsT, gaT, gbT, x, w["wsT"], w["bs"], w["wpaT"], w["wpbT"], w["wout"],
        w["gffn"], w["wrT"], w["br"])

    tr = EXPERT_TILE
    counts = counts[:, 0]
    padded = (counts + tr - 1) // tr * tr
    ends = jnp.cumsum(padded)
    starts = ends - padded
    n_tiles = (TOP_K * T) // tr + N_EXPERTS
    n_rows = n_tiles * tr
    dest = rank
    for e in range(N_EXPERTS):
        dest = dest + jnp.where(ids == e, starts[e], 0)
    tile_start = jnp.arange(n_tiles, dtype=jnp.int32) * tr
    tile_expert = jnp.minimum(
        jnp.sum((tile_start[:, None] >= ends[None, :]).astype(jnp.int32), axis=1), N_EXPERTS - 1)
    n_valid = (ends[-1] // tr).astype(jnp.int32)[None]

    x_sorted = _sc_scatter_rows(h2.reshape(T, PACKED_W), dest, n_rows)
    y_sorted = _experts(tile_expert, n_valid, x_sorted, w["wgu"], w["bgu"], w["wd"], w["bd"])
    y_tok = _sc_gather_rows(y_sorted, dest.reshape(TOP_K * T)).reshape(TOP_K, T, PACKED_W)
    out = _combine_dense(wts.T, x1.reshape(T, D_MODEL), w["gfin"], y_tok)
    return out.reshape(B, S, D_MODEL)


def kernel(x_prompt, x_sample, norm_mix_g, w_in, q_norm_g, k_norm_g, sgu_norm_g, w_spatial,
           b_spatial, w_proj_attn, w_proj_sgu, w_out, norm_ffn_g, w_router, b_router,
           w_gate_up, b_gate_up, w_down, b_down, norm_final_g):
    w = _prepare_weights(norm_mix_g, w_in, q_norm_g, k_norm_g, sgu_norm_g, w_spatial,
                         b_spatial, w_proj_attn, w_proj_sgu, w_out, norm_ffn_g, w_router,
                         b_router, w_gate_up, b_gate_up, w_down, b_down, norm_final_g)
    x_prompt, w["wgu"] = lax.optimization_barrier((x_prompt, w["wgu"]))
    s_max = max(x_prompt.shape[1], x_sample.shape[1])
    rope = (*_rope_tables(s_max, w["q_gain"], math.log2(math.e) / math.sqrt(HEAD_DIM)),
            *_rope_tables(s_max, w["k_gain"], 1.0))
    return (_trunk(x_prompt, w, rope), _trunk(x_sample, w, rope))
```

```python
import math

import jax
import jax.numpy as jnp
from jax import lax
from jax.experimental import pallas as pl
from jax.experimental.pallas import tpu as pltpu
from jax.experimental.pallas import tpu_sc as plsc

D_MODEL = 1024
GRID_W = 64
N_HEADS = 8
N_KV_HEADS = 2
HEAD_DIM = 64
GQA_GROUP = N_HEADS // N_KV_HEADS
Q_W = N_HEADS * HEAD_DIM
KV_W = N_KV_HEADS * HEAD_DIM
ROPE_AXIS_DIM = HEAD_DIM // 2
ROPE_THETA = 10000.0
SGU_GROUPS = 8
SGU_W = D_MODEL // 2
SGU_GROUP_DIM = SGU_W // SGU_GROUPS
CHUNK = 128
IN_W = Q_W + 2 * KV_W + 2 * SGU_W + 2 * D_MODEL
N_EXPERTS = 32
TOP_K = 4
D_FF = D_MODEL
SWIGLU_LIMIT = 7.0
SWIGLU_ALPHA = 1.702
EPS = 1e-6

_Q0, _K0, _V0 = 0, Q_W, Q_W + KV_W
_U0 = Q_W + 2 * KV_W
_VS0 = _U0 + SGU_W
_GA0 = _VS0 + SGU_W
_GB0 = _GA0 + D_MODEL

TOKEN_TILE = 512
_IN_PROJ_SUBTILES = 2
Q_TILE = 512
EXPERT_TILE = 512
COMBINE_TILE = 512
PACKED_W = D_MODEL // 2
SC_WINDOW = 64

_LANES = 128
_BF16_SUBLANES = 16
_SUM_ROWS = _BF16_SUBLANES
_GU_BLOCK = 256
_NEG_BIG = -1e30
_HIGH_HALF_MASK = 0xFFFF0000
_MAX_LAGGED_EXPONENT = 80.0
_KEY_TILES_PER_TRIP = 4
_MIB = 1024 * 1024
_VMEM_LIMIT_MIB = 40

_NT_DIMS = (((1,), (1,)), ((), ()))
_TN_DIMS = (((0,), (0,)), ((), ()))


def _compiler_params(*semantics):
    return pltpu.CompilerParams(
        dimension_semantics=semantics, vmem_limit_bytes=_VMEM_LIMIT_MIB * _MIB)


def _swap_adjacent_rows(x):
    n = x.shape[0]
    row = lax.broadcasted_iota(jnp.int32, x.shape, 0)
    nxt = pltpu.roll(x, n - 1, 0)
    prv = pltpu.roll(x, 1, 0)
    return jnp.where((row & 1) == 0, nxt, prv)


def _head_norm_rope(z, tab_a, tab_b, n_heads):
    tm = z.shape[1]
    z3 = z.reshape(n_heads, HEAD_DIM, tm)
    ms = jnp.mean(z3 * z3, axis=1, keepdims=True)
    r = lax.rsqrt(ms + EPS)
    zs = _swap_adjacent_rows(z).reshape(n_heads, HEAD_DIM, tm)
    out = (z3 * tab_a[None] + zs * tab_b[None]) * r
    return out.reshape(n_heads * HEAD_DIM, tm)


def _gelu(x):
    return 0.5 * x * (1.0 + lax.erf(x * (1.0 / math.sqrt(2.0))))


def _pack_row_halves(x):
    half = x.shape[1] // 2

    def bf16_bits(v):
        return lax.bitcast_convert_type(v.astype(jnp.bfloat16).astype(jnp.float32), jnp.uint32)

    word = (bf16_bits(x[:, :half]) >> 16) | (bf16_bits(x[:, half:]) & jnp.uint32(_HIGH_HALF_MASK))
    return lax.bitcast_convert_type(word, jnp.int32)


def _unpack_row_halves(p):
    word = lax.bitcast_convert_type(p, jnp.uint32)
    lo = lax.bitcast_convert_type(word << 16, jnp.float32)
    hi = lax.bitcast_convert_type(word & jnp.uint32(_HIGH_HALF_MASK), jnp.float32)
    return jnp.concatenate([lo, hi], axis=1)


def _tile_lanes(x, reps):
    return jnp.concatenate([x] * reps, axis=1) if reps > 1 else x


def _in_proj_kernel(x_ref, gmix_ref, w_ref, qa_ref, qb_ref, ka_ref, kb_ref, gs_ref,
                    qT_ref, k_ref, vT_ref, uT_ref, vsT_ref, gaT_ref, gbT_ref):
    tm = x_ref.shape[1]
    sub = tm // _IN_PROJ_SUBTILES
    for i in range(_IN_PROJ_SUBTILES):
        tok = slice(i * sub, (i + 1) * sub)
        x = x_ref[0, tok, :]
        ms = jnp.mean(x * x, axis=-1, keepdims=True)
        h = (x * lax.rsqrt(ms + EPS) * gmix_ref[...]).astype(jnp.bfloat16)

        def proj(r0, rows, h=h):
            return lax.dot_general(w_ref[r0:r0 + rows, :], h, _NT_DIMS,
                                   preferred_element_type=jnp.float32)

        zq = proj(_Q0, Q_W)
        qT_ref[0, :, tok] = _head_norm_rope(
            zq, qa_ref[:, tok], qb_ref[:, tok], N_HEADS).astype(jnp.bfloat16)

        zkv = proj(_K0, 2 * KV_W)
        kT = _head_norm_rope(zkv[:KV_W], ka_ref[:, tok], kb_ref[:, tok], N_KV_HEADS)
        k_ref[0, tok, :] = kT.T.astype(jnp.bfloat16)
        vT_ref[0, 0, :, tok] = zkv[KV_W:].astype(jnp.bfloat16)

        uT_ref[0, :, tok] = _gelu(proj(_U0, SGU_W)).astype(jnp.bfloat16)

        vs = _gelu(proj(_VS0, SGU_W))
        vms = jnp.mean(vs * vs, axis=0, keepdims=True)
        gs = _tile_lanes(gs_ref[...], sub // _LANES)
        vsT_ref[0, :, tok] = (vs * lax.rsqrt(vms + EPS) * gs).astype(jnp.bfloat16)

        gaT_ref[0, :, tok] = jax.nn.sigmoid(proj(_GA0, D_MODEL)).astype(jnp.bfloat16)
        gbT_ref[0, :, tok] = jax.nn.sigmoid(proj(_GB0, D_MODEL)).astype(jnp.bfloat16)


def _in_proj(x, gmix, w_inT, qa, qb, ka, kb, gs):
    B, S, _ = x.shape
    tm = TOKEN_TILE
    nt = S // tm
    bf = jnp.bfloat16
    const2 = lambda b, i: (0, 0)
    tab = pl.BlockSpec((HEAD_DIM, tm), lambda b, i: (0, i))
    fm = lambda rows: pl.BlockSpec((1, rows, tm), lambda b, i: (b, 0, i))
    return pl.pallas_call(
        _in_proj_kernel,
        grid=(B, nt),
        in_specs=[
            pl.BlockSpec((1, tm, D_MODEL), lambda b, i: (b, i, 0)),
            pl.BlockSpec((1, D_MODEL), const2),
            pl.BlockSpec((IN_W, D_MODEL), const2),
            tab, tab, tab, tab,
            pl.BlockSpec((SGU_W, _LANES), const2),
        ],
        out_specs=[
            fm(Q_W),
            pl.BlockSpec((1, tm, KV_W), lambda b, i: (b, i, 0)),
            pl.BlockSpec((1, 1, KV_W, tm), lambda b, i: (b, i, 0, 0)),
            fm(SGU_W), fm(SGU_W), fm(D_MODEL), fm(D_MODEL),
        ],
        out_shape=[
            jax.ShapeDtypeStruct((B, Q_W, S), bf),
            jax.ShapeDtypeStruct((B, S, KV_W), bf),
            jax.ShapeDtypeStruct((B, nt, KV_W, tm), bf),
            jax.ShapeDtypeStruct((B, SGU_W, S), bf),
            jax.ShapeDtypeStruct((B, SGU_W, S), bf),
            jax.ShapeDtypeStruct((B, D_MODEL, S), bf),
            jax.ShapeDtypeStruct((B, D_MODEL, S), bf),
        ],
        compiler_params=_compiler_params("arbitrary", "arbitrary"),
        name="in_proj",
    )(x, gmix, w_inT, qa, qb, ka, kb, gs)


def _attention_kernel(qT_ref, k_ref, vT_ref, o_ref, m_ref, gap_ref, acc_ref):
    tq = qT_ref.shape[2]
    n_kt = vT_ref.shape[1]
    tk = vT_ref.shape[3]
    nq = GQA_GROUP * tq
    ones_rows = (lax.broadcasted_iota(jnp.int32, (_SUM_ROWS, tk), 0) == 0).astype(jnp.bfloat16)

    zeros = jnp.zeros((HEAD_DIM, nq), jnp.bfloat16)
    qp = []
    for kv in range(N_KV_HEADS):
        heads = [qT_ref[0, (kv * GQA_GROUP + g) * HEAD_DIM:(kv * GQA_GROUP + g + 1) * HEAD_DIM, :]
                 for g in range(GQA_GROUP)]
        qk = jnp.concatenate(heads, axis=1)
        qp.append(jnp.concatenate([qk, zeros] if kv == 0 else [zeros, qk], axis=0))

    def key_tile(ki):
        return k_ref[0, pl.ds(pl.multiple_of(ki * tk, tk), tk), :]

    def value_rows(ki, kv):
        vt = vT_ref[0, ki]
        return jnp.concatenate([vt[kv * HEAD_DIM:(kv + 1) * HEAD_DIM, :], ones_rows], axis=0)

    acc_ref[...] = jnp.zeros(acc_ref.shape, jnp.float32)
    gap_ref[...] = jnp.zeros(gap_ref.shape, jnp.float32)
    def as_reference(m):
        return m.astype(jnp.bfloat16).astype(jnp.float32)

    first_keys = k_ref[0, 0:_BF16_SUBLANES, :]
    for kv in range(N_KV_HEADS):
        s0 = jnp.dot(first_keys, qp[kv], preferred_element_type=jnp.float32)
        m_ref[kv] = as_reference(jnp.max(s0, axis=0, keepdims=True))

    ones_cols = (lax.broadcasted_iota(jnp.int32, (tk, _LANES), 1) == 0).astype(jnp.bfloat16)
    ref_row = lax.broadcasted_iota(jnp.int32, (_BF16_SUBLANES, nq), 0) == 0
    pad_rows = jnp.zeros((_LANES - _BF16_SUBLANES, nq), jnp.bfloat16)

    def fast_tile(ki):
        kt = jnp.concatenate([key_tile(ki), ones_cols], axis=1)
        for kv in range(N_KV_HEADS):
            m_old = m_ref[kv]
            neg_ref = jnp.where(ref_row, -m_old, 0.0).astype(jnp.bfloat16)
            q_aug = jnp.concatenate([qp[kv], neg_ref, pad_rows], axis=0)
            s = jnp.dot(kt, q_aug, preferred_element_type=jnp.float32)
            p = jnp.exp2(s).astype(jnp.bfloat16)
            over = jnp.max(s, axis=0, keepdims=True)
            pv = jnp.dot(value_rows(ki, kv), p, preferred_element_type=jnp.float32)
            m_new = as_reference(m_old + jnp.maximum(over, 0.0))
            acc_ref[kv] = (acc_ref[kv] + pv) * jnp.exp2(m_old - m_new)
            m_ref[kv] = m_new
            gap_ref[kv] = jnp.maximum(gap_ref[kv], over)

    per_trip = math.gcd(n_kt, _KEY_TILES_PER_TRIP)

    def fast_group(j, carry):
        for t in range(per_trip):
            fast_tile(per_trip * j + t)
        return carry

    lax.fori_loop(0, n_kt // per_trip, fast_group, 0)

    @pl.when(jnp.max(gap_ref[...]) > _MAX_LAGGED_EXPONENT)
    def _():
        m_ref[...] = jnp.full(m_ref.shape, _NEG_BIG, jnp.float32)
        acc_ref[...] = jnp.zeros(acc_ref.shape, jnp.float32)

        def robust_tile(ki, carry):
            kt = key_tile(ki)
            for kv in range(N_KV_HEADS):
                s = jnp.dot(kt, qp[kv], preferred_element_type=jnp.float32)
                m_old = m_ref[kv]
                m_new = jnp.maximum(m_old, jnp.max(s, axis=0, keepdims=True))
                p = jnp.exp2(s - m_new).astype(jnp.bfloat16)
                pv = jnp.dot(value_rows(ki, kv), p, preferred_element_type=jnp.float32)
                acc_ref[kv] = jnp.exp2(m_old - m_new) * acc_ref[kv] + pv
                m_ref[kv] = m_new
            return carry

        lax.fori_loop(0, n_kt, robust_tile, 0)

    for kv in range(N_KV_HEADS):
        acc = acc_ref[kv]
        o = acc[:HEAD_DIM] / acc[HEAD_DIM:HEAD_DIM + 1]
        for g in range(GQA_GROUP):
            r0 = (kv * GQA_GROUP + g) * HEAD_DIM
            o_ref[0, r0:r0 + HEAD_DIM, :] = o[:, g * tq:(g + 1) * tq].astype(jnp.bfloat16)


def _attention(qT, k, vT):
    B, _, S = qT.shape
    tq = Q_TILE
    nq = GQA_GROUP * tq
    n_kt, tk = vT.shape[1], vT.shape[3]
    return pl.pallas_call(
        _attention_kernel,
        grid=(B, S // tq),
        in_specs=[
            pl.BlockSpec((1, Q_W, tq), lambda b, i: (b, 0, i)),
            pl.BlockSpec((1, S, KV_W), lambda b, i: (b, 0, 0)),
            pl.BlockSpec((1, n_kt, KV_W, tk), lambda b, i: (b, 0, 0, 0)),
        ],
        out_specs=pl.BlockSpec((1, Q_W, tq), lambda b, i: (b, 0, i)),
        out_shape=jax.ShapeDtypeStruct((B, Q_W, S), jnp.bfloat16),
        scratch_shapes=[
            pltpu.VMEM((N_KV_HEADS, 1, nq), jnp.float32),
            pltpu.VMEM((N_KV_HEADS, 1, nq), jnp.float32),
            pltpu.VMEM((N_KV_HEADS, HEAD_DIM + _SUM_ROWS, nq), jnp.float32),
        ],
        compiler_params=_compiler_params("arbitrary", "arbitrary"),
        name="attention",
    )(qT, k, vT)


def _post_kernel(attnT_ref, uT_ref, vsT_ref, gaT_ref, gbT_ref, x_ref,
                 wsT_ref, bs_ref, wpaT_ref, wpbT_ref, wout_ref, gffn_ref, wrT_ref, br_ref,
                 x1_ref, h2_ref, ids_ref, wts_ref, rank_ref, cnt_ref, carry_ref):
    tm = x_ref.shape[1]
    n_chunks = tm // CHUNK
    first = jnp.logical_and(pl.program_id(0) == 0, pl.program_id(1) == 0)

    @pl.when(first)
    def _():
        carry_ref[...] = jnp.zeros(carry_ref.shape, jnp.float32)

    gate_rows = []
    for g in range(SGU_GROUPS):
        r0 = g * SGU_GROUP_DIM
        vs_g = vsT_ref[0, r0:r0 + SGU_GROUP_DIM, :]
        lhs = jnp.concatenate(
            [vs_g[:, c * CHUNK:(c + 1) * CHUNK] for c in range(n_chunks)], axis=0)
        mixed = jnp.dot(lhs, wsT_ref[g], preferred_element_type=jnp.float32)
        mixed = mixed + bs_ref[g]
        mixedT = jnp.concatenate(
            [mixed[c * SGU_GROUP_DIM:(c + 1) * SGU_GROUP_DIM] for c in range(n_chunks)], axis=1)
        u_g = uT_ref[0, r0:r0 + SGU_GROUP_DIM, :].astype(jnp.float32)
        gate_rows.append((u_g * mixedT).astype(jnp.bfloat16))
    gateT = jnp.concatenate(gate_rows, axis=0)

    paT = jnp.dot(wpaT_ref[...], attnT_ref[0], preferred_element_type=jnp.float32)
    pbT = jnp.dot(wpbT_ref[...], gateT, preferred_element_type=jnp.float32)
    mT = (gaT_ref[0].astype(jnp.float32) * paT
          + gbT_ref[0].astype(jnp.float32) * pbT).astype(jnp.bfloat16)
    y = lax.dot_general(mT, wout_ref[...], _TN_DIMS, preferred_element_type=jnp.float32)
    x1 = x_ref[0] + y
    x1_ref[0] = x1

    ms = jnp.mean(x1 * x1, axis=-1, keepdims=True)
    h2 = x1 * lax.rsqrt(ms + EPS) * gffn_ref[...]
    h2_ref[0] = _pack_row_halves(h2)

    h_hi = h2.astype(jnp.bfloat16)
    h_lo = (h2 - h_hi.astype(jnp.float32)).astype(jnp.bfloat16)
    by_hi = lax.dot_general(wrT_ref[...], h_hi, _NT_DIMS, preferred_element_type=jnp.float32)
    by_lo = lax.dot_general(wrT_ref[:N_EXPERTS, :], h_lo, _NT_DIMS,
                            preferred_element_type=jnp.float32)
    logits = by_hi[:N_EXPERTS] + by_hi[N_EXPERTS:] + by_lo + br_ref[...]
    eidx = lax.broadcasted_iota(jnp.int32, logits.shape, 0).astype(jnp.float32)
    work = logits
    vals, ids, sels = [], [], []
    for _ in range(TOP_K):
        mx = jnp.max(work, axis=0, keepdims=True)
        idx = jnp.min(jnp.where(work == mx, eidx, float(N_EXPERTS)), axis=0, keepdims=True)
        sel = eidx == idx
        vals.append(mx)
        ids.append(idx)
        sels.append(sel)
        work = jnp.where(sel, -jnp.inf, work)
    exps = [jnp.exp(v - vals[0]) for v in vals]
    denom = exps[0] + exps[1] + exps[2] + exps[3]
    ids_ref[...] = jnp.concatenate(ids, axis=0).astype(jnp.int32)
    wts_ref[...] = jnp.concatenate([e / denom for e in exps], axis=0)

    onehot = [s.astype(jnp.float32) for s in sels]
    hits = onehot[0] + onehot[1] + onehot[2] + onehot[3]
    ti = lax.broadcasted_iota(jnp.int32, (tm, tm), 0)
    tj = lax.broadcasted_iota(jnp.int32, (tm, tm), 1)
    upper = (ti < tj).astype(jnp.bfloat16)
    prefix = jnp.dot(hits.astype(jnp.bfloat16), upper, preferred_element_type=jnp.float32)
    base = prefix + carry_ref[...]
    ranks = [jnp.sum(oh * base, axis=0, keepdims=True) for oh in onehot]
    rank_ref[...] = jnp.concatenate(ranks, axis=0).astype(jnp.int32)
    carry_ref[...] = carry_ref[...] + jnp.sum(hits, axis=1, keepdims=True)
    cnt_ref[...] = carry_ref[...].astype(jnp.int32)


def _post(attnT, uT, vsT, gaT, gbT, x, wsT, bs, wpaT, wpbT, wout, gffn, wrT, br):
    B, S, _ = x.shape
    tm = TOKEN_TILE
    nt = S // tm
    T = B * S
    fm = lambda rows: pl.BlockSpec((1, rows, tm), lambda b, i: (b, 0, i))
    rowm = pl.BlockSpec((1, tm, D_MODEL), lambda b, i: (b, i, 0))
    c2 = lambda b, i: (0, 0)
    c3 = lambda b, i: (0, 0, 0)
    tokT = pl.BlockSpec((TOP_K, tm), lambda b, i: (0, b * nt + i))
    return pl.pallas_call(
        _post_kernel,
        grid=(B, nt),
        in_specs=[
            fm(Q_W), fm(SGU_W), fm(SGU_W), fm(D_MODEL), fm(D_MODEL), rowm,
            pl.BlockSpec((SGU_GROUPS, CHUNK, CHUNK), c3),
            pl.BlockSpec((SGU_GROUPS, 1, CHUNK), c3),
            pl.BlockSpec((D_MODEL, Q_W), c2),
            pl.BlockSpec((D_MODEL, SGU_W), c2),
            pl.BlockSpec((D_MODEL, D_MODEL), c2),
            pl.BlockSpec((1, D_MODEL), c2),
            pl.BlockSpec((2 * N_EXPERTS, D_MODEL), c2),
            pl.BlockSpec((N_EXPERTS, 1), c2),
        ],
        out_specs=[rowm, pl.BlockSpec((1, tm, PACKED_W), lambda b, i: (b, i, 0)),
                   tokT, tokT, tokT, pl.BlockSpec((N_EXPERTS, 1), c2)],
        out_shape=[
            jax.ShapeDtypeStruct((B, S, D_MODEL), jnp.float32),
            jax.ShapeDtypeStruct((B, S, PACKED_W), jnp.int32),
            jax.ShapeDtypeStruct((TOP_K, T), jnp.int32),
            jax.ShapeDtypeStruct((TOP_K, T), jnp.float32),
            jax.ShapeDtypeStruct((TOP_K, T), jnp.int32),
            jax.ShapeDtypeStruct((N_EXPERTS, 1), jnp.int32),
        ],
        scratch_shapes=[pltpu.VMEM((N_EXPERTS, 1), jnp.float32)],
        compiler_params=_compiler_params("arbitrary", "arbitrary"),
        name="post",
    )(attnT, uT, vsT, gaT, gbT, x, wsT, bs, wpaT, wpbT, wout, gffn, wrT, br)


def _sc_kernel(name, body, out_type, n_items, row_shape, row_dtype):
    info = plsc.get_sparse_core_info()
    n_workers = info.num_cores * info.num_subcores
    per_worker = n_items // n_workers
    n_windows = per_worker // SC_WINDOW
    assert per_worker * n_workers == n_items and n_windows * SC_WINDOW == per_worker
    assert n_windows % 2 == 0

    def wrapped(*refs):
        wid = lax.axis_index("subcore") * info.num_cores + lax.axis_index("core")
        body(wid * per_worker, n_windows, *refs)

    return pl.kernel(
        wrapped,
        name=name,
        out_type=out_type,
        mesh=plsc.VectorSubcoreMesh(core_axis_name="core", subcore_axis_name="subcore"),
        scratch_types=[
            pltpu.VMEM((SC_WINDOW,), jnp.int32),
            pltpu.VMEM((SC_WINDOW,), jnp.int32),
            pltpu.VMEM((SC_WINDOW,) + row_shape, row_dtype),
            pltpu.VMEM((SC_WINDOW,) + row_shape, row_dtype),
            pltpu.SemaphoreType.DMA,
            pltpu.SemaphoreType.DMA,
        ],
    )


def _sc_gather_rows(table, idx):
    n, d, win = idx.shape[0], table.shape[1], SC_WINDOW

    def body(base, n_windows, table_hbm, idx_hbm, out_hbm, idx0, idx1, rows0, rows1, sem0, sem1):
        def window(c):
            return pl.ds(pl.multiple_of(base + c * win, win), win)

        def fetch(c, idx_v, rows_v, sem):
            pltpu.sync_copy(idx_hbm.at[window(c)], idx_v)
            pltpu.async_copy(table_hbm.at[idx_v], rows_v, sem)

        def drain(c, idx_v, rows_v, sem):
            pltpu.make_async_copy(table_hbm.at[idx_v], rows_v, sem).wait()
            pltpu.sync_copy(rows_v, out_hbm.at[window(c)])

        fetch(0, idx0, rows0, sem0)

        @pl.loop(0, n_windows, step=2)
        def _(c):
            fetch(c + 1, idx1, rows1, sem1)
            drain(c, idx0, rows0, sem0)

            @pl.when(c + 2 < n_windows)
            def _():
                fetch(c + 2, idx0, rows0, sem0)

            drain(c + 1, idx1, rows1, sem1)

    out_type = jax.ShapeDtypeStruct((n, d), table.dtype)
    return _sc_kernel("sc_gather_rows", body, out_type, n, (d,), table.dtype)(table, idx)


def _sc_scatter_rows(src, dest, n_rows):
    T, d = src.shape
    win = SC_WINDOW

    def body(base, n_windows, src_hbm, dest_hbm, out_hbm, idx0, idx1, rows0, rows1, sem0, sem1):
        def window(c, slot=0):
            return pl.ds(pl.multiple_of(slot * T + base + c * win, win), win)

        def fetch(c, rows_v, sem):
            pltpu.async_copy(src_hbm.at[window(c)], rows_v, sem)

        def push(c, rows_v, sem):
            pltpu.make_async_copy(src_hbm.at[window(c)], rows_v, sem).wait()
            for k in range(TOP_K):
                idx_v = idx0 if k % 2 == 0 else idx1
                pltpu.sync_copy(dest_hbm.at[window(c, k)], idx_v)
                pltpu.sync_copy(rows_v, out_hbm.at[idx_v])

        fetch(0, rows0, sem0)

        @pl.loop(0, n_windows, step=2)
        def _(c):
            fetch(c + 1, rows1, sem1)
            push(c, rows0, sem0)

            @pl.when(c + 2 < n_windows)
            def _():
                fetch(c + 2, rows0, sem0)

            push(c + 1, rows1, sem1)

    out_type = jax.ShapeDtypeStruct((n_rows, d), src.dtype)
    return _sc_kernel("sc_scatter_rows", body, out_type, T, (d,), src.dtype)(
        src, dest.reshape(TOP_K * T))


def _combine_dense_kernel(w_ref, x1_ref, gfin_ref, y_ref, o_ref):
    w = w_ref[...]
    moe = w[:, 0:1] * _unpack_row_halves(y_ref[0])
    for k in range(1, TOP_K):
        moe = moe + w[:, k:k + 1] * _unpack_row_halves(y_ref[k])
    x2 = x1_ref[...] + moe
    ms = jnp.mean(x2 * x2, axis=-1, keepdims=True)
    o_ref[...] = x2 * lax.rsqrt(ms + EPS) * gfin_ref[...]


def _combine_dense(wts, x1, gfin, y_tok):
    T = x1.shape[0]
    tt = COMBINE_TILE
    return pl.pallas_call(
        _combine_dense_kernel,
        grid=(T // tt,),
        in_specs=[
            pl.BlockSpec((tt, TOP_K), lambda i: (i, 0)),
            pl.BlockSpec((tt, D_MODEL), lambda i: (i, 0)),
            pl.BlockSpec((1, D_MODEL), lambda i: (0, 0)),
            pl.BlockSpec((TOP_K, tt, PACKED_W), lambda i: (0, i, 0)),
        ],
        out_specs=pl.BlockSpec((tt, D_MODEL), lambda i: (i, 0)),
        out_shape=jax.ShapeDtypeStruct((T, D_MODEL), jnp.float32),
        compiler_params=_compiler_params("arbitrary"),
        name="combine_dense",
    )(wts, x1, gfin, y_tok)


def _gate_up_prep_kernel(w_ref, o_ref):
    half = _GU_BLOCK // 2
    i = lax.broadcasted_iota(jnp.int32, (_GU_BLOCK, _GU_BLOCK), 0)
    j = lax.broadcasted_iota(jnp.int32, (_GU_BLOCK, _GU_BLOCK), 1)
    perm = (i == jnp.where(j < half, 2 * j, 2 * (j - half) + 1)).astype(jnp.bfloat16)
    for c in range(2 * D_FF // _GU_BLOCK):
        cols = slice(c * _GU_BLOCK, (c + 1) * _GU_BLOCK)
        blk = w_ref[0, :, cols].astype(jnp.bfloat16)
        o_ref[0, :, cols] = jnp.dot(blk, perm, preferred_element_type=jnp.float32).astype(jnp.bfloat16)


def _gate_up_prep(w_gate_up):
    spec = pl.BlockSpec((1, D_MODEL, 2 * D_FF), lambda e: (e, 0, 0))
    return pl.pallas_call(
        _gate_up_prep_kernel,
        grid=(N_EXPERTS,),
        in_specs=[spec],
        out_specs=spec,
        out_shape=jax.ShapeDtypeStruct(w_gate_up.shape, jnp.bfloat16),
        compiler_params=_compiler_params("arbitrary"),
        name="gate_up_prep",
    )(w_gate_up)


def _split_gate_up(gu):
    half = _GU_BLOCK // 2
    n = gu.shape[1] // _GU_BLOCK
    glu = jnp.concatenate([gu[:, c * _GU_BLOCK:c * _GU_BLOCK + half] for c in range(n)], axis=1)
    lin = jnp.concatenate([gu[:, c * _GU_BLOCK + half:(c + 1) * _GU_BLOCK] for c in range(n)], axis=1)
    return glu, lin


def _experts_kernel(te_ref, nv_ref, x_ref, wgu_ref, bgu_ref, wd_ref, bd_ref, y_ref):
    @pl.when(pl.program_id(0) < nv_ref[0])
    def _():
        x = _unpack_row_halves(x_ref[...]).astype(jnp.bfloat16)
        gu = jnp.dot(x, wgu_ref[0], preferred_element_type=jnp.float32) + bgu_ref[0]
        glu, lin = _split_gate_up(gu)
        glu = jnp.minimum(glu, SWIGLU_LIMIT)
        lin = jnp.clip(lin, -SWIGLU_LIMIT, SWIGLU_LIMIT)
        a = glu * jax.nn.sigmoid(SWIGLU_ALPHA * glu) * (lin + 1.0)
        y = jnp.dot(a.astype(jnp.bfloat16), wd_ref[0],
                    preferred_element_type=jnp.float32) + bd_ref[0]
        y_ref[...] = _pack_row_halves(y)


def _experts(tile_expert, n_valid, x_sorted, wgu, bgu, wd, bd):
    n_rows = x_sorted.shape[0]
    tr = EXPERT_TILE
    n_tiles = n_rows // tr

    def row_map(j, te, nv):
        return (jnp.minimum(j, nv[0] - 1), 0)

    def exp_map(j, te, nv):
        return (te[j], 0, 0)

    grid_spec = pltpu.PrefetchScalarGridSpec(
        num_scalar_prefetch=2,
        grid=(n_tiles,),
        in_specs=[
            pl.BlockSpec((tr, PACKED_W), row_map),
            pl.BlockSpec((1, D_MODEL, 2 * D_FF), exp_map),
            pl.BlockSpec((1, 1, 2 * D_FF), exp_map),
            pl.BlockSpec((1, D_FF, D_MODEL), exp_map),
            pl.BlockSpec((1, 1, D_MODEL), exp_map),
        ],
        out_specs=pl.BlockSpec((tr, PACKED_W), row_map),
    )
    return pl.pallas_call(
        _experts_kernel,
        grid_spec=grid_spec,
        out_shape=jax.ShapeDtypeStruct((n_rows, PACKED_W), jnp.int32),
        compiler_params=_compiler_params("arbitrary"),
        name="experts",
    )(tile_expert, n_valid, x_sorted, wgu, bgu, wd, bd)


def _rope_tables(S, gain, scale):
    t = jnp.arange(S, dtype=jnp.int32)
    r = (t // GRID_W).astype(jnp.float32)
    c = (t % GRID_W).astype(jnp.float32)
    inv = jnp.float32(ROPE_THETA) ** (
        -jnp.arange(0, ROPE_AXIS_DIM, 2, dtype=jnp.float32) / ROPE_AXIS_DIM)
    ang = jnp.concatenate([r[None, :] * inv[:, None], c[None, :] * inv[:, None]], axis=0)
    cos = jnp.repeat(jnp.cos(ang), 2, axis=0)
    sin = jnp.repeat(jnp.sin(ang), 2, axis=0)
    sign = jnp.where(jnp.arange(HEAD_DIM) % 2 == 0, -1.0, 1.0).astype(jnp.float32)
    g = gain.astype(jnp.float32) * scale
    g_swapped = g.reshape(HEAD_DIM // 2, 2)[:, ::-1].reshape(HEAD_DIM)
    return g[:, None] * cos, (g_swapped * sign)[:, None] * sin


def _split_hi_lo(w):
    hi = w.astype(jnp.bfloat16)
    lo = (w - hi.astype(jnp.float32)).astype(jnp.bfloat16)
    return jnp.concatenate([hi, lo], axis=0)


def _prepare_weights(norm_mix_g, w_in, q_norm_g, k_norm_g, sgu_norm_g, w_spatial, b_spatial,
                     w_proj_attn, w_proj_sgu, w_out, norm_ffn_g, w_router, b_router,
                     w_gate_up, b_gate_up, w_down, b_down, norm_final_g):
    bf = jnp.bfloat16
    l = 0
    return dict(
        gmix=norm_mix_g[l][None, :],
        w_inT=w_in[l].T.astype(bf),
        q_gain=q_norm_g[l], k_gain=k_norm_g[l],
        gs=jnp.broadcast_to(sgu_norm_g[l][:, None], (SGU_W, _LANES)),
        wsT=jnp.swapaxes(w_spatial[l], 1, 2).astype(bf),
        bs=b_spatial[l][:, None, :],
        wpaT=w_proj_attn[l].T.astype(bf),
        wpbT=w_proj_sgu[l].T.astype(bf),
        wout=w_out[l].astype(bf),
        gffn=norm_ffn_g[l][None, :],
        wrT=_split_hi_lo(w_router[l].T),
        br=b_router[l][:, None],
        wgu=_gate_up_prep(w_gate_up[l]),
        bgu=b_gate_up[l].reshape(N_EXPERTS, -1, _GU_BLOCK // 2, 2).transpose(0, 1, 3, 2)
        .reshape(N_EXPERTS, 1, 2 * D_FF),
        wd=w_down[l].astype(bf),
        bd=b_down[l][:, None, :],
        gfin=norm_final_g[None, :],
    )


def _trunk(x, w, rope):
    B, S, _ = x.shape
    T = B * S
    qT, k, vT, uT, vsT, gaT, gbT = _in_proj(x, w["gmix"], w["w_inT"], *rope, w["gs"])
    attnT = _attention(qT, k, vT)
    x1, h2, ids, wts, rank, counts = _post(
        attnT, uT, vsT, gaT, gbT, x, w["wsT"], w["bs"], w["wpaT"], w["wpbT"], w["wout"],
        w["gffn"], w["wrT"], w["br"])

    tr = EXPERT_TILE
    counts = counts[:, 0]
    padded = (counts + tr - 1) // tr * tr
    ends = jnp.cumsum(padded)
    starts = ends - padded
    n_tiles = (TOP_K * T) // tr + N_EXPERTS
    n_rows = n_tiles * tr
    dest = rank
    for e in range(N_EXPERTS):
        dest = dest + jnp.where(ids == e, starts[e], 0)
    tile_start = jnp.arange(n_tiles, dtype=jnp.int32) * tr
    tile_expert = jnp.minimum(
        jnp.sum((tile_start[:, None] >= ends[None, :]).astype(jnp.int32), axis=1), N_EXPERTS - 1)
    n_valid = (ends[-1] // tr).astype(jnp.int32)[None]

    x_sorted = _sc_scatter_rows(h2.reshape(T, PACKED_W), dest, n_rows)
    y_sorted = _experts(tile_expert, n_valid, x_sorted, w["wgu"], w["bgu"], w["wd"], w["bd"])
    y_tok = _sc_gather_rows(y_sorted, dest.reshape(TOP_K * T)).reshape(TOP_K, T, PACKED_W)
    out = _combine_dense(wts.T, x1.reshape(T, D_MODEL), w["gfin"], y_tok)
    return out.reshape(B, S, D_MODEL)


def kernel(x_prompt, x_sample, norm_mix_g, w_in, q_norm_g, k_norm_g, sgu_norm_g, w_spatial,
           b_spatial, w_proj_attn, w_proj_sgu, w_out, norm_ffn_g, w_router, b_router,
           w_gate_up, b_gate_up, w_down, b_down, norm_final_g):
    w = _prepare_weights(norm_mix_g, w_in, q_norm_g, k_norm_g, sgu_norm_g, w_spatial,
                         b_spatial, w_proj_attn, w_proj_sgu, w_out, norm_ffn_g, w_router,
                         b_router, w_gate_up, b_gate_up, w_down, b_down, norm_final_g)
    x_prompt, w["wgu"], w["wd"] = lax.optimization_barrier((x_prompt, w["wgu"], w["wd"]))
    s_max = max(x_prompt.shape[1], x_sample.shape[1])
    rope = (*_rope_tables(s_max, w["q_gain"], math.log2(math.e) / math.sqrt(HEAD_DIM)),
            *_rope_tables(s_max, w["k_gain"], 1.0))
    return (_trunk(x_prompt, w, rope), _trunk(x_sample, w, rope))
```

```python
import math

import jax
import jax.numpy as jnp
from jax import lax
from jax.experimental import pallas as pl
from jax.experimental.pallas import tpu as pltpu
from jax.experimental.pallas import tpu_sc as plsc

D_MODEL = 1024
GRID_W = 64
N_HEADS = 8
N_KV_HEADS = 2
HEAD_DIM = 64
GQA_GROUP = N_HEADS // N_KV_HEADS
Q_W = N_HEADS * HEAD_DIM
KV_W = N_KV_HEADS * HEAD_DIM
ROPE_AXIS_DIM = HEAD_DIM // 2
ROPE_THETA = 10000.0
SGU_GROUPS = 8
SGU_W = D_MODEL // 2
SGU_GROUP_DIM = SGU_W // SGU_GROUPS
CHUNK = 128
IN_W = Q_W + 2 * KV_W + 2 * SGU_W + 2 * D_MODEL
N_EXPERTS = 32
TOP_K = 4
D_FF = D_MODEL
SWIGLU_LIMIT = 7.0
SWIGLU_ALPHA = 1.702
EPS = 1e-6

_Q0, _K0, _V0 = 0, Q_W, Q_W + KV_W
_U0 = Q_W + 2 * KV_W
_VS0 = _U0 + SGU_W
_GA0 = _VS0 + SGU_W
_GB0 = _GA0 + D_MODEL

TOKEN_TILE = 512
_IN_PROJ_SUBTILES = 2
Q_TILE = 512
EXPERT_TILE = 512
COMBINE_TILE = 512
PACKED_W = D_MODEL // 2
SC_WINDOW = 64

_LANES = 128
_BF16_SUBLANES = 16
_SUM_ROWS = _BF16_SUBLANES
_GU_BLOCK = 256
_NEG_BIG = -1e30
_HIGH_HALF_MASK = 0xFFFF0000
_MAX_LAGGED_EXPONENT = 80.0
_KEY_TILES_PER_TRIP = 8
_MIB = 1024 * 1024
_VMEM_LIMIT_MIB = 40

_NT_DIMS = (((1,), (1,)), ((), ()))
_TN_DIMS = (((0,), (0,)), ((), ()))


def _compiler_params(*semantics):
    return pltpu.CompilerParams(
        dimension_semantics=semantics, vmem_limit_bytes=_VMEM_LIMIT_MIB * _MIB)


def _swap_adjacent_rows(x):
    n = x.shape[0]
    row = lax.broadcasted_iota(jnp.int32, x.shape, 0)
    nxt = pltpu.roll(x, n - 1, 0)
    prv = pltpu.roll(x, 1, 0)
    return jnp.where((row & 1) == 0, nxt, prv)


def _head_norm_rope(z, tab_a, tab_b, n_heads):
    tm = z.shape[1]
    z3 = z.reshape(n_heads, HEAD_DIM, tm)
    ms = jnp.mean(z3 * z3, axis=1, keepdims=True)
    r = lax.rsqrt(ms + EPS)
    zs = _swap_adjacent_rows(z).reshape(n_heads, HEAD_DIM, tm)
    out = (z3 * tab_a[None] + zs * tab_b[None]) * r
    return out.reshape(n_heads * HEAD_DIM, tm)


def _gelu(x):
    return 0.5 * x * (1.0 + lax.erf(x * (1.0 / math.sqrt(2.0))))


def _pack_row_halves(x):
    half = x.shape[1] // 2

    def bf16_bits(v):
        return lax.bitcast_convert_type(v.astype(jnp.bfloat16).astype(jnp.float32), jnp.uint32)

    word = (bf16_bits(x[:, :half]) >> 16) | (bf16_bits(x[:, half:]) & jnp.uint32(_HIGH_HALF_MASK))
    return lax.bitcast_convert_type(word, jnp.int32)


def _unpack_row_halves(p):
    word = lax.bitcast_convert_type(p, jnp.uint32)
    lo = lax.bitcast_convert_type(word << 16, jnp.float32)
    hi = lax.bitcast_convert_type(word & jnp.uint32(_HIGH_HALF_MASK), jnp.float32)
    return jnp.concatenate([lo, hi], axis=1)


def _tile_lanes(x, reps):
    return jnp.concatenate([x] * reps, axis=1) if reps > 1 else x


def _in_proj_kernel(x_ref, gmix_ref, w_ref, qa_ref, qb_ref, ka_ref, kb_ref, gs_ref,
                    qT_ref, k_ref, vT_ref, uT_ref, vsT_ref, gaT_ref, gbT_ref):
    tm = x_ref.shape[1]
    sub = tm // _IN_PROJ_SUBTILES
    for i in range(_IN_PROJ_SUBTILES):
        tok = slice(i * sub, (i + 1) * sub)
        x = x_ref[0, tok, :]
        ms = jnp.mean(x * x, axis=-1, keepdims=True)
        h = (x * lax.rsqrt(ms + EPS) * gmix_ref[...]).astype(jnp.bfloat16)

        def proj(r0, rows, h=h):
            return lax.dot_general(w_ref[r0:r0 + rows, :], h, _NT_DIMS,
                                   preferred_element_type=jnp.float32)

        zq = proj(_Q0, Q_W)
        qT_ref[0, :, tok] = _head_norm_rope(
            zq, qa_ref[:, tok], qb_ref[:, tok], N_HEADS).astype(jnp.bfloat16)

        zkv = proj(_K0, 2 * KV_W)
        kT = _head_norm_rope(zkv[:KV_W], ka_ref[:, tok], kb_ref[:, tok], N_KV_HEADS)
        k_ref[0, tok, :] = kT.T.astype(jnp.bfloat16)
        vT_ref[0, 0, :, tok] = zkv[KV_W:].astype(jnp.bfloat16)

        uT_ref[0, :, tok] = _gelu(proj(_U0, SGU_W)).astype(jnp.bfloat16)

        vs = _gelu(proj(_VS0, SGU_W))
        vms = jnp.mean(vs * vs, axis=0, keepdims=True)
        gs = _tile_lanes(gs_ref[...], sub // _LANES)
        vsT_ref[0, :, tok] = (vs * lax.rsqrt(vms + EPS) * gs).astype(jnp.bfloat16)

        gaT_ref[0, :, tok] = jax.nn.sigmoid(proj(_GA0, D_MODEL)).astype(jnp.bfloat16)
        gbT_ref[0, :, tok] = jax.nn.sigmoid(proj(_GB0, D_MODEL)).astype(jnp.bfloat16)


def _in_proj(x, gmix, w_inT, qa, qb, ka, kb, gs):
    B, S, _ = x.shape
    tm = TOKEN_TILE
    nt = S // tm
    bf = jnp.bfloat16
    const2 = lambda b, i: (0, 0)
    tab = pl.BlockSpec((HEAD_DIM, tm), lambda b, i: (0, i))
    fm = lambda rows: pl.BlockSpec((1, rows, tm), lambda b, i: (b, 0, i))
    return pl.pallas_call(
        _in_proj_kernel,
        grid=(B, nt),
        in_specs=[
            pl.BlockSpec((1, tm, D_MODEL), lambda b, i: (b, i, 0)),
            pl.BlockSpec((1, D_MODEL), const2),
            pl.BlockSpec((IN_W, D_MODEL), const2),
            tab, tab, tab, tab,
            pl.BlockSpec((SGU_W, _LANES), const2),
        ],
        out_specs=[
            fm(Q_W),
            pl.BlockSpec((1, tm, KV_W), lambda b, i: (b, i, 0)),
            pl.BlockSpec((1, 1, KV_W, tm), lambda b, i: (b, i, 0, 0)),
            fm(SGU_W), fm(SGU_W), fm(D_MODEL), fm(D_MODEL),
        ],
        out_shape=[
            jax.ShapeDtypeStruct((B, Q_W, S), bf),
            jax.ShapeDtypeStruct((B, S, KV_W), bf),
            jax.ShapeDtypeStruct((B, nt, KV_W, tm), bf),
            jax.ShapeDtypeStruct((B, SGU_W, S), bf),
            jax.ShapeDtypeStruct((B, SGU_W, S), bf),
            jax.ShapeDtypeStruct((B, D_MODEL, S), bf),
            jax.ShapeDtypeStruct((B, D_MODEL, S), bf),
        ],
        compiler_params=_compiler_params("arbitrary", "arbitrary"),
        name="in_proj",
    )(x, gmix, w_inT, qa, qb, ka, kb, gs)


def _attention_kernel(qT_ref, k_ref, vT_ref, o_ref, m_ref, gap_ref, acc_ref):
    tq = qT_ref.shape[2]
    n_kt = vT_ref.shape[1]
    tk = vT_ref.shape[3]
    nq = GQA_GROUP * tq
    ones_rows = (lax.broadcasted_iota(jnp.int32, (_SUM_ROWS, tk), 0) == 0).astype(jnp.bfloat16)

    zeros = jnp.zeros((HEAD_DIM, nq), jnp.bfloat16)
    qp = []
    for kv in range(N_KV_HEADS):
        heads = [qT_ref[0, (kv * GQA_GROUP + g) * HEAD_DIM:(kv * GQA_GROUP + g + 1) * HEAD_DIM, :]
                 for g in range(GQA_GROUP)]
        qk = jnp.concatenate(heads, axis=1)
        qp.append(jnp.concatenate([qk, zeros] if kv == 0 else [zeros, qk], axis=0))

    def key_tile(ki):
        return k_ref[0, pl.ds(pl.multiple_of(ki * tk, tk), tk), :]

    def value_rows(ki, kv):
        vt = vT_ref[0, ki]
        return jnp.concatenate([vt[kv * HEAD_DIM:(kv + 1) * HEAD_DIM, :], ones_rows], axis=0)

    acc_ref[...] = jnp.zeros(acc_ref.shape, jnp.float32)
    gap_ref[...] = jnp.zeros(gap_ref.shape, jnp.float32)
    def as_reference(m):
        return m.astype(jnp.bfloat16).astype(jnp.float32)

    first_keys = k_ref[0, 0:_BF16_SUBLANES, :]
    for kv in range(N_KV_HEADS):
        s0 = jnp.dot(first_keys, qp[kv], preferred_element_type=jnp.float32)
        m_ref[kv] = as_reference(jnp.max(s0, axis=0, keepdims=True))

    ones_cols = (lax.broadcasted_iota(jnp.int32, (tk, _LANES), 1) == 0).astype(jnp.bfloat16)
    ref_row = lax.broadcasted_iota(jnp.int32, (_BF16_SUBLANES, nq), 0) == 0
    pad_rows = jnp.zeros((_LANES - _BF16_SUBLANES, nq), jnp.bfloat16)

    def fast_tile(ki):
        kt = jnp.concatenate([key_tile(ki), ones_cols], axis=1)
        for kv in range(N_KV_HEADS):
            m_old = m_ref[kv]
            neg_ref = jnp.where(ref_row, -m_old, 0.0).astype(jnp.bfloat16)
            q_aug = jnp.concatenate([qp[kv], neg_ref, pad_rows], axis=0)
            s = jnp.dot(kt, q_aug, preferred_element_type=jnp.float32)
            p = jnp.exp2(s).astype(jnp.bfloat16)
            over = jnp.max(s, axis=0, keepdims=True)
            pv = jnp.dot(value_rows(ki, kv), p, preferred_element_type=jnp.float32)
            m_new = as_reference(m_old + jnp.maximum(over, 0.0))
            acc_ref[kv] = (acc_ref[kv] + pv) * jnp.exp2(m_old - m_new)
            m_ref[kv] = m_new
            gap_ref[kv] = jnp.maximum(gap_ref[kv], over)

    per_trip = math.gcd(n_kt, _KEY_TILES_PER_TRIP)

    def fast_group(j, carry):
        for t in range(per_trip):
            fast_tile(per_trip * j + t)
        return carry

    lax.fori_loop(0, n_kt // per_trip, fast_group, 0)

    @pl.when(jnp.max(gap_ref[...]) > _MAX_LAGGED_EXPONENT)
    def _():
        m_ref[...] = jnp.full(m_ref.shape, _NEG_BIG, jnp.float32)
        acc_ref[...] = jnp.zeros(acc_ref.shape, jnp.float32)

        def robust_tile(ki, carry):
            kt = key_tile(ki)
            for kv in range(N_KV_HEADS):
                s = jnp.dot(kt, qp[kv], preferred_element_type=jnp.float32)
                m_old = m_ref[kv]
                m_new = jnp.maximum(m_old, jnp.max(s, axis=0, keepdims=True))
                p = jnp.exp2(s - m_new).astype(jnp.bfloat16)
                pv = jnp.dot(value_rows(ki, kv), p, preferred_element_type=jnp.float32)
                acc_ref[kv] = jnp.exp2(m_old - m_new) * acc_ref[kv] + pv
                m_ref[kv] = m_new
            return carry

        lax.fori_loop(0, n_kt, robust_tile, 0)

    for kv in range(N_KV_HEADS):
        acc = acc_ref[kv]
        o = acc[:HEAD_DIM] / acc[HEAD_DIM:HEAD_DIM + 1]
        for g in range(GQA_GROUP):
            r0 = (kv * GQA_GROUP + g) * HEAD_DIM
            o_ref[0, r0:r0 + HEAD_DIM, :] = o[:, g * tq:(g + 1) * tq].astype(jnp.bfloat16)


def _attention(qT, k, vT):
    B, _, S = qT.shape
    tq = Q_TILE
    nq = GQA_GROUP * tq
    n_kt, tk = vT.shape[1], vT.shape[3]
    return pl.pallas_call(
        _attention_kernel,
        grid=(B, S // tq),
        in_specs=[
            pl.BlockSpec((1, Q_W, tq), lambda b, i: (b, 0, i)),
            pl.BlockSpec((1, S, KV_W), lambda b, i: (b, 0, 0)),
            pl.BlockSpec((1, n_kt, KV_W, tk), lambda b, i: (b, 0, 0, 0)),
        ],
        out_specs=pl.BlockSpec((1, Q_W, tq), lambda b, i: (b, 0, i)),
        out_shape=jax.ShapeDtypeStruct((B, Q_W, S), jnp.bfloat16),
        scratch_shapes=[
            pltpu.VMEM((N_KV_HEADS, 1, nq), jnp.float32),
            pltpu.VMEM((N_KV_HEADS, 1, nq), jnp.float32),
            pltpu.VMEM((N_KV_HEADS, HEAD_DIM + _SUM_ROWS, nq), jnp.float32),
        ],
        compiler_params=_compiler_params("arbitrary", "arbitrary"),
        name="attention",
    )(qT, k, vT)


def _post_kernel(attnT_ref, uT_ref, vsT_ref, gaT_ref, gbT_ref, x_ref,
                 wsT_ref, bs_ref, wpaT_ref, wpbT_ref, wout_ref, gffn_ref, wrT_ref, br_ref,
                 x1_ref, h2_ref, ids_ref, wts_ref, rank_ref, cnt_ref, carry_ref):
    tm = x_ref.shape[1]
    n_chunks = tm // CHUNK
    first = jnp.logical_and(pl.program_id(0) == 0, pl.program_id(1) == 0)

    @pl.when(first)
    def _():
        carry_ref[...] = jnp.zeros(carry_ref.shape, jnp.float32)

    gate_rows = []
    for g in range(SGU_GROUPS):
        r0 = g * SGU_GROUP_DIM
        vs_g = vsT_ref[0, r0:r0 + SGU_GROUP_DIM, :]
        lhs = jnp.concatenate(
            [vs_g[:, c * CHUNK:(c + 1) * CHUNK] for c in range(n_chunks)], axis=0)
        mixed = jnp.dot(lhs, wsT_ref[g], preferred_element_type=jnp.float32)
        mixed = mixed + bs_ref[g]
        mixedT = jnp.concatenate(
            [mixed[c * SGU_GROUP_DIM:(c + 1) * SGU_GROUP_DIM] for c in range(n_chunks)], axis=1)
        u_g = uT_ref[0, r0:r0 + SGU_GROUP_DIM, :].astype(jnp.float32)
        gate_rows.append((u_g * mixedT).astype(jnp.bfloat16))
    gateT = jnp.concatenate(gate_rows, axis=0)

    paT = jnp.dot(wpaT_ref[...], attnT_ref[0], preferred_element_type=jnp.float32)
    pbT = jnp.dot(wpbT_ref[...], gateT, preferred_element_type=jnp.float32)
    mT = (gaT_ref[0].astype(jnp.float32) * paT
          + gbT_ref[0].astype(jnp.float32) * pbT).astype(jnp.bfloat16)
    y = lax.dot_general(mT, wout_ref[...], _TN_DIMS, preferred_element_type=jnp.float32)
    x1 = x_ref[0] + y
    x1_ref[0] = x1

    ms = jnp.mean(x1 * x1, axis=-1, keepdims=True)
    h2 = x1 * lax.rsqrt(ms + EPS) * gffn_ref[...]
    h2_ref[0] = _pack_row_halves(h2)

    h_hi = h2.astype(jnp.bfloat16)
    h_lo = (h2 - h_hi.astype(jnp.float32)).astype(jnp.bfloat16)
    by_hi = lax.dot_general(wrT_ref[...], h_hi, _NT_DIMS, preferred_element_type=jnp.float32)
    by_lo = lax.dot_general(wrT_ref[:N_EXPERTS, :], h_lo, _NT_DIMS,
                            preferred_element_type=jnp.float32)
    logits = by_hi[:N_EXPERTS] + by_hi[N_EXPERTS:] + by_lo + br_ref[...]
    eidx = lax.broadcasted_iota(jnp.int32, logits.shape, 0).astype(jnp.float32)
    work = logits
    vals, ids, sels = [], [], []
    for _ in range(TOP_K):
        mx = jnp.max(work, axis=0, keepdims=True)
        idx = jnp.min(jnp.where(work == mx, eidx, float(N_EXPERTS)), axis=0, keepdims=True)
        sel = eidx == idx
        vals.append(mx)
        ids.append(idx)
        sels.append(sel)
        work = jnp.where(sel, -jnp.inf, work)
    exps = [jnp.exp(v - vals[0]) for v in vals]
    denom = exps[0] + exps[1] + exps[2] + exps[3]
    ids_ref[...] = jnp.concatenate(ids, axis=0).astype(jnp.int32)
    wts_ref[...] = jnp.concatenate([e / denom for e in exps], axis=0)

    onehot = [s.astype(jnp.float32) for s in sels]
    hits = onehot[0] + onehot[1] + onehot[2] + onehot[3]
    ti = lax.broadcasted_iota(jnp.int32, (tm, tm), 0)
    tj = lax.broadcasted_iota(jnp.int32, (tm, tm), 1)
    upper = (ti < tj).astype(jnp.bfloat16)
    prefix = jnp.dot(hits.astype(jnp.bfloat16), upper, preferred_element_type=jnp.float32)
    base = prefix + carry_ref[...]
    ranks = [jnp.sum(oh * base, axis=0, keepdims=True) for oh in onehot]
    rank_ref[...] = jnp.concatenate(ranks, axis=0).astype(jnp.int32)
    carry_ref[...] = carry_ref[...] + jnp.sum(hits, axis=1, keepdims=True)
    cnt_ref[...] = carry_ref[...].astype(jnp.int32)


def _post(attnT, uT, vsT, gaT, gbT, x, wsT, bs, wpaT, wpbT, wout, gffn, wrT, br):
    B, S, _ = x.shape
    tm = TOKEN_TILE
    nt = S // tm
    T = B * S
    fm = lambda rows: pl.BlockSpec((1, rows, tm), lambda b, i: (b, 0, i))
    rowm = pl.BlockSpec((1, tm, D_MODEL), lambda b, i: (b, i, 0))
    c2 = lambda b, i: (0, 0)
    c3 = lambda b, i: (0, 0, 0)
    tokT = pl.BlockSpec((TOP_K, tm), lambda b, i: (0, b * nt + i))
    return pl.pallas_call(
        _post_kernel,
        grid=(B, nt),
        in_specs=[
            fm(Q_W), fm(SGU_W), fm(SGU_W), fm(D_MODEL), fm(D_MODEL), rowm,
            pl.BlockSpec((SGU_GROUPS, CHUNK, CHUNK), c3),
            pl.BlockSpec((SGU_GROUPS, 1, CHUNK), c3),
            pl.BlockSpec((D_MODEL, Q_W), c2),
            pl.BlockSpec((D_MODEL, SGU_W), c2),
            pl.BlockSpec((D_MODEL, D_MODEL), c2),
            pl.BlockSpec((1, D_MODEL), c2),
            pl.BlockSpec((2 * N_EXPERTS, D_MODEL), c2),
            pl.BlockSpec((N_EXPERTS, 1), c2),
        ],
        out_specs=[rowm, pl.BlockSpec((1, tm, PACKED_W), lambda b, i: (b, i, 0)),
                   tokT, tokT, tokT, pl.BlockSpec((N_EXPERTS, 1), c2)],
        out_shape=[
            jax.ShapeDtypeStruct((B, S, D_MODEL), jnp.float32),
            jax.ShapeDtypeStruct((B, S, PACKED_W), jnp.int32),
            jax.ShapeDtypeStruct((TOP_K, T), jnp.int32),
            jax.ShapeDtypeStruct((TOP_K, T), jnp.float32),
            jax.ShapeDtypeStruct((TOP_K, T), jnp.int32),
            jax.ShapeDtypeStruct((N_EXPERTS, 1), jnp.int32),
        ],
        scratch_shapes=[pltpu.VMEM((N_EXPERTS, 1), jnp.float32)],
        compiler_params=_compiler_params("arbitrary", "arbitrary"),
        name="post",
    )(attnT, uT, vsT, gaT, gbT, x, wsT, bs, wpaT, wpbT, wout, gffn, wrT, br)


def _sc_kernel(name, body, out_type, n_items, row_shape, row_dtype):
    info = plsc.get_sparse_core_info()
    n_workers = info.num_cores * info.num_subcores
    per_worker = n_items // n_workers
    n_windows = per_worker // SC_WINDOW
    assert per_worker * n_workers == n_items and n_windows * SC_WINDOW == per_worker
    assert n_windows % 2 == 0

    def wrapped(*refs):
        wid = lax.axis_index("subcore") * info.num_cores + lax.axis_index("core")
        body(wid * per_worker, n_windows, *refs)

    return pl.kernel(
        wrapped,
        name=name,
        out_type=out_type,
        mesh=plsc.VectorSubcoreMesh(core_axis_name="core", subcore_axis_name="subcore"),
        scratch_types=[
            pltpu.VMEM((SC_WINDOW,), jnp.int32),
            pltpu.VMEM((SC_WINDOW,), jnp.int32),
            pltpu.VMEM((SC_WINDOW,) + row_shape, row_dtype),
            pltpu.VMEM((SC_WINDOW,) + row_shape, row_dtype),
            pltpu.SemaphoreType.DMA,
            pltpu.SemaphoreType.DMA,
        ],
    )


def _sc_gather_rows(table, idx):
    n, d, win = idx.shape[0], table.shape[1], SC_WINDOW

    def body(base, n_windows, table_hbm, idx_hbm, out_hbm, idx0, idx1, rows0, rows1, sem0, sem1):
        def window(c):
            return pl.ds(pl.multiple_of(base + c * win, win), win)

        def fetch(c, idx_v, rows_v, sem):
            pltpu.sync_copy(idx_hbm.at[window(c)], idx_v)
            pltpu.async_copy(table_hbm.at[idx_v], rows_v, sem)

        def drain(c, idx_v, rows_v, sem):
            pltpu.make_async_copy(table_hbm.at[idx_v], rows_v, sem).wait()
            pltpu.sync_copy(rows_v, out_hbm.at[window(c)])

        fetch(0, idx0, rows0, sem0)

        @pl.loop(0, n_windows, step=2)
        def _(c):
            fetch(c + 1, idx1, rows1, sem1)
            drain(c, idx0, rows0, sem0)

            @pl.when(c + 2 < n_windows)
            def _():
                fetch(c + 2, idx0, rows0, sem0)

            drain(c + 1, idx1, rows1, sem1)

    out_type = jax.ShapeDtypeStruct((n, d), table.dtype)
    return _sc_kernel("sc_gather_rows", body, out_type, n, (d,), table.dtype)(table, idx)


def _sc_scatter_rows(src, dest, n_rows):
    T, d = src.shape
    win = SC_WINDOW

    def body(base, n_windows, src_hbm, dest_hbm, out_hbm, idx0, idx1, rows0, rows1, sem0, sem1):
        def window(c, slot=0):
            return pl.ds(pl.multiple_of(slot * T + base + c * win, win), win)

        def fetch(c, rows_v, sem):
            pltpu.async_copy(src_hbm.at[window(c)], rows_v, sem)

        def push(c, rows_v, sem):
            pltpu.make_async_copy(src_hbm.at[window(c)], rows_v, sem).wait()
            for k in range(TOP_K):
                idx_v = idx0 if k % 2 == 0 else idx1
                pltpu.sync_copy(dest_hbm.at[window(c, k)], idx_v)
                pltpu.sync_copy(rows_v, out_hbm.at[idx_v])

        fetch(0, rows0, sem0)

        @pl.loop(0, n_windows, step=2)
        def _(c):
            fetch(c + 1, rows1, sem1)
            push(c, rows0, sem0)

            @pl.when(c + 2 < n_windows)
            def _():
                fetch(c + 2, rows0, sem0)

            push(c + 1, rows1, sem1)

    out_type = jax.ShapeDtypeStruct((n_rows, d), src.dtype)
    return _sc_kernel("sc_scatter_rows", body, out_type, T, (d,), src.dtype)(
        src, dest.reshape(TOP_K * T))


def _combine_dense_kernel(w_ref, x1_ref, gfin_ref, y_ref, o_ref):
    w = w_ref[...]
    moe = w[:, 0:1] * _unpack_row_halves(y_ref[0])
    for k in range(1, TOP_K):
        moe = moe + w[:, k:k + 1] * _unpack_row_halves(y_ref[k])
    x2 = x1_ref[...] + moe
    ms = jnp.mean(x2 * x2, axis=-1, keepdims=True)
    o_ref[...] = x2 * lax.rsqrt(ms + EPS) * gfin_ref[...]


def _combine_dense(wts, x1, gfin, y_tok):
    T = x1.shape[0]
    tt = COMBINE_TILE
    return pl.pallas_call(
        _combine_dense_kernel,
        grid=(T // tt,),
        in_specs=[
            pl.BlockSpec((tt, TOP_K), lambda i: (i, 0)),
            pl.BlockSpec((tt, D_MODEL), lambda i: (i, 0)),
            pl.BlockSpec((1, D_MODEL), lambda i: (0, 0)),
            pl.BlockSpec((TOP_K, tt, PACKED_W), lambda i: (0, i, 0)),
        ],
        out_specs=pl.BlockSpec((tt, D_MODEL), lambda i: (i, 0)),
        out_shape=jax.ShapeDtypeStruct((T, D_MODEL), jnp.float32),
        compiler_params=_compiler_params("arbitrary"),
        name="combine_dense",
    )(wts, x1, gfin, y_tok)


def _gate_up_prep_kernel(w_ref, o_ref):
    half = _GU_BLOCK // 2
    i = lax.broadcasted_iota(jnp.int32, (_GU_BLOCK, _GU_BLOCK), 0)
    j = lax.broadcasted_iota(jnp.int32, (_GU_BLOCK, _GU_BLOCK), 1)
    perm = (i == jnp.where(j < half, 2 * j, 2 * (j - half) + 1)).astype(jnp.bfloat16)
    for c in range(2 * D_FF // _GU_BLOCK):
        cols = slice(c * _GU_BLOCK, (c + 1) * _GU_BLOCK)
        blk = w_ref[0, :, cols].astype(jnp.bfloat16)
        o_ref[0, :, cols] = jnp.dot(blk, perm, preferred_element_type=jnp.float32).astype(jnp.bfloat16)


def _gate_up_prep(w_gate_up):
    spec = pl.BlockSpec((1, D_MODEL, 2 * D_FF), lambda e: (e, 0, 0))
    return pl.pallas_call(
        _gate_up_prep_kernel,
        grid=(N_EXPERTS,),
        in_specs=[spec],
        out_specs=spec,
        out_shape=jax.ShapeDtypeStruct(w_gate_up.shape, jnp.bfloat16),
        compiler_params=_compiler_params("arbitrary"),
        name="gate_up_prep",
    )(w_gate_up)


def _split_gate_up(gu):
    half = _GU_BLOCK // 2
    n = gu.shape[1] // _GU_BLOCK
    glu = jnp.concatenate([gu[:, c * _GU_BLOCK:c * _GU_BLOCK + half] for c in range(n)], axis=1)
    lin = jnp.concatenate([gu[:, c * _GU_BLOCK + half:(c + 1) * _GU_BLOCK] for c in range(n)], axis=1)
    return glu, lin


def _experts_kernel(te_ref, nv_ref, x_ref, wgu_ref, bgu_ref, wd_ref, bd_ref, y_ref):
    @pl.when(pl.program_id(0) < nv_ref[0])
    def _():
        x = _unpack_row_halves(x_ref[...]).astype(jnp.bfloat16)
        gu = jnp.dot(x, wgu_ref[0], preferred_element_type=jnp.float32) + bgu_ref[0]
        glu, lin = _split_gate_up(gu)
        glu = jnp.minimum(glu, SWIGLU_LIMIT)
        lin = jnp.clip(lin, -SWIGLU_LIMIT, SWIGLU_LIMIT)
        a = glu * jax.nn.sigmoid(SWIGLU_ALPHA * glu) * (lin + 1.0)
        y = jnp.dot(a.astype(jnp.bfloat16), wd_ref[0],
                    preferred_element_type=jnp.float32) + bd_ref[0]
        y_ref[...] = _pack_row_halves(y)


def _experts(tile_expert, n_valid, x_sorted, wgu, bgu, wd, bd):
    n_rows = x_sorted.shape[0]
    tr = EXPERT_TILE
    n_tiles = n_rows // tr

    def row_map(j, te, nv):
        return (jnp.minimum(j, nv[0] - 1), 0)

    def exp_map(j, te, nv):
        return (te[j], 0, 0)

    grid_spec = pltpu.PrefetchScalarGridSpec(
        num_scalar_prefetch=2,
        grid=(n_tiles,),
        in_specs=[
            pl.BlockSpec((tr, PACKED_W), row_map),
            pl.BlockSpec((1, D_MODEL, 2 * D_FF), exp_map),
            pl.BlockSpec((1, 1, 2 * D_FF), exp_map),
            pl.BlockSpec((1, D_FF, D_MODEL), exp_map),
            pl.BlockSpec((1, 1, D_MODEL), exp_map),
        ],
        out_specs=pl.BlockSpec((tr, PACKED_W), row_map),
    )
    return pl.pallas_call(
        _experts_kernel,
        grid_spec=grid_spec,
        out_shape=jax.ShapeDtypeStruct((n_rows, PACKED_W), jnp.int32),
        compiler_params=_compiler_params("arbitrary"),
        name="experts",
    )(tile_expert, n_valid, x_sorted, wgu, bgu, wd, bd)


def _rope_tables(S, gain, scale):
    t = jnp.arange(S, dtype=jnp.int32)
    r = (t // GRID_W).astype(jnp.float32)
    c = (t % GRID_W).astype(jnp.float32)
    inv = jnp.float32(ROPE_THETA) ** (
        -jnp.arange(0, ROPE_AXIS_DIM, 2, dtype=jnp.float32) / ROPE_AXIS_DIM)
    ang = jnp.concatenate([r[None, :] * inv[:, None], c[None, :] * inv[:, None]], axis=0)
    cos = jnp.repeat(jnp.cos(ang), 2, axis=0)
    sin = jnp.repeat(jnp.sin(ang), 2, axis=0)
    sign = jnp.where(jnp.arange(HEAD_DIM) % 2 == 0, -1.0, 1.0).astype(jnp.float32)
    g = gain.astype(jnp.float32) * scale
    g_swapped = g.reshape(HEAD_DIM // 2, 2)[:, ::-1].reshape(HEAD_DIM)
    return g[:, None] * cos, (g_swapped * sign)[:, None] * sin


def _split_hi_lo(w):
    hi = w.astype(jnp.bfloat16)
    lo = (w - hi.astype(jnp.float32)).astype(jnp.bfloat16)
    return jnp.concatenate([hi, lo], axis=0)


def _prepare_weights(norm_mix_g, w_in, q_norm_g, k_norm_g, sgu_norm_g, w_spatial, b_spatial,
                     w_proj_attn, w_proj_sgu, w_out, norm_ffn_g, w_router, b_router,
                     w_gate_up, b_gate_up, w_down, b_down, norm_final_g):
    bf = jnp.bfloat16
    l = 0
    return dict(
        gmix=norm_mix_g[l][None, :],
        w_inT=w_in[l].T.astype(bf),
        q_gain=q_norm_g[l], k_gain=k_norm_g[l],
        gs=jnp.broadcast_to(sgu_norm_g[l][:, None], (SGU_W, _LANES)),
        wsT=jnp.swapaxes(w_spatial[l], 1, 2).astype(bf),
        bs=b_spatial[l][:, None, :],
        wpaT=w_proj_attn[l].T.astype(bf),
        wpbT=w_proj_sgu[l].T.astype(bf),
        wout=w_out[l].astype(bf),
        gffn=norm_ffn_g[l][None, :],
        wrT=_split_hi_lo(w_router[l].T),
        br=b_router[l][:, None],
        wgu=_gate_up_prep(w_gate_up[l]),
        bgu=b_gate_up[l].reshape(N_EXPERTS, -1, _GU_BLOCK // 2, 2).transpose(0, 1, 3, 2)
        .reshape(N_EXPERTS, 1, 2 * D_FF),
        wd=w_down[l].astype(bf),
        bd=b_down[l][:, None, :],
        gfin=norm_final_g[None, :],
    )


def _trunk(x, w, rope):
    B, S, _ = x.shape
    T = B * S
    qT, k, vT, uT, vsT, gaT, gbT = _in_proj(x, w["gmix"], w["w_inT"], *rope, w["gs"])
    attnT = _attention(qT, k, vT)
    x1, h2, ids, wts, rank, counts = _post(
        attnT, uT, vsT, gaT, gbT, x, w["wsT"], w["bs"], w["wpaT"], w["wpbT"], w["wout"],
        w["gffn"], w["wrT"], w["br"])

    tr = EXPERT_TILE
    counts = counts[:, 0]
    padded = (counts + tr - 1) // tr * tr
    ends = jnp.cumsum(padded)
    starts = ends - padded
    n_tiles = (TOP_K * T) // tr + N_EXPERTS
    n_rows = n_tiles * tr
    dest = rank
    for e in range(N_EXPERTS):
        dest = dest + jnp.where(ids == e, starts[e], 0)
    tile_start = jnp.arange(n_tiles, dtype=jnp.int32) * tr
    tile_expert = jnp.minimum(
        jnp.sum((tile_start[:, None] >= ends[None, :]).astype(jnp.int32), axis=1), N_EXPERTS - 1)
    n_valid = (ends[-1] // tr).astype(jnp.int32)[None]

    x_sorted = _sc_scatter_rows(h2.reshape(T, PACKED_W), dest, n_rows)
    y_sorted = _experts(tile_expert, n_valid, x_sorted, w["wgu"], w["bgu"], w["wd"], w["bd"])
    y_tok = _sc_gather_rows(y_sorted, dest.reshape(TOP_K * T)).reshape(TOP_K, T, PACKED_W)
    out = _combine_dense(wts.T, x1.reshape(T, D_MODEL), w["gfin"], y_tok)
    return out.reshape(B, S, D_MODEL)


def kernel(x_prompt, x_sample, norm_mix_g, w_in, q_norm_g, k_norm_g, sgu_norm_g, w_spatial,
           b_spatial, w_proj_attn, w_proj_sgu, w_out, norm_ffn_g, w_router, b_router,
           w_gate_up, b_gate_up, w_down, b_down, norm_final_g):
    w = _prepare_weights(norm_mix_g, w_in, q_norm_g, k_norm_g, sgu_norm_g, w_spatial,
                         b_spatial, w_proj_attn, w_proj_sgu, w_out, norm_ffn_g, w_router,
                         b_router, w_gate_up, b_gate_up, w_down, b_down, norm_final_g)
    x_prompt, w["wgu"], w["wd"] = lax.optimization_barrier((x_prompt, w["wgu"], w["wd"]))
    s_max = max(x_prompt.shape[1], x_sample.shape[1])
    rope = (*_rope_tables(s_max, w["q_gain"], math.log2(math.e) / math.sqrt(HEAD_DIM)),
            *_rope_tables(s_max, w["k_gain"], 1.0))
    return (_trunk(x_prompt, w, rope), _trunk(x_sample, w, rope))
```

```python
import math

import jax
import jax.numpy as jnp
from jax import lax
from jax.experimental import pallas as pl
from jax.experimental.pallas import tpu as pltpu
from jax.experimental.pallas import tpu_sc as plsc

D_MODEL = 1024
GRID_W = 64
N_HEADS = 8
N_KV_HEADS = 2
HEAD_DIM = 64
GQA_GROUP = N_HEADS // N_KV_HEADS
Q_W = N_HEADS * HEAD_DIM
KV_W = N_KV_HEADS * HEAD_DIM
ROPE_AXIS_DIM = HEAD_DIM // 2
ROPE_THETA = 10000.0
SGU_GROUPS = 8
SGU_W = D_MODEL // 2
SGU_GROUP_DIM = SGU_W // SGU_GROUPS
CHUNK = 128
IN_W = Q_W + 2 * KV_W + 2 * SGU_W + 2 * D_MODEL
N_EXPERTS = 32
TOP_K = 4
D_FF = D_MODEL
SWIGLU_LIMIT = 7.0
SWIGLU_ALPHA = 1.702
EPS = 1e-6

_Q0, _K0, _V0 = 0, Q_W, Q_W + KV_W
_U0 = Q_W + 2 * KV_W
_VS0 = _U0 + SGU_W
_GA0 = _VS0 + SGU_W
_GB0 = _GA0 + D_MODEL

TOKEN_TILE = 512
_IN_PROJ_SUBTILES = 2
Q_TILE = 512
EXPERT_TILE = 512
COMBINE_TILE = 512
_COMBINE_RING = 3
PACKED_W = D_MODEL // 2
SC_WINDOW = 64

_LANES = 128
_BF16_SUBLANES = 16
_SUM_ROWS = _BF16_SUBLANES
_GU_BLOCK = 256
_NEG_BIG = -1e30
_HIGH_HALF_MASK = 0xFFFF0000
_MAX_LAGGED_EXPONENT = 80.0
_KEY_TILES_PER_TRIP = 8
_MIB = 1024 * 1024
_VMEM_LIMIT_MIB = 40

_NT_DIMS = (((1,), (1,)), ((), ()))
_TN_DIMS = (((0,), (0,)), ((), ()))


def _compiler_params(*semantics):
    return pltpu.CompilerParams(
        dimension_semantics=semantics, vmem_limit_bytes=_VMEM_LIMIT_MIB * _MIB)


def _swap_adjacent_rows(x):
    n = x.shape[0]
    row = lax.broadcasted_iota(jnp.int32, x.shape, 0)
    nxt = pltpu.roll(x, n - 1, 0)
    prv = pltpu.roll(x, 1, 0)
    return jnp.where((row & 1) == 0, nxt, prv)


def _head_norm_rope(z, tab_a, tab_b, n_heads):
    tm = z.shape[1]
    z3 = z.reshape(n_heads, HEAD_DIM, tm)
    ms = jnp.mean(z3 * z3, axis=1, keepdims=True)
    r = lax.rsqrt(ms + EPS)
    zs = _swap_adjacent_rows(z).reshape(n_heads, HEAD_DIM, tm)
    out = (z3 * tab_a[None] + zs * tab_b[None]) * r
    return out.reshape(n_heads * HEAD_DIM, tm)


def _gelu(x):
    return 0.5 * x * (1.0 + lax.erf(x * (1.0 / math.sqrt(2.0))))


def _pack_row_halves(x):
    half = x.shape[1] // 2

    def bf16_bits(v):
        return lax.bitcast_convert_type(v.astype(jnp.bfloat16).astype(jnp.float32), jnp.uint32)

    word = (bf16_bits(x[:, :half]) >> 16) | (bf16_bits(x[:, half:]) & jnp.uint32(_HIGH_HALF_MASK))
    return lax.bitcast_convert_type(word, jnp.int32)


def _unpack_row_halves(p):
    word = lax.bitcast_convert_type(p, jnp.uint32)
    lo = lax.bitcast_convert_type(word << 16, jnp.float32)
    hi = lax.bitcast_convert_type(word & jnp.uint32(_HIGH_HALF_MASK), jnp.float32)
    return jnp.concatenate([lo, hi], axis=1)


def _tile_lanes(x, reps):
    return jnp.concatenate([x] * reps, axis=1) if reps > 1 else x


def _in_proj_kernel(x_ref, gmix_ref, w_ref, qa_ref, qb_ref, ka_ref, kb_ref, gs_ref,
                    qT_ref, k_ref, vT_ref, uT_ref, vsT_ref, gaT_ref, gbT_ref):
    tm = x_ref.shape[1]
    sub = tm // _IN_PROJ_SUBTILES
    for i in range(_IN_PROJ_SUBTILES):
        tok = slice(i * sub, (i + 1) * sub)
        x = x_ref[0, tok, :]
        ms = jnp.mean(x * x, axis=-1, keepdims=True)
        h = (x * lax.rsqrt(ms + EPS) * gmix_ref[...]).astype(jnp.bfloat16)

        def proj(r0, rows, h=h):
            return lax.dot_general(w_ref[r0:r0 + rows, :], h, _NT_DIMS,
                                   preferred_element_type=jnp.float32)

        zq = proj(_Q0, Q_W)
        qT_ref[0, :, tok] = _head_norm_rope(
            zq, qa_ref[:, tok], qb_ref[:, tok], N_HEADS).astype(jnp.bfloat16)

        zkv = proj(_K0, 2 * KV_W)
        kT = _head_norm_rope(zkv[:KV_W], ka_ref[:, tok], kb_ref[:, tok], N_KV_HEADS)
        k_ref[0, tok, :] = kT.T.astype(jnp.bfloat16)
        vT_ref[0, 0, :, tok] = zkv[KV_W:].astype(jnp.bfloat16)

        uT_ref[0, :, tok] = _gelu(proj(_U0, SGU_W)).astype(jnp.bfloat16)

        vs = _gelu(proj(_VS0, SGU_W))
        vms = jnp.mean(vs * vs, axis=0, keepdims=True)
        gs = _tile_lanes(gs_ref[...], sub // _LANES)
        vsT_ref[0, :, tok] = (vs * lax.rsqrt(vms + EPS) * gs).astype(jnp.bfloat16)

        gaT_ref[0, :, tok] = jax.nn.sigmoid(proj(_GA0, D_MODEL)).astype(jnp.bfloat16)
        gbT_ref[0, :, tok] = jax.nn.sigmoid(proj(_GB0, D_MODEL)).astype(jnp.bfloat16)


def _in_proj(x, gmix, w_inT, qa, qb, ka, kb, gs):
    B, S, _ = x.shape
    tm = TOKEN_TILE
    nt = S // tm
    bf = jnp.bfloat16
    const2 = lambda b, i: (0, 0)
    tab = pl.BlockSpec((HEAD_DIM, tm), lambda b, i: (0, i))
    fm = lambda rows: pl.BlockSpec((1, rows, tm), lambda b, i: (b, 0, i))
    return pl.pallas_call(
        _in_proj_kernel,
        grid=(B, nt),
        in_specs=[
            pl.BlockSpec((1, tm, D_MODEL), lambda b, i: (b, i, 0)),
            pl.BlockSpec((1, D_MODEL), const2),
            pl.BlockSpec((IN_W, D_MODEL), const2),
            tab, tab, tab, tab,
            pl.BlockSpec((SGU_W, _LANES), const2),
        ],
        out_specs=[
            fm(Q_W),
            pl.BlockSpec((1, tm, KV_W), lambda b, i: (b, i, 0)),
            pl.BlockSpec((1, 1, KV_W, tm), lambda b, i: (b, i, 0, 0)),
            fm(SGU_W), fm(SGU_W), fm(D_MODEL), fm(D_MODEL),
        ],
        out_shape=[
            jax.ShapeDtypeStruct((B, Q_W, S), bf),
            jax.ShapeDtypeStruct((B, S, KV_W), bf),
            jax.ShapeDtypeStruct((B, nt, KV_W, tm), bf),
            jax.ShapeDtypeStruct((B, SGU_W, S), bf),
            jax.ShapeDtypeStruct((B, SGU_W, S), bf),
            jax.ShapeDtypeStruct((B, D_MODEL, S), bf),
            jax.ShapeDtypeStruct((B, D_MODEL, S), bf),
        ],
        compiler_params=_compiler_params("arbitrary", "arbitrary"),
        name="in_proj",
    )(x, gmix, w_inT, qa, qb, ka, kb, gs)


def _attention_kernel(qT_ref, k_ref, vT_ref, o_ref, m_ref, gap_ref, acc_ref):
    tq = qT_ref.shape[2]
    n_kt = vT_ref.shape[1]
    tk = vT_ref.shape[3]
    nq = GQA_GROUP * tq
    ones_rows = (lax.broadcasted_iota(jnp.int32, (_SUM_ROWS, tk), 0) == 0).astype(jnp.bfloat16)

    zeros = jnp.zeros((HEAD_DIM, nq), jnp.bfloat16)
    qp = []
    for kv in range(N_KV_HEADS):
        heads = [qT_ref[0, (kv * GQA_GROUP + g) * HEAD_DIM:(kv * GQA_GROUP + g + 1) * HEAD_DIM, :]
                 for g in range(GQA_GROUP)]
        qk = jnp.concatenate(heads, axis=1)
        qp.append(jnp.concatenate([qk, zeros] if kv == 0 else [zeros, qk], axis=0))

    def key_tile(ki):
        return k_ref[0, pl.ds(pl.multiple_of(ki * tk, tk), tk), :]

    def value_rows(ki, kv):
        vt = vT_ref[0, ki]
        return jnp.concatenate([vt[kv * HEAD_DIM:(kv + 1) * HEAD_DIM, :], ones_rows], axis=0)

    acc_ref[...] = jnp.zeros(acc_ref.shape, jnp.float32)
    gap_ref[...] = jnp.zeros(gap_ref.shape, jnp.float32)
    def as_reference(m):
        return m.astype(jnp.bfloat16).astype(jnp.float32)

    first_keys = k_ref[0, 0:_BF16_SUBLANES, :]
    for kv in range(N_KV_HEADS):
        s0 = jnp.dot(first_keys, qp[kv], preferred_element_type=jnp.float32)
        m_ref[kv] = as_reference(jnp.max(s0, axis=0, keepdims=True))

    ones_cols = (lax.broadcasted_iota(jnp.int32, (tk, _LANES), 1) == 0).astype(jnp.bfloat16)
    ref_row = lax.broadcasted_iota(jnp.int32, (_BF16_SUBLANES, nq), 0) == 0
    pad_rows = jnp.zeros((_LANES - _BF16_SUBLANES, nq), jnp.bfloat16)

    def fast_tile(ki):
        kt = jnp.concatenate([key_tile(ki), ones_cols], axis=1)
        for kv in range(N_KV_HEADS):
            m_old = m_ref[kv]
            neg_ref = jnp.where(ref_row, -m_old, 0.0).astype(jnp.bfloat16)
            q_aug = jnp.concatenate([qp[kv], neg_ref, pad_rows], axis=0)
            s = jnp.dot(kt, q_aug, preferred_element_type=jnp.float32)
            p = jnp.exp2(s).astype(jnp.bfloat16)
            over = jnp.max(s, axis=0, keepdims=True)
            pv = jnp.dot(value_rows(ki, kv), p, preferred_element_type=jnp.float32)
            m_new = as_reference(m_old + jnp.maximum(over, 0.0))
            acc_ref[kv] = (acc_ref[kv] + pv) * jnp.exp2(m_old - m_new)
            m_ref[kv] = m_new
            gap_ref[kv] = jnp.maximum(gap_ref[kv], over)

    per_trip = math.gcd(n_kt, _KEY_TILES_PER_TRIP)

    def fast_group(j, carry):
        for t in range(per_trip):
            fast_tile(per_trip * j + t)
        return carry

    lax.fori_loop(0, n_kt // per_trip, fast_group, 0)

    @pl.when(jnp.max(gap_ref[...]) > _MAX_LAGGED_EXPONENT)
    def _():
        m_ref[...] = jnp.full(m_ref.shape, _NEG_BIG, jnp.float32)
        acc_ref[...] = jnp.zeros(acc_ref.shape, jnp.float32)

        def robust_tile(ki, carry):
            kt = key_tile(ki)
            for kv in range(N_KV_HEADS):
                s = jnp.dot(kt, qp[kv], preferred_element_type=jnp.float32)
                m_old = m_ref[kv]
                m_new = jnp.maximum(m_old, jnp.max(s, axis=0, keepdims=True))
                p = jnp.exp2(s - m_new).astype(jnp.bfloat16)
                pv = jnp.dot(value_rows(ki, kv), p, preferred_element_type=jnp.float32)
                acc_ref[kv] = jnp.exp2(m_old - m_new) * acc_ref[kv] + pv
                m_ref[kv] = m_new
            return carry

        lax.fori_loop(0, n_kt, robust_tile, 0)

    for kv in range(N_KV_HEADS):
        acc = acc_ref[kv]
        o = acc[:HEAD_DIM] / acc[HEAD_DIM:HEAD_DIM + 1]
        for g in range(GQA_GROUP):
            r0 = (kv * GQA_GROUP + g) * HEAD_DIM
            o_ref[0, r0:r0 + HEAD_DIM, :] = o[:, g * tq:(g + 1) * tq].astype(jnp.bfloat16)


def _attention(qT, k, vT):
    B, _, S = qT.shape
    tq = Q_TILE
    nq = GQA_GROUP * tq
    n_kt, tk = vT.shape[1], vT.shape[3]
    return pl.pallas_call(
        _attention_kernel,
        grid=(B, S // tq),
        in_specs=[
            pl.BlockSpec((1, Q_W, tq), lambda b, i: (b, 0, i)),
            pl.BlockSpec((1, S, KV_W), lambda b, i: (b, 0, 0)),
            pl.BlockSpec((1, n_kt, KV_W, tk), lambda b, i: (b, 0, 0, 0)),
        ],
        out_specs=pl.BlockSpec((1, Q_W, tq), lambda b, i: (b, 0, i)),
        out_shape=jax.ShapeDtypeStruct((B, Q_W, S), jnp.bfloat16),
        scratch_shapes=[
            pltpu.VMEM((N_KV_HEADS, 1, nq), jnp.float32),
            pltpu.VMEM((N_KV_HEADS, 1, nq), jnp.float32),
            pltpu.VMEM((N_KV_HEADS, HEAD_DIM + _SUM_ROWS, nq), jnp.float32),
        ],
        compiler_params=_compiler_params("arbitrary", "arbitrary"),
        name="attention",
    )(qT, k, vT)


def _post_kernel(attnT_ref, uT_ref, vsT_ref, gaT_ref, gbT_ref, x_ref,
                 wsT_ref, bs_ref, wpaT_ref, wpbT_ref, wout_ref, gffn_ref, wrT_ref, br_ref,
                 x1_ref, h2_ref, ids_ref, wts_ref, rank_ref, cnt_ref, carry_ref):
    tm = x_ref.shape[1]
    n_chunks = tm // CHUNK
    first = jnp.logical_and(pl.program_id(0) == 0, pl.program_id(1) == 0)

    @pl.when(first)
    def _():
        carry_ref[...] = jnp.zeros(carry_ref.shape, jnp.float32)

    gate_rows = []
    for g in range(SGU_GROUPS):
        r0 = g * SGU_GROUP_DIM
        vs_g = vsT_ref[0, r0:r0 + SGU_GROUP_DIM, :]
        lhs = jnp.concatenate(
            [vs_g[:, c * CHUNK:(c + 1) * CHUNK] for c in range(n_chunks)], axis=0)
        mixed = jnp.dot(lhs, wsT_ref[g], preferred_element_type=jnp.float32)
        mixed = mixed + bs_ref[g]
        mixedT = jnp.concatenate(
            [mixed[c * SGU_GROUP_DIM:(c + 1) * SGU_GROUP_DIM] for c in range(n_chunks)], axis=1)
        u_g = uT_ref[0, r0:r0 + SGU_GROUP_DIM, :].astype(jnp.float32)
        gate_rows.append((u_g * mixedT).astype(jnp.bfloat16))
    gateT = jnp.concatenate(gate_rows, axis=0)

    paT = jnp.dot(wpaT_ref[...], attnT_ref[0], preferred_element_type=jnp.float32)
    pbT = jnp.dot(wpbT_ref[...], gateT, preferred_element_type=jnp.float32)
    mT = (gaT_ref[0].astype(jnp.float32) * paT
          + gbT_ref[0].astype(jnp.float32) * pbT).astype(jnp.bfloat16)
    y = lax.dot_general(mT, wout_ref[...], _TN_DIMS, preferred_element_type=jnp.float32)
    x1 = x_ref[0] + y
    x1_ref[0] = x1

    ms = jnp.mean(x1 * x1, axis=-1, keepdims=True)
    h2 = x1 * lax.rsqrt(ms + EPS) * gffn_ref[...]
    h2_ref[0] = _pack_row_halves(h2)

    h_hi = h2.astype(jnp.bfloat16)
    h_lo = (h2 - h_hi.astype(jnp.float32)).astype(jnp.bfloat16)
    by_hi = lax.dot_general(wrT_ref[...], h_hi, _NT_DIMS, preferred_element_type=jnp.float32)
    by_lo = lax.dot_general(wrT_ref[:N_EXPERTS, :], h_lo, _NT_DIMS,
                            preferred_element_type=jnp.float32)
    logits = by_hi[:N_EXPERTS] + by_hi[N_EXPERTS:] + by_lo + br_ref[...]
    eidx = lax.broadcasted_iota(jnp.int32, logits.shape, 0).astype(jnp.float32)
    work = logits
    vals, ids, sels = [], [], []
    for _ in range(TOP_K):
        mx = jnp.max(work, axis=0, keepdims=True)
        idx = jnp.min(jnp.where(work == mx, eidx, float(N_EXPERTS)), axis=0, keepdims=True)
        sel = eidx == idx
        vals.append(mx)
        ids.append(idx)
        sels.append(sel)
        work = jnp.where(sel, -jnp.inf, work)
    exps = [jnp.exp(v - vals[0]) for v in vals]
    denom = exps[0] + exps[1] + exps[2] + exps[3]
    ids_ref[...] = jnp.concatenate(ids, axis=0).astype(jnp.int32)
    wts_ref[...] = jnp.concatenate([e / denom for e in exps], axis=0)

    onehot = [s.astype(jnp.float32) for s in sels]
    hits = onehot[0] + onehot[1] + onehot[2] + onehot[3]
    ti = lax.broadcasted_iota(jnp.int32, (tm, tm), 0)
    tj = lax.broadcasted_iota(jnp.int32, (tm, tm), 1)
    upper = (ti < tj).astype(jnp.bfloat16)
    prefix = jnp.dot(hits.astype(jnp.bfloat16), upper, preferred_element_type=jnp.float32)
    base = prefix + carry_ref[...]
    ranks = [jnp.sum(oh * base, axis=0, keepdims=True) for oh in onehot]
    rank_ref[...] = jnp.concatenate(ranks, axis=0).astype(jnp.int32)
    carry_ref[...] = carry_ref[...] + jnp.sum(hits, axis=1, keepdims=True)
    cnt_ref[...] = carry_ref[...].astype(jnp.int32)


def _post(attnT, uT, vsT, gaT, gbT, x, wsT, bs, wpaT, wpbT, wout, gffn, wrT, br):
    B, S, _ = x.shape
    tm = TOKEN_TILE
    nt = S // tm
    T = B * S
    fm = lambda rows: pl.BlockSpec((1, rows, tm), lambda b, i: (b, 0, i))
    rowm = pl.BlockSpec((1, tm, D_MODEL), lambda b, i: (b, i, 0))
    c2 = lambda b, i: (0, 0)
    c3 = lambda b, i: (0, 0, 0)
    tokT = pl.BlockSpec((TOP_K, tm), lambda b, i: (0, b * nt + i))
    return pl.pallas_call(
        _post_kernel,
        grid=(B, nt),
        in_specs=[
            fm(Q_W), fm(SGU_W), fm(SGU_W), fm(D_MODEL), fm(D_MODEL), rowm,
            pl.BlockSpec((SGU_GROUPS, CHUNK, CHUNK), c3),
            pl.BlockSpec((SGU_GROUPS, 1, CHUNK), c3),
            pl.BlockSpec((D_MODEL, Q_W), c2),
            pl.BlockSpec((D_MODEL, SGU_W), c2),
            pl.BlockSpec((D_MODEL, D_MODEL), c2),
            pl.BlockSpec((1, D_MODEL), c2),
            pl.BlockSpec((2 * N_EXPERTS, D_MODEL), c2),
            pl.BlockSpec((N_EXPERTS, 1), c2),
        ],
        out_specs=[rowm, pl.BlockSpec((1, tm, PACKED_W), lambda b, i: (b, i, 0)),
                   tokT, tokT, tokT, pl.BlockSpec((N_EXPERTS, 1), c2)],
        out_shape=[
            jax.ShapeDtypeStruct((B, S, D_MODEL), jnp.float32),
            jax.ShapeDtypeStruct((B, S, PACKED_W), jnp.int32),
            jax.ShapeDtypeStruct((TOP_K, T), jnp.int32),
            jax.ShapeDtypeStruct((TOP_K, T), jnp.float32),
            jax.ShapeDtypeStruct((TOP_K, T), jnp.int32),
            jax.ShapeDtypeStruct((N_EXPERTS, 1), jnp.int32),
        ],
        scratch_shapes=[pltpu.VMEM((N_EXPERTS, 1), jnp.float32)],
        compiler_params=_compiler_params("arbitrary", "arbitrary"),
        name="post",
    )(attnT, uT, vsT, gaT, gbT, x, wsT, bs, wpaT, wpbT, wout, gffn, wrT, br)


def _sc_kernel(name, body, out_type, n_items, row_shape, row_dtype):
    info = plsc.get_sparse_core_info()
    n_workers = info.num_cores * info.num_subcores
    per_worker = n_items // n_workers
    n_windows = per_worker // SC_WINDOW
    assert per_worker * n_workers == n_items and n_windows * SC_WINDOW == per_worker
    assert n_windows % 2 == 0

    def wrapped(*refs):
        wid = lax.axis_index("subcore") * info.num_cores + lax.axis_index("core")
        body(wid * per_worker, n_windows, *refs)

    return pl.kernel(
        wrapped,
        name=name,
        out_type=out_type,
        mesh=plsc.VectorSubcoreMesh(core_axis_name="core", subcore_axis_name="subcore"),
        scratch_types=[
            pltpu.VMEM((SC_WINDOW,), jnp.int32),
            pltpu.VMEM((SC_WINDOW,), jnp.int32),
            pltpu.VMEM((SC_WINDOW,) + row_shape, row_dtype),
            pltpu.VMEM((SC_WINDOW,) + row_shape, row_dtype),
            pltpu.SemaphoreType.DMA,
            pltpu.SemaphoreType.DMA,
        ],
    )


def _sc_gather_rows(table, idx):
    n, d, win = idx.shape[0], table.shape[1], SC_WINDOW

    def body(base, n_windows, table_hbm, idx_hbm, out_hbm, idx0, idx1, rows0, rows1, sem0, sem1):
        def window(c):
            return pl.ds(pl.multiple_of(base + c * win, win), win)

        def fetch(c, idx_v, rows_v, sem):
            pltpu.sync_copy(idx_hbm.at[window(c)], idx_v)
            pltpu.async_copy(table_hbm.at[idx_v], rows_v, sem)

        def drain(c, idx_v, rows_v, sem):
            pltpu.make_async_copy(table_hbm.at[idx_v], rows_v, sem).wait()
            pltpu.sync_copy(rows_v, out_hbm.at[window(c)])

        fetch(0, idx0, rows0, sem0)

        @pl.loop(0, n_windows, step=2)
        def _(c):
            fetch(c + 1, idx1, rows1, sem1)
            drain(c, idx0, rows0, sem0)

            @pl.when(c + 2 < n_windows)
            def _():
                fetch(c + 2, idx0, rows0, sem0)

            drain(c + 1, idx1, rows1, sem1)

    out_type = jax.ShapeDtypeStruct((n, d), table.dtype)
    return _sc_kernel("sc_gather_rows", body, out_type, n, (d,), table.dtype)(table, idx)


def _sc_scatter_rows(src, dest, n_rows):
    T, d = src.shape
    win = SC_WINDOW

    def body(base, n_windows, src_hbm, dest_hbm, out_hbm, idx0, idx1, rows0, rows1, sem0, sem1):
        def window(c, slot=0):
            return pl.ds(pl.multiple_of(slot * T + base + c * win, win), win)

        def fetch(c, rows_v, sem):
            pltpu.async_copy(src_hbm.at[window(c)], rows_v, sem)

        def push(c, rows_v, sem):
            pltpu.make_async_copy(src_hbm.at[window(c)], rows_v, sem).wait()
            for k in range(TOP_K):
                idx_v = idx0 if k % 2 == 0 else idx1
                pltpu.sync_copy(dest_hbm.at[window(c, k)], idx_v)
                pltpu.sync_copy(rows_v, out_hbm.at[idx_v])

        fetch(0, rows0, sem0)

        @pl.loop(0, n_windows, step=2)
        def _(c):
            fetch(c + 1, rows1, sem1)
            push(c, rows0, sem0)

            @pl.when(c + 2 < n_windows)
            def _():
                fetch(c + 2, rows0, sem0)

            push(c + 1, rows1, sem1)

    out_type = jax.ShapeDtypeStruct((n_rows, d), src.dtype)
    return _sc_kernel("sc_scatter_rows", body, out_type, T, (d,), src.dtype)(
        src, dest.reshape(TOP_K * T))


def _combine_dense_kernel(w_ref, x1_ref, gfin_ref, y_hbm, o_ref, ybuf, sems):
    tt = x1_ref.shape[0]
    i = pl.program_id(0)
    n = pl.num_programs(0)

    def tile_copy(j):
        slot = j % _COMBINE_RING
        src = y_hbm.at[:, pl.ds(pl.multiple_of(j * tt, tt), tt), :]
        return pltpu.make_async_copy(src, ybuf.at[slot], sems.at[slot])

    @pl.when(i == 0)
    def _():
        for j in range(_COMBINE_RING - 1):
            @pl.when(j < n)
            def _(j=j):
                tile_copy(j).start()

    @pl.when(i + _COMBINE_RING - 1 < n)
    def _():
        tile_copy(i + _COMBINE_RING - 1).start()

    tile_copy(i).wait()
    y = ybuf[i % _COMBINE_RING]
    w = w_ref[...]
    moe = w[:, 0:1] * _unpack_row_halves(y[0])
    for k in range(1, TOP_K):
        moe = moe + w[:, k:k + 1] * _unpack_row_halves(y[k])
    x2 = x1_ref[...] + moe
    ms = jnp.mean(x2 * x2, axis=-1, keepdims=True)
    o_ref[...] = x2 * lax.rsqrt(ms + EPS) * gfin_ref[...]


def _combine_dense(wts, x1, gfin, y_tok):
    T = x1.shape[0]
    tt = COMBINE_TILE
    return pl.pallas_call(
        _combine_dense_kernel,
        grid=(T // tt,),
        in_specs=[
            pl.BlockSpec((tt, TOP_K), lambda i: (i, 0)),
            pl.BlockSpec((tt, D_MODEL), lambda i: (i, 0)),
            pl.BlockSpec((1, D_MODEL), lambda i: (0, 0)),
            pl.BlockSpec(memory_space=pl.ANY),
        ],
        out_specs=pl.BlockSpec((tt, D_MODEL), lambda i: (i, 0)),
        out_shape=jax.ShapeDtypeStruct((T, D_MODEL), jnp.float32),
        scratch_shapes=[pltpu.VMEM((_COMBINE_RING, TOP_K, tt, PACKED_W), jnp.int32),
                        pltpu.SemaphoreType.DMA((_COMBINE_RING,))],
        compiler_params=_compiler_params("arbitrary"),
        name="combine_dense",
    )(wts, x1, gfin, y_tok)


def _gate_up_prep_kernel(w_ref, o_ref):
    half = _GU_BLOCK // 2
    i = lax.broadcasted_iota(jnp.int32, (_GU_BLOCK, _GU_BLOCK), 0)
    j = lax.broadcasted_iota(jnp.int32, (_GU_BLOCK, _GU_BLOCK), 1)
    perm = (i == jnp.where(j < half, 2 * j, 2 * (j - half) + 1)).astype(jnp.bfloat16)
    for c in range(2 * D_FF // _GU_BLOCK):
        cols = slice(c * _GU_BLOCK, (c + 1) * _GU_BLOCK)
        blk = w_ref[0, :, cols].astype(jnp.bfloat16)
        o_ref[0, :, cols] = jnp.dot(blk, perm, preferred_element_type=jnp.float32).astype(jnp.bfloat16)


def _gate_up_prep(w_gate_up):
    spec = pl.BlockSpec((1, D_MODEL, 2 * D_FF), lambda e: (e, 0, 0))
    return pl.pallas_call(
        _gate_up_prep_kernel,
        grid=(N_EXPERTS,),
        in_specs=[spec],
        out_specs=spec,
        out_shape=jax.ShapeDtypeStruct(w_gate_up.shape, jnp.bfloat16),
        compiler_params=_compiler_params("arbitrary"),
        name="gate_up_prep",
    )(w_gate_up)


def _split_gate_up(gu):
    half = _GU_BLOCK // 2
    n = gu.shape[1] // _GU_BLOCK
    glu = jnp.concatenate([gu[:, c * _GU_BLOCK:c * _GU_BLOCK + half] for c in range(n)], axis=1)
    lin = jnp.concatenate([gu[:, c * _GU_BLOCK + half:(c + 1) * _GU_BLOCK] for c in range(n)], axis=1)
    return glu, lin


def _experts_kernel(te_ref, nv_ref, x_ref, wgu_ref, bgu_ref, wd_ref, bd_ref, y_ref):
    @pl.when(pl.program_id(0) < nv_ref[0])
    def _():
        x = _unpack_row_halves(x_ref[...]).astype(jnp.bfloat16)
        gu = jnp.dot(x, wgu_ref[0], preferred_element_type=jnp.float32) + bgu_ref[0]
        glu, lin = _split_gate_up(gu)
        glu = jnp.minimum(glu, SWIGLU_LIMIT)
        lin = jnp.clip(lin, -SWIGLU_LIMIT, SWIGLU_LIMIT)
        a = glu * jax.nn.sigmoid(SWIGLU_ALPHA * glu) * (lin + 1.0)
        y = jnp.dot(a.astype(jnp.bfloat16), wd_ref[0],
                    preferred_element_type=jnp.float32) + bd_ref[0]
        y_ref[...] = _pack_row_halves(y)


def _experts(tile_expert, n_valid, x_sorted, wgu, bgu, wd, bd):
    n_rows = x_sorted.shape[0]
    tr = EXPERT_TILE
    n_tiles = n_rows // tr

    def row_map(j, te, nv):
        return (jnp.minimum(j, nv[0] - 1), 0)

    def exp_map(j, te, nv):
        return (te[j], 0, 0)

    grid_spec = pltpu.PrefetchScalarGridSpec(
        num_scalar_prefetch=2,
        grid=(n_tiles,),
        in_specs=[
            pl.BlockSpec((tr, PACKED_W), row_map),
            pl.BlockSpec((1, D_MODEL, 2 * D_FF), exp_map),
            pl.BlockSpec((1, 1, 2 * D_FF), exp_map),
            pl.BlockSpec((1, D_FF, D_MODEL), exp_map),
            pl.BlockSpec((1, 1, D_MODEL), exp_map),
        ],
        out_specs=pl.BlockSpec((tr, PACKED_W), row_map),
    )
    return pl.pallas_call(
        _experts_kernel,
        grid_spec=grid_spec,
        out_shape=jax.ShapeDtypeStruct((n_rows, PACKED_W), jnp.int32),
        compiler_params=_compiler_params("arbitrary"),
        name="experts",
    )(tile_expert, n_valid, x_sorted, wgu, bgu, wd, bd)


def _rope_tables(S, gain, scale):
    t = jnp.arange(S, dtype=jnp.int32)
    r = (t // GRID_W).astype(jnp.float32)
    c = (t % GRID_W).astype(jnp.float32)
    inv = jnp.float32(ROPE_THETA) ** (
        -jnp.arange(0, ROPE_AXIS_DIM, 2, dtype=jnp.float32) / ROPE_AXIS_DIM)
    ang = jnp.concatenate([r[None, :] * inv[:, None], c[None, :] * inv[:, None]], axis=0)
    cos = jnp.repeat(jnp.cos(ang), 2, axis=0)
    sin = jnp.repeat(jnp.sin(ang), 2, axis=0)
    sign = jnp.where(jnp.arange(HEAD_DIM) % 2 == 0, -1.0, 1.0).astype(jnp.float32)
    g = gain.astype(jnp.float32) * scale
    g_swapped = g.reshape(HEAD_DIM // 2, 2)[:, ::-1].reshape(HEAD_DIM)
    return g[:, None] * cos, (g_swapped * sign)[:, None] * sin


def _split_hi_lo(w):
    hi = w.astype(jnp.bfloat16)
    lo = (w - hi.astype(jnp.float32)).astype(jnp.bfloat16)
    return jnp.concatenate([hi, lo], axis=0)


def _prepare_weights(norm_mix_g, w_in, q_norm_g, k_norm_g, sgu_norm_g, w_spatial, b_spatial,
                     w_proj_attn, w_proj_sgu, w_out, norm_ffn_g, w_router, b_router,
                     w_gate_up, b_gate_up, w_down, b_down, norm_final_g):
    bf = jnp.bfloat16
    l = 0
    return dict(
        gmix=norm_mix_g[l][None, :],
        w_inT=w_in[l].T.astype(bf),
        q_gain=q_norm_g[l], k_gain=k_norm_g[l],
        gs=jnp.broadcast_to(sgu_norm_g[l][:, None], (SGU_W, _LANES)),
        wsT=jnp.swapaxes(w_spatial[l], 1, 2).astype(bf),
        bs=b_spatial[l][:, None, :],
        wpaT=w_proj_attn[l].T.astype(bf),
        wpbT=w_proj_sgu[l].T.astype(bf),
        wout=w_out[l].astype(bf),
        gffn=norm_ffn_g[l][None, :],
        wrT=_split_hi_lo(w_router[l].T),
        br=b_router[l][:, None],
        wgu=_gate_up_prep(w_gate_up[l]),
        bgu=b_gate_up[l].reshape(N_EXPERTS, -1, _GU_BLOCK // 2, 2).transpose(0, 1, 3, 2)
        .reshape(N_EXPERTS, 1, 2 * D_FF),
        wd=w_down[l].astype(bf),
        bd=b_down[l][:, None, :],
        gfin=norm_final_g[None, :],
    )


def _trunk(x, w, rope):
    B, S, _ = x.shape
    T = B * S
    qT, k, vT, uT, vsT, gaT, gbT = _in_proj(x, w["gmix"], w["w_inT"], *rope, w["gs"])
    attnT = _attention(qT, k, vT)
    x1, h2, ids, wts, rank, counts = _post(
        attnT, uT, vsT, gaT, gbT, x, w["wsT"], w["bs"], w["wpaT"], w["wpbT"], w["wout"],
        w["gffn"], w["wrT"], w["br"])

    tr = EXPERT_TILE
    counts = counts[:, 0]
    padded = (counts + tr - 1) // tr * tr
    ends = jnp.cumsum(padded)
    starts = ends - padded
    n_tiles = (TOP_K * T) // tr + N_EXPERTS
    n_rows = n_tiles * tr
    dest = rank
    for e in range(N_EXPERTS):
        dest = dest + jnp.where(ids == e, starts[e], 0)
    tile_start = jnp.arange(n_tiles, dtype=jnp.int32) * tr
    tile_expert = jnp.minimum(
        jnp.sum((tile_start[:, None] >= ends[None, :]).astype(jnp.int32), axis=1), N_EXPERTS - 1)
    n_valid = (ends[-1] // tr).astype(jnp.int32)[None]

    x_sorted = _sc_scatter_rows(h2.reshape(T, PACKED_W), dest, n_rows)
    y_sorted = _experts(tile_expert, n_valid, x_sorted, w["wgu"], w["bgu"], w["wd"], w["bd"])
    y_tok = _sc_gather_rows(y_sorted, dest.reshape(TOP_K * T)).reshape(TOP_K, T, PACKED_W)
    out = _combine_dense(wts.T, x1.reshape(T, D_MODEL), w["gfin"], y_tok)
    return out.reshape(B, S, D_MODEL)


def kernel(x_prompt, x_sample, norm_mix_g, w_in, q_norm_g, k_norm_g, sgu_norm_g, w_spatial,
           b_spatial, w_proj_attn, w_proj_sgu, w_out, norm_ffn_g, w_router, b_router,
           w_gate_up, b_gate_up, w_down, b_down, norm_final_g):
    w = _prepare_weights(norm_mix_g, w_in, q_norm_g, k_norm_g, sgu_norm_g, w_spatial,
                         b_spatial, w_proj_attn, w_proj_sgu, w_out, norm_ffn_g, w_router,
                         b_router, w_gate_up, b_gate_up, w_down, b_down, norm_final_g)
    x_prompt, w["wgu"], w["wd"] = lax.optimization_barrier((x_prompt, w["wgu"], w["wd"]))
    s_max = max(x_prompt.shape[1], x_sample.shape[1])
    rope = (*_rope_tables(s_max, w["q_gain"], math.log2(math.e) / math.sqrt(HEAD_DIM)),
            *_rope_tables(s_max, w["k_gain"], 1.0))
    return (_trunk(x_prompt, w, rope), _trunk(x_sample, w, rope))
```
